```python
import jax, jax.numpy as jnp
from jax import lax
import numpy as np

D_MODEL = 1024
BATCH = 2
SEQ = 8192
DEPTH = 2

D_MIX = D_MODEL
HEAD_DIM = 64
N_ATTN_HEADS = 8
ATTN_DIM = N_ATTN_HEADS * HEAD_DIM
N_CONV_GROUPS = 8
CONV_DIM = D_MIX - ATTN_DIM
CONV_WIDTH = 3
IN_COLS = 3 * ATTN_DIM + 3 * CONV_DIM
D_FF = ((8 * D_MODEL // 3 + 255) // 256) * 256
Q_BLOCK = 128
EPS = 1e-6

kernel_name = "hymba_stickbreak_shortconv_swiglu"


def _rms_norm(x, gain):
    x32 = x.astype(jnp.float32)
    y = x32 * lax.rsqrt(jnp.mean(x32 * x32, axis=-1, keepdims=True) + EPS)
    return (y * gain.astype(jnp.float32)).astype(x.dtype)


def _stick_breaking_attention(q, k, v):
    b, h, s, dh = q.shape
    nb = s // Q_BLOCK
    scale = dh ** -0.5
    k32 = k.astype(jnp.float32)
    v32 = v.astype(jnp.float32)
    q_blocks = q.reshape(b, h, nb, Q_BLOCK, dh).transpose(2, 0, 1, 3, 4)
    starts = jnp.arange(nb, dtype=jnp.int32) * Q_BLOCK
    key_pos = jnp.arange(s, dtype=jnp.int32)

    def one_block(args):
        q_blk, start = args
        z = jnp.einsum('bhqd,bhkd->bhqk', q_blk.astype(jnp.float32), k32) * scale
        q_pos = start + jnp.arange(Q_BLOCK, dtype=jnp.int32)
        causal = key_pos[None, :] < q_pos[:, None]
        sp = jax.nn.softplus(z)
        neg_log_rem = jnp.where(causal, sp, 0.0)
        after = lax.cumsum(neg_log_rem, axis=3, reverse=True) - neg_log_rem
        log_a = (z - sp) - after
        a = jnp.where(causal, jnp.exp(log_a), 0.0)
        return jnp.einsum('bhqk,bhkd->bhqd', a, v32)

    out = lax.map(one_block, (q_blocks, starts))
    return out.transpose(1, 2, 0, 3, 4).reshape(b, h, s, dh).astype(q.dtype)


def _short_conv_mixer(b_gate, c_gate, u, conv_w):
    h = c_gate * u
    rhs = conv_w.astype(h.dtype)[:, None, :]
    y = lax.conv_general_dilated(h, rhs, window_strides=(1,), padding=[(CONV_WIDTH - 1, 0)],
                                 dimension_numbers=('NWC', 'WIO', 'NWC'),
                                 feature_group_count=CONV_DIM)
    return b_gate * y


def setup_inputs(seed: int = 0) -> dict:
    key = jax.random.key(seed)
    ks = jax.random.split(key, 12)
    f32 = jnp.float32

    def gain(k, shape):
        return jnp.ones(shape, f32) + 0.01 * jax.random.normal(k, shape, f32)

    x = jax.random.normal(ks[0], (BATCH, SEQ, D_MODEL), f32)
    norm_mix = gain(ks[1], (DEPTH, D_MODEL))
    w_in = jax.random.normal(ks[2], (DEPTH, D_MODEL, IN_COLS), f32) * D_MODEL ** -0.5
    q_norm = gain(ks[3], (DEPTH, HEAD_DIM))
    k_norm = gain(ks[4], (DEPTH, HEAD_DIM))
    conv_w = jax.random.normal(ks[5], (DEPTH, CONV_WIDTH, CONV_DIM), f32) * CONV_WIDTH ** -0.5
    w_out = jax.random.normal(ks[6], (DEPTH, D_MIX, D_MODEL), f32) * D_MIX ** -0.5
    norm_ffn = gain(ks[7], (DEPTH, D_MODEL))
    w_gate = jax.random.normal(ks[8], (DEPTH, D_MODEL, D_FF), f32) * D_MODEL ** -0.5
    w_up = jax.random.normal(ks[9], (DEPTH, D_MODEL, D_FF), f32) * D_MODEL ** -0.5
    w_down = jax.random.normal(ks[10], (DEPTH, D_FF, D_MODEL), f32) * D_FF ** -0.5
    return {"x": x, "norm_mix": norm_mix, "w_in": w_in, "q_norm": q_norm, "k_norm": k_norm,
            "conv_w": conv_w, "w_out": w_out, "norm_ffn": norm_ffn, "w_gate": w_gate,
            "w_up": w_up, "w_down": w_down}


def reference(x, norm_mix, w_in, q_norm, k_norm, conv_w, w_out, norm_ffn, w_gate, w_up, w_down):
    b, s, _ = x.shape
    splits = [ATTN_DIM, 2 * ATTN_DIM, 3 * ATTN_DIM,
              3 * ATTN_DIM + CONV_DIM, 3 * ATTN_DIM + 2 * CONV_DIM]
    for l in range(DEPTH):
        h = _rms_norm(x, norm_mix[l])
        proj = h @ w_in[l].astype(h.dtype)
        q, k, v, cb, cc, cu = jnp.split(proj, splits, axis=-1)
        q = _rms_norm(q.reshape(b, s, N_ATTN_HEADS, HEAD_DIM), q_norm[l])
        k = _rms_norm(k.reshape(b, s, N_ATTN_HEADS, HEAD_DIM), k_norm[l])
        v = v.reshape(b, s, N_ATTN_HEADS, HEAD_DIM)
        attn = _stick_breaking_attention(q.transpose(0, 2, 1, 3), k.transpose(0, 2, 1, 3),
                                         v.transpose(0, 2, 1, 3))
        attn = attn.transpose(0, 2, 1, 3).reshape(b, s, ATTN_DIM)
        conv = _short_conv_mixer(cb, cc, cu, conv_w[l])
        mix = jnp.concatenate([attn, conv], axis=-1)
        x = x + mix @ w_out[l].astype(mix.dtype)
        h = _rms_norm(x, norm_ffn[l])
        g = h @ w_gate[l].astype(h.dtype)
        u = h @ w_up[l].astype(h.dtype)
        x = x + (jax.nn.silu(g) * u) @ w_down[l].astype(h.dtype)
    return x
```

```python
import functools
import math

import jax
import jax.numpy as jnp
from jax import lax
from jax.experimental import pallas as pl
from jax.experimental.pallas import tpu as pltpu

F32 = jnp.float32
BF16 = jnp.bfloat16

HEAD_DIM = 64
EPS = 1e-6
LANES = 128
MXU_DIM = 256
HEADS_PER_BLOCK = LANES // HEAD_DIM
SUB = 128
LOG2E = 1.4426950408889634
LN2 = 0.6931471805599453

ROW_TILE = 512
Q_TILE = 256
K_TILE = 256
FF_CHUNK = 256
CONV_HALO = 8
VMEM_LIMIT = 56 * 1024 * 1024


def _rms_scale(x):
    return lax.rsqrt(jnp.mean(x * x, axis=-1, keepdims=True) + EPS)


def _inproj_kernel(x_ref, gain_ref, w_ref, qg_ref, kg_ref, cw_ref,
                   q_ref, k_ref, v_ref, c_ref, hbuf_ref, *, tiles_per_seq, attn_dim, conv_dim):
    tm = x_ref.shape[0]
    i = pl.program_id(0)
    x = x_ref[...]
    h = ((x * _rms_scale(x)) * gain_ref[...]).astype(BF16)

    def proj(c0, width):
        return jnp.dot(h, w_ref[:, c0:c0 + width], preferred_element_type=F32)

    r = lax.broadcasted_iota(jnp.int32, (MXU_DIM, MXU_DIM), 0) // HEAD_DIM
    c = lax.broadcasted_iota(jnp.int32, (MXU_DIM, MXU_DIM), 1) // HEAD_DIM
    seg = jnp.where(r == c, 1.0, 0.0).astype(BF16)

    def head_norm(p, gain):
        p2 = (p * p).astype(BF16)
        ssq = jnp.concatenate(
            [jnp.dot(p2[:, s:s + MXU_DIM], seg, preferred_element_type=F32)
             for s in range(0, attn_dim, MXU_DIM)], axis=1)
        return (p * lax.rsqrt(ssq * (1.0 / HEAD_DIM) + EPS)) * gain

    q_ref[...] = (head_norm(proj(0, attn_dim), qg_ref[...]) * (HEAD_DIM ** -0.5)).astype(BF16)
    k_ref[...] = head_norm(proj(attn_dim, attn_dim), kg_ref[...]).astype(BF16)
    v_ref[...] = proj(2 * attn_dim, attn_dim).astype(BF16)

    cb = proj(3 * attn_dim, conv_dim)
    hh = proj(3 * attn_dim + conv_dim, conv_dim) * proj(3 * attn_dim + 2 * conv_dim, conv_dim)

    @pl.when(i % tiles_per_seq == 0)
    def _():
        hbuf_ref[0:CONV_HALO, :] = jnp.zeros((CONV_HALO, conv_dim), F32)

    hbuf_ref[CONV_HALO:CONV_HALO + tm, :] = hh
    h1 = hbuf_ref[CONV_HALO - 1:CONV_HALO - 1 + tm, :]
    h2 = hbuf_ref[CONV_HALO - 2:CONV_HALO - 2 + tm, :]
    y = cw_ref[0:1, :] * h2 + cw_ref[1:2, :] * h1 + cw_ref[2:3, :] * hh
    c_ref[...] = (cb * y).astype(BF16)
    hbuf_ref[0:CONV_HALO, :] = hh[tm - CONV_HALO:tm, :]


def _inproj(x2d, gain, w_bf16, q_gain, k_gain, conv_w, *, seq, attn_dim, conv_dim):
    n, d = x2d.shape
    tm = ROW_TILE
    cols = w_bf16.shape[1]
    const = lambda i: (0, 0)
    row = lambda i: (i, 0)
    out_shapes = [jax.ShapeDtypeStruct((n, attn_dim), BF16)] * 3 + [jax.ShapeDtypeStruct((n, conv_dim), BF16)]
    return pl.pallas_call(
        functools.partial(_inproj_kernel, tiles_per_seq=seq // tm, attn_dim=attn_dim, conv_dim=conv_dim),
        grid=(n // tm,),
        in_specs=[
            pl.BlockSpec((tm, d), row),
            pl.BlockSpec((1, d), const),
            pl.BlockSpec((d, cols), const, pipeline_mode=pl.Buffered(1)),
            pl.BlockSpec((1, attn_dim), const),
            pl.BlockSpec((1, attn_dim), const),
            pl.BlockSpec(conv_w.shape, const),
        ],
        out_specs=[pl.BlockSpec((tm, attn_dim), row)] * 3 + [pl.BlockSpec((tm, conv_dim), row)],
        out_shape=out_shapes,
        scratch_shapes=[pltpu.VMEM((CONV_HALO + tm, conv_dim), F32)],
        compiler_params=pltpu.CompilerParams(
            dimension_semantics=("arbitrary",), vmem_limit_bytes=VMEM_LIMIT),
    )(x2d, gain, w_bf16, q_gain, k_gain, conv_w)


def _attn_kernel(q_ref, k_ref, v_ref, o_ref, acc_ref, carry_ref):
    tq = q_ref.shape[0]
    tk = K_TILE
    i = pl.program_id(2)

    q = q_ref[...]
    qlane = lax.broadcasted_iota(jnp.int32, (tq, LANES), 1)
    q_heads = [jnp.where((qlane // HEAD_DIM) == h, q, jnp.zeros_like(q)) for h in range(HEADS_PER_BLOCK)]
    vlane = lax.broadcasted_iota(jnp.int32, (tk, LANES), 1)

    r = lax.broadcasted_iota(jnp.int32, (2 * SUB, 2 * SUB), 0) % SUB
    c = lax.broadcasted_iota(jnp.int32, (2 * SUB, 2 * SUB), 1)
    tri = jnp.where((c >= SUB) | (r >= c), 1.0, 0.0).astype(BF16)

    qpos = lax.broadcasted_iota(jnp.int32, (tq, tk), 0)
    kpos = lax.broadcasted_iota(jnp.int32, (tq, tk), 1)
    causal = kpos < qpos

    acc_ref[...] = jnp.zeros_like(acc_ref)
    carry_ref[...] = jnp.zeros_like(carry_ref)

    def process(j, diag):
        start = pl.multiple_of(j * tk, tk)
        k_blk = k_ref[pl.ds(start, tk), :]
        v_blk = v_ref[pl.ds(start, tk), :]
        a_parts = []
        for h in range(HEADS_PER_BLOCK):
            z = lax.dot_general(q_heads[h], k_blk, (((1,), (1,)), ((), ())),
                                preferred_element_type=F32)
            t = jnp.exp2(jnp.abs(z) * (-LOG2E))
            sp = jnp.maximum(z, 0.0) + jnp.log2(1.0 + t) * LN2
            if diag:
                sp = jnp.where(causal, sp, 0.0)
            hi = sp.astype(BF16)
            lo = (sp - hi.astype(F32)).astype(BF16)
            carry = carry_ref[h]
            a_cols = [None] * (tk // SUB)
            for s in reversed(range(tk // SUB)):
                sl = slice(s * SUB, (s + 1) * SUB)
                cs = jnp.dot(jnp.concatenate([hi[:, sl], lo[:, sl]], axis=1), tri,
                             preferred_element_type=F32)
                g = cs[:, :SUB] + carry
                carry = carry + cs[:, SUB:]
                a = jnp.exp2((z[:, sl] - g) * LOG2E)
                if diag:
                    a = jnp.where(causal[:, sl], a, 0.0)
                a_cols[s] = a.astype(BF16)
            carry_ref[h] = carry
            a_parts.append(jnp.concatenate(a_cols, axis=1))
        a_cat = jnp.concatenate(a_parts, axis=1)
        v_cat = jnp.concatenate(
            [jnp.where((vlane // HEAD_DIM) == h, v_blk, jnp.zeros_like(v_blk))
             for h in range(HEADS_PER_BLOCK)], axis=0)
        acc_ref[...] += jnp.dot(a_cat, v_cat, preferred_element_type=F32)

    process(i, True)

    def body(it, _):
        process(i - 1 - it, False)
        return 0

    lax.fori_loop(0, i, body, 0)
    o_ref[...] = acc_ref[...].astype(o_ref.dtype)


def _attention(q, k, v, *, batch, seq):
    n, attn_dim = q.shape
    tq = Q_TILE
    assert tq == K_TILE
    nq = seq // tq
    blocks = attn_dim // LANES
    return pl.pallas_call(
        _attn_kernel,
        grid=(batch, blocks, nq),
        in_specs=[
            pl.BlockSpec((tq, LANES), lambda b, p, i: (b * nq + i, p)),
            pl.BlockSpec((seq, LANES), lambda b, p, i: (b, p)),
            pl.BlockSpec((seq, LANES), lambda b, p, i: (b, p)),
        ],
        out_specs=pl.BlockSpec((tq, LANES), lambda b, p, i: (b * nq + i, p)),
        out_shape=jax.ShapeDtypeStruct((n, attn_dim), BF16),
        scratch_shapes=[pltpu.VMEM((tq, LANES), F32),
                        pltpu.VMEM((HEADS_PER_BLOCK, tq, SUB), F32)],
        compiler_params=pltpu.CompilerParams(
            dimension_semantics=("arbitrary", "arbitrary", "arbitrary"), vmem_limit_bytes=VMEM_LIMIT),
    )(q, k, v)


def _ffn_kernel(x_ref, attn_ref, conv_ref, wo_ref, gain_ref, wg_ref, wu_ref, wd_ref, o_ref, act_ref):
    attn_dim = attn_ref.shape[1]
    d_ff = wg_ref.shape[1]
    mix = (jnp.dot(attn_ref[...], wo_ref[0:attn_dim, :], preferred_element_type=F32)
           + jnp.dot(conv_ref[...], wo_ref[attn_dim:, :], preferred_element_type=F32))
    x1 = x_ref[...] + mix
    h = ((x1 * _rms_scale(x1)) * gain_ref[...]).astype(BF16)
    for c0 in range(0, d_ff, FF_CHUNK):
        g = jnp.dot(h, wg_ref[:, c0:c0 + FF_CHUNK], preferred_element_type=F32)
        u = jnp.dot(h, wu_ref[:, c0:c0 + FF_CHUNK], preferred_element_type=F32)
        act_ref[:, c0:c0 + FF_CHUNK] = ((g * jax.nn.sigmoid(g)) * u).astype(BF16)
    o_ref[...] = x1 + jnp.dot(act_ref[...], wd_ref[...], preferred_element_type=F32)


def _ffn(x2d, attn, conv, w_out, gain, w_gate, w_up, w_down):
    n, d = x2d.shape
    tm = ROW_TILE
    d_ff = w_gate.shape[1]
    const = lambda i: (0, 0)
    row = lambda i: (i, 0)
    resident = functools.partial(pl.BlockSpec, index_map=const, pipeline_mode=pl.Buffered(1))
    return pl.pallas_call(
        _ffn_kernel,
        grid=(n // tm,),
        in_specs=[
            pl.BlockSpec((tm, d), row),
            pl.BlockSpec((tm, attn.shape[1]), row),
            pl.BlockSpec((tm, conv.shape[1]), row),
            resident(w_out.shape),
            pl.BlockSpec((1, d), const),
            resident(w_gate.shape),
            resident(w_up.shape),
            resident(w_down.shape),
        ],
        out_specs=pl.BlockSpec((tm, d), row),
        out_shape=jax.ShapeDtypeStruct((n, d), F32),
        scratch_shapes=[pltpu.VMEM((tm, d_ff), BF16)],
        compiler_params=pltpu.CompilerParams(
            dimension_semantics=("arbitrary",), vmem_limit_bytes=VMEM_LIMIT),
    )(x2d, attn, conv, w_out, gain, w_gate, w_up, w_down)


def kernel(x, norm_mix, w_in, q_norm, k_norm, conv_w, w_out, norm_ffn, w_gate, w_up, w_down):
    batch, seq, d = x.shape
    depth = w_in.shape[0]
    conv_dim = conv_w.shape[2]
    attn_dim = w_out.shape[1] - conv_dim
    n_heads = attn_dim // HEAD_DIM
    assert seq % ROW_TILE == 0 and seq % Q_TILE == 0 and attn_dim % MXU_DIM == 0
    assert w_gate.shape[2] % FF_CHUNK == 0 and conv_w.shape[1] == 3

    xf = x.reshape(batch * seq, d)
    for l in range(depth):
        q, k, v, conv = _inproj(
            xf, norm_mix[l][None, :], w_in[l].astype(BF16),
            jnp.tile(q_norm[l], n_heads)[None, :], jnp.tile(k_norm[l], n_heads)[None, :], conv_w[l],
            seq=seq, attn_dim=attn_dim, conv_dim=conv_dim)
        attn = _attention(q, k, v, batch=batch, seq=seq)
        xf = _ffn(xf, attn, conv, w_out[l].astype(BF16), norm_ffn[l][None, :],
                  w_gate[l].astype(BF16), w_up[l].astype(BF16), w_down[l].astype(BF16))
    return xf.reshape(batch, seq, d)
```

```python
import functools
import math

import jax
import jax.numpy as jnp
from jax import lax
from jax.experimental import pallas as pl
from jax.experimental.pallas import tpu as pltpu

F32 = jnp.float32
BF16 = jnp.bfloat16

HEAD_DIM = 64
EPS = 1e-6
LANES = 128
MXU_DIM = 256
HEADS_PER_BLOCK = LANES // HEAD_DIM
SUB = 128
LOG2E = 1.4426950408889634
LN2 = 0.6931471805599453
SKIP_ABOVE = 104.0

ROW_TILE = 512
Q_TILE = 256
K_TILE = 256
FF_CHUNK = 256
CONV_HALO = 8
VMEM_LIMIT = 56 * 1024 * 1024


def _rms_scale(x):
    return lax.rsqrt(jnp.mean(x * x, axis=-1, keepdims=True) + EPS)


def _inproj_kernel(x_ref, gain_ref, w_ref, qg_ref, kg_ref, cw_ref,
                   q_ref, k_ref, v_ref, c_ref, hbuf_ref, *, tiles_per_seq, attn_dim, conv_dim):
    tm = x_ref.shape[0]
    i = pl.program_id(0)
    x = x_ref[...]
    h = ((x * _rms_scale(x)) * gain_ref[...]).astype(BF16)

    def proj(c0, width):
        return jnp.dot(h, w_ref[:, c0:c0 + width], preferred_element_type=F32)

    r = lax.broadcasted_iota(jnp.int32, (MXU_DIM, MXU_DIM), 0) // HEAD_DIM
    c = lax.broadcasted_iota(jnp.int32, (MXU_DIM, MXU_DIM), 1) // HEAD_DIM
    seg = jnp.where(r == c, 1.0, 0.0).astype(BF16)

    def head_norm(p, gain):
        p2 = (p * p).astype(BF16)
        ssq = jnp.concatenate(
            [jnp.dot(p2[:, s:s + MXU_DIM], seg, preferred_element_type=F32)
             for s in range(0, attn_dim, MXU_DIM)], axis=1)
        return (p * lax.rsqrt(ssq * (1.0 / HEAD_DIM) + EPS)) * gain

    q_ref[...] = (head_norm(proj(0, attn_dim), qg_ref[...]) * (HEAD_DIM ** -0.5)).astype(BF16)
    k_ref[...] = head_norm(proj(attn_dim, attn_dim), kg_ref[...]).astype(BF16)
    v_ref[...] = proj(2 * attn_dim, attn_dim).astype(BF16)

    cb = proj(3 * attn_dim, conv_dim)
    hh = proj(3 * attn_dim + conv_dim, conv_dim) * proj(3 * attn_dim + 2 * conv_dim, conv_dim)

    @pl.when(i % tiles_per_seq == 0)
    def _():
        hbuf_ref[0:CONV_HALO, :] = jnp.zeros((CONV_HALO, conv_dim), F32)

    hbuf_ref[CONV_HALO:CONV_HALO + tm, :] = hh
    h1 = hbuf_ref[CONV_HALO - 1:CONV_HALO - 1 + tm, :]
    h2 = hbuf_ref[CONV_HALO - 2:CONV_HALO - 2 + tm, :]
    y = cw_ref[0:1, :] * h2 + cw_ref[1:2, :] * h1 + cw_ref[2:3, :] * hh
    c_ref[...] = (cb * y).astype(BF16)
    hbuf_ref[0:CONV_HALO, :] = hh[tm - CONV_HALO:tm, :]


def _inproj(x2d, gain, w_bf16, q_gain, k_gain, conv_w, *, seq, attn_dim, conv_dim):
    n, d = x2d.shape
    tm = ROW_TILE
    cols = w_bf16.shape[1]
    const = lambda i: (0, 0)
    row = lambda i: (i, 0)
    out_shapes = [jax.ShapeDtypeStruct((n, attn_dim), BF16)] * 3 + [jax.ShapeDtypeStruct((n, conv_dim), BF16)]
    return pl.pallas_call(
        functools.partial(_inproj_kernel, tiles_per_seq=seq // tm, attn_dim=attn_dim, conv_dim=conv_dim),
        grid=(n // tm,),
        in_specs=[
            pl.BlockSpec((tm, d), row),
            pl.BlockSpec((1, d), const),
            pl.BlockSpec((d, cols), const, pipeline_mode=pl.Buffered(1)),
            pl.BlockSpec((1, attn_dim), const),
            pl.BlockSpec((1, attn_dim), const),
            pl.BlockSpec(conv_w.shape, const),
        ],
        out_specs=[pl.BlockSpec((tm, attn_dim), row)] * 3 + [pl.BlockSpec((tm, conv_dim), row)],
        out_shape=out_shapes,
        scratch_shapes=[pltpu.VMEM((CONV_HALO + tm, conv_dim), F32)],
        compiler_params=pltpu.CompilerParams(
            dimension_semantics=("arbitrary",), vmem_limit_bytes=VMEM_LIMIT),
    )(x2d, gain, w_bf16, q_gain, k_gain, conv_w)


def _attn_kernel(q_ref, k_ref, v_ref, o_ref, acc_ref, carry_ref):
    tq = q_ref.shape[0]
    tk = K_TILE
    i = pl.program_id(2)

    q = q_ref[...]
    qlane = lax.broadcasted_iota(jnp.int32, (tq, LANES), 1)
    q_heads = [jnp.where((qlane // HEAD_DIM) == h, q, jnp.zeros_like(q)) for h in range(HEADS_PER_BLOCK)]
    vlane = lax.broadcasted_iota(jnp.int32, (tk, LANES), 1)

    r = lax.broadcasted_iota(jnp.int32, (2 * SUB, 2 * SUB), 0) % SUB
    c = lax.broadcasted_iota(jnp.int32, (2 * SUB, 2 * SUB), 1)
    tri = jnp.where((c >= SUB) | (r >= c), 1.0, 0.0).astype(BF16)

    qpos = lax.broadcasted_iota(jnp.int32, (tq, tk), 0)
    kpos = lax.broadcasted_iota(jnp.int32, (tq, tk), 1)
    causal = kpos < qpos

    acc_ref[...] = jnp.zeros_like(acc_ref)
    carry_ref[...] = jnp.zeros_like(carry_ref)

    def process(j, diag):
        start = pl.multiple_of(j * tk, tk)
        k_blk = k_ref[pl.ds(start, tk), :]
        v_blk = v_ref[pl.ds(start, tk), :]
        a_parts = []
        min_carry = None
        for h in range(HEADS_PER_BLOCK):
            z = lax.dot_general(q_heads[h], k_blk, (((1,), (1,)), ((), ())),
                                preferred_element_type=F32)
            t = jnp.exp2(jnp.abs(z) * (-LOG2E))
            sp = jnp.maximum(z, 0.0) + jnp.log2(1.0 + t) * LN2
            if diag:
                sp = jnp.where(causal, sp, 0.0)
            hi = sp.astype(BF16)
            lo = (sp - hi.astype(F32)).astype(BF16)
            carry = carry_ref[h]
            a_cols = [None] * (tk // SUB)
            for s in reversed(range(tk // SUB)):
                sl = slice(s * SUB, (s + 1) * SUB)
                cs = jnp.dot(jnp.concatenate([hi[:, sl], lo[:, sl]], axis=1), tri,
                             preferred_element_type=F32)
                g = cs[:, :SUB] + carry
                carry = carry + cs[:, SUB:]
                a = jnp.exp2((z[:, sl] - g) * LOG2E)
                if diag:
                    a = jnp.where(causal[:, sl], a, 0.0)
                a_cols[s] = a.astype(BF16)
            carry_ref[h] = carry
            m = jnp.min(carry)
            min_carry = m if min_carry is None else jnp.minimum(min_carry, m)
            a_parts.append(jnp.concatenate(a_cols, axis=1))
        a_cat = jnp.concatenate(a_parts, axis=1)
        v_cat = jnp.concatenate(
            [jnp.where((vlane // HEAD_DIM) == h, v_blk, jnp.zeros_like(v_blk))
             for h in range(HEADS_PER_BLOCK)], axis=0)
        acc_ref[...] += jnp.dot(a_cat, v_cat, preferred_element_type=F32)
        return min_carry

    def cond(state):
        j, min_carry = state
        return jnp.logical_and(j >= 0, min_carry < SKIP_ABOVE)

    def body(state):
        j, _ = state
        return j - 1, process(j, False)

    lax.while_loop(cond, body, (i - 1, process(i, True)))
    o_ref[...] = acc_ref[...].astype(o_ref.dtype)


def _attention(q, k, v, *, batch, seq):
    n, attn_dim = q.shape
    tq = Q_TILE
    assert tq == K_TILE
    nq = seq // tq
    blocks = attn_dim // LANES
    return pl.pallas_call(
        _attn_kernel,
        grid=(batch, blocks, nq),
        in_specs=[
            pl.BlockSpec((tq, LANES), lambda b, p, i: (b * nq + i, p)),
            pl.BlockSpec((seq, LANES), lambda b, p, i: (b, p)),
            pl.BlockSpec((seq, LANES), lambda b, p, i: (b, p)),
        ],
        out_specs=pl.BlockSpec((tq, LANES), lambda b, p, i: (b * nq + i, p)),
        out_shape=jax.ShapeDtypeStruct((n, attn_dim), BF16),
        scratch_shapes=[pltpu.VMEM((tq, LANES), F32),
                        pltpu.VMEM((HEADS_PER_BLOCK, tq, SUB), F32)],
        compiler_params=pltpu.CompilerParams(
            dimension_semantics=("arbitrary", "arbitrary", "arbitrary"), vmem_limit_bytes=VMEM_LIMIT),
    )(q, k, v)


def _ffn_kernel(x_ref, attn_ref, conv_ref, wo_ref, gain_ref, wg_ref, wu_ref, wd_ref, o_ref, act_ref):
    attn_dim = attn_ref.shape[1]
    d_ff = wg_ref.shape[1]
    mix = (jnp.dot(attn_ref[...], wo_ref[0:attn_dim, :], preferred_element_type=F32)
           + jnp.dot(conv_ref[...], wo_ref[attn_dim:, :], preferred_element_type=F32))
    x1 = x_ref[...] + mix
    h = ((x1 * _rms_scale(x1)) * gain_ref[...]).astype(BF16)
    for c0 in range(0, d_ff, FF_CHUNK):
        g = jnp.dot(h, wg_ref[:, c0:c0 + FF_CHUNK], preferred_element_type=F32)
        u = jnp.dot(h, wu_ref[:, c0:c0 + FF_CHUNK], preferred_element_type=F32)
        act_ref[:, c0:c0 + FF_CHUNK] = ((g * jax.nn.sigmoid(g)) * u).astype(BF16)
    o_ref[...] = x1 + jnp.dot(act_ref[...], wd_ref[...], preferred_element_type=F32)


def _ffn(x2d, attn, conv, w_out, gain, w_gate, w_up, w_down):
    n, d = x2d.shape
    tm = ROW_TILE
    d_ff = w_gate.shape[1]
    const = lambda i: (0, 0)
    row = lambda i: (i, 0)
    resident = functools.partial(pl.BlockSpec, index_map=const, pipeline_mode=pl.Buffered(1))
    return pl.pallas_call(
        _ffn_kernel,
        grid=(n // tm,),
        in_specs=[
            pl.BlockSpec((tm, d), row),
            pl.BlockSpec((tm, attn.shape[1]), row),
            pl.BlockSpec((tm, conv.shape[1]), row),
            resident(w_out.shape),
            pl.BlockSpec((1, d), const),
            resident(w_gate.shape),
            resident(w_up.shape),
            resident(w_down.shape),
        ],
        out_specs=pl.BlockSpec((tm, d), row),
        out_shape=jax.ShapeDtypeStruct((n, d), F32),
        scratch_shapes=[pltpu.VMEM((tm, d_ff), BF16)],
        compiler_params=pltpu.CompilerParams(
            dimension_semantics=("arbitrary",), vmem_limit_bytes=VMEM_LIMIT),
    )(x2d, attn, conv, w_out, gain, w_gate, w_up, w_down)


def kernel(x, norm_mix, w_in, q_norm, k_norm, conv_w, w_out, norm_ffn, w_gate, w_up, w_down):
    batch, seq, d = x.shape
    depth = w_in.shape[0]
    conv_dim = conv_w.shape[2]
    attn_dim = w_out.shape[1] - conv_dim
    n_heads = attn_dim // HEAD_DIM
    assert seq % ROW_TILE == 0 and seq % Q_TILE == 0 and attn_dim % MXU_DIM == 0
    assert w_gate.shape[2] % FF_CHUNK == 0 and conv_w.shape[1] == 3

    xf = x.reshape(batch * seq, d)
    for l in range(depth):
        q, k, v, conv = _inproj(
            xf, norm_mix[l][None, :], w_in[l].astype(BF16),
            jnp.tile(q_norm[l], n_heads)[None, :], jnp.tile(k_norm[l], n_heads)[None, :], conv_w[l],
            seq=seq, attn_dim=attn_dim, conv_dim=conv_dim)
        attn = _attention(q, k, v, batch=batch, seq=seq)
        xf = _ffn(xf, attn, conv, w_out[l].astype(BF16), norm_ffn[l][None, :],
                  w_gate[l].astype(BF16), w_up[l].astype(BF16), w_down[l].astype(BF16))
    return xf.reshape(batch, seq, d)
```

```python
import functools

import jax
import jax.numpy as jnp
from jax import lax
from jax.experimental import pallas as pl
from jax.experimental.pallas import tpu as pltpu

F32 = jnp.float32
BF16 = jnp.bfloat16

HEAD_DIM = 64
EPS = 1e-6
LANES = 128
MXU_DIM = 256
HEADS_PER_BLOCK = LANES // HEAD_DIM
SUB = 128
LOG2E = 1.4426950408889634
LN2 = 0.6931471805599453
SKIP_ABOVE = 104.0

ROW_TILE = 512
LOOKBACK_SUBS = 2
WALK_SUBS = 2
FF_CHUNK = 256
CONV_HALO = 8
VMEM_LIMIT = 56 * 1024 * 1024


def _rms_scale(x):
    return lax.rsqrt(jnp.mean(x * x, axis=-1, keepdims=True) + EPS)


def _inproj_kernel(x_ref, gain_ref, w_ref, qg_ref, kg_ref, cw_ref,
                   q_ref, k_ref, v_ref, c_ref, hbuf_ref, *, tiles_per_seq, attn_dim, conv_dim):
    tm = x_ref.shape[0]
    i = pl.program_id(0)
    x = x_ref[...]
    h = ((x * _rms_scale(x)) * gain_ref[...]).astype(BF16)

    def proj(c0, width):
        return jnp.dot(h, w_ref[:, c0:c0 + width], preferred_element_type=F32)

    r = lax.broadcasted_iota(jnp.int32, (MXU_DIM, MXU_DIM), 0) // HEAD_DIM
    c = lax.broadcasted_iota(jnp.int32, (MXU_DIM, MXU_DIM), 1) // HEAD_DIM
    seg = jnp.where(r == c, 1.0, 0.0).astype(BF16)

    def head_norm(p, gain):
        p2 = (p * p).astype(BF16)
        ssq = jnp.concatenate(
            [jnp.dot(p2[:, s:s + MXU_DIM], seg, preferred_element_type=F32)
             for s in range(0, attn_dim, MXU_DIM)], axis=1)
        return (p * lax.rsqrt(ssq * (1.0 / HEAD_DIM) + EPS)) * gain

    q_ref[...] = (head_norm(proj(0, attn_dim), qg_ref[...]) * (HEAD_DIM ** -0.5)).astype(BF16)
    k_ref[...] = head_norm(proj(attn_dim, attn_dim), kg_ref[...]).astype(BF16)
    v_ref[...] = proj(2 * attn_dim, attn_dim).astype(BF16)

    cb = proj(3 * attn_dim, conv_dim)
    hh = proj(3 * attn_dim + conv_dim, conv_dim) * proj(3 * attn_dim + 2 * conv_dim, conv_dim)

    @pl.when(i % tiles_per_seq == 0)
    def _():
        hbuf_ref[0:CONV_HALO, :] = jnp.zeros((CONV_HALO, conv_dim), F32)

    hbuf_ref[CONV_HALO:CONV_HALO + tm, :] = hh
    h1 = hbuf_ref[CONV_HALO - 1:CONV_HALO - 1 + tm, :]
    h2 = hbuf_ref[CONV_HALO - 2:CONV_HALO - 2 + tm, :]
    y = cw_ref[0:1, :] * h2 + cw_ref[1:2, :] * h1 + cw_ref[2:3, :] * hh
    c_ref[...] = (cb * y).astype(BF16)
    hbuf_ref[0:CONV_HALO, :] = hh[tm - CONV_HALO:tm, :]


def _inproj(x2d, gain, w_bf16, q_gain, k_gain, conv_w, *, seq, attn_dim, conv_dim):
    n, d = x2d.shape
    tm = ROW_TILE
    cols = w_bf16.shape[1]
    const = lambda i: (0, 0)
    row = lambda i: (i, 0)
    out_shapes = [jax.ShapeDtypeStruct((n, attn_dim), BF16)] * 3 + [jax.ShapeDtypeStruct((n, conv_dim), BF16)]
    return pl.pallas_call(
        functools.partial(_inproj_kernel, tiles_per_seq=seq // tm, attn_dim=attn_dim, conv_dim=conv_dim),
        grid=(n // tm,),
        in_specs=[
            pl.BlockSpec((tm, d), row),
            pl.BlockSpec((1, d), const),
            pl.BlockSpec((d, cols), const, pipeline_mode=pl.Buffered(1)),
            pl.BlockSpec((1, attn_dim), const),
            pl.BlockSpec((1, attn_dim), const),
            pl.BlockSpec(conv_w.shape, const),
        ],
        out_specs=[pl.BlockSpec((tm, attn_dim), row)] * 3 + [pl.BlockSpec((tm, conv_dim), row)],
        out_shape=out_shapes,
        scratch_shapes=[pltpu.VMEM((CONV_HALO + tm, conv_dim), F32)],
        compiler_params=pltpu.CompilerParams(
            dimension_semantics=("arbitrary",), vmem_limit_bytes=VMEM_LIMIT),
    )(x2d, gain, w_bf16, q_gain, k_gain, conv_w)


def _attn_kernel(q_ref, k_ref, v_ref, o_ref, acc_ref, carry_ref):
    tq = q_ref.shape[0]
    n_blocks = q_ref.shape[1] // LANES
    i = pl.program_id(1)
    assert tq == SUB
    rows = HEADS_PER_BLOCK * tq

    lane = lax.broadcasted_iota(jnp.int32, (tq, LANES), 1)
    r = lax.broadcasted_iota(jnp.int32, (2 * SUB, 2 * SUB), 0) % SUB
    c = lax.broadcasted_iota(jnp.int32, (2 * SUB, 2 * SUB), 1)
    tri = jnp.where((c >= SUB) | (r >= c), 1.0, 0.0).astype(BF16)
    causal = (lax.broadcasted_iota(jnp.int32, (rows, SUB), 1)
              < lax.broadcasted_iota(jnp.int32, (rows, SUB), 0) % tq)

    def span(start, nsub, diag, first):
        width = nsub * SUB
        order = list(reversed(range(nsub)))
        z_all, lhs_parts = [], []
        for p in range(n_blocks):
            cols = slice(p * LANES, (p + 1) * LANES)
            q = q_ref[:, cols]
            q2 = jnp.concatenate([jnp.where((lane // HEAD_DIM) == h, q, jnp.zeros_like(q))
                                  for h in range(HEADS_PER_BLOCK)], axis=0)
            z = lax.dot_general(q2, k_ref[pl.ds(start, width), cols], (((1,), (1,)), ((), ())),
                                preferred_element_type=F32)
            z_all.append(z)
            for s in order:
                z_s = z[:, s * SUB:(s + 1) * SUB]
                t = jnp.exp2(jnp.abs(z_s) * (-LOG2E))
                sp = jnp.maximum(z_s, 0.0) + jnp.log(1.0 + t)
                if diag and s == nsub - 1:
                    sp = jnp.where(causal, sp, 0.0)
                hi = sp.astype(BF16)
                lo = (sp - hi.astype(F32)).astype(BF16)
                lhs_parts.append(jnp.concatenate([hi, lo], axis=1))
        cs_all = jnp.dot(jnp.concatenate(lhs_parts, axis=0), tri, preferred_element_type=F32)
        for p in range(n_blocks):
            cols = slice(p * LANES, (p + 1) * LANES)
            carry = None if first else carry_ref[p]
            a_cols = [None] * nsub
            for idx, s in enumerate(order):
                cs = cs_all[(p * nsub + idx) * rows:(p * nsub + idx + 1) * rows]
                g = cs[:, :SUB] if carry is None else cs[:, :SUB] + carry
                carry = cs[:, SUB:] if carry is None else carry + cs[:, SUB:]
                a = jnp.exp2((z_all[p][:, s * SUB:(s + 1) * SUB] - g) * LOG2E)
                if diag and s == nsub - 1:
                    a = jnp.where(causal, a, 0.0)
                a_cols[s] = a.astype(BF16)
            carry_ref[p] = carry
            o2 = jnp.dot(jnp.concatenate(a_cols, axis=1), v_ref[pl.ds(start, width), cols],
                         preferred_element_type=F32)
            out = o2[0:tq]
            for h in range(1, HEADS_PER_BLOCK):
                out = jnp.where((lane // HEAD_DIM) == h, o2[h * tq:(h + 1) * tq], out)
            if first:
                acc_ref[:, cols] = out
            else:
                acc_ref[:, cols] += out

    for n_back in range(LOOKBACK_SUBS + 1):
        is_case = (i == n_back) if n_back < LOOKBACK_SUBS else (i >= n_back)

        @pl.when(is_case)
        def _(n_back=n_back):
            span(pl.multiple_of((i - n_back) * SUB, SUB), n_back + 1, True, True)

    def unfinished():
        return (jnp.min(carry_ref[...]) < SKIP_ABOVE).astype(jnp.int32)

    def cond(state):
        rem, go = state
        return jnp.logical_and(rem >= WALK_SUBS * SUB, go > 0)

    def body(state):
        rem, _ = state
        start = pl.multiple_of(rem - WALK_SUBS * SUB, SUB)
        span(start, WALK_SUBS, False, False)
        return start, unfinished()

    rem, go = lax.while_loop(cond, body, (jnp.maximum(i - LOOKBACK_SUBS, 0) * SUB, unfinished()))
    for tail in range(1, WALK_SUBS):
        @pl.when(jnp.logical_and(rem == tail * SUB, go > 0))
        def _(tail=tail):
            span(0, tail, False, False)

    o_ref[...] = acc_ref[...].astype(o_ref.dtype)


def _attention(q, k, v, *, batch, seq):
    n, attn_dim = q.shape
    tq = SUB
    nq = seq // tq
    return pl.pallas_call(
        _attn_kernel,
        grid=(batch, nq),
        in_specs=[
            pl.BlockSpec((tq, attn_dim), lambda b, i: (b * nq + i, 0)),
            pl.BlockSpec((seq, attn_dim), lambda b, i: (b, 0), pipeline_mode=pl.Buffered(1)),
            pl.BlockSpec((seq, attn_dim), lambda b, i: (b, 0), pipeline_mode=pl.Buffered(1)),
        ],
        out_specs=pl.BlockSpec((tq, attn_dim), lambda b, i: (b * nq + i, 0)),
        out_shape=jax.ShapeDtypeStruct((n, attn_dim), BF16),
        scratch_shapes=[pltpu.VMEM((tq, attn_dim), F32),
                        pltpu.VMEM((attn_dim // LANES, HEADS_PER_BLOCK * tq, SUB), F32)],
        compiler_params=pltpu.CompilerParams(
            dimension_semantics=("arbitrary", "arbitrary"), vmem_limit_bytes=VMEM_LIMIT),
    )(q, k, v)


def _ffn_kernel(x_ref, attn_ref, conv_ref, wo_ref, gain_ref, wg_ref, wu_ref, wd_ref, o_ref, act_ref):
    attn_dim = attn_ref.shape[1]
    d_ff = wg_ref.shape[1]
    mix = (jnp.dot(attn_ref[...], wo_ref[0:attn_dim, :], preferred_element_type=F32)
           + jnp.dot(conv_ref[...], wo_ref[attn_dim:, :], preferred_element_type=F32))
    x1 = x_ref[...] + mix
    h = ((x1 * _rms_scale(x1)) * gain_ref[...]).astype(BF16)
    for c0 in range(0, d_ff, FF_CHUNK):
        g = jnp.dot(h, wg_ref[:, c0:c0 + FF_CHUNK], preferred_element_type=F32)
        u = jnp.dot(h, wu_ref[:, c0:c0 + FF_CHUNK], preferred_element_type=F32)
        act_ref[:, c0:c0 + FF_CHUNK] = ((g * jax.nn.sigmoid(g)) * u).astype(BF16)
    o_ref[...] = x1 + jnp.dot(act_ref[...], wd_ref[...], preferred_element_type=F32)


def _ffn(x2d, attn, conv, w_out, gain, w_gate, w_up, w_down):
    n, d = x2d.shape
    tm = ROW_TILE
    d_ff = w_gate.shape[1]
    const = lambda i: (0, 0)
    row = lambda i: (i, 0)
    resident = functools.partial(pl.BlockSpec, index_map=const, pipeline_mode=pl.Buffered(1))
    return pl.pallas_call(
        _ffn_kernel,
        grid=(n // tm,),
        in_specs=[
            pl.BlockSpec((tm, d), row),
            pl.BlockSpec((tm, attn.shape[1]), row),
            pl.BlockSpec((tm, conv.shape[1]), row),
            resident(w_out.shape),
            pl.BlockSpec((1, d), const),
            resident(w_gate.shape),
            resident(w_up.shape),
            resident(w_down.shape),
        ],
        out_specs=pl.BlockSpec((tm, d), row),
        out_shape=jax.ShapeDtypeStruct((n, d), F32),
        scratch_shapes=[pltpu.VMEM((tm, d_ff), BF16)],
        compiler_params=pltpu.CompilerParams(
            dimension_semantics=("arbitrary",), vmem_limit_bytes=VMEM_LIMIT),
    )(x2d, attn, conv, w_out, gain, w_gate, w_up, w_down)


def kernel(x, norm_mix, w_in, q_norm, k_norm, conv_w, w_out, norm_ffn, w_gate, w_up, w_down):
    batch, seq, d = x.shape
    depth = w_in.shape[0]
    conv_dim = conv_w.shape[2]
    attn_dim = w_out.shape[1] - conv_dim
    n_heads = attn_dim // HEAD_DIM
    assert seq % ROW_TILE == 0 and seq % SUB == 0 and attn_dim % MXU_DIM == 0
    assert w_gate.shape[2] % FF_CHUNK == 0 and conv_w.shape[1] == 3

    xf = x.reshape(batch * seq, d)
    for l in range(depth):
        q, k, v, conv = _inproj(
            xf, norm_mix[l][None, :], w_in[l].astype(BF16),
            jnp.tile(q_norm[l], n_heads)[None, :], jnp.tile(k_norm[l], n_heads)[None, :], conv_w[l],
            seq=seq, attn_dim=attn_dim, conv_dim=conv_dim)
        attn = _attention(q, k, v, batch=batch, seq=seq)
        xf = _ffn(xf, attn, conv, w_out[l].astype(BF16), norm_ffn[l][None, :],
                  w_gate[l].astype(BF16), w_up[l].astype(BF16), w_down[l].astype(BF16))
    return xf.reshape(batch, seq, d)
```

```python
import functools

import jax
import jax.numpy as jnp
from jax import lax
from jax.experimental import pallas as pl
from jax.experimental.pallas import tpu as pltpu

F32 = jnp.float32
BF16 = jnp.bfloat16

HEAD_DIM = 64
EPS = 1e-6
LANES = 128
SUBLANES = 8
BF16_SUBLANES = 16
W_CAST_ROWS = 128
MXU_DIM = 256
HEADS_PER_BLOCK = LANES // HEAD_DIM
SUB = 128
LOG2E = 1.4426950408889634
LN2 = 0.6931471805599453
SKIP_ABOVE = 104.0

ROW_TILE = 512
LOOKBACK_SUBS = 2
WALK_SUBS = 2
FF_CHUNK = 256
CONV_HALO = 8
VMEM_LIMIT = 56 * 1024 * 1024


def _rms_scale(x):
    return lax.rsqrt(jnp.mean(x * x, axis=-1, keepdims=True) + EPS)


def _inproj_kernel(x_ref, gain_ref, w_ref, qg_ref, kg_ref, cw_ref, *rest,
                   tiles_per_seq, attn_dim, conv_dim, n_cast):
    cast_in, (q_ref, k_ref, v_ref, c_ref) = rest[:n_cast], rest[n_cast:n_cast + 4]
    cast_out, (hbuf_ref, wbf_ref) = rest[n_cast + 4:2 * n_cast + 4], rest[2 * n_cast + 4:]
    tm = x_ref.shape[0]
    i = pl.program_id(0)

    @pl.when(i == 0)
    def _():
        def cast_rows(c, _):
            rows = pl.ds(pl.multiple_of(c * W_CAST_ROWS, W_CAST_ROWS), W_CAST_ROWS)
            wbf_ref[rows, :] = w_ref[rows, :].astype(BF16)
            return 0
        lax.fori_loop(0, w_ref.shape[0] // W_CAST_ROWS, cast_rows, 0)
        hbuf_ref[0:CONV_HALO, :] = jnp.zeros((CONV_HALO, conv_dim), F32)

    for src, dst in zip(cast_in, cast_out):
        dst[...] = src[...].astype(BF16)

    x = x_ref[...]
    h = ((x * _rms_scale(x)) * gain_ref[...]).astype(BF16)

    def proj(c0, width):
        return jnp.dot(h, wbf_ref[:, c0:c0 + width], preferred_element_type=F32)

    cb = proj(3 * attn_dim, conv_dim)
    hh = proj(3 * attn_dim + conv_dim, conv_dim) * proj(3 * attn_dim + 2 * conv_dim, conv_dim)
    hbuf_ref[0:CONV_HALO, :] = jnp.where(i % tiles_per_seq == 0, 0.0, hbuf_ref[0:CONV_HALO, :])
    hbuf_ref[CONV_HALO:CONV_HALO + tm, :] = hh
    h1 = hbuf_ref[CONV_HALO - 1:CONV_HALO - 1 + tm, :]
    h2 = hbuf_ref[CONV_HALO - 2:CONV_HALO - 2 + tm, :]
    y = cw_ref[0:1, :] * h2 + cw_ref[1:2, :] * h1 + cw_ref[2:3, :] * hh
    c_ref[...] = (cb * y).astype(BF16)
    hbuf_ref[0:CONV_HALO, :] = hh[tm - CONV_HALO:tm, :]

    r = lax.broadcasted_iota(jnp.int32, (MXU_DIM, MXU_DIM), 0) // HEAD_DIM
    c = lax.broadcasted_iota(jnp.int32, (MXU_DIM, MXU_DIM), 1) // HEAD_DIM
    seg = jnp.where(r == c, 1.0, 0.0).astype(BF16)

    def head_norm(p, gain):
        p2 = (p * p).astype(BF16)
        ssq = jnp.concatenate(
            [jnp.dot(p2[:, s:s + MXU_DIM], seg, preferred_element_type=F32)
             for s in range(0, attn_dim, MXU_DIM)], axis=1)
        return (p * lax.rsqrt(ssq * (1.0 / HEAD_DIM) + EPS)) * gain

    q_ref[...] = (head_norm(proj(0, attn_dim), qg_ref[...]) * (HEAD_DIM ** -0.5)).astype(BF16)
    k_ref[...] = head_norm(proj(attn_dim, attn_dim), kg_ref[...]).astype(BF16)
    v_ref[...] = proj(2 * attn_dim, attn_dim).astype(BF16)


def _cast_chunks(rows, n_steps):
    return max(c for c in range(1, n_steps + 1) if rows % c == 0 and (rows // c) % BF16_SUBLANES == 0)


def _inproj(x2d, gain, w_in, layer, q_gain, k_gain, conv_w, later_weights, *, seq, attn_dim, conv_dim):
    n, d = x2d.shape
    tm = ROW_TILE
    n_steps = n // tm
    cols = w_in.shape[2]
    const = lambda i: (0, 0)
    row = lambda i: (i, 0)
    cast_in_specs, cast_out_specs, cast_shapes = [], [], []
    for w in later_weights:
        _, rows, wcols = w.shape
        chunks = _cast_chunks(rows, n_steps)
        cast_in_specs.append(pl.BlockSpec(
            (None, rows // chunks, wcols), lambda i, chunks=chunks: (layer, jnp.minimum(i, chunks - 1), 0)))
        cast_out_specs.append(pl.BlockSpec(
            (rows // chunks, wcols), lambda i, chunks=chunks: (jnp.minimum(i, chunks - 1), 0)))
        cast_shapes.append(jax.ShapeDtypeStruct((rows, wcols), BF16))
    out_shapes = [jax.ShapeDtypeStruct((n, attn_dim), BF16)] * 3 + [jax.ShapeDtypeStruct((n, conv_dim), BF16)]
    return pl.pallas_call(
        functools.partial(_inproj_kernel, tiles_per_seq=seq // tm, attn_dim=attn_dim, conv_dim=conv_dim,
                          n_cast=len(later_weights)),
        grid=(n_steps,),
        in_specs=[
            pl.BlockSpec((tm, d), row),
            pl.BlockSpec((1, d), const),
            pl.BlockSpec((None, d, cols), lambda i: (layer, 0, 0), pipeline_mode=pl.Buffered(1)),
            pl.BlockSpec((1, attn_dim), const),
            pl.BlockSpec((1, attn_dim), const),
            pl.BlockSpec(conv_w.shape, const),
        ] + cast_in_specs,
        out_specs=[pl.BlockSpec((tm, attn_dim), row)] * 3 + [pl.BlockSpec((tm, conv_dim), row)] + cast_out_specs,
        out_shape=out_shapes + cast_shapes,
        scratch_shapes=[pltpu.VMEM((CONV_HALO + tm, conv_dim), F32), pltpu.VMEM((d, cols), BF16)],
        compiler_params=pltpu.CompilerParams(
            dimension_semantics=("arbitrary",), vmem_limit_bytes=VMEM_LIMIT),
    )(x2d, gain, w_in, q_gain, k_gain, conv_w, *later_weights)


def _attn_kernel(q_ref, k_ref, v_ref, o_ref, acc_ref, carry_ref, min_ref):
    tq = q_ref.shape[0]
    n_blocks = q_ref.shape[1] // LANES
    i = pl.program_id(1)
    assert tq == SUB
    rows = HEADS_PER_BLOCK * tq

    lane = lax.broadcasted_iota(jnp.int32, (tq, LANES), 1)
    r = lax.broadcasted_iota(jnp.int32, (2 * SUB, 2 * SUB), 0) % SUB
    c = lax.broadcasted_iota(jnp.int32, (2 * SUB, 2 * SUB), 1)
    tri = jnp.where((c >= SUB) | (r >= c), 1.0, 0.0).astype(BF16)
    causal = (lax.broadcasted_iota(jnp.int32, (rows, SUB), 1)
              < lax.broadcasted_iota(jnp.int32, (rows, SUB), 0) % tq)

    def span(start, nsub, diag, first):
        width = nsub * SUB
        order = list(reversed(range(nsub)))
        z_all, lhs_parts = [], []
        for p in range(n_blocks):
            cols = slice(p * LANES, (p + 1) * LANES)
            q = q_ref[:, cols]
            q2 = jnp.concatenate([jnp.where((lane // HEAD_DIM) == h, q, jnp.zeros_like(q))
                                  for h in range(HEADS_PER_BLOCK)], axis=0)
            z = lax.dot_general(q2, k_ref[pl.ds(start, width), cols], (((1,), (1,)), ((), ())),
                                preferred_element_type=F32)
            z_all.append(z)
            for s in order:
                z_s = z[:, s * SUB:(s + 1) * SUB]
                t = jnp.exp2(jnp.abs(z_s) * (-LOG2E))
                sp = jnp.maximum(z_s, 0.0) + jnp.log(1.0 + t)
                if diag and s == nsub - 1:
                    sp = jnp.where(causal, sp, 0.0)
                hi = sp.astype(BF16)
                lo = (sp - hi.astype(F32)).astype(BF16)
                lhs_parts.append(jnp.concatenate([hi, lo], axis=1))
        cs_all = jnp.dot(jnp.concatenate(lhs_parts, axis=0), tri, preferred_element_type=F32)
        carries = []
        for p in range(n_blocks):
            cols = slice(p * LANES, (p + 1) * LANES)
            carry = None if first else carry_ref[p]
            a_cols = [None] * nsub
            for idx, s in enumerate(order):
                cs = cs_all[(p * nsub + idx) * rows:(p * nsub + idx + 1) * rows]
                g = cs[:, :SUB] if carry is None else cs[:, :SUB] + carry
                carry = cs[:, SUB:] if carry is None else carry + cs[:, SUB:]
                a = jnp.exp2((z_all[p][:, s * SUB:(s + 1) * SUB] - g) * LOG2E)
                if diag and s == nsub - 1:
                    a = jnp.where(causal, a, 0.0)
                a_cols[s] = a.astype(BF16)
            carry_ref[p] = carry
            carries.append(carry)
            o2 = jnp.dot(jnp.concatenate(a_cols, axis=1), v_ref[pl.ds(start, width), cols],
                         preferred_element_type=F32)
            out = o2[0:tq]
            for h in range(1, HEADS_PER_BLOCK):
                out = jnp.where((lane // HEAD_DIM) == h, o2[h * tq:(h + 1) * tq], out)
            if first:
                acc_ref[:, cols] = out
            else:
                acc_ref[:, cols] += out
        while len(carries) > 1:
            odd = carries[-1:] if len(carries) % 2 else []
            carries = [jnp.minimum(a, b) for a, b in zip(carries[0::2], carries[1::2])] + odd
        m = carries[0]
        while m.shape[0] > SUBLANES:
            half = m.shape[0] // 2
            m = jnp.minimum(m[:half], m[half:])
        min_ref[...] = m

    for n_back in range(LOOKBACK_SUBS + 1):
        is_case = (i == n_back) if n_back < LOOKBACK_SUBS else (i >= n_back)

        @pl.when(is_case)
        def _(n_back=n_back):
            span(pl.multiple_of((i - n_back) * SUB, SUB), n_back + 1, True, True)

    def unfinished():
        return (jnp.min(min_ref[...]) < SKIP_ABOVE).astype(jnp.int32)

    def cond(state):
        rem, go = state
        return jnp.logical_and(rem >= WALK_SUBS * SUB, go > 0)

    def body(state):
        rem, _ = state
        start = pl.multiple_of(rem - WALK_SUBS * SUB, SUB)
        span(start, WALK_SUBS, False, False)
        return start, unfinished()

    rem, go = lax.while_loop(cond, body, (jnp.maximum(i - LOOKBACK_SUBS, 0) * SUB, unfinished()))
    for tail in range(1, WALK_SUBS):
        @pl.when(jnp.logical_and(rem == tail * SUB, go > 0))
        def _(tail=tail):
            span(0, tail, False, False)

    o_ref[...] = acc_ref[...].astype(o_ref.dtype)


def _attention(q, k, v, *, batch, seq):
    n, attn_dim = q.shape
    tq = SUB
    nq = seq // tq
    return pl.pallas_call(
        _attn_kernel,
        grid=(batch, nq),
        in_specs=[
            pl.BlockSpec((tq, attn_dim), lambda b, i: (b * nq + i, 0)),
            pl.BlockSpec((seq, attn_dim), lambda b, i: (b, 0), pipeline_mode=pl.Buffered(1)),
            pl.BlockSpec((seq, attn_dim), lambda b, i: (b, 0), pipeline_mode=pl.Buffered(1)),
        ],
        out_specs=pl.BlockSpec((tq, attn_dim), lambda b, i: (b * nq + i, 0)),
        out_shape=jax.ShapeDtypeStruct((n, attn_dim), BF16),
        scratch_shapes=[pltpu.VMEM((tq, attn_dim), F32),
                        pltpu.VMEM((attn_dim // LANES, HEADS_PER_BLOCK * tq, SUB), F32),
                        pltpu.VMEM((SUBLANES, SUB), F32)],
        compiler_params=pltpu.CompilerParams(
            dimension_semantics=("arbitrary", "arbitrary"), vmem_limit_bytes=VMEM_LIMIT),
    )(q, k, v)


def _ffn_kernel(x_ref, attn_ref, conv_ref, wo_ref, gain_ref, wg_ref, wu_ref, wd_ref, o_ref, act_ref):
    attn_dim = attn_ref.shape[1]
    d_ff = wg_ref.shape[1]
    mix = (jnp.dot(attn_ref[...], wo_ref[0:attn_dim, :], preferred_element_type=F32)
           + jnp.dot(conv_ref[...], wo_ref[attn_dim:, :], preferred_element_type=F32))
    x1 = x_ref[...] + mix
    h = ((x1 * _rms_scale(x1)) * gain_ref[...]).astype(BF16)
    for c0 in range(0, d_ff, FF_CHUNK):
        g = jnp.dot(h, wg_ref[:, c0:c0 + FF_CHUNK], preferred_element_type=F32)
        u = jnp.dot(h, wu_ref[:, c0:c0 + FF_CHUNK], preferred_element_type=F32)
        act_ref[:, c0:c0 + FF_CHUNK] = ((g * jax.nn.sigmoid(g)) * u).astype(BF16)
    o_ref[...] = x1 + jnp.dot(act_ref[...], wd_ref[...], preferred_element_type=F32)


def _ffn(x2d, attn, conv, w_out, gain, w_gate, w_up, w_down):
    n, d = x2d.shape
    tm = ROW_TILE
    d_ff = w_gate.shape[1]
    const = lambda i: (0, 0)
    row = lambda i: (i, 0)
    resident = functools.partial(pl.BlockSpec, index_map=const, pipeline_mode=pl.Buffered(1))
    return pl.pallas_call(
        _ffn_kernel,
        grid=(n // tm,),
        in_specs=[
            pl.BlockSpec((tm, d), row),
            pl.BlockSpec((tm, attn.shape[1]), row),
            pl.BlockSpec((tm, conv.shape[1]), row),
            resident(w_out.shape),
            pl.BlockSpec((1, d), const),
            resident(w_gate.shape),
            resident(w_up.shape),
            resident(w_down.shape),
        ],
        out_specs=pl.BlockSpec((tm, d), row),
        out_shape=jax.ShapeDtypeStruct((n, d), F32),
        scratch_shapes=[pltpu.VMEM((tm, d_ff), BF16)],
        compiler_params=pltpu.CompilerParams(
            dimension_semantics=("arbitrary",), vmem_limit_bytes=VMEM_LIMIT),
    )(x2d, attn, conv, w_out, gain, w_gate, w_up, w_down)


def kernel(x, norm_mix, w_in, q_norm, k_norm, conv_w, w_out, norm_ffn, w_gate, w_up, w_down):
    batch, seq, d = x.shape
    depth = w_in.shape[0]
    conv_dim = conv_w.shape[2]
    attn_dim = w_out.shape[1] - conv_dim
    n_heads = attn_dim // HEAD_DIM
    assert seq % ROW_TILE == 0 and seq % SUB == 0 and attn_dim % MXU_DIM == 0
    assert w_gate.shape[2] % FF_CHUNK == 0 and conv_w.shape[1] == 3

    xf = x.reshape(batch * seq, d)
    for l in range(depth):
        q, k, v, conv, wo, wg, wu, wd = _inproj(
            xf, norm_mix[l][None, :], w_in, l,
            jnp.tile(q_norm[l], n_heads)[None, :], jnp.tile(k_norm[l], n_heads)[None, :], conv_w[l],
            (w_out, w_gate, w_up, w_down), seq=seq, attn_dim=attn_dim, conv_dim=conv_dim)
        attn = _attention(q, k, v, batch=batch, seq=seq)
        xf = _ffn(xf, attn, conv, wo, norm_ffn[l][None, :], wg, wu, wd)
    return xf.reshape(batch, seq, d)
```

```python
import functools

import jax
import jax.numpy as jnp
from jax import lax
from jax.experimental import pallas as pl
from jax.experimental.pallas import tpu as pltpu

F32 = jnp.float32
BF16 = jnp.bfloat16

HEAD_DIM = 64
EPS = 1e-6
LANES = 128
SUBLANES = 8
BF16_SUBLANES = 16
W_CAST_ROWS = 128
MXU_DIM = 256
HEADS_PER_BLOCK = LANES // HEAD_DIM
SUB = 128
LOG2E = 1.4426950408889634
LN2 = 0.6931471805599453
SKIP_ABOVE = 104.0

ROW_TILE = 512
ATTN_TILES = 4
CS_BLOCKS = 1
LOOKBACK_SUBS = 2
WALK_SUBS = 2
FF_CHUNK = 256
CONV_HALO = 8
VMEM_LIMIT = 56 * 1024 * 1024


def _rms_scale(x):
    return lax.rsqrt(jnp.mean(x * x, axis=-1, keepdims=True) + EPS)


def _inproj_kernel(x_ref, gain_ref, w_ref, qg_ref, kg_ref, cw_ref, *rest,
                   tiles_per_seq, attn_dim, conv_dim, n_cast):
    cast_in, (q_ref, k_ref, v_ref, c_ref) = rest[:n_cast], rest[n_cast:n_cast + 4]
    cast_out, (hbuf_ref, wbf_ref) = rest[n_cast + 4:2 * n_cast + 4], rest[2 * n_cast + 4:]
    tm = x_ref.shape[0]
    i = pl.program_id(0)

    @pl.when(i == 0)
    def _():
        def cast_rows(c, _):
            rows = pl.ds(pl.multiple_of(c * W_CAST_ROWS, W_CAST_ROWS), W_CAST_ROWS)
            wbf_ref[rows, :] = w_ref[rows, :].astype(BF16)
            return 0
        lax.fori_loop(0, w_ref.shape[0] // W_CAST_ROWS, cast_rows, 0)
        hbuf_ref[0:CONV_HALO, :] = jnp.zeros((CONV_HALO, conv_dim), F32)

    for src, dst in zip(cast_in, cast_out):
        dst[...] = src[...].astype(BF16)

    x = x_ref[...]
    h = ((x * _rms_scale(x)) * gain_ref[...]).astype(BF16)

    def proj(c0, width):
        return jnp.dot(h, wbf_ref[:, c0:c0 + width], preferred_element_type=F32)

    cb = proj(3 * attn_dim, conv_dim)
    hh = proj(3 * attn_dim + conv_dim, conv_dim) * proj(3 * attn_dim + 2 * conv_dim, conv_dim)
    hbuf_ref[0:CONV_HALO, :] = jnp.where(i % tiles_per_seq == 0, 0.0, hbuf_ref[0:CONV_HALO, :])
    hbuf_ref[CONV_HALO:CONV_HALO + tm, :] = hh
    h1 = hbuf_ref[CONV_HALO - 1:CONV_HALO - 1 + tm, :]
    h2 = hbuf_ref[CONV_HALO - 2:CONV_HALO - 2 + tm, :]
    y = cw_ref[0:1, :] * h2 + cw_ref[1:2, :] * h1 + cw_ref[2:3, :] * hh
    c_ref[...] = (cb * y).astype(BF16)
    hbuf_ref[0:CONV_HALO, :] = hh[tm - CONV_HALO:tm, :]

    r = lax.broadcasted_iota(jnp.int32, (MXU_DIM, MXU_DIM), 0) // HEAD_DIM
    c = lax.broadcasted_iota(jnp.int32, (MXU_DIM, MXU_DIM), 1) // HEAD_DIM
    seg = jnp.where(r == c, 1.0, 0.0).astype(BF16)

    def head_norm(p, gain):
        p2 = (p * p).astype(BF16)
        ssq = jnp.concatenate(
            [jnp.dot(p2[:, s:s + MXU_DIM], seg, preferred_element_type=F32)
             for s in range(0, attn_dim, MXU_DIM)], axis=1)
        return (p * lax.rsqrt(ssq * (1.0 / HEAD_DIM) + EPS)) * gain

    q_ref[...] = (head_norm(proj(0, attn_dim), qg_ref[...]) * (HEAD_DIM ** -0.5)).astype(BF16)
    k_ref[...] = head_norm(proj(attn_dim, attn_dim), kg_ref[...]).astype(BF16)
    v_ref[...] = proj(2 * attn_dim, attn_dim).astype(BF16)


def _cast_chunks(rows, n_steps):
    return max(c for c in range(1, n_steps + 1) if rows % c == 0 and (rows // c) % BF16_SUBLANES == 0)


def _inproj(x2d, gain, w_in, layer, q_gain, k_gain, conv_w, later_weights, *, seq, attn_dim, conv_dim):
    n, d = x2d.shape
    tm = ROW_TILE
    n_steps = n // tm
    cols = w_in.shape[2]
    const = lambda i: (0, 0)
    row = lambda i: (i, 0)
    cast_in_specs, cast_out_specs, cast_shapes = [], [], []
    for w in later_weights:
        _, rows, wcols = w.shape
        chunks = _cast_chunks(rows, n_steps)
        cast_in_specs.append(pl.BlockSpec(
            (None, rows // chunks, wcols), lambda i, chunks=chunks: (layer, jnp.minimum(i, chunks - 1), 0)))
        cast_out_specs.append(pl.BlockSpec(
            (rows // chunks, wcols), lambda i, chunks=chunks: (jnp.minimum(i, chunks - 1), 0)))
        cast_shapes.append(jax.ShapeDtypeStruct((rows, wcols), BF16))
    out_shapes = [jax.ShapeDtypeStruct((n, attn_dim), BF16)] * 3 + [jax.ShapeDtypeStruct((n, conv_dim), BF16)]
    return pl.pallas_call(
        functools.partial(_inproj_kernel, tiles_per_seq=seq // tm, attn_dim=attn_dim, conv_dim=conv_dim,
                          n_cast=len(later_weights)),
        grid=(n_steps,),
        in_specs=[
            pl.BlockSpec((tm, d), row),
            pl.BlockSpec((1, d), const),
            pl.BlockSpec((None, d, cols), lambda i: (layer, 0, 0), pipeline_mode=pl.Buffered(1)),
            pl.BlockSpec((1, attn_dim), const),
            pl.BlockSpec((1, attn_dim), const),
            pl.BlockSpec(conv_w.shape, const),
        ] + cast_in_specs,
        out_specs=[pl.BlockSpec((tm, attn_dim), row)] * 3 + [pl.BlockSpec((tm, conv_dim), row)] + cast_out_specs,
        out_shape=out_shapes + cast_shapes,
        scratch_shapes=[pltpu.VMEM((CONV_HALO + tm, conv_dim), F32), pltpu.VMEM((d, cols), BF16)],
        compiler_params=pltpu.CompilerParams(
            dimension_semantics=("arbitrary",), vmem_limit_bytes=VMEM_LIMIT),
    )(x2d, gain, w_in, q_gain, k_gain, conv_w, *later_weights)


def _attn_kernel(q_ref, k_ref, v_ref, o_ref, acc_ref, carry_ref, min_ref):
    n_tiles = q_ref.shape[0] // SUB
    n_blocks = q_ref.shape[1] // LANES
    i = pl.program_id(1)
    rows = HEADS_PER_BLOCK * SUB
    assert n_tiles >= LOOKBACK_SUBS

    lane = lax.broadcasted_iota(jnp.int32, (SUB, LANES), 1)
    r = lax.broadcasted_iota(jnp.int32, (2 * SUB, 2 * SUB), 0) % SUB
    c = lax.broadcasted_iota(jnp.int32, (2 * SUB, 2 * SUB), 1)
    tri = jnp.where((c >= SUB) | (r >= c), 1.0, 0.0).astype(BF16)
    causal = (lax.broadcasted_iota(jnp.int32, (rows, SUB), 1)
              < lax.broadcasted_iota(jnp.int32, (rows, SUB), 0) % SUB)

    def span(jobs, diag, first):
        z_all, piece, cs_all = {}, {}, {}
        for t, start, nsub in jobs:
            trow = slice(t * SUB, (t + 1) * SUB)
            for p in range(n_blocks):
                if p % CS_BLOCKS == 0:
                    lhs_parts = []
                cols = slice(p * LANES, (p + 1) * LANES)
                q = q_ref[trow, cols]
                q2 = jnp.concatenate([jnp.where((lane // HEAD_DIM) == h, q, jnp.zeros_like(q))
                                      for h in range(HEADS_PER_BLOCK)], axis=0)
                z = lax.dot_general(q2, k_ref[pl.ds(start, nsub * SUB), cols], (((1,), (1,)), ((), ())),
                                    preferred_element_type=F32)
                z_all[t, p] = z
                for s in reversed(range(nsub)):
                    z_s = z[:, s * SUB:(s + 1) * SUB]
                    e = jnp.exp2(jnp.abs(z_s) * (-LOG2E))
                    sp = jnp.maximum(z_s, 0.0) + jnp.log(1.0 + e)
                    if diag and s == nsub - 1:
                        sp = jnp.where(causal, sp, 0.0)
                    hi = sp.astype(BF16)
                    lo = (sp - hi.astype(F32)).astype(BF16)
                    piece[t, p, s] = len(lhs_parts)
                    lhs_parts.append(jnp.concatenate([hi, lo], axis=1))
                if (p + 1) % CS_BLOCKS == 0 or p == n_blocks - 1:
                    cs_all[t, p // CS_BLOCKS] = jnp.dot(jnp.concatenate(lhs_parts, axis=0), tri,
                                                        preferred_element_type=F32)
        for t, start, nsub in jobs:
            trow = slice(t * SUB, (t + 1) * SUB)
            carries = []
            for p in range(n_blocks):
                cols = slice(p * LANES, (p + 1) * LANES)
                carry = None if first else carry_ref[t, p]
                a_cols = [None] * nsub
                for s in reversed(range(nsub)):
                    cs = cs_all[t, p // CS_BLOCKS][piece[t, p, s] * rows:(piece[t, p, s] + 1) * rows]
                    g = cs[:, :SUB] if carry is None else cs[:, :SUB] + carry
                    carry = cs[:, SUB:] if carry is None else carry + cs[:, SUB:]
                    a = jnp.exp2((z_all[t, p][:, s * SUB:(s + 1) * SUB] - g) * LOG2E)
                    if diag and s == nsub - 1:
                        a = jnp.where(causal, a, 0.0)
                    a_cols[s] = a.astype(BF16)
                carry_ref[t, p] = carry
                carries.append(carry)
                o2 = jnp.dot(jnp.concatenate(a_cols, axis=1), v_ref[pl.ds(start, nsub * SUB), cols],
                             preferred_element_type=F32)
                out = o2[0:SUB]
                for h in range(1, HEADS_PER_BLOCK):
                    out = jnp.where((lane // HEAD_DIM) == h, o2[h * SUB:(h + 1) * SUB], out)
                if first:
                    acc_ref[trow, cols] = out
                else:
                    acc_ref[trow, cols] += out
            while len(carries) > 1:
                odd = carries[-1:] if len(carries) % 2 else []
                carries = [jnp.minimum(a, b) for a, b in zip(carries[0::2], carries[1::2])] + odd
            m = carries[0]
            while m.shape[0] > SUBLANES:
                half = m.shape[0] // 2
                m = jnp.minimum(m[:half], m[half:])
            min_ref[t] = m

    @pl.when(i == 0)
    def _():
        span([(t, max(t - LOOKBACK_SUBS, 0) * SUB, min(t, LOOKBACK_SUBS) + 1) for t in range(n_tiles)], True, True)

    @pl.when(i > 0)
    def _():
        span([(t, pl.multiple_of((i * n_tiles + t - LOOKBACK_SUBS) * SUB, SUB), LOOKBACK_SUBS + 1)
              for t in range(n_tiles)], True, True)

    @pl.when(jnp.min(min_ref[...]) < SKIP_ABOVE)
    def _():
        for t in range(n_tiles):
            def unfinished(t=t):
                return (jnp.min(min_ref[t]) < SKIP_ABOVE).astype(jnp.int32)

            def cond(state):
                rem, go = state
                return jnp.logical_and(rem >= WALK_SUBS * SUB, go > 0)

            def body(state, t=t, unfinished=unfinished):
                rem, _ = state
                start = pl.multiple_of(rem - WALK_SUBS * SUB, SUB)
                span([(t, start, WALK_SUBS)], False, False)
                return start, unfinished()

            remaining = jnp.maximum(i * n_tiles + t - LOOKBACK_SUBS, 0) * SUB
            rem, go = lax.while_loop(cond, body, (remaining, unfinished()))
            for tail in range(1, WALK_SUBS):
                @pl.when(jnp.logical_and(rem == tail * SUB, go > 0))
                def _(t=t, tail=tail):
                    span([(t, 0, tail)], False, False)

    o_ref[...] = acc_ref[...].astype(o_ref.dtype)


def _attention(q, k, v, *, batch, seq):
    n, attn_dim = q.shape
    tq = ATTN_TILES * SUB
    nq = seq // tq
    return pl.pallas_call(
        _attn_kernel,
        grid=(batch, nq),
        in_specs=[
            pl.BlockSpec((tq, attn_dim), lambda b, i: (b * nq + i, 0)),
            pl.BlockSpec((seq, attn_dim), lambda b, i: (b, 0), pipeline_mode=pl.Buffered(1)),
            pl.BlockSpec((seq, attn_dim), lambda b, i: (b, 0), pipeline_mode=pl.Buffered(1)),
        ],
        out_specs=pl.BlockSpec((tq, attn_dim), lambda b, i: (b * nq + i, 0)),
        out_shape=jax.ShapeDtypeStruct((n, attn_dim), BF16),
        scratch_shapes=[pltpu.VMEM((tq, attn_dim), F32),
                        pltpu.VMEM((ATTN_TILES, attn_dim // LANES, HEADS_PER_BLOCK * SUB, SUB), F32),
                        pltpu.VMEM((ATTN_TILES, SUBLANES, SUB), F32)],
        compiler_params=pltpu.CompilerParams(
            dimension_semantics=("arbitrary", "arbitrary"), vmem_limit_bytes=VMEM_LIMIT),
    )(q, k, v)


def _ffn_kernel(x_ref, attn_ref, conv_ref, wo_ref, gain_ref, wg_ref, wu_ref, wd_ref, o_ref, act_ref):
    attn_dim = attn_ref.shape[1]
    d_ff = wg_ref.shape[1]
    mix = (jnp.dot(attn_ref[...], wo_ref[0:attn_dim, :], preferred_element_type=F32)
           + jnp.dot(conv_ref[...], wo_ref[attn_dim:, :], preferred_element_type=F32))
    x1 = x_ref[...] + mix
    h = ((x1 * _rms_scale(x1)) * gain_ref[...]).astype(BF16)
    for c0 in range(0, d_ff, FF_CHUNK):
        g = jnp.dot(h, wg_ref[:, c0:c0 + FF_CHUNK], preferred_element_type=F32)
        u = jnp.dot(h, wu_ref[:, c0:c0 + FF_CHUNK], preferred_element_type=F32)
        act_ref[:, c0:c0 + FF_CHUNK] = ((g * jax.nn.sigmoid(g)) * u).astype(BF16)
    o_ref[...] = x1 + jnp.dot(act_ref[...], wd_ref[...], preferred_element_type=F32)


def _ffn(x2d, attn, conv, w_out, gain, w_gate, w_up, w_down):
    n, d = x2d.shape
    tm = ROW_TILE
    d_ff = w_gate.shape[1]
    const = lambda i: (0, 0)
    row = lambda i: (i, 0)
    resident = functools.partial(pl.BlockSpec, index_map=const, pipeline_mode=pl.Buffered(1))
    return pl.pallas_call(
        _ffn_kernel,
        grid=(n // tm,),
        in_specs=[
            pl.BlockSpec((tm, d), row),
            pl.BlockSpec((tm, attn.shape[1]), row),
            pl.BlockSpec((tm, conv.shape[1]), row),
            resident(w_out.shape),
            pl.BlockSpec((1, d), const),
            resident(w_gate.shape),
            resident(w_up.shape),
            resident(w_down.shape),
        ],
        out_specs=pl.BlockSpec((tm, d), row),
        out_shape=jax.ShapeDtypeStruct((n, d), F32),
        scratch_shapes=[pltpu.VMEM((tm, d_ff), BF16)],
        compiler_params=pltpu.CompilerParams(
            dimension_semantics=("arbitrary",), vmem_limit_bytes=VMEM_LIMIT),
    )(x2d, attn, conv, w_out, gain, w_gate, w_up, w_down)


def kernel(x, norm_mix, w_in, q_norm, k_norm, conv_w, w_out, norm_ffn, w_gate, w_up, w_down):
    batch, seq, d = x.shape
    depth = w_in.shape[0]
    conv_dim = conv_w.shape[2]
    attn_dim = w_out.shape[1] - conv_dim
    n_heads = attn_dim // HEAD_DIM
    assert seq % ROW_TILE == 0 and seq % (ATTN_TILES * SUB) == 0 and attn_dim % MXU_DIM == 0
    assert w_gate.shape[2] % FF_CHUNK == 0 and conv_w.shape[1] == 3

    xf = x.reshape(batch * seq, d)
    for l in range(depth):
        q, k, v, conv, wo, wg, wu, wd = _inproj(
            xf, norm_mix[l][None, :], w_in, l,
            jnp.tile(q_norm[l], n_heads)[None, :], jnp.tile(k_norm[l], n_heads)[None, :], conv_w[l],
            (w_out, w_gate, w_up, w_down), seq=seq, attn_dim=attn_dim, conv_dim=conv_dim)
        attn = _attention(q, k, v, batch=batch, seq=seq)
        xf = _ffn(xf, attn, conv, wo, norm_ffn[l][None, :], wg, wu, wd)
    return xf.reshape(batch, seq, d)
```

```python
import functools

import jax
import jax.numpy as jnp
from jax import lax
from jax.experimental import pallas as pl
from jax.experimental.pallas import tpu as pltpu

F32 = jnp.float32
BF16 = jnp.bfloat16

HEAD_DIM = 64
EPS = 1e-6
LANES = 128
SUBLANES = 8
BF16_SUBLANES = 16
W_CAST_ROWS = 128
MXU_DIM = 256
HEADS_PER_BLOCK = LANES // HEAD_DIM
SUB = 128
LOG2E = 1.4426950408889634
LN2 = 0.6931471805599453
SKIP_ABOVE = 104.0

ROW_TILE = 512
ATTN_TILES = 4
CS_BLOCKS = 1
LOOKBACK_SUBS = 2
WALK_SUBS = 2
FF_CHUNK = 256
CONV_HALO = 8
VMEM_LIMIT = 56 * 1024 * 1024


def _rms_scale(x):
    return lax.rsqrt(jnp.mean(x * x, axis=-1, keepdims=True) + EPS)


def _inproj_kernel(x_ref, gain_ref, w_ref, qg_ref, kg_ref, cw_ref, q_ref, k_ref, v_ref, c_ref, hbuf_ref, wbf_ref,
                   *, tiles_per_seq, attn_dim, conv_dim):
    tm = x_ref.shape[0]
    i = pl.program_id(0)

    @pl.when(i == 0)
    def _():
        def cast_rows(c, _):
            rows = pl.ds(pl.multiple_of(c * W_CAST_ROWS, W_CAST_ROWS), W_CAST_ROWS)
            wbf_ref[rows, :] = w_ref[rows, :].astype(BF16)
            return 0
        lax.fori_loop(0, w_ref.shape[0] // W_CAST_ROWS, cast_rows, 0)
        hbuf_ref[0:CONV_HALO, :] = jnp.zeros((CONV_HALO, conv_dim), F32)

    x = x_ref[...]
    h = ((x * _rms_scale(x)) * gain_ref[...]).astype(BF16)

    def proj(c0, width):
        return jnp.dot(h, wbf_ref[:, c0:c0 + width], preferred_element_type=F32)

    cb = proj(3 * attn_dim, conv_dim)
    hh = proj(3 * attn_dim + conv_dim, conv_dim) * proj(3 * attn_dim + 2 * conv_dim, conv_dim)
    hbuf_ref[0:CONV_HALO, :] = jnp.where(i % tiles_per_seq == 0, 0.0, hbuf_ref[0:CONV_HALO, :])
    hbuf_ref[CONV_HALO:CONV_HALO + tm, :] = hh
    h1 = hbuf_ref[CONV_HALO - 1:CONV_HALO - 1 + tm, :]
    h2 = hbuf_ref[CONV_HALO - 2:CONV_HALO - 2 + tm, :]
    y = cw_ref[0:1, :] * h2 + cw_ref[1:2, :] * h1 + cw_ref[2:3, :] * hh
    c_ref[...] = (cb * y).astype(BF16)
    hbuf_ref[0:CONV_HALO, :] = hh[tm - CONV_HALO:tm, :]

    r = lax.broadcasted_iota(jnp.int32, (MXU_DIM, MXU_DIM), 0) // HEAD_DIM
    c = lax.broadcasted_iota(jnp.int32, (MXU_DIM, MXU_DIM), 1) // HEAD_DIM
    seg = jnp.where(r == c, 1.0, 0.0).astype(BF16)

    def head_norm(p, gain):
        p2 = (p * p).astype(BF16)
        ssq = jnp.concatenate(
            [jnp.dot(p2[:, s:s + MXU_DIM], seg, preferred_element_type=F32)
             for s in range(0, attn_dim, MXU_DIM)], axis=1)
        return (p * lax.rsqrt(ssq * (1.0 / HEAD_DIM) + EPS)) * gain

    q_ref[...] = (head_norm(proj(0, attn_dim), qg_ref[...]) * (HEAD_DIM ** -0.5)).astype(BF16)
    k_ref[...] = head_norm(proj(attn_dim, attn_dim), kg_ref[...]).astype(BF16)
    v_ref[...] = proj(2 * attn_dim, attn_dim).astype(BF16)


def _cast_specs(weights, layer, n_steps, step_of):
    in_specs, out_specs, shapes = [], [], []
    for w in weights:
        _, rows, cols = w.shape
        chunks = max(c for c in range(1, n_steps + 1) if rows % c == 0 and (rows // c) % BF16_SUBLANES == 0)
        chunk_of = lambda *idx, chunks=chunks: jnp.minimum(step_of(*idx), chunks - 1)
        in_specs.append(pl.BlockSpec((None, rows // chunks, cols),
                                     lambda *idx, chunk_of=chunk_of: (layer, chunk_of(*idx), 0)))
        out_specs.append(pl.BlockSpec((rows // chunks, cols), lambda *idx, chunk_of=chunk_of: (chunk_of(*idx), 0)))
        shapes.append(jax.ShapeDtypeStruct((rows, cols), BF16))
    return in_specs, out_specs, shapes


def _inproj(x2d, gain, w_in, layer, q_gain, k_gain, conv_w, *, seq, attn_dim, conv_dim):
    n, d = x2d.shape
    tm = ROW_TILE
    n_steps = n // tm
    cols = w_in.shape[2]
    const = lambda i: (0, 0)
    row = lambda i: (i, 0)
    out_shapes = [jax.ShapeDtypeStruct((n, attn_dim), BF16)] * 3 + [jax.ShapeDtypeStruct((n, conv_dim), BF16)]
    return pl.pallas_call(
        functools.partial(_inproj_kernel, tiles_per_seq=seq // tm, attn_dim=attn_dim, conv_dim=conv_dim),
        grid=(n_steps,),
        in_specs=[
            pl.BlockSpec((tm, d), row),
            pl.BlockSpec((1, d), const),
            pl.BlockSpec((None, d, cols), lambda i: (layer, 0, 0), pipeline_mode=pl.Buffered(1)),
            pl.BlockSpec((1, attn_dim), const),
            pl.BlockSpec((1, attn_dim), const),
            pl.BlockSpec(conv_w.shape, const),
        ],
        out_specs=[pl.BlockSpec((tm, attn_dim), row)] * 3 + [pl.BlockSpec((tm, conv_dim), row)],
        out_shape=out_shapes,
        scratch_shapes=[pltpu.VMEM((CONV_HALO + tm, conv_dim), F32), pltpu.VMEM((d, cols), BF16)],
        compiler_params=pltpu.CompilerParams(
            dimension_semantics=("arbitrary",), vmem_limit_bytes=VMEM_LIMIT),
    )(x2d, gain, w_in, q_gain, k_gain, conv_w)


def _attn_kernel(q_ref, k_ref, v_ref, *rest, n_cast):
    cast_in, o_ref, cast_out = rest[:n_cast], rest[n_cast], rest[n_cast + 1:2 * n_cast + 1]
    acc_ref, carry_ref, min_ref = rest[2 * n_cast + 1:]
    for src, dst in zip(cast_in, cast_out):
        dst[...] = src[...].astype(BF16)

    n_tiles = q_ref.shape[0] // SUB
    n_blocks = q_ref.shape[1] // LANES
    i = pl.program_id(1)
    rows = HEADS_PER_BLOCK * SUB
    assert n_tiles >= LOOKBACK_SUBS

    lane = lax.broadcasted_iota(jnp.int32, (SUB, LANES), 1)
    r = lax.broadcasted_iota(jnp.int32, (2 * SUB, 2 * SUB), 0) % SUB
    c = lax.broadcasted_iota(jnp.int32, (2 * SUB, 2 * SUB), 1)
    tri = jnp.where((c >= SUB) | (r >= c), 1.0, 0.0).astype(BF16)
    causal = (lax.broadcasted_iota(jnp.int32, (rows, SUB), 1)
              < lax.broadcasted_iota(jnp.int32, (rows, SUB), 0) % SUB)

    def span(jobs, diag, first):
        z_all, piece, cs_all = {}, {}, {}
        for t, start, nsub in jobs:
            trow = slice(t * SUB, (t + 1) * SUB)
            for p in range(n_blocks):
                if p % CS_BLOCKS == 0:
                    lhs_parts = []
                cols = slice(p * LANES, (p + 1) * LANES)
                q = q_ref[trow, cols]
                q2 = jnp.concatenate([jnp.where((lane // HEAD_DIM) == h, q, jnp.zeros_like(q))
                                      for h in range(HEADS_PER_BLOCK)], axis=0)
                z = lax.dot_general(q2, k_ref[pl.ds(start, nsub * SUB), cols], (((1,), (1,)), ((), ())),
                                    preferred_element_type=F32)
                z_all[t, p] = z
                for s in reversed(range(nsub)):
                    z_s = z[:, s * SUB:(s + 1) * SUB]
                    e = jnp.exp2(jnp.abs(z_s) * (-LOG2E))
                    sp = jnp.maximum(z_s, 0.0) + jnp.log(1.0 + e)
                    if diag and s == nsub - 1:
                        sp = jnp.where(causal, sp, 0.0)
                    hi = sp.astype(BF16)
                    lo = (sp - hi.astype(F32)).astype(BF16)
                    piece[t, p, s] = len(lhs_parts)
                    lhs_parts.append(jnp.concatenate([hi, lo], axis=1))
                if (p + 1) % CS_BLOCKS == 0 or p == n_blocks - 1:
                    cs_all[t, p // CS_BLOCKS] = jnp.dot(jnp.concatenate(lhs_parts, axis=0), tri,
                                                        preferred_element_type=F32)
        for t, start, nsub in jobs:
            trow = slice(t * SUB, (t + 1) * SUB)
            carries = []
            for p in range(n_blocks):
                cols = slice(p * LANES, (p + 1) * LANES)
                carry = None if first else carry_ref[t, p]
                a_cols = [None] * nsub
                for s in reversed(range(nsub)):
                    cs = cs_all[t, p // CS_BLOCKS][piece[t, p, s] * rows:(piece[t, p, s] + 1) * rows]
                    g = cs[:, :SUB] if carry is None else cs[:, :SUB] + carry
                    carry = cs[:, SUB:] if carry is None else carry + cs[:, SUB:]
                    a = jnp.exp2((z_all[t, p][:, s * SUB:(s + 1) * SUB] - g) * LOG2E)
                    if diag and s == nsub - 1:
                        a = jnp.where(causal, a, 0.0)
                    a_cols[s] = a.astype(BF16)
                carry_ref[t, p] = carry
                carries.append(carry)
                o2 = jnp.dot(jnp.concatenate(a_cols, axis=1), v_ref[pl.ds(start, nsub * SUB), cols],
                             preferred_element_type=F32)
                out = o2[0:SUB]
                for h in range(1, HEADS_PER_BLOCK):
                    out = jnp.where((lane // HEAD_DIM) == h, o2[h * SUB:(h + 1) * SUB], out)
                if first:
                    acc_ref[trow, cols] = out
                else:
                    acc_ref[trow, cols] += out
            while len(carries) > 1:
                odd = carries[-1:] if len(carries) % 2 else []
                carries = [jnp.minimum(a, b) for a, b in zip(carries[0::2], carries[1::2])] + odd
            m = carries[0]
            while m.shape[0] > SUBLANES:
                half = m.shape[0] // 2
                m = jnp.minimum(m[:half], m[half:])
            min_ref[t] = m

    @pl.when(i == 0)
    def _():
        span([(t, max(t - LOOKBACK_SUBS, 0) * SUB, min(t, LOOKBACK_SUBS) + 1) for t in range(n_tiles)], True, True)

    @pl.when(i > 0)
    def _():
        span([(t, pl.multiple_of((i * n_tiles + t - LOOKBACK_SUBS) * SUB, SUB), LOOKBACK_SUBS + 1)
              for t in range(n_tiles)], True, True)

    @pl.when(jnp.min(min_ref[...]) < SKIP_ABOVE)
    def _():
        for t in range(n_tiles):
            def unfinished(t=t):
                return (jnp.min(min_ref[t]) < SKIP_ABOVE).astype(jnp.int32)

            def cond(state):
                rem, go = state
                return jnp.logical_and(rem >= WALK_SUBS * SUB, go > 0)

            def body(state, t=t, unfinished=unfinished):
                rem, _ = state
                start = pl.multiple_of(rem - WALK_SUBS * SUB, SUB)
                span([(t, start, WALK_SUBS)], False, False)
                return start, unfinished()

            remaining = jnp.maximum(i * n_tiles + t - LOOKBACK_SUBS, 0) * SUB
            rem, go = lax.while_loop(cond, body, (remaining, unfinished()))
            for tail in range(1, WALK_SUBS):
                @pl.when(jnp.logical_and(rem == tail * SUB, go > 0))
                def _(t=t, tail=tail):
                    span([(t, 0, tail)], False, False)

    o_ref[...] = acc_ref[...].astype(o_ref.dtype)


def _attention(q, k, v, later_weights, layer, *, batch, seq):
    n, attn_dim = q.shape
    tq = ATTN_TILES * SUB
    nq = seq // tq
    cast_in_specs, cast_out_specs, cast_shapes = _cast_specs(
        later_weights, layer, batch * nq, lambda b, i: b * nq + i)
    return pl.pallas_call(
        functools.partial(_attn_kernel, n_cast=len(later_weights)),
        grid=(batch, nq),
        in_specs=[
            pl.BlockSpec((tq, attn_dim), lambda b, i: (b * nq + i, 0)),
            pl.BlockSpec((seq, attn_dim), lambda b, i: (b, 0), pipeline_mode=pl.Buffered(1)),
            pl.BlockSpec((seq, attn_dim), lambda b, i: (b, 0), pipeline_mode=pl.Buffered(1)),
        ] + cast_in_specs,
        out_specs=[pl.BlockSpec((tq, attn_dim), lambda b, i: (b * nq + i, 0))] + cast_out_specs,
        out_shape=[jax.ShapeDtypeStruct((n, attn_dim), BF16)] + cast_shapes,
        scratch_shapes=[pltpu.VMEM((tq, attn_dim), F32),
                        pltpu.VMEM((ATTN_TILES, attn_dim // LANES, HEADS_PER_BLOCK * SUB, SUB), F32),
                        pltpu.VMEM((ATTN_TILES, SUBLANES, SUB), F32)],
        compiler_params=pltpu.CompilerParams(
            dimension_semantics=("arbitrary", "arbitrary"), vmem_limit_bytes=VMEM_LIMIT),
    )(q, k, v, *later_weights)


def _ffn_kernel(x_ref, attn_ref, conv_ref, wo_ref, gain_ref, wg_ref, wu_ref, wd_ref, o_ref, act_ref):
    attn_dim = attn_ref.shape[1]
    d_ff = wg_ref.shape[1]
    mix = (jnp.dot(attn_ref[...], wo_ref[0:attn_dim, :], preferred_element_type=F32)
           + jnp.dot(conv_ref[...], wo_ref[attn_dim:, :], preferred_element_type=F32))
    x1 = x_ref[...] + mix
    h = ((x1 * _rms_scale(x1)) * gain_ref[...]).astype(BF16)
    for c0 in range(0, d_ff, FF_CHUNK):
        g = jnp.dot(h, wg_ref[:, c0:c0 + FF_CHUNK], preferred_element_type=F32)
        u = jnp.dot(h, wu_ref[:, c0:c0 + FF_CHUNK], preferred_element_type=F32)
        act_ref[:, c0:c0 + FF_CHUNK] = ((g * jax.nn.sigmoid(g)) * u).astype(BF16)
    o_ref[...] = x1 + jnp.dot(act_ref[...], wd_ref[...], preferred_element_type=F32)


def _ffn(x2d, attn, conv, w_out, gain, w_gate, w_up, w_down):
    n, d = x2d.shape
    tm = ROW_TILE
    d_ff = w_gate.shape[1]
    const = lambda i: (0, 0)
    row = lambda i: (i, 0)
    resident = functools.partial(pl.BlockSpec, index_map=const, pipeline_mode=pl.Buffered(1))
    return pl.pallas_call(
        _ffn_kernel,
        grid=(n // tm,),
        in_specs=[
            pl.BlockSpec((tm, d), row),
            pl.BlockSpec((tm, attn.shape[1]), row),
            pl.BlockSpec((tm, conv.shape[1]), row),
            resident(w_out.shape),
            pl.BlockSpec((1, d), const),
            resident(w_gate.shape),
            resident(w_up.shape),
            resident(w_down.shape),
        ],
        out_specs=pl.BlockSpec((tm, d), row),
        out_shape=jax.ShapeDtypeStruct((n, d), F32),
        scratch_shapes=[pltpu.VMEM((tm, d_ff), BF16)],
        compiler_params=pltpu.CompilerParams(
            dimension_semantics=("arbitrary",), vmem_limit_bytes=VMEM_LIMIT),
    )(x2d, attn, conv, w_out, gain, w_gate, w_up, w_down)


def kernel(x, norm_mix, w_in, q_norm, k_norm, conv_w, w_out, norm_ffn, w_gate, w_up, w_down):
    batch, seq, d = x.shape
    depth = w_in.shape[0]
    conv_dim = conv_w.shape[2]
    attn_dim = w_out.shape[1] - conv_dim
    n_heads = attn_dim // HEAD_DIM
    assert seq % ROW_TILE == 0 and seq % (ATTN_TILES * SUB) == 0 and attn_dim % MXU_DIM == 0
    assert w_gate.shape[2] % FF_CHUNK == 0 and conv_w.shape[1] == 3

    xf = x.reshape(batch * seq, d)
    for l in range(depth):
        q, k, v, conv = _inproj(
            xf, norm_mix[l][None, :], w_in, l,
            jnp.tile(q_norm[l], n_heads)[None, :], jnp.tile(k_norm[l], n_heads)[None, :], conv_w[l],
            seq=seq, attn_dim=attn_dim, conv_dim=conv_dim)
        attn, wo, wg, wu, wd = _attention(q, k, v, (w_out, w_gate, w_up, w_down), l, batch=batch, seq=seq)
        xf = _ffn(xf, attn, conv, wo, norm_ffn[l][None, :], wg, wu, wd)
    return xf.reshape(batch, seq, d)
```

```python
import functools

import jax
import jax.numpy as jnp
from jax import lax
from jax.experimental import pallas as pl
from jax.experimental.pallas import tpu as pltpu

F32 = jnp.float32
BF16 = jnp.bfloat16

HEAD_DIM = 64
EPS = 1e-6
LANES = 128
SUBLANES = 8
BF16_SUBLANES = 16
W_CAST_ROWS = 128
MXU_DIM = 256
HEADS_PER_BLOCK = LANES // HEAD_DIM
SUB = 128
LOG2E = 1.4426950408889634
LN2 = 0.6931471805599453
SKIP_ABOVE = 104.0

ROW_TILE = 512
FFN_ROW_TILE = 1024
ATTN_TILES = 4
CS_BLOCKS = 1
LOOKBACK_SUBS = 2
WALK_SUBS = 2
FF_CHUNK = 256
CONV_HALO = 8
VMEM_LIMIT = 56 * 1024 * 1024


def _rms_scale(x):
    return lax.rsqrt(jnp.mean(x * x, axis=-1, keepdims=True) + EPS)


def _inproj_kernel(x_ref, gain_ref, w_ref, qg_ref, kg_ref, cw_ref, *rest,
                   tiles_per_seq, attn_dim, conv_dim, n_cast):
    cast_in, (q_ref, k_ref, v_ref, c_ref) = rest[:n_cast], rest[n_cast:n_cast + 4]
    cast_out, (hbuf_ref, wbf_ref) = rest[n_cast + 4:2 * n_cast + 4], rest[2 * n_cast + 4:]
    tm = x_ref.shape[0]
    i = pl.program_id(0)

    @pl.when(i == 0)
    def _():
        def cast_rows(c, _):
            rows = pl.ds(pl.multiple_of(c * W_CAST_ROWS, W_CAST_ROWS), W_CAST_ROWS)
            wbf_ref[rows, :] = w_ref[rows, :].astype(BF16)
            return 0
        lax.fori_loop(0, w_ref.shape[0] // W_CAST_ROWS, cast_rows, 0)
        hbuf_ref[0:CONV_HALO, :] = jnp.zeros((CONV_HALO, conv_dim), F32)

    for src, dst in zip(cast_in, cast_out):
        dst[...] = src[...].astype(BF16)

    x = x_ref[...]
    h = ((x * _rms_scale(x)) * gain_ref[...]).astype(BF16)

    def proj(c0, width):
        return jnp.dot(h, wbf_ref[:, c0:c0 + width], preferred_element_type=F32)

    cb = proj(3 * attn_dim, conv_dim)
    hh = proj(3 * attn_dim + conv_dim, conv_dim) * proj(3 * attn_dim + 2 * conv_dim, conv_dim)
    hbuf_ref[0:CONV_HALO, :] = jnp.where(i % tiles_per_seq == 0, 0.0, hbuf_ref[0:CONV_HALO, :])
    hbuf_ref[CONV_HALO:CONV_HALO + tm, :] = hh
    h1 = hbuf_ref[CONV_HALO - 1:CONV_HALO - 1 + tm, :]
    h2 = hbuf_ref[CONV_HALO - 2:CONV_HALO - 2 + tm, :]
    y = cw_ref[0:1, :] * h2 + cw_ref[1:2, :] * h1 + cw_ref[2:3, :] * hh
    c_ref[...] = (cb * y).astype(BF16)
    hbuf_ref[0:CONV_HALO, :] = hh[tm - CONV_HALO:tm, :]

    r = lax.broadcasted_iota(jnp.int32, (MXU_DIM, MXU_DIM), 0) // HEAD_DIM
    c = lax.broadcasted_iota(jnp.int32, (MXU_DIM, MXU_DIM), 1) // HEAD_DIM
    seg = jnp.where(r == c, 1.0, 0.0).astype(BF16)

    def head_norm(p, gain):
        p2 = (p * p).astype(BF16)
        ssq = jnp.concatenate(
            [jnp.dot(p2[:, s:s + MXU_DIM], seg, preferred_element_type=F32)
             for s in range(0, attn_dim, MXU_DIM)], axis=1)
        return (p * lax.rsqrt(ssq * (1.0 / HEAD_DIM) + EPS)) * gain

    q_ref[...] = (head_norm(proj(0, attn_dim), qg_ref[...]) * (HEAD_DIM ** -0.5)).astype(BF16)
    k_ref[...] = head_norm(proj(attn_dim, attn_dim), kg_ref[...]).astype(BF16)
    v_ref[...] = proj(2 * attn_dim, attn_dim).astype(BF16)


def _cast_specs(weights, layer, n_steps, step_of):
    in_specs, out_specs, shapes = [], [], []
    for w in weights:
        _, rows, cols = w.shape
        chunks = max(c for c in range(1, n_steps + 1) if rows % c == 0 and (rows // c) % BF16_SUBLANES == 0)
        chunk_of = lambda *idx, chunks=chunks: jnp.minimum(step_of(*idx), chunks - 1)
        in_specs.append(pl.BlockSpec((None, rows // chunks, cols),
                                     lambda *idx, chunk_of=chunk_of: (layer, chunk_of(*idx), 0)))
        out_specs.append(pl.BlockSpec((rows // chunks, cols), lambda *idx, chunk_of=chunk_of: (chunk_of(*idx), 0)))
        shapes.append(jax.ShapeDtypeStruct((rows, cols), BF16))
    return in_specs, out_specs, shapes


def _inproj(x2d, gain, w_in, layer, q_gain, k_gain, conv_w, later_weights, *, seq, attn_dim, conv_dim):
    n, d = x2d.shape
    tm = ROW_TILE
    n_steps = n // tm
    cols = w_in.shape[2]
    const = lambda i: (0, 0)
    row = lambda i: (i, 0)
    cast_in_specs, cast_out_specs, cast_shapes = _cast_specs(later_weights, layer, n_steps, lambda i: i)
    out_shapes = [jax.ShapeDtypeStruct((n, attn_dim), BF16)] * 3 + [jax.ShapeDtypeStruct((n, conv_dim), BF16)]
    return pl.pallas_call(
        functools.partial(_inproj_kernel, tiles_per_seq=seq // tm, attn_dim=attn_dim, conv_dim=conv_dim,
                          n_cast=len(later_weights)),
        grid=(n_steps,),
        in_specs=[
            pl.BlockSpec((tm, d), row),
            pl.BlockSpec((1, d), const),
            pl.BlockSpec((None, d, cols), lambda i: (layer, 0, 0), pipeline_mode=pl.Buffered(1)),
            pl.BlockSpec((1, attn_dim), const),
            pl.BlockSpec((1, attn_dim), const),
            pl.BlockSpec(conv_w.shape, const),
        ] + cast_in_specs,
        out_specs=[pl.BlockSpec((tm, attn_dim), row)] * 3 + [pl.BlockSpec((tm, conv_dim), row)] + cast_out_specs,
        out_shape=out_shapes + cast_shapes,
        scratch_shapes=[pltpu.VMEM((CONV_HALO + tm, conv_dim), F32), pltpu.VMEM((d, cols), BF16)],
        compiler_params=pltpu.CompilerParams(
            dimension_semantics=("arbitrary",), vmem_limit_bytes=VMEM_LIMIT),
    )(x2d, gain, w_in, q_gain, k_gain, conv_w, *later_weights)


def _attn_kernel(q_ref, k_ref, v_ref, o_ref, acc_ref, carry_ref, min_ref):
    n_tiles = q_ref.shape[0] // SUB
    n_blocks = q_ref.shape[1] // LANES
    i = pl.program_id(1)
    rows = HEADS_PER_BLOCK * SUB
    assert n_tiles >= LOOKBACK_SUBS

    lane = lax.broadcasted_iota(jnp.int32, (SUB, LANES), 1)
    r = lax.broadcasted_iota(jnp.int32, (2 * SUB, 2 * SUB), 0) % SUB
    c = lax.broadcasted_iota(jnp.int32, (2 * SUB, 2 * SUB), 1)
    tri = jnp.where((c >= SUB) | (r >= c), 1.0, 0.0).astype(BF16)
    causal = (lax.broadcasted_iota(jnp.int32, (rows, SUB), 1)
              < lax.broadcasted_iota(jnp.int32, (rows, SUB), 0) % SUB)

    def span(jobs, diag, first):
        z_all, piece, cs_all = {}, {}, {}
        for t, start, nsub in jobs:
            trow = slice(t * SUB, (t + 1) * SUB)
            for p in range(n_blocks):
                if p % CS_BLOCKS == 0:
                    lhs_parts = []
                cols = slice(p * LANES, (p + 1) * LANES)
                q = q_ref[trow, cols]
                q2 = jnp.concatenate([jnp.where((lane // HEAD_DIM) == h, q, jnp.zeros_like(q))
                                      for h in range(HEADS_PER_BLOCK)], axis=0)
                z = lax.dot_general(q2, k_ref[pl.ds(start, nsub * SUB), cols], (((1,), (1,)), ((), ())),
                                    preferred_element_type=F32)
                z_all[t, p] = z
                for s in reversed(range(nsub)):
                    z_s = z[:, s * SUB:(s + 1) * SUB]
                    e = jnp.exp2(jnp.abs(z_s) * (-LOG2E))
                    sp = jnp.maximum(z_s, 0.0) + jnp.log(1.0 + e)
                    if diag and s == nsub - 1:
                        sp = jnp.where(causal, sp, 0.0)
                    hi = sp.astype(BF16)
                    lo = (sp - hi.astype(F32)).astype(BF16)
                    piece[t, p, s] = len(lhs_parts)
                    lhs_parts.append(jnp.concatenate([hi, lo], axis=1))
                if (p + 1) % CS_BLOCKS == 0 or p == n_blocks - 1:
                    cs_all[t, p // CS_BLOCKS] = jnp.dot(jnp.concatenate(lhs_parts, axis=0), tri,
                                                        preferred_element_type=F32)
        for t, start, nsub in jobs:
            trow = slice(t * SUB, (t + 1) * SUB)
            carries = []
            for p in range(n_blocks):
                cols = slice(p * LANES, (p + 1) * LANES)
                carry = None if first else carry_ref[t, p]
                a_cols = [None] * nsub
                for s in reversed(range(nsub)):
                    cs = cs_all[t, p // CS_BLOCKS][piece[t, p, s] * rows:(piece[t, p, s] + 1) * rows]
                    g = cs[:, :SUB] if carry is None else cs[:, :SUB] + carry
                    carry = cs[:, SUB:] if carry is None else carry + cs[:, SUB:]
                    a = jnp.exp2((z_all[t, p][:, s * SUB:(s + 1) * SUB] - g) * LOG2E)
                    if diag and s == nsub - 1:
                        a = jnp.where(causal, a, 0.0)
                    a_cols[s] = a.astype(BF16)
                carry_ref[t, p] = carry
                carries.append(carry)
                o2 = jnp.dot(jnp.concatenate(a_cols, axis=1), v_ref[pl.ds(start, nsub * SUB), cols],
                             preferred_element_type=F32)
                out = o2[0:SUB]
                for h in range(1, HEADS_PER_BLOCK):
                    out = jnp.where((lane // HEAD_DIM) == h, o2[h * SUB:(h + 1) * SUB], out)
                if first:
                    acc_ref[trow, cols] = out
                else:
                    acc_ref[trow, cols] += out
            while len(carries) > 1:
                odd = carries[-1:] if len(carries) % 2 else []
                carries = [jnp.minimum(a, b) for a, b in zip(carries[0::2], carries[1::2])] + odd
            m = carries[0]
            while m.shape[0] > SUBLANES:
                half = m.shape[0] // 2
                m = jnp.minimum(m[:half], m[half:])
            min_ref[t] = m

    @pl.when(i == 0)
    def _():
        span([(t, max(t - LOOKBACK_SUBS, 0) * SUB, min(t, LOOKBACK_SUBS) + 1) for t in range(n_tiles)], True, True)

    @pl.when(i > 0)
    def _():
        span([(t, pl.multiple_of((i * n_tiles + t - LOOKBACK_SUBS) * SUB, SUB), LOOKBACK_SUBS + 1)
              for t in range(n_tiles)], True, True)

    @pl.when(jnp.min(min_ref[...]) < SKIP_ABOVE)
    def _():
        for t in range(n_tiles):
            def unfinished(t=t):
                return (jnp.min(min_ref[t]) < SKIP_ABOVE).astype(jnp.int32)

            def cond(state):
                rem, go = state
                return jnp.logical_and(rem >= WALK_SUBS * SUB, go > 0)

            def body(state, t=t, unfinished=unfinished):
                rem, _ = state
                start = pl.multiple_of(rem - WALK_SUBS * SUB, SUB)
                span([(t, start, WALK_SUBS)], False, False)
                return start, unfinished()

            remaining = jnp.maximum(i * n_tiles + t - LOOKBACK_SUBS, 0) * SUB
            rem, go = lax.while_loop(cond, body, (remaining, unfinished()))
            for tail in range(1, WALK_SUBS):
                @pl.when(jnp.logical_and(rem == tail * SUB, go > 0))
                def _(t=t, tail=tail):
                    span([(t, 0, tail)], False, False)

    o_ref[...] = acc_ref[...].astype(o_ref.dtype)


def _attention(q, k, v, *, batch, seq):
    n, attn_dim = q.shape
    tq = ATTN_TILES * SUB
    nq = seq // tq
    return pl.pallas_call(
        _attn_kernel,
        grid=(batch, nq),
        in_specs=[
            pl.BlockSpec((tq, attn_dim), lambda b, i: (b * nq + i, 0)),
            pl.BlockSpec((seq, attn_dim), lambda b, i: (b, 0)),
            pl.BlockSpec((seq, attn_dim), lambda b, i: (b, 0)),
        ],
        out_specs=pl.BlockSpec((tq, attn_dim), lambda b, i: (b * nq + i, 0)),
        out_shape=jax.ShapeDtypeStruct((n, attn_dim), BF16),
        scratch_shapes=[pltpu.VMEM((tq, attn_dim), F32),
                        pltpu.VMEM((ATTN_TILES, attn_dim // LANES, HEADS_PER_BLOCK * SUB, SUB), F32),
                        pltpu.VMEM((ATTN_TILES, SUBLANES, SUB), F32)],
        compiler_params=pltpu.CompilerParams(
            dimension_semantics=("arbitrary", "arbitrary"), vmem_limit_bytes=VMEM_LIMIT),
    )(q, k, v)


def _ffn_kernel(x_ref, attn_ref, conv_ref, wo_ref, gain_ref, wg_ref, wu_ref, wd_ref, o_ref, act_ref):
    attn_dim = attn_ref.shape[1]
    d_ff = wg_ref.shape[1]
    mix = (jnp.dot(attn_ref[...], wo_ref[0:attn_dim, :], preferred_element_type=F32)
           + jnp.dot(conv_ref[...], wo_ref[attn_dim:, :], preferred_element_type=F32))
    x1 = x_ref[...] + mix
    h = ((x1 * _rms_scale(x1)) * gain_ref[...]).astype(BF16)
    for c0 in range(0, d_ff, FF_CHUNK):
        g = jnp.dot(h, wg_ref[:, c0:c0 + FF_CHUNK], preferred_element_type=F32)
        u = jnp.dot(h, wu_ref[:, c0:c0 + FF_CHUNK], preferred_element_type=F32)
        act_ref[:, c0:c0 + FF_CHUNK] = ((g * jax.nn.sigmoid(g)) * u).astype(BF16)
    o_ref[...] = x1 + jnp.dot(act_ref[...], wd_ref[...], preferred_element_type=F32)


def _ffn(x2d, attn, conv, w_out, gain, w_gate, w_up, w_down):
    n, d = x2d.shape
    tm = FFN_ROW_TILE
    d_ff = w_gate.shape[1]
    const = lambda i: (0, 0)
    row = lambda i: (i, 0)
    resident = functools.partial(pl.BlockSpec, index_map=const, pipeline_mode=pl.Buffered(1))
    return pl.pallas_call(
        _ffn_kernel,
        grid=(n // tm,),
        in_specs=[
            pl.BlockSpec((tm, d), row),
            pl.BlockSpec((tm, attn.shape[1]), row),
            pl.BlockSpec((tm, conv.shape[1]), row),
            resident(w_out.shape),
            pl.BlockSpec((1, d), const),
            resident(w_gate.shape),
            resident(w_up.shape),
            resident(w_down.shape),
        ],
        out_specs=pl.BlockSpec((tm, d), row),
        out_shape=jax.ShapeDtypeStruct((n, d), F32),
        scratch_shapes=[pltpu.VMEM((tm, d_ff), BF16)],
        compiler_params=pltpu.CompilerParams(
            dimension_semantics=("arbitrary",), vmem_limit_bytes=VMEM_LIMIT),
    )(x2d, attn, conv, w_out, gain, w_gate, w_up, w_down)


def kernel(x, norm_mix, w_in, q_norm, k_norm, conv_w, w_out, norm_ffn, w_gate, w_up, w_down):
    batch, seq, d = x.shape
    depth = w_in.shape[0]
    conv_dim = conv_w.shape[2]
    attn_dim = w_out.shape[1] - conv_dim
    n_heads = attn_dim // HEAD_DIM
    assert seq % ROW_TILE == 0 and seq % (ATTN_TILES * SUB) == 0 and attn_dim % MXU_DIM == 0
    assert (batch * seq) % FFN_ROW_TILE == 0
    assert w_gate.shape[2] % FF_CHUNK == 0 and conv_w.shape[1] == 3

    xf = x.reshape(batch * seq, d)
    for l in range(depth):
        q, k, v, conv, wo, wg, wu, wd = _inproj(
            xf, norm_mix[l][None, :], w_in, l,
            jnp.tile(q_norm[l], n_heads)[None, :], jnp.tile(k_norm[l], n_heads)[None, :], conv_w[l],
            (w_out, w_gate, w_up, w_down), seq=seq, attn_dim=attn_dim, conv_dim=conv_dim)
        attn = _attention(q, k, v, batch=batch, seq=seq)
        xf = _ffn(xf, attn, conv, wo, norm_ffn[l][None, :], wg, wu, wd)
    return xf.reshape(batch, seq, d)
```

```python
import functools

import jax
import jax.numpy as jnp
from jax import lax
from jax.experimental import pallas as pl
from jax.experimental.pallas import tpu as pltpu

F32 = jnp.float32
BF16 = jnp.bfloat16

HEAD_DIM = 64
EPS = 1e-6
LANES = 128
SUBLANES = 8
BF16_SUBLANES = 16
W_CAST_ROWS = 128
MXU_DIM = 256
HEADS_PER_BLOCK = LANES // HEAD_DIM
SUB = 128
LOG2E = 1.4426950408889634
LN2 = 0.6931471805599453
SKIP_ABOVE = 104.0

ROW_TILE = 512
FFN_ROW_TILE = 1024
ATTN_TILES = 4
CS_BLOCKS = 1
LOOKBACK_SUBS = 2
WALK_SUBS = 2
FF_CHUNK = 256
CONV_SHIFTS = 2
VMEM_LIMIT = 56 * 1024 * 1024


def _rms_scale(x):
    return lax.rsqrt(jnp.mean(x * x, axis=-1, keepdims=True) + EPS)


def _inproj_kernel(x_ref, gain_ref, w_ref, qg_ref, kg_ref, cw_ref, *rest,
                   tiles_per_seq, attn_dim, conv_dim, n_cast):
    cast_in, (q_ref, k_ref, v_ref, c_ref) = rest[:n_cast], rest[n_cast:n_cast + 4]
    cast_out, (halo_ref, wbf_ref) = rest[n_cast + 4:2 * n_cast + 4], rest[2 * n_cast + 4:]
    tm = x_ref.shape[0]
    i = pl.program_id(0)

    @pl.when(i == 0)
    def _():
        def cast_rows(c, _):
            rows = pl.ds(pl.multiple_of(c * W_CAST_ROWS, W_CAST_ROWS), W_CAST_ROWS)
            wbf_ref[rows, :] = w_ref[rows, :].astype(BF16)
            return 0
        lax.fori_loop(0, w_ref.shape[0] // W_CAST_ROWS, cast_rows, 0)
        halo_ref[...] = jnp.zeros_like(halo_ref)

    for src, dst in zip(cast_in, cast_out):
        dst[...] = src[...].astype(BF16)

    x = x_ref[...]
    h = ((x * _rms_scale(x)) * gain_ref[...]).astype(BF16)

    def proj(c0, width):
        return jnp.dot(h, wbf_ref[:, c0:c0 + width], preferred_element_type=F32)

    r = lax.broadcasted_iota(jnp.int32, (MXU_DIM, MXU_DIM), 0) // HEAD_DIM
    c = lax.broadcasted_iota(jnp.int32, (MXU_DIM, MXU_DIM), 1) // HEAD_DIM
    seg = jnp.where(r == c, 1.0, 0.0).astype(BF16)

    def head_norm(p, gain):
        p2 = (p * p).astype(BF16)
        ssq = jnp.concatenate(
            [jnp.dot(p2[:, s:s + MXU_DIM], seg, preferred_element_type=F32)
             for s in range(0, attn_dim, MXU_DIM)], axis=1)
        return (p * lax.rsqrt(ssq * (1.0 / HEAD_DIM) + EPS)) * gain

    q_ref[...] = (head_norm(proj(0, attn_dim), qg_ref[...]) * (HEAD_DIM ** -0.5)).astype(BF16)
    k_ref[...] = head_norm(proj(attn_dim, attn_dim), kg_ref[...]).astype(BF16)

    cb = proj(3 * attn_dim, conv_dim)
    hh = proj(3 * attn_dim + conv_dim, conv_dim) * proj(3 * attn_dim + 2 * conv_dim, conv_dim)
    row = lax.broadcasted_iota(jnp.int32, (SUBLANES, conv_dim), 0)
    seq_start = i % tiles_per_seq == 0

    def shift_down(a, j):
        prev_last = jnp.where(seq_start, 0.0, halo_ref[j, SUBLANES - 1:SUBLANES, :])
        halo_ref[j] = a[tm - SUBLANES:tm, :]
        rolled = pltpu.roll(a, 1, 0)
        head = jnp.where(row == 0, jnp.broadcast_to(prev_last, (SUBLANES, conv_dim)), rolled[0:SUBLANES])
        return jnp.concatenate([head, rolled[SUBLANES:]], axis=0)

    y = cw_ref[2:3, :] * hh + shift_down(cw_ref[1:2, :] * hh + shift_down(cw_ref[0:1, :] * hh, 0), 1)
    c_ref[...] = (cb * y).astype(BF16)

    v_ref[...] = proj(2 * attn_dim, attn_dim).astype(BF16)


def _cast_specs(weights, layer, n_steps, step_of):
    in_specs, out_specs, shapes = [], [], []
    for w in weights:
        _, rows, cols = w.shape
        chunks = max(c for c in range(1, n_steps + 1) if rows % c == 0 and (rows // c) % BF16_SUBLANES == 0)
        chunk_of = lambda *idx, chunks=chunks: jnp.minimum(step_of(*idx), chunks - 1)
        in_specs.append(pl.BlockSpec((None, rows // chunks, cols),
                                     lambda *idx, chunk_of=chunk_of: (layer, chunk_of(*idx), 0)))
        out_specs.append(pl.BlockSpec((rows // chunks, cols), lambda *idx, chunk_of=chunk_of: (chunk_of(*idx), 0)))
        shapes.append(jax.ShapeDtypeStruct((rows, cols), BF16))
    return in_specs, out_specs, shapes


def _inproj(x2d, gain, w_in, layer, q_gain, k_gain, conv_w, later_weights, *, seq, attn_dim, conv_dim):
    n, d = x2d.shape
    tm = ROW_TILE
    n_steps = n // tm
    cols = w_in.shape[2]
    const = lambda i: (0, 0)
    row = lambda i: (i, 0)
    cast_in_specs, cast_out_specs, cast_shapes = _cast_specs(later_weights, layer, n_steps, lambda i: i)
    out_shapes = [jax.ShapeDtypeStruct((n, attn_dim), BF16)] * 3 + [jax.ShapeDtypeStruct((n, conv_dim), BF16)]
    return pl.pallas_call(
        functools.partial(_inproj_kernel, tiles_per_seq=seq // tm, attn_dim=attn_dim, conv_dim=conv_dim,
                          n_cast=len(later_weights)),
        grid=(n_steps,),
        in_specs=[
            pl.BlockSpec((tm, d), row),
            pl.BlockSpec((1, d), const),
            pl.BlockSpec((None, d, cols), lambda i: (layer, 0, 0), pipeline_mode=pl.Buffered(1)),
            pl.BlockSpec((1, attn_dim), const),
            pl.BlockSpec((1, attn_dim), const),
            pl.BlockSpec(conv_w.shape, const),
        ] + cast_in_specs,
        out_specs=[pl.BlockSpec((tm, attn_dim), row)] * 3 + [pl.BlockSpec((tm, conv_dim), row)] + cast_out_specs,
        out_shape=out_shapes + cast_shapes,
        scratch_shapes=[pltpu.VMEM((CONV_SHIFTS, SUBLANES, conv_dim), F32), pltpu.VMEM((d, cols), BF16)],
        compiler_params=pltpu.CompilerParams(
            dimension_semantics=("arbitrary",), vmem_limit_bytes=VMEM_LIMIT),
    )(x2d, gain, w_in, q_gain, k_gain, conv_w, *later_weights)


def _attn_kernel(q_ref, k_ref, v_ref, o_ref, acc_ref, carry_ref, min_ref):
    n_tiles = q_ref.shape[0] // SUB
    n_blocks = q_ref.shape[1] // LANES
    i = pl.program_id(1)
    rows = HEADS_PER_BLOCK * SUB
    assert n_tiles >= LOOKBACK_SUBS

    lane = lax.broadcasted_iota(jnp.int32, (SUB, LANES), 1)
    r = lax.broadcasted_iota(jnp.int32, (2 * SUB, 2 * SUB), 0) % SUB
    c = lax.broadcasted_iota(jnp.int32, (2 * SUB, 2 * SUB), 1)
    tri = jnp.where((c >= SUB) | (r >= c), 1.0, 0.0).astype(BF16)
    causal = (lax.broadcasted_iota(jnp.int32, (rows, SUB), 1)
              < lax.broadcasted_iota(jnp.int32, (rows, SUB), 0) % SUB)

    def span(jobs, diag, first):
        z_all, piece, cs_all = {}, {}, {}
        for t, start, nsub in jobs:
            trow = slice(t * SUB, (t + 1) * SUB)
            for p in range(n_blocks):
                if p % CS_BLOCKS == 0:
                    lhs_parts = []
                cols = slice(p * LANES, (p + 1) * LANES)
                q = q_ref[trow, cols]
                q2 = jnp.concatenate([jnp.where((lane // HEAD_DIM) == h, q, jnp.zeros_like(q))
                                      for h in range(HEADS_PER_BLOCK)], axis=0)
                z = lax.dot_general(q2, k_ref[pl.ds(start, nsub * SUB), cols], (((1,), (1,)), ((), ())),
                                    preferred_element_type=F32)
                z_all[t, p] = z
                for s in reversed(range(nsub)):
                    z_s = z[:, s * SUB:(s + 1) * SUB]
                    e = jnp.exp2(jnp.abs(z_s) * (-LOG2E))
                    sp = jnp.maximum(z_s, 0.0) + jnp.log(1.0 + e)
                    if diag and s == nsub - 1:
                        sp = jnp.where(causal, sp, 0.0)
                    hi = sp.astype(BF16)
                    lo = (sp - hi.astype(F32)).astype(BF16)
                    piece[t, p, s] = len(lhs_parts)
                    lhs_parts.append(jnp.concatenate([hi, lo], axis=1))
                if (p + 1) % CS_BLOCKS == 0 or p == n_blocks - 1:
                    cs_all[t, p // CS_BLOCKS] = jnp.dot(jnp.concatenate(lhs_parts, axis=0), tri,
                                                        preferred_element_type=F32)
        for t, start, nsub in jobs:
            trow = slice(t * SUB, (t + 1) * SUB)
            carries = []
            for p in range(n_blocks):
                cols = slice(p * LANES, (p + 1) * LANES)
                carry = None if first else carry_ref[t, p]
                a_cols = [None] * nsub
                for s in reversed(range(nsub)):
                    cs = cs_all[t, p // CS_BLOCKS][piece[t, p, s] * rows:(piece[t, p, s] + 1) * rows]
                    g = cs[:, :SUB] if carry is None else cs[:, :SUB] + carry
                    carry = cs[:, SUB:] if carry is None else carry + cs[:, SUB:]
                    a = jnp.exp2((z_all[t, p][:, s * SUB:(s + 1) * SUB] - g) * LOG2E)
                    if diag and s == nsub - 1:
                        a = jnp.where(causal, a, 0.0)
                    a_cols[s] = a.astype(BF16)
                carry_ref[t, p] = carry
                carries.append(carry)
                o2 = jnp.dot(jnp.concatenate(a_cols, axis=1), v_ref[pl.ds(start, nsub * SUB), cols],
                             preferred_element_type=F32)
                out = o2[0:SUB]
                for h in range(1, HEADS_PER_BLOCK):
                    out = jnp.where((lane // HEAD_DIM) == h, o2[h * SUB:(h + 1) * SUB], out)
                if first:
                    acc_ref[trow, cols] = out
                else:
                    acc_ref[trow, cols] += out
            while len(carries) > 1:
                odd = carries[-1:] if len(carries) % 2 else []
                carries = [jnp.minimum(a, b) for a, b in zip(carries[0::2], carries[1::2])] + odd
            m = carries[0]
            while m.shape[0] > SUBLANES:
                half = m.shape[0] // 2
                m = jnp.minimum(m[:half], m[half:])
            min_ref[t] = m

    @pl.when(i == 0)
    def _():
        span([(t, max(t - LOOKBACK_SUBS, 0) * SUB, min(t, LOOKBACK_SUBS) + 1) for t in range(n_tiles)], True, True)

    @pl.when(i > 0)
    def _():
        span([(t, pl.multiple_of((i * n_tiles + t - LOOKBACK_SUBS) * SUB, SUB), LOOKBACK_SUBS + 1)
              for t in range(n_tiles)], True, True)

    @pl.when(jnp.min(min_ref[...]) < SKIP_ABOVE)
    def _():
        for t in range(n_tiles):
            def unfinished(t=t):
                return (jnp.min(min_ref[t]) < SKIP_ABOVE).astype(jnp.int32)

            def cond(state):
                rem, go = state
                return jnp.logical_and(rem >= WALK_SUBS * SUB, go > 0)

            def body(state, t=t, unfinished=unfinished):
                rem, _ = state
                start = pl.multiple_of(rem - WALK_SUBS * SUB, SUB)
                span([(t, start, WALK_SUBS)], False, False)
                return start, unfinished()

            remaining = jnp.maximum(i * n_tiles + t - LOOKBACK_SUBS, 0) * SUB
            rem, go = lax.while_loop(cond, body, (remaining, unfinished()))
            for tail in range(1, WALK_SUBS):
                @pl.when(jnp.logical_and(rem == tail * SUB, go > 0))
                def _(t=t, tail=tail):
                    span([(t, 0, tail)], False, False)

    o_ref[...] = acc_ref[...].astype(o_ref.dtype)


def _attention(q, k, v, *, batch, seq):
    n, attn_dim = q.shape
    tq = ATTN_TILES * SUB
    nq = seq // tq
    return pl.pallas_call(
        _attn_kernel,
        grid=(batch, nq),
        in_specs=[
            pl.BlockSpec((tq, attn_dim), lambda b, i: (b * nq + i, 0)),
            pl.BlockSpec((seq, attn_dim), lambda b, i: (b, 0)),
            pl.BlockSpec((seq, attn_dim), lambda b, i: (b, 0)),
        ],
        out_specs=pl.BlockSpec((tq, attn_dim), lambda b, i: (b * nq + i, 0)),
        out_shape=jax.ShapeDtypeStruct((n, attn_dim), BF16),
        scratch_shapes=[pltpu.VMEM((tq, attn_dim), F32),
                        pltpu.VMEM((ATTN_TILES, attn_dim // LANES, HEADS_PER_BLOCK * SUB, SUB), F32),
                        pltpu.VMEM((ATTN_TILES, SUBLANES, SUB), F32)],
        compiler_params=pltpu.CompilerParams(
            dimension_semantics=("arbitrary", "arbitrary"), vmem_limit_bytes=VMEM_LIMIT),
    )(q, k, v)


def _ffn_kernel(x_ref, attn_ref, conv_ref, wo_ref, gain_ref, wg_ref, wu_ref, wd_ref, o_ref, act_ref):
    attn_dim = attn_ref.shape[1]
    d_ff = wg_ref.shape[1]
    mix = (jnp.dot(attn_ref[...], wo_ref[0:attn_dim, :], preferred_element_type=F32)
           + jnp.dot(conv_ref[...], wo_ref[attn_dim:, :], preferred_element_type=F32))
    x1 = x_ref[...] + mix
    h = ((x1 * _rms_scale(x1)) * gain_ref[...]).astype(BF16)
    for c0 in range(0, d_ff, FF_CHUNK):
        g = jnp.dot(h, wg_ref[:, c0:c0 + FF_CHUNK], preferred_element_type=F32)
        u = jnp.dot(h, wu_ref[:, c0:c0 + FF_CHUNK], preferred_element_type=F32)
        act_ref[:, c0:c0 + FF_CHUNK] = ((g * jax.nn.sigmoid(g)) * u).astype(BF16)
    o_ref[...] = x1 + jnp.dot(act_ref[...], wd_ref[...], preferred_element_type=F32)


def _ffn(x2d, attn, conv, w_out, gain, w_gate, w_up, w_down):
    n, d = x2d.shape
    tm = FFN_ROW_TILE
    d_ff = w_gate.shape[1]
    const = lambda i: (0, 0)
    row = lambda i: (i, 0)
    resident = functools.partial(pl.BlockSpec, index_map=const, pipeline_mode=pl.Buffered(1))
    return pl.pallas_call(
        _ffn_kernel,
        grid=(n // tm,),
        in_specs=[
            pl.BlockSpec((tm, d), row),
            pl.BlockSpec((tm, attn.shape[1]), row),
            pl.BlockSpec((tm, conv.shape[1]), row),
            resident(w_out.shape),
            pl.BlockSpec((1, d), const),
            resident(w_gate.shape),
            resident(w_up.shape),
            resident(w_down.shape),
        ],
        out_specs=pl.BlockSpec((tm, d), row),
        out_shape=jax.ShapeDtypeStruct((n, d), F32),
        scratch_shapes=[pltpu.VMEM((tm, d_ff), BF16)],
        compiler_params=pltpu.CompilerParams(
            dimension_semantics=("arbitrary",), vmem_limit_bytes=VMEM_LIMIT),
    )(x2d, attn, conv, w_out, gain, w_gate, w_up, w_down)


def kernel(x, norm_mix, w_in, q_norm, k_norm, conv_w, w_out, norm_ffn, w_gate, w_up, w_down):
    batch, seq, d = x.shape
    depth = w_in.shape[0]
    conv_dim = conv_w.shape[2]
    attn_dim = w_out.shape[1] - conv_dim
    n_heads = attn_dim // HEAD_DIM
    assert seq % ROW_TILE == 0 and seq % (ATTN_TILES * SUB) == 0 and attn_dim % MXU_DIM == 0
    assert (batch * seq) % FFN_ROW_TILE == 0
    assert w_gate.shape[2] % FF_CHUNK == 0 and conv_w.shape[1] == 3

    xf = x.reshape(batch * seq, d)
    for l in range(depth):
        q, k, v, conv, wo, wg, wu, wd = _inproj(
            xf, norm_mix[l][None, :], w_in, l,
            jnp.tile(q_norm[l], n_heads)[None, :], jnp.tile(k_norm[l], n_heads)[None, :], conv_w[l],
            (w_out, w_gate, w_up, w_down), seq=seq, attn_dim=attn_dim, conv_dim=conv_dim)
        attn = _attention(q, k, v, batch=batch, seq=seq)
        xf = _ffn(xf, attn, conv, wo, norm_ffn[l][None, :], wg, wu, wd)
    return xf.reshape(batch, seq, d)
```

```python
import functools

import jax
import jax.numpy as jnp
from jax import lax
from jax.experimental import pallas as pl
from jax.experimental.pallas import tpu as pltpu

F32 = jnp.float32
BF16 = jnp.bfloat16

HEAD_DIM = 64
EPS = 1e-6
LANES = 128
SUBLANES = 8
BF16_SUBLANES = 16
W_CAST_ROWS = 128
MXU_DIM = 256
HEADS_PER_BLOCK = LANES // HEAD_DIM
SUB = 128
LOG2E = 1.4426950408889634
LN2 = 0.6931471805599453
SKIP_ABOVE = 104.0

ROW_TILE = 512
FFN_ROW_TILE = 1024
ATTN_TILES = 4
NEAR_SUBS = 1
FAR_ROWS = 32
WALK_SUBS = 2
FF_CHUNK = 256
CONV_SHIFTS = 2
VMEM_LIMIT = 56 * 1024 * 1024


def _rms_scale(x):
    return lax.rsqrt(jnp.mean(x * x, axis=-1, keepdims=True) + EPS)


def _inproj_kernel(x_ref, gain_ref, w_ref, qg_ref, kg_ref, cw_ref, *rest,
                   tiles_per_seq, attn_dim, conv_dim, n_cast):
    cast_in, (q_ref, k_ref, v_ref, c_ref) = rest[:n_cast], rest[n_cast:n_cast + 4]
    cast_out, (halo_ref, wbf_ref) = rest[n_cast + 4:2 * n_cast + 4], rest[2 * n_cast + 4:]
    tm = x_ref.shape[0]
    i = pl.program_id(0)

    @pl.when(i == 0)
    def _():
        def cast_rows(c, _):
            rows = pl.ds(pl.multiple_of(c * W_CAST_ROWS, W_CAST_ROWS), W_CAST_ROWS)
            wbf_ref[rows, :] = w_ref[rows, :].astype(BF16)
            return 0
        lax.fori_loop(0, w_ref.shape[0] // W_CAST_ROWS, cast_rows, 0)
        halo_ref[...] = jnp.zeros_like(halo_ref)

    for src, dst in zip(cast_in, cast_out):
        dst[...] = src[...].astype(BF16)

    x = x_ref[...]
    h = ((x * _rms_scale(x)) * gain_ref[...]).astype(BF16)

    def proj(c0, width):
        return jnp.dot(h, wbf_ref[:, c0:c0 + width], preferred_element_type=F32)

    r = lax.broadcasted_iota(jnp.int32, (MXU_DIM, MXU_DIM), 0) // HEAD_DIM
    c = lax.broadcasted_iota(jnp.int32, (MXU_DIM, MXU_DIM), 1) // HEAD_DIM
    seg = jnp.where(r == c, 1.0, 0.0).astype(BF16)

    def head_norm(p, gain):
        p2 = (p * p).astype(BF16)
        ssq = jnp.concatenate(
            [jnp.dot(p2[:, s:s + MXU_DIM], seg, preferred_element_type=F32)
             for s in range(0, attn_dim, MXU_DIM)], axis=1)
        return (p * lax.rsqrt(ssq * (1.0 / HEAD_DIM) + EPS)) * gain

    q_ref[...] = (head_norm(proj(0, attn_dim), qg_ref[...]) * (HEAD_DIM ** -0.5)).astype(BF16)
    k_ref[...] = head_norm(proj(attn_dim, attn_dim), kg_ref[...]).astype(BF16)

    cb = proj(3 * attn_dim, conv_dim)
    hh = proj(3 * attn_dim + conv_dim, conv_dim) * proj(3 * attn_dim + 2 * conv_dim, conv_dim)
    row = lax.broadcasted_iota(jnp.int32, (SUBLANES, conv_dim), 0)
    seq_start = i % tiles_per_seq == 0

    def shift_down(a, j):
        prev_last = jnp.where(seq_start, 0.0, halo_ref[j, SUBLANES - 1:SUBLANES, :])
        halo_ref[j] = a[tm - SUBLANES:tm, :]
        rolled = pltpu.roll(a, 1, 0)
        head = jnp.where(row == 0, jnp.broadcast_to(prev_last, (SUBLANES, conv_dim)), rolled[0:SUBLANES])
        return jnp.concatenate([head, rolled[SUBLANES:]], axis=0)

    y = cw_ref[2:3, :] * hh + shift_down(cw_ref[1:2, :] * hh + shift_down(cw_ref[0:1, :] * hh, 0), 1)
    c_ref[...] = (cb * y).astype(BF16)

    v_ref[...] = proj(2 * attn_dim, attn_dim).astype(BF16)


def _cast_specs(weights, layer, n_steps, step_of):
    in_specs, out_specs, shapes = [], [], []
    for w in weights:
        _, rows, cols = w.shape
        chunks = max(c for c in range(1, n_steps + 1) if rows % c == 0 and (rows // c) % BF16_SUBLANES == 0)
        chunk_of = lambda *idx, chunks=chunks: jnp.minimum(step_of(*idx), chunks - 1)
        in_specs.append(pl.BlockSpec((None, rows // chunks, cols),
                                     lambda *idx, chunk_of=chunk_of: (layer, chunk_of(*idx), 0)))
        out_specs.append(pl.BlockSpec((rows // chunks, cols), lambda *idx, chunk_of=chunk_of: (chunk_of(*idx), 0)))
        shapes.append(jax.ShapeDtypeStruct((rows, cols), BF16))
    return in_specs, out_specs, shapes


def _inproj(x2d, gain, w_in, layer, q_gain, k_gain, conv_w, later_weights, *, seq, attn_dim, conv_dim):
    n, d = x2d.shape
    tm = ROW_TILE
    n_steps = n // tm
    cols = w_in.shape[2]
    const = lambda i: (0, 0)
    row = lambda i: (i, 0)
    cast_in_specs, cast_out_specs, cast_shapes = _cast_specs(later_weights, layer, n_steps, lambda i: i)
    out_shapes = [jax.ShapeDtypeStruct((n, attn_dim), BF16)] * 3 + [jax.ShapeDtypeStruct((n, conv_dim), BF16)]
    return pl.pallas_call(
        functools.partial(_inproj_kernel, tiles_per_seq=seq // tm, attn_dim=attn_dim, conv_dim=conv_dim,
                          n_cast=len(later_weights)),
        grid=(n_steps,),
        in_specs=[
            pl.BlockSpec((tm, d), row),
            pl.BlockSpec((1, d), const),
            pl.BlockSpec((None, d, cols), lambda i: (layer, 0, 0), pipeline_mode=pl.Buffered(1)),
            pl.BlockSpec((1, attn_dim), const),
            pl.BlockSpec((1, attn_dim), const),
            pl.BlockSpec(conv_w.shape, const),
        ] + cast_in_specs,
        out_specs=[pl.BlockSpec((tm, attn_dim), row)] * 3 + [pl.BlockSpec((tm, conv_dim), row)] + cast_out_specs,
        out_shape=out_shapes + cast_shapes,
        scratch_shapes=[pltpu.VMEM((CONV_SHIFTS, SUBLANES, conv_dim), F32), pltpu.VMEM((d, cols), BF16)],
        compiler_params=pltpu.CompilerParams(
            dimension_semantics=("arbitrary",), vmem_limit_bytes=VMEM_LIMIT),
    )(x2d, gain, w_in, q_gain, k_gain, conv_w, *later_weights)


def _min_to_vreg(arrays):
    chunks = [a[r:r + SUBLANES] for a in arrays for r in range(0, a.shape[0], SUBLANES)]
    while len(chunks) > 1:
        odd = chunks[-1:] if len(chunks) % 2 else []
        chunks = [jnp.minimum(a, b) for a, b in zip(chunks[0::2], chunks[1::2])] + odd
    return chunks[0]


def _attn_kernel(q_ref, k_ref, v_ref, o_ref, acc_ref, carry_ref, min_ref):
    n_tiles = q_ref.shape[0] // SUB
    n_blocks = q_ref.shape[1] // LANES
    i = pl.program_id(1)
    assert n_tiles >= NEAR_SUBS + 1

    r = lax.broadcasted_iota(jnp.int32, (2 * SUB, 2 * SUB), 0) % SUB
    c = lax.broadcasted_iota(jnp.int32, (2 * SUB, 2 * SUB), 1)
    tri = jnp.where((c >= SUB) | (r >= c), 1.0, 0.0).astype(BF16)
    causal = (lax.broadcasted_iota(jnp.int32, (HEADS_PER_BLOCK * SUB, SUB), 1)
              < lax.broadcasted_iota(jnp.int32, (HEADS_PER_BLOCK * SUB, SUB), 0) % SUB)

    def head_rows(row0, nrows):
        return [slice(h * SUB + row0, h * SUB + row0 + nrows) for h in range(HEADS_PER_BLOCK)]

    def softplus_split(z_s, masked):
        e = jnp.exp2(jnp.abs(z_s) * (-LOG2E))
        sp = jnp.maximum(z_s, 0.0) + jnp.log(1.0 + e)
        if masked:
            sp = jnp.where(causal[:z_s.shape[0]], sp, 0.0)
        hi = sp.astype(BF16)
        lo = (sp - hi.astype(F32)).astype(BF16)
        return jnp.concatenate([hi, lo], axis=1)

    def span(jobs, whole_tiles):
        z_all, cs_all = {}, {}
        for j, (t, row0, nrows, start, nsub, diag, first) in enumerate(jobs):
            assert not diag or (row0 == 0 and nrows == SUB)
            qrows = slice(t * SUB + row0, t * SUB + row0 + nrows)
            lane = lax.broadcasted_iota(jnp.int32, (nrows, LANES), 1)
            for p in range(n_blocks):
                cols = slice(p * LANES, (p + 1) * LANES)
                q = q_ref[qrows, cols]
                q2 = jnp.concatenate([jnp.where((lane // HEAD_DIM) == h, q, jnp.zeros_like(q))
                                      for h in range(HEADS_PER_BLOCK)], axis=0)
                z = lax.dot_general(q2, k_ref[pl.ds(start, nsub * SUB), cols], (((1,), (1,)), ((), ())),
                                    preferred_element_type=F32)
                z_all[j, p] = z
                lhs_parts = [softplus_split(z[:, s * SUB:(s + 1) * SUB], diag and s == nsub - 1)
                             for s in reversed(range(nsub))]
                cs_all[j, p] = jnp.dot(jnp.concatenate(lhs_parts, axis=0), tri, preferred_element_type=F32)
        latest = {}
        for j, (t, row0, nrows, start, nsub, diag, first) in enumerate(jobs):
            qrows = slice(t * SUB + row0, t * SUB + row0 + nrows)
            lane = lax.broadcasted_iota(jnp.int32, (nrows, LANES), 1)
            rows2 = HEADS_PER_BLOCK * nrows
            for p in range(n_blocks):
                cols = slice(p * LANES, (p + 1) * LANES)
                carry = None if first else jnp.concatenate([carry_ref[t, p, rs] for rs in head_rows(row0, nrows)], axis=0)
                a_cols = [None] * nsub
                for idx, s in enumerate(reversed(range(nsub))):
                    cs = cs_all[j, p][idx * rows2:(idx + 1) * rows2]
                    g = cs[:, :SUB] if carry is None else cs[:, :SUB] + carry
                    carry = cs[:, SUB:] if carry is None else carry + cs[:, SUB:]
                    a = jnp.exp2((z_all[j, p][:, s * SUB:(s + 1) * SUB] - g) * LOG2E)
                    if diag and s == nsub - 1:
                        a = jnp.where(causal, a, 0.0)
                    a_cols[s] = a.astype(BF16)
                for h, rs in enumerate(head_rows(row0, nrows)):
                    carry_ref[t, p, rs] = carry[h * nrows:(h + 1) * nrows]
                latest.setdefault(t, {}).setdefault(p, []).append((row0, nrows, carry))
                o2 = jnp.dot(jnp.concatenate(a_cols, axis=1), v_ref[pl.ds(start, nsub * SUB), cols],
                             preferred_element_type=F32)
                out = o2[0:nrows]
                for h in range(1, HEADS_PER_BLOCK):
                    out = jnp.where((lane // HEAD_DIM) == h, o2[h * nrows:(h + 1) * nrows], out)
                if first:
                    acc_ref[qrows, cols] = out
                else:
                    acc_ref[qrows, cols] += out
        for t, by_block in latest.items():
            if whole_tiles:
                parts = []
                for entries in by_block.values():
                    covered = []
                    for row0, nrows, carry in reversed(entries):
                        lo_hi = [(row0, row0 + nrows)]
                        for c0, c1 in covered:
                            lo_hi = [piece for a, b in lo_hi
                                     for piece in ((a, min(b, c0)), (max(a, c1), b)) if piece[0] < piece[1]]
                        for a, b in lo_hi:
                            parts += [carry[h * nrows + a - row0:h * nrows + b - row0]
                                      for h in range(HEADS_PER_BLOCK)]
                        covered.append((row0, row0 + nrows))
                    assert sorted(covered)[0][0] == 0 and max(c1 for _, c1 in covered) == SUB
                min_ref[t] = _min_to_vreg(parts)
            else:
                min_ref[t] = _min_to_vreg([carry_ref[t, p] for p in range(n_blocks)])

    def window(tiles):
        n_keys = (NEAR_SUBS + 2) * SUB
        far = [slice(h * SUB, h * SUB + FAR_ROWS) for h in range(HEADS_PER_BLOCK)]
        rest = [slice(h * SUB + FAR_ROWS, (h + 1) * SUB) for h in range(HEADS_PER_BLOCK)]
        lane = lax.broadcasted_iota(jnp.int32, (SUB, LANES), 1)
        lane_far = lax.broadcasted_iota(jnp.int32, (FAR_ROWS, LANES), 1)
        ctx = {}
        for t, tile in tiles:
            start = pl.multiple_of((tile - NEAR_SUBS - 1) * SUB, SUB)
            for p in range(n_blocks):
                cols = slice(p * LANES, (p + 1) * LANES)
                q = q_ref[t * SUB:(t + 1) * SUB, cols]
                q2 = jnp.concatenate([jnp.where((lane // HEAD_DIM) == h, q, jnp.zeros_like(q))
                                      for h in range(HEADS_PER_BLOCK)], axis=0)
                z = lax.dot_general(q2, k_ref[pl.ds(start, n_keys), cols], (((1,), (1,)), ((), ())),
                                    preferred_element_type=F32)
                z_far = jnp.concatenate([z[rs, 0:SUB] for rs in far], axis=0)
                lhs = [softplus_split(z[:, s * SUB:(s + 1) * SUB], s == NEAR_SUBS + 1)
                       for s in range(NEAR_SUBS + 1, 0, -1)] + [softplus_split(z_far, False)]
                cs = jnp.dot(jnp.concatenate(lhs, axis=0), tri, preferred_element_type=F32)
                ctx[t, p] = (start, z, z_far, cs)
        for t, tile in tiles:
            parts = []
            for p in range(n_blocks):
                cols = slice(p * LANES, (p + 1) * LANES)
                start, z, z_far, cs = ctx[t, p]
                carry, a_cols = None, []
                for idx, s in enumerate(range(NEAR_SUBS + 1, 0, -1)):
                    blk = cs[idx * 2 * SUB:(idx + 1) * 2 * SUB]
                    g = blk[:, :SUB] if carry is None else blk[:, :SUB] + carry
                    carry = blk[:, SUB:] if carry is None else carry + blk[:, SUB:]
                    a = jnp.exp2((z[:, s * SUB:(s + 1) * SUB] - g) * LOG2E)
                    if s == NEAR_SUBS + 1:
                        a = jnp.where(causal, a, 0.0)
                    a_cols.insert(0, a.astype(BF16))
                blk = cs[(NEAR_SUBS + 1) * 2 * SUB:]
                carry_far = jnp.concatenate([carry[rs] for rs in far], axis=0)
                a_far = jnp.exp2((z_far - (blk[:, :SUB] + carry_far)) * LOG2E).astype(BF16)
                carry_far = carry_far + blk[:, SUB:]
                for h in range(HEADS_PER_BLOCK):
                    carry_ref[t, p, far[h]] = carry_far[h * FAR_ROWS:(h + 1) * FAR_ROWS]
                    carry_ref[t, p, rest[h]] = carry[rest[h]]
                    parts.append(carry[rest[h]])
                parts.append(carry_far)
                o2 = jnp.dot(jnp.concatenate(a_cols, axis=1), v_ref[pl.ds(start + SUB, (NEAR_SUBS + 1) * SUB), cols],
                             preferred_element_type=F32)
                o_far = jnp.dot(a_far, v_ref[pl.ds(start, SUB), cols], preferred_element_type=F32)
                out, out_far = o2[0:SUB], o_far[0:FAR_ROWS]
                for h in range(1, HEADS_PER_BLOCK):
                    out = jnp.where((lane // HEAD_DIM) == h, o2[h * SUB:(h + 1) * SUB], out)
                    out_far = jnp.where((lane_far // HEAD_DIM) == h,
                                        o_far[h * FAR_ROWS:(h + 1) * FAR_ROWS], out_far)
                acc_ref[t * SUB:t * SUB + FAR_ROWS, cols] = out[:FAR_ROWS] + out_far
                acc_ref[t * SUB + FAR_ROWS:(t + 1) * SUB, cols] = out[FAR_ROWS:]
            min_ref[t] = _min_to_vreg(parts)

    @pl.when(i == 0)
    def _():
        early = [t for t in range(n_tiles) if t <= NEAR_SUBS]
        span([(t, 0, SUB, 0, t + 1, True, True) for t in early], True)
        window([(t, t) for t in range(n_tiles) if t > NEAR_SUBS])

    @pl.when(i > 0)
    def _():
        window([(t, i * n_tiles + t) for t in range(n_tiles)])

    @pl.when(jnp.min(min_ref[...]) < SKIP_ABOVE)
    def _():
        for t in range(n_tiles):
            tile = i * n_tiles + t
            if FAR_ROWS < SUB:
                @pl.when(jnp.logical_and(tile > NEAR_SUBS, jnp.min(min_ref[t]) < SKIP_ABOVE))
                def _(t=t, tile=tile):
                    start = pl.multiple_of((tile - NEAR_SUBS - 1) * SUB, SUB)
                    span([(t, FAR_ROWS, SUB - FAR_ROWS, start, 1, False, False)], False)

            def unfinished(t=t):
                return (jnp.min(min_ref[t]) < SKIP_ABOVE).astype(jnp.int32)

            def cond(state):
                rem, go = state
                return jnp.logical_and(rem >= WALK_SUBS * SUB, go > 0)

            def body(state, t=t, unfinished=unfinished):
                rem, _ = state
                start = pl.multiple_of(rem - WALK_SUBS * SUB, SUB)
                span([(t, 0, SUB, start, WALK_SUBS, False, False)], False)
                return start, unfinished()

            remaining = jnp.maximum(tile - NEAR_SUBS - 1, 0) * SUB
            rem, go = lax.while_loop(cond, body, (remaining, unfinished()))
            for tail in range(1, WALK_SUBS):
                @pl.when(jnp.logical_and(rem == tail * SUB, go > 0))
                def _(t=t, tail=tail):
                    span([(t, 0, SUB, 0, tail, False, False)], False)

    o_ref[...] = acc_ref[...].astype(o_ref.dtype)


def _attention(q, k, v, *, batch, seq):
    n, attn_dim = q.shape
    tq = ATTN_TILES * SUB
    nq = seq // tq
    return pl.pallas_call(
        _attn_kernel,
        grid=(batch, nq),
        in_specs=[
            pl.BlockSpec((tq, attn_dim), lambda b, i: (b * nq + i, 0)),
            pl.BlockSpec((seq, attn_dim), lambda b, i: (b, 0)),
            pl.BlockSpec((seq, attn_dim), lambda b, i: (b, 0)),
        ],
        out_specs=pl.BlockSpec((tq, attn_dim), lambda b, i: (b * nq + i, 0)),
        out_shape=jax.ShapeDtypeStruct((n, attn_dim), BF16),
        scratch_shapes=[pltpu.VMEM((tq, attn_dim), F32),
                        pltpu.VMEM((ATTN_TILES, attn_dim // LANES, HEADS_PER_BLOCK * SUB, SUB), F32),
                        pltpu.VMEM((ATTN_TILES, SUBLANES, SUB), F32)],
        compiler_params=pltpu.CompilerParams(
            dimension_semantics=("arbitrary", "arbitrary"), vmem_limit_bytes=VMEM_LIMIT),
    )(q, k, v)


def _ffn_kernel(x_ref, attn_ref, conv_ref, wo_ref, gain_ref, wg_ref, wu_ref, wd_ref, o_ref, act_ref):
    attn_dim = attn_ref.shape[1]
    d_ff = wg_ref.shape[1]
    mix = (jnp.dot(attn_ref[...], wo_ref[0:attn_dim, :], preferred_element_type=F32)
           + jnp.dot(conv_ref[...], wo_ref[attn_dim:, :], preferred_element_type=F32))
    x1 = x_ref[...] + mix
    h = ((x1 * _rms_scale(x1)) * gain_ref[...]).astype(BF16)
    for c0 in range(0, d_ff, FF_CHUNK):
        g = jnp.dot(h, wg_ref[:, c0:c0 + FF_CHUNK], preferred_element_type=F32)
        u = jnp.dot(h, wu_ref[:, c0:c0 + FF_CHUNK], preferred_element_type=F32)
        act_ref[:, c0:c0 + FF_CHUNK] = ((g * jax.nn.sigmoid(g)) * u).astype(BF16)
    o_ref[...] = x1 + jnp.dot(act_ref[...], wd_ref[...], preferred_element_type=F32)


def _ffn(x2d, attn, conv, w_out, gain, w_gate, w_up, w_down):
    n, d = x2d.shape
    tm = FFN_ROW_TILE
    d_ff = w_gate.shape[1]
    const = lambda i: (0, 0)
    row = lambda i: (i, 0)
    resident = functools.partial(pl.BlockSpec, index_map=const, pipeline_mode=pl.Buffered(1))
    return pl.pallas_call(
        _ffn_kernel,
        grid=(n // tm,),
        in_specs=[
            pl.BlockSpec((tm, d), row),
            pl.BlockSpec((tm, attn.shape[1]), row),
            pl.BlockSpec((tm, conv.shape[1]), row),
            resident(w_out.shape),
            pl.BlockSpec((1, d), const),
            resident(w_gate.shape),
            resident(w_up.shape),
            resident(w_down.shape),
        ],
        out_specs=pl.BlockSpec((tm, d), row),
        out_shape=jax.ShapeDtypeStruct((n, d), F32),
        scratch_shapes=[pltpu.VMEM((tm, d_ff), BF16)],
        compiler_params=pltpu.CompilerParams(
            dimension_semantics=("arbitrary",), vmem_limit_bytes=VMEM_LIMIT),
    )(x2d, attn, conv, w_out, gain, w_gate, w_up, w_down)


def kernel(x, norm_mix, w_in, q_norm, k_norm, conv_w, w_out, norm_ffn, w_gate, w_up, w_down):
    batch, seq, d = x.shape
    depth = w_in.shape[0]
    conv_dim = conv_w.shape[2]
    attn_dim = w_out.shape[1] - conv_dim
    n_heads = attn_dim // HEAD_DIM
    assert seq % ROW_TILE == 0 and seq % (ATTN_TILES * SUB) == 0 and attn_dim % MXU_DIM == 0
    assert (batch * seq) % FFN_ROW_TILE == 0
    assert w_gate.shape[2] % FF_CHUNK == 0 and conv_w.shape[1] == CONV_SHIFTS + 1

    xf = x.reshape(batch * seq, d)
    for l in range(depth):
        q, k, v, conv, wo, wg, wu, wd = _inproj(
            xf, norm_mix[l][None, :], w_in, l,
            jnp.tile(q_norm[l], n_heads)[None, :], jnp.tile(k_norm[l], n_heads)[None, :], conv_w[l],
            (w_out, w_gate, w_up, w_down), seq=seq, attn_dim=attn_dim, conv_dim=conv_dim)
        attn = _attention(q, k, v, batch=batch, seq=seq)
        xf = _ffn(xf, attn, conv, wo, norm_ffn[l][None, :], wg, wu, wd)
    return xf.reshape(batch, seq, d)
```

```python
import functools

import jax
import jax.numpy as jnp
from jax import lax
from jax.experimental import pallas as pl
from jax.experimental.pallas import tpu as pltpu

F32 = jnp.float32
BF16 = jnp.bfloat16

HEAD_DIM = 64
EPS = 1e-6
LANES = 128
SUBLANES = 8
BF16_SUBLANES = 16
W_CAST_ROWS = 128
MXU_DIM = 256
HEADS_PER_BLOCK = LANES // HEAD_DIM
SUB = 128
LOG2E = 1.4426950408889634
LN2 = 0.6931471805599453
SKIP_ABOVE = 104.0

ROW_TILE = 512
FFN_ROW_TILE = 1024
NEAR_SUBS = 1
FAR_ROWS = 32
WALK_SUBS = 2
FF_CHUNK = 256
CONV_SHIFTS = 2
VMEM_LIMIT = 56 * 1024 * 1024


def _rms_scale(x):
    return lax.rsqrt(jnp.mean(x * x, axis=-1, keepdims=True) + EPS)


def _cast_kernel(src_ref, dst_ref):
    dst_ref[...] = src_ref[...].astype(BF16)


def _cast_layer(w, layer):
    _, rows, cols = w.shape
    return pl.pallas_call(
        _cast_kernel,
        grid=(rows // W_CAST_ROWS,),
        in_specs=[pl.BlockSpec((None, W_CAST_ROWS, cols), lambda i: (layer, i, 0))],
        out_specs=pl.BlockSpec((W_CAST_ROWS, cols), lambda i: (i, 0)),
        out_shape=jax.ShapeDtypeStruct((rows, cols), BF16),
        compiler_params=pltpu.CompilerParams(dimension_semantics=("arbitrary",)),
    )(w)


def _cast_specs(weights, layer, n_steps, step_of):
    in_specs, out_specs, shapes = [], [], []
    for w in weights:
        _, rows, cols = w.shape
        chunks = max(c for c in range(1, n_steps + 1) if rows % c == 0 and (rows // c) % BF16_SUBLANES == 0)
        chunk_of = lambda *idx, chunks=chunks: jnp.minimum(step_of(*idx), chunks - 1)
        in_specs.append(pl.BlockSpec((None, rows // chunks, cols),
                                     lambda *idx, chunk_of=chunk_of: (layer, chunk_of(*idx), 0)))
        out_specs.append(pl.BlockSpec((rows // chunks, cols), lambda *idx, chunk_of=chunk_of: (chunk_of(*idx), 0)))
        shapes.append(jax.ShapeDtypeStruct((rows, cols), BF16))
    return in_specs, out_specs, shapes


def _min_to_vreg(arrays):
    chunks = [a[r:r + SUBLANES] for a in arrays for r in range(0, a.shape[0], SUBLANES)]
    while len(chunks) > 1:
        odd = chunks[-1:] if len(chunks) % 2 else []
        chunks = [jnp.minimum(a, b) for a, b in zip(chunks[0::2], chunks[1::2])] + odd
    return chunks[0]


def _mix_kernel(x_ref, gain_ref, wbf_ref, qg_ref, kg_ref, cw_ref, *rest, attn_dim, conv_dim, n_cast):
    cast_in, (o_ref, c_ref), cast_out = rest[:n_cast], rest[n_cast:n_cast + 2], rest[n_cast + 2:2 * n_cast + 2]
    halo_ref, q_s, k_s, v_s, acc_ref, carry_ref, min_ref = rest[2 * n_cast + 2:]
    tm = x_ref.shape[0]
    n_tiles = tm // SUB
    n_blocks = attn_dim // LANES
    j = pl.program_id(1)
    last = pl.num_programs(1) - 1
    group = j - 1
    slot = (j + 1) % 2
    assert n_tiles >= NEAR_SUBS + 1

    for src, dst in zip(cast_in, cast_out):
        dst[...] = src[...].astype(BF16)

    r = lax.broadcasted_iota(jnp.int32, (2 * SUB, 2 * SUB), 0) % SUB
    c = lax.broadcasted_iota(jnp.int32, (2 * SUB, 2 * SUB), 1)
    tri = jnp.where((c >= SUB) | (r >= c), 1.0, 0.0).astype(BF16)
    causal = (lax.broadcasted_iota(jnp.int32, (HEADS_PER_BLOCK * SUB, SUB), 1)
              < lax.broadcasted_iota(jnp.int32, (HEADS_PER_BLOCK * SUB, SUB), 0) % SUB)

    def project_stages(seq_start):
        rows_now = pl.ds(pl.multiple_of(j * tm, tm), tm)
        st = {}

        def proj(c0, width):
            return jnp.dot(st["h"], wbf_ref[:, c0:c0 + width], preferred_element_type=F32)

        def norm_x():
            x = x_ref[...]
            st["h"] = ((x * _rms_scale(x)) * gain_ref[...]).astype(BF16)

        def head_norm(p, gain):
            r = lax.broadcasted_iota(jnp.int32, (MXU_DIM, MXU_DIM), 0) // HEAD_DIM
            c = lax.broadcasted_iota(jnp.int32, (MXU_DIM, MXU_DIM), 1) // HEAD_DIM
            seg = jnp.where(r == c, 1.0, 0.0).astype(BF16)
            p2 = (p * p).astype(BF16)
            ssq = jnp.concatenate(
                [jnp.dot(p2[:, s:s + MXU_DIM], seg, preferred_element_type=F32)
                 for s in range(0, attn_dim, MXU_DIM)], axis=1)
            return (p * lax.rsqrt(ssq * (1.0 / HEAD_DIM) + EPS)) * gain

        def conv_gate():
            st["cb"] = proj(3 * attn_dim, conv_dim)

        def conv_input():
            st["hh"] = proj(3 * attn_dim + conv_dim, conv_dim) * proj(3 * attn_dim + 2 * conv_dim, conv_dim)

        def conv_out():
            hh = st["hh"]
            row = lax.broadcasted_iota(jnp.int32, (SUBLANES, conv_dim), 0)

            def shift_down(a, n):
                prev_last = (jnp.zeros((1, conv_dim), F32) if seq_start
                             else halo_ref[n, SUBLANES - 1:SUBLANES, :])
                halo_ref[n] = a[tm - SUBLANES:tm, :]
                rolled = pltpu.roll(a, 1, 0)
                head = jnp.where(row == 0, jnp.broadcast_to(prev_last, (SUBLANES, conv_dim)), rolled[0:SUBLANES])
                return jnp.concatenate([head, rolled[SUBLANES:]], axis=0)

            y = cw_ref[2:3, :] * hh + shift_down(cw_ref[1:2, :] * hh + shift_down(cw_ref[0:1, :] * hh, 0), 1)
            c_ref[...] = (st["cb"] * y).astype(BF16)

        def q_out():
            q_s[j % 2] = (head_norm(proj(0, attn_dim), qg_ref[...]) * (HEAD_DIM ** -0.5)).astype(BF16)

        def k_out():
            k_s[rows_now, :] = head_norm(proj(attn_dim, attn_dim), kg_ref[...]).astype(BF16)

        def v_proj():
            st["v"] = proj(2 * attn_dim, attn_dim).astype(BF16)

        def v_out():
            v_s[rows_now, :] = st["v"]

        return [norm_x, conv_gate, conv_input, conv_out], [q_out, k_out, v_proj], v_out

    def head_rows(row0, nrows):
        return [slice(h * SUB + row0, h * SUB + row0 + nrows) for h in range(HEADS_PER_BLOCK)]

    def softplus_split(z_s, masked):
        e = jnp.exp2(jnp.abs(z_s) * (-LOG2E))
        sp = jnp.maximum(z_s, 0.0) + jnp.log(1.0 + e)
        if masked:
            sp = jnp.where(causal[:z_s.shape[0]], sp, 0.0)
        hi = sp.astype(BF16)
        lo = (sp - hi.astype(F32)).astype(BF16)
        return jnp.concatenate([hi, lo], axis=1)

    def span(jobs, whole_tiles):
        z_all, cs_all = {}, {}
        for n, (t, row0, nrows, start, nsub, diag, first) in enumerate(jobs):
            assert not diag or (row0 == 0 and nrows == SUB)
            qrows = slice(t * SUB + row0, t * SUB + row0 + nrows)
            lane = lax.broadcasted_iota(jnp.int32, (nrows, LANES), 1)
            for p in range(n_blocks):
                cols = slice(p * LANES, (p + 1) * LANES)
                q = q_s[slot, qrows, cols]
                q2 = jnp.concatenate([jnp.where((lane // HEAD_DIM) == h, q, jnp.zeros_like(q))
                                      for h in range(HEADS_PER_BLOCK)], axis=0)
                z = lax.dot_general(q2, k_s[pl.ds(start, nsub * SUB), cols], (((1,), (1,)), ((), ())),
                                    preferred_element_type=F32)
                z_all[n, p] = z
                lhs_parts = [softplus_split(z[:, s * SUB:(s + 1) * SUB], diag and s == nsub - 1)
                             for s in reversed(range(nsub))]
                cs_all[n, p] = jnp.dot(jnp.concatenate(lhs_parts, axis=0), tri, preferred_element_type=F32)
        latest = {}
        for n, (t, row0, nrows, start, nsub, diag, first) in enumerate(jobs):
            qrows = slice(t * SUB + row0, t * SUB + row0 + nrows)
            lane = lax.broadcasted_iota(jnp.int32, (nrows, LANES), 1)
            rows2 = HEADS_PER_BLOCK * nrows
            for p in range(n_blocks):
                cols = slice(p * LANES, (p + 1) * LANES)
                carry = None if first else jnp.concatenate([carry_ref[t, p, rs] for rs in head_rows(row0, nrows)], axis=0)
                a_cols = [None] * nsub
                for idx, s in enumerate(reversed(range(nsub))):
                    cs = cs_all[n, p][idx * rows2:(idx + 1) * rows2]
                    g = cs[:, :SUB] if carry is None else cs[:, :SUB] + carry
                    carry = cs[:, SUB:] if carry is None else carry + cs[:, SUB:]
                    a = jnp.exp2((z_all[n, p][:, s * SUB:(s + 1) * SUB] - g) * LOG2E)
                    if diag and s == nsub - 1:
                        a = jnp.where(causal, a, 0.0)
                    a_cols[s] = a.astype(BF16)
                for h, rs in enumerate(head_rows(row0, nrows)):
                    carry_ref[t, p, rs] = carry[h * nrows:(h + 1) * nrows]
                latest.setdefault(t, {}).setdefault(p, []).append((row0, nrows, carry))
                o2 = jnp.dot(jnp.concatenate(a_cols, axis=1), v_s[pl.ds(start, nsub * SUB), cols],
                             preferred_element_type=F32)
                out = o2[0:nrows]
                for h in range(1, HEADS_PER_BLOCK):
                    out = jnp.where((lane // HEAD_DIM) == h, o2[h * nrows:(h + 1) * nrows], out)
                if first:
                    acc_ref[qrows, cols] = out
                else:
                    acc_ref[qrows, cols] += out
        for t, by_block in latest.items():
            if whole_tiles:
                assert all(len(e) == 1 and e[0][:2] == (0, SUB) for e in by_block.values())
                min_ref[t] = _min_to_vreg([e[0][2] for e in by_block.values()])
            else:
                min_ref[t] = _min_to_vreg([carry_ref[t, p] for p in range(n_blocks)])

    def window(tiles):
        n_keys = (NEAR_SUBS + 2) * SUB
        far = [slice(h * SUB, h * SUB + FAR_ROWS) for h in range(HEADS_PER_BLOCK)]
        rest = [slice(h * SUB + FAR_ROWS, (h + 1) * SUB) for h in range(HEADS_PER_BLOCK)]
        lane = lax.broadcasted_iota(jnp.int32, (SUB, LANES), 1)
        lane_far = lax.broadcasted_iota(jnp.int32, (FAR_ROWS, LANES), 1)
        ctx, scores, outputs = {}, [], []

        def score(t, tile, p):
            start = pl.multiple_of((tile - NEAR_SUBS - 1) * SUB, SUB)
            cols = slice(p * LANES, (p + 1) * LANES)
            q = q_s[slot, t * SUB:(t + 1) * SUB, cols]
            q2 = jnp.concatenate([jnp.where((lane // HEAD_DIM) == h, q, jnp.zeros_like(q))
                                  for h in range(HEADS_PER_BLOCK)], axis=0)
            z = lax.dot_general(q2, k_s[pl.ds(start, n_keys), cols], (((1,), (1,)), ((), ())),
                                preferred_element_type=F32)
            z_far = jnp.concatenate([z[rs, 0:SUB] for rs in far], axis=0)
            lhs = [softplus_split(z[:, s * SUB:(s + 1) * SUB], s == NEAR_SUBS + 1)
                   for s in range(NEAR_SUBS + 1, 0, -1)] + [softplus_split(z_far, False)]
            cs = jnp.dot(jnp.concatenate(lhs, axis=0), tri, preferred_element_type=F32)
            ctx[t, p] = (start, z, z_far, cs)

        def output(t, p, parts):
            cols = slice(p * LANES, (p + 1) * LANES)
            start, z, z_far, cs = ctx[t, p]
            carry, a_cols = None, []
            for idx, s in enumerate(range(NEAR_SUBS + 1, 0, -1)):
                blk = cs[idx * 2 * SUB:(idx + 1) * 2 * SUB]
                g = blk[:, :SUB] if carry is None else blk[:, :SUB] + carry
                carry = blk[:, SUB:] if carry is None else carry + blk[:, SUB:]
                a = jnp.exp2((z[:, s * SUB:(s + 1) * SUB] - g) * LOG2E)
                if s == NEAR_SUBS + 1:
                    a = jnp.where(causal, a, 0.0)
                a_cols.insert(0, a.astype(BF16))
            blk = cs[(NEAR_SUBS + 1) * 2 * SUB:]
            carry_far = jnp.concatenate([carry[rs] for rs in far], axis=0)
            a_far = jnp.exp2((z_far - (blk[:, :SUB] + carry_far)) * LOG2E).astype(BF16)
            carry_far = carry_far + blk[:, SUB:]
            for h in range(HEADS_PER_BLOCK):
                carry_ref[t, p, far[h]] = carry_far[h * FAR_ROWS:(h + 1) * FAR_ROWS]
                carry_ref[t, p, rest[h]] = carry[rest[h]]
                parts.append(carry[rest[h]])
            parts.append(carry_far)
            o2 = jnp.dot(jnp.concatenate(a_cols, axis=1), v_s[pl.ds(start + SUB, (NEAR_SUBS + 1) * SUB), cols],
                         preferred_element_type=F32)
            o_far = jnp.dot(a_far, v_s[pl.ds(start, SUB), cols], preferred_element_type=F32)
            out, out_far = o2[0:SUB], o_far[0:FAR_ROWS]
            for h in range(1, HEADS_PER_BLOCK):
                out = jnp.where((lane // HEAD_DIM) == h, o2[h * SUB:(h + 1) * SUB], out)
                out_far = jnp.where((lane_far // HEAD_DIM) == h,
                                    o_far[h * FAR_ROWS:(h + 1) * FAR_ROWS], out_far)
            acc_ref[t * SUB:t * SUB + FAR_ROWS, cols] = out[:FAR_ROWS] + out_far
            acc_ref[t * SUB + FAR_ROWS:(t + 1) * SUB, cols] = out[FAR_ROWS:]
            if p == n_blocks - 1:
                min_ref[t] = _min_to_vreg(parts)

        for t, tile in tiles:
            parts = []
            for p in range(n_blocks):
                scores.append(functools.partial(score, t, tile, p))
                outputs.append(functools.partial(output, t, p, parts))
        return scores, outputs

    def interleave(main, side):
        every = -(-len(main) // (len(side) + 1))
        side = list(side)
        for n, stage in enumerate(main):
            stage()
            if (n + 1) % every == 0 and side:
                side.pop(0)()
        for stage in side:
            stage()

    def run(stages):
        for stage in stages:
            stage()

    @pl.when(j == 0)
    def _():
        free, qk, v_out = project_stages(True)
        run(free + qk + [v_out])
        min_ref[...] = jnp.full(min_ref.shape, SKIP_ABOVE, F32)

    @pl.when(j == 1)
    def _():
        early = [t for t in range(n_tiles) if t <= NEAR_SUBS]
        span([(t, 0, SUB, 0, t + 1, True, True) for t in early], True)
        scores, outputs = window([(t, t) for t in range(n_tiles) if t > NEAR_SUBS])
        free, qk, v_out = project_stages(False)
        interleave(scores, free)
        interleave(outputs, qk)
        v_out()

    @pl.when(jnp.logical_and(j > 1, j < last))
    def _():
        scores, outputs = window([(t, group * n_tiles + t) for t in range(n_tiles)])
        free, qk, v_out = project_stages(False)
        interleave(scores, free)
        interleave(outputs, qk)
        v_out()

    @pl.when(jnp.logical_and(j > 1, j == last))
    def _():
        scores, outputs = window([(t, group * n_tiles + t) for t in range(n_tiles)])
        run(scores + outputs)

    @pl.when(jnp.logical_and(j > 0, jnp.min(min_ref[...]) < SKIP_ABOVE))
    def _():
        for t in range(n_tiles):
            tile = group * n_tiles + t
            if FAR_ROWS < SUB:
                @pl.when(jnp.logical_and(tile > NEAR_SUBS, jnp.min(min_ref[t]) < SKIP_ABOVE))
                def _(t=t, tile=tile):
                    start = pl.multiple_of((tile - NEAR_SUBS - 1) * SUB, SUB)
                    span([(t, FAR_ROWS, SUB - FAR_ROWS, start, 1, False, False)], False)

            def unfinished(t=t):
                return (jnp.min(min_ref[t]) < SKIP_ABOVE).astype(jnp.int32)

            def cond(state):
                rem, go = state
                return jnp.logical_and(rem >= WALK_SUBS * SUB, go > 0)

            def body(state, t=t, unfinished=unfinished):
                rem, _ = state
                start = pl.multiple_of(rem - WALK_SUBS * SUB, SUB)
                span([(t, 0, SUB, start, WALK_SUBS, False, False)], False)
                return start, unfinished()

            remaining = jnp.maximum(tile - NEAR_SUBS - 1, 0) * SUB
            rem, go = lax.while_loop(cond, body, (remaining, unfinished()))
            for tail in range(1, WALK_SUBS):
                @pl.when(jnp.logical_and(rem == tail * SUB, go > 0))
                def _(t=t, tail=tail):
                    span([(t, 0, SUB, 0, tail, False, False)], False)

    @pl.when(j > 0)
    def _():
        o_ref[...] = acc_ref[...].astype(o_ref.dtype)


def _mix(x2d, gain, w_bf16, q_gain, k_gain, conv_w, later_weights, layer, *, batch, seq, attn_dim, conv_dim):
    n, d = x2d.shape
    tm = ROW_TILE
    nq = seq // tm
    assert nq >= 2
    cols = w_bf16.shape[1]
    const = lambda b, j: (0, 0)
    now = lambda b, j: (b * nq + jnp.minimum(j, nq - 1), 0)
    before = lambda b, j: (b * nq + jnp.maximum(j - 1, 0), 0)
    cast_in_specs, cast_out_specs, cast_shapes = _cast_specs(
        later_weights, layer, batch * (nq + 1), lambda b, j: b * (nq + 1) + j)
    return pl.pallas_call(
        functools.partial(_mix_kernel, attn_dim=attn_dim, conv_dim=conv_dim, n_cast=len(later_weights)),
        grid=(batch, nq + 1),
        in_specs=[
            pl.BlockSpec((tm, d), now),
            pl.BlockSpec((1, d), const),
            pl.BlockSpec((d, cols), const, pipeline_mode=pl.Buffered(1)),
            pl.BlockSpec((1, attn_dim), const),
            pl.BlockSpec((1, attn_dim), const),
            pl.BlockSpec(conv_w.shape, const),
        ] + cast_in_specs,
        out_specs=[pl.BlockSpec((tm, attn_dim), before), pl.BlockSpec((tm, conv_dim), now)] + cast_out_specs,
        out_shape=[jax.ShapeDtypeStruct((n, attn_dim), BF16), jax.ShapeDtypeStruct((n, conv_dim), BF16)] + cast_shapes,
        scratch_shapes=[pltpu.VMEM((CONV_SHIFTS, SUBLANES, conv_dim), F32),
                        pltpu.VMEM((2, tm, attn_dim), BF16),
                        pltpu.VMEM((seq, attn_dim), BF16),
                        pltpu.VMEM((seq, attn_dim), BF16),
                        pltpu.VMEM((tm, attn_dim), F32),
                        pltpu.VMEM((tm // SUB, attn_dim // LANES, HEADS_PER_BLOCK * SUB, SUB), F32),
                        pltpu.VMEM((tm // SUB, SUBLANES, SUB), F32)],
        compiler_params=pltpu.CompilerParams(
            dimension_semantics=("arbitrary", "arbitrary"), vmem_limit_bytes=VMEM_LIMIT),
    )(x2d, gain, w_bf16, q_gain, k_gain, conv_w, *later_weights)


def _ffn_kernel(x_ref, attn_ref, conv_ref, wo_ref, gain_ref, wg_ref, wu_ref, wd_ref, o_ref, act_ref):
    attn_dim = attn_ref.shape[1]
    d_ff = wg_ref.shape[1]
    mix = (jnp.dot(attn_ref[...], wo_ref[0:attn_dim, :], preferred_element_type=F32)
           + jnp.dot(conv_ref[...], wo_ref[attn_dim:, :], preferred_element_type=F32))
    x1 = x_ref[...] + mix
    h = ((x1 * _rms_scale(x1)) * gain_ref[...]).astype(BF16)
    for c0 in range(0, d_ff, FF_CHUNK):
        g = jnp.dot(h, wg_ref[:, c0:c0 + FF_CHUNK], preferred_element_type=F32)
        u = jnp.dot(h, wu_ref[:, c0:c0 + FF_CHUNK], preferred_element_type=F32)
        act_ref[:, c0:c0 + FF_CHUNK] = ((g * jax.nn.sigmoid(g)) * u).astype(BF16)
    o_ref[...] = x1 + jnp.dot(act_ref[...], wd_ref[...], preferred_element_type=F32)


def _ffn(x2d, attn, conv, w_out, gain, w_gate, w_up, w_down):
    n, d = x2d.shape
    tm = FFN_ROW_TILE
    d_ff = w_gate.shape[1]
    const = lambda i: (0, 0)
    row = lambda i: (i, 0)
    resident = functools.partial(pl.BlockSpec, index_map=const, pipeline_mode=pl.Buffered(1))
    return pl.pallas_call(
        _ffn_kernel,
        grid=(n // tm,),
        in_specs=[
            pl.BlockSpec((tm, d), row),
            pl.BlockSpec((tm, attn.shape[1]), row),
            pl.BlockSpec((tm, conv.shape[1]), row),
            resident(w_out.shape),
            pl.BlockSpec((1, d), const),
            resident(w_gate.shape),
            resident(w_up.shape),
            resident(w_down.shape),
        ],
        out_specs=pl.BlockSpec((tm, d), row),
        out_shape=jax.ShapeDtypeStruct((n, d), F32),
        scratch_shapes=[pltpu.VMEM((tm, d_ff), BF16)],
        compiler_params=pltpu.CompilerParams(
            dimension_semantics=("arbitrary",), vmem_limit_bytes=VMEM_LIMIT),
    )(x2d, attn, conv, w_out, gain, w_gate, w_up, w_down)


def kernel(x, norm_mix, w_in, q_norm, k_norm, conv_w, w_out, norm_ffn, w_gate, w_up, w_down):
    batch, seq, d = x.shape
    depth = w_in.shape[0]
    conv_dim = conv_w.shape[2]
    attn_dim = w_out.shape[1] - conv_dim
    n_heads = attn_dim // HEAD_DIM
    assert seq % ROW_TILE == 0 and ROW_TILE % SUB == 0 and attn_dim % MXU_DIM == 0
    assert (batch * seq) % FFN_ROW_TILE == 0 and w_in.shape[1] % W_CAST_ROWS == 0
    assert w_gate.shape[2] % FF_CHUNK == 0 and conv_w.shape[1] == CONV_SHIFTS + 1

    xf = x.reshape(batch * seq, d)
    for l in range(depth):
        attn, conv, wo, wg, wu, wd = _mix(
            xf, norm_mix[l][None, :], _cast_layer(w_in, l),
            jnp.tile(q_norm[l], n_heads)[None, :], jnp.tile(k_norm[l], n_heads)[None, :], conv_w[l],
            (w_out, w_gate, w_up, w_down), l, batch=batch, seq=seq, attn_dim=attn_dim, conv_dim=conv_dim)
        xf = _ffn(xf, attn, conv, wo, norm_ffn[l][None, :], wg, wu, wd)
    return xf.reshape(batch, seq, d)
```

```python
import functools

import jax
import jax.numpy as jnp
from jax import lax
from jax.experimental import pallas as pl
from jax.experimental.pallas import tpu as pltpu

F32 = jnp.float32
BF16 = jnp.bfloat16

HEAD_DIM = 64
EPS = 1e-6
LANES = 128
SUBLANES = 8
BF16_SUBLANES = 16
W_CAST_ROWS = 128
MXU_DIM = 256
HEADS_PER_BLOCK = LANES // HEAD_DIM
SUB = 128
LOG2E = 1.4426950408889634
LN2 = 0.6931471805599453
SKIP_ABOVE = 104.0

ROW_TILE = 1024
FFN_ROW_TILE = 1024
ATTN_TILES = 8
NEAR_SUBS = 1
FAR_ROWS = 32
WALK_SUBS = 2
FF_CHUNK = 256
CONV_SHIFTS = 2
VMEM_LIMIT = 56 * 1024 * 1024


def _rms_scale(x):
    return lax.rsqrt(jnp.mean(x * x, axis=-1, keepdims=True) + EPS)


def _inproj_kernel(x_ref, gain_ref, w_ref, qg_ref, kg_ref, cw_ref, *rest,
                   tiles_per_seq, attn_dim, conv_dim, n_cast):
    cast_in, (q_ref, kt_ref, v_ref, c_ref) = rest[:n_cast], rest[n_cast:n_cast + 4]
    cast_out, (halo_ref, wbf_ref) = rest[n_cast + 4:2 * n_cast + 4], rest[2 * n_cast + 4:]
    tm = x_ref.shape[0]
    i = pl.program_id(0)

    @pl.when(i == 0)
    def _():
        def cast_rows(c, _):
            rows = pl.ds(pl.multiple_of(c * W_CAST_ROWS, W_CAST_ROWS), W_CAST_ROWS)
            wbf_ref[rows, :] = w_ref[rows, :].astype(BF16)
            return 0
        lax.fori_loop(0, w_ref.shape[0] // W_CAST_ROWS, cast_rows, 0)
        halo_ref[...] = jnp.zeros_like(halo_ref)

    for src, dst in zip(cast_in, cast_out):
        dst[...] = src[...].astype(BF16)

    x = x_ref[...]
    h = ((x * _rms_scale(x)) * gain_ref[...]).astype(BF16)

    def proj(c0, width):
        return jnp.dot(h, wbf_ref[:, c0:c0 + width], preferred_element_type=F32)

    r = lax.broadcasted_iota(jnp.int32, (MXU_DIM, MXU_DIM), 0) // HEAD_DIM
    c = lax.broadcasted_iota(jnp.int32, (MXU_DIM, MXU_DIM), 1) // HEAD_DIM
    seg = jnp.where(r == c, 1.0, 0.0).astype(BF16)

    def head_norm(p, gain):
        p2 = (p * p).astype(BF16)
        ssq = jnp.concatenate(
            [jnp.dot(p2[:, s:s + MXU_DIM], seg, preferred_element_type=F32)
             for s in range(0, attn_dim, MXU_DIM)], axis=1)
        return (p * lax.rsqrt(ssq * (1.0 / HEAD_DIM) + EPS)) * gain

    q_ref[...] = (head_norm(proj(0, attn_dim), qg_ref[...]) * (HEAD_DIM ** -0.5)).astype(BF16)
    kt_ref[...] = jnp.transpose(head_norm(proj(attn_dim, attn_dim), kg_ref[...])).astype(BF16)

    cb = proj(3 * attn_dim, conv_dim)
    hh = proj(3 * attn_dim + conv_dim, conv_dim) * proj(3 * attn_dim + 2 * conv_dim, conv_dim)
    row = lax.broadcasted_iota(jnp.int32, (SUBLANES, conv_dim), 0)
    seq_start = i % tiles_per_seq == 0

    def shift_down(a, n):
        prev_last = jnp.where(seq_start, 0.0, halo_ref[n, SUBLANES - 1:SUBLANES, :])
        halo_ref[n] = a[tm - SUBLANES:tm, :]
        rolled = pltpu.roll(a, 1, 0)
        head = jnp.where(row == 0, jnp.broadcast_to(prev_last, (SUBLANES, conv_dim)), rolled[0:SUBLANES])
        return jnp.concatenate([head, rolled[SUBLANES:]], axis=0)

    y = cw_ref[2:3, :] * hh + shift_down(cw_ref[1:2, :] * hh + shift_down(cw_ref[0:1, :] * hh, 0), 1)
    c_ref[...] = (cb * y).astype(BF16)

    v_ref[...] = proj(2 * attn_dim, attn_dim).astype(BF16)


def _cast_specs(weights, layer, n_steps, step_of):
    in_specs, out_specs, shapes = [], [], []
    for w in weights:
        _, rows, cols = w.shape
        chunks = max(c for c in range(1, n_steps + 1) if rows % c == 0 and (rows // c) % BF16_SUBLANES == 0)
        chunk_of = lambda *idx, chunks=chunks: jnp.minimum(step_of(*idx), chunks - 1)
        in_specs.append(pl.BlockSpec((None, rows // chunks, cols),
                                     lambda *idx, chunk_of=chunk_of: (layer, chunk_of(*idx), 0)))
        out_specs.append(pl.BlockSpec((rows // chunks, cols), lambda *idx, chunk_of=chunk_of: (chunk_of(*idx), 0)))
        shapes.append(jax.ShapeDtypeStruct((rows, cols), BF16))
    return in_specs, out_specs, shapes


def _inproj(x2d, gain, w_in, layer, q_gain, k_gain, conv_w, later_weights, *, seq, attn_dim, conv_dim):
    n, d = x2d.shape
    tm = ROW_TILE
    n_steps = n // tm
    cols = w_in.shape[2]
    const = lambda i: (0, 0)
    row = lambda i: (i, 0)
    cast_in_specs, cast_out_specs, cast_shapes = _cast_specs(later_weights, layer, n_steps, lambda i: i)
    rows_by = lambda width: jax.ShapeDtypeStruct((n, width), BF16)
    out_shapes = [rows_by(attn_dim), jax.ShapeDtypeStruct((attn_dim, n), BF16), rows_by(attn_dim), rows_by(conv_dim)]
    return pl.pallas_call(
        functools.partial(_inproj_kernel, tiles_per_seq=seq // tm, attn_dim=attn_dim, conv_dim=conv_dim,
                          n_cast=len(later_weights)),
        grid=(n_steps,),
        in_specs=[
            pl.BlockSpec((tm, d), row),
            pl.BlockSpec((1, d), const),
            pl.BlockSpec((None, d, cols), lambda i: (layer, 0, 0), pipeline_mode=pl.Buffered(1)),
            pl.BlockSpec((1, attn_dim), const),
            pl.BlockSpec((1, attn_dim), const),
            pl.BlockSpec(conv_w.shape, const),
        ] + cast_in_specs,
        out_specs=[pl.BlockSpec((tm, attn_dim), row), pl.BlockSpec((attn_dim, tm), lambda i: (0, i)),
                   pl.BlockSpec((tm, attn_dim), row), pl.BlockSpec((tm, conv_dim), row)] + cast_out_specs,
        out_shape=out_shapes + cast_shapes,
        scratch_shapes=[pltpu.VMEM((CONV_SHIFTS, SUBLANES, conv_dim), F32), pltpu.VMEM((d, cols), BF16)],
        compiler_params=pltpu.CompilerParams(
            dimension_semantics=("arbitrary",), vmem_limit_bytes=VMEM_LIMIT),
    )(x2d, gain, w_in, q_gain, k_gain, conv_w, *later_weights)


def _min_to_vreg(arrays):
    chunks = [a[r:r + SUBLANES] for a in arrays for r in range(0, a.shape[0], SUBLANES)]
    while len(chunks) > 1:
        odd = chunks[-1:] if len(chunks) % 2 else []
        chunks = [jnp.minimum(a, b) for a, b in zip(chunks[0::2], chunks[1::2])] + odd
    return chunks[0]


def _attn_kernel(q_ref, kt_ref, v_ref, o_ref, acc_ref, carry_ref, min_ref):
    n_tiles = q_ref.shape[0] // SUB
    n_blocks = q_ref.shape[1] // LANES
    i = pl.program_id(1)
    assert n_tiles >= NEAR_SUBS + 1

    r = lax.broadcasted_iota(jnp.int32, (2 * SUB, 2 * SUB), 0) % SUB
    c = lax.broadcasted_iota(jnp.int32, (2 * SUB, 2 * SUB), 1)
    tri = jnp.where((c >= SUB) | (r >= c), 1.0, 0.0).astype(BF16)
    causal = (lax.broadcasted_iota(jnp.int32, (HEADS_PER_BLOCK * SUB, SUB), 1)
              < lax.broadcasted_iota(jnp.int32, (HEADS_PER_BLOCK * SUB, SUB), 0) % SUB)

    def head_rows(row0, nrows):
        return [slice(h * SUB + row0, h * SUB + row0 + nrows) for h in range(HEADS_PER_BLOCK)]

    def softplus_split(z_s, masked):
        e = jnp.exp2(jnp.abs(z_s) * (-LOG2E))
        sp = jnp.maximum(z_s, 0.0) + jnp.log(1.0 + e)
        if masked:
            sp = jnp.where(causal[:z_s.shape[0]], sp, 0.0)
        hi = sp.astype(BF16)
        lo = (sp - hi.astype(F32)).astype(BF16)
        return jnp.concatenate([hi, lo], axis=1)

    def span(jobs, whole_tiles):
        z_all, cs_all = {}, {}
        for n, (t, row0, nrows, start, nsub, diag, first) in enumerate(jobs):
            assert not diag or (row0 == 0 and nrows == SUB)
            qrows = slice(t * SUB + row0, t * SUB + row0 + nrows)
            lane = lax.broadcasted_iota(jnp.int32, (nrows, LANES), 1)
            for p in range(n_blocks):
                cols = slice(p * LANES, (p + 1) * LANES)
                q = q_ref[qrows, cols]
                q2 = jnp.concatenate([jnp.where((lane // HEAD_DIM) == h, q, jnp.zeros_like(q))
                                      for h in range(HEADS_PER_BLOCK)], axis=0)
                z = jnp.dot(q2, kt_ref[cols, pl.ds(start, nsub * SUB)], preferred_element_type=F32)
                z_all[n, p] = z
                lhs_parts = [softplus_split(z[:, s * SUB:(s + 1) * SUB], diag and s == nsub - 1)
                             for s in reversed(range(nsub))]
                cs_all[n, p] = jnp.dot(jnp.concatenate(lhs_parts, axis=0), tri, preferred_element_type=F32)
        latest = {}
        for n, (t, row0, nrows, start, nsub, diag, first) in enumerate(jobs):
            qrows = slice(t * SUB + row0, t * SUB + row0 + nrows)
            lane = lax.broadcasted_iota(jnp.int32, (nrows, LANES), 1)
            rows2 = HEADS_PER_BLOCK * nrows
            for p in range(n_blocks):
                cols = slice(p * LANES, (p + 1) * LANES)
                carry = None if first else jnp.concatenate([carry_ref[t, p, rs] for rs in head_rows(row0, nrows)], axis=0)
                a_cols = [None] * nsub
                for idx, s in enumerate(reversed(range(nsub))):
                    cs = cs_all[n, p][idx * rows2:(idx + 1) * rows2]
                    g = cs[:, :SUB] if carry is None else cs[:, :SUB] + carry
                    carry = cs[:, SUB:] if carry is None else carry + cs[:, SUB:]
                    a = jnp.exp2((z_all[n, p][:, s * SUB:(s + 1) * SUB] - g) * LOG2E)
                    if diag and s == nsub - 1:
                        a = jnp.where(causal, a, 0.0)
                    a_cols[s] = a.astype(BF16)
                for h, rs in enumerate(head_rows(row0, nrows)):
                    carry_ref[t, p, rs] = carry[h * nrows:(h + 1) * nrows]
                latest.setdefault(t, {}).setdefault(p, []).append((row0, nrows, carry))
                o2 = jnp.dot(jnp.concatenate(a_cols, axis=1), v_ref[pl.ds(start, nsub * SUB), cols],
                             preferred_element_type=F32)
                out = o2[0:nrows]
                for h in range(1, HEADS_PER_BLOCK):
                    out = jnp.where((lane // HEAD_DIM) == h, o2[h * nrows:(h + 1) * nrows], out)
                if first:
                    acc_ref[qrows, cols] = out
                else:
                    acc_ref[qrows, cols] += out
        for t, by_block in latest.items():
            if whole_tiles:
                assert all(len(e) == 1 and e[0][:2] == (0, SUB) for e in by_block.values())
                min_ref[t] = _min_to_vreg([e[0][2] for e in by_block.values()])
            else:
                min_ref[t] = _min_to_vreg([carry_ref[t, p] for p in range(n_blocks)])

    def window(tiles):
        n_keys = (NEAR_SUBS + 2) * SUB
        far = [slice(h * SUB, h * SUB + FAR_ROWS) for h in range(HEADS_PER_BLOCK)]
        rest = [slice(h * SUB + FAR_ROWS, (h + 1) * SUB) for h in range(HEADS_PER_BLOCK)]
        lane = lax.broadcasted_iota(jnp.int32, (SUB, LANES), 1)
        lane_far = lax.broadcasted_iota(jnp.int32, (FAR_ROWS, LANES), 1)
        ctx = {}
        for t, tile in tiles:
            start = pl.multiple_of((tile - NEAR_SUBS - 1) * SUB, SUB)
            for p in range(n_blocks):
                cols = slice(p * LANES, (p + 1) * LANES)
                q = q_ref[t * SUB:(t + 1) * SUB, cols]
                q2 = jnp.concatenate([jnp.where((lane // HEAD_DIM) == h, q, jnp.zeros_like(q))
                                      for h in range(HEADS_PER_BLOCK)], axis=0)
                z = jnp.dot(q2, kt_ref[cols, pl.ds(start, n_keys)],
                            preferred_element_type=F32)
                z_far = jnp.concatenate([z[rs, 0:SUB] for rs in far], axis=0)
                lhs = [softplus_split(z[:, s * SUB:(s + 1) * SUB], s == NEAR_SUBS + 1)
                       for s in range(NEAR_SUBS + 1, 0, -1)] + [softplus_split(z_far, False)]
                cs = jnp.dot(jnp.concatenate(lhs, axis=0), tri, preferred_element_type=F32)
                ctx[t, p] = (start, z, z_far, cs)
        for t, tile in tiles:
            parts = []
            for p in range(n_blocks):
                cols = slice(p * LANES, (p + 1) * LANES)
                start, z, z_far, cs = ctx[t, p]
                carry, a_cols = None, []
                for idx, s in enumerate(range(NEAR_SUBS + 1, 0, -1)):
                    blk = cs[idx * 2 * SUB:(idx + 1) * 2 * SUB]
                    g = blk[:, :SUB] if carry is None else blk[:, :SUB] + carry
                    carry = blk[:, SUB:] if carry is None else carry + blk[:, SUB:]
                    a = jnp.exp2((z[:, s * SUB:(s + 1) * SUB] - g) * LOG2E)
                    if s == NEAR_SUBS + 1:
                        a = jnp.where(causal, a, 0.0)
                    a_cols.insert(0, a.astype(BF16))
                blk = cs[(NEAR_SUBS + 1) * 2 * SUB:]
                carry_far = jnp.concatenate([carry[rs] for rs in far], axis=0)
                a_far = jnp.exp2((z_far - (blk[:, :SUB] + carry_far)) * LOG2E).astype(BF16)
                carry_far = carry_far + blk[:, SUB:]
                for h in range(HEADS_PER_BLOCK):
                    carry_ref[t, p, far[h]] = carry_far[h * FAR_ROWS:(h + 1) * FAR_ROWS]
                    carry_ref[t, p, rest[h]] = carry[rest[h]]
                    parts.append(carry[rest[h]])
                parts.append(carry_far)
                o2 = jnp.dot(jnp.concatenate(a_cols, axis=1), v_ref[pl.ds(start + SUB, (NEAR_SUBS + 1) * SUB), cols],
                             preferred_element_type=F32)
                o_far = jnp.dot(a_far, v_ref[pl.ds(start, SUB), cols], preferred_element_type=F32)
                out, out_far = o2[0:SUB], o_far[0:FAR_ROWS]
                for h in range(1, HEADS_PER_BLOCK):
                    out = jnp.where((lane // HEAD_DIM) == h, o2[h * SUB:(h + 1) * SUB], out)
                    out_far = jnp.where((lane_far // HEAD_DIM) == h,
                                        o_far[h * FAR_ROWS:(h + 1) * FAR_ROWS], out_far)
                acc_ref[t * SUB:t * SUB + FAR_ROWS, cols] = out[:FAR_ROWS] + out_far
                acc_ref[t * SUB + FAR_ROWS:(t + 1) * SUB, cols] = out[FAR_ROWS:]
            min_ref[t] = _min_to_vreg(parts)

    @pl.when(i == 0)
    def _():
        early = [t for t in range(n_tiles) if t <= NEAR_SUBS]
        span([(t, 0, SUB, 0, t + 1, True, True) for t in early], True)
        window([(t, t) for t in range(n_tiles) if t > NEAR_SUBS])

    @pl.when(i > 0)
    def _():
        window([(t, i * n_tiles + t) for t in range(n_tiles)])

    @pl.when(jnp.min(min_ref[...]) < SKIP_ABOVE)
    def _():
        for t in range(n_tiles):
            tile = i * n_tiles + t
            if FAR_ROWS < SUB:
                @pl.when(jnp.logical_and(tile > NEAR_SUBS, jnp.min(min_ref[t]) < SKIP_ABOVE))
                def _(t=t, tile=tile):
                    start = pl.multiple_of((tile - NEAR_SUBS - 1) * SUB, SUB)
                    span([(t, FAR_ROWS, SUB - FAR_ROWS, start, 1, False, False)], False)

            def unfinished(t=t):
                return (jnp.min(min_ref[t]) < SKIP_ABOVE).astype(jnp.int32)

            def cond(state):
                rem, go = state
                return jnp.logical_and(rem >= WALK_SUBS * SUB, go > 0)

            def body(state, t=t, unfinished=unfinished):
                rem, _ = state
                start = pl.multiple_of(rem - WALK_SUBS * SUB, SUB)
                span([(t, 0, SUB, start, WALK_SUBS, False, False)], False)
                return start, unfinished()

            remaining = jnp.maximum(tile - NEAR_SUBS - 1, 0) * SUB
            rem, go = lax.while_loop(cond, body, (remaining, unfinished()))
            for tail in range(1, WALK_SUBS):
                @pl.when(jnp.logical_and(rem == tail * SUB, go > 0))
                def _(t=t, tail=tail):
                    span([(t, 0, SUB, 0, tail, False, False)], False)

    o_ref[...] = acc_ref[...].astype(o_ref.dtype)


def _attention(q, kt, v, *, batch, seq):
    n, attn_dim = q.shape
    tq = ATTN_TILES * SUB
    nq = seq // tq
    return pl.pallas_call(
        _attn_kernel,
        grid=(batch, nq),
        in_specs=[
            pl.BlockSpec((tq, attn_dim), lambda b, i: (b * nq + i, 0)),
            pl.BlockSpec((attn_dim, seq), lambda b, i: (0, b)),
            pl.BlockSpec((seq, attn_dim), lambda b, i: (b, 0)),
        ],
        out_specs=pl.BlockSpec((tq, attn_dim), lambda b, i: (b * nq + i, 0)),
        out_shape=jax.ShapeDtypeStruct((n, attn_dim), BF16),
        scratch_shapes=[pltpu.VMEM((tq, attn_dim), F32),
                        pltpu.VMEM((ATTN_TILES, attn_dim // LANES, HEADS_PER_BLOCK * SUB, SUB), F32),
                        pltpu.VMEM((ATTN_TILES, SUBLANES, SUB), F32)],
        compiler_params=pltpu.CompilerParams(
            dimension_semantics=("arbitrary", "arbitrary"), vmem_limit_bytes=VMEM_LIMIT),
    )(q, kt, v)


def _ffn_kernel(x_ref, attn_ref, conv_ref, wo_ref, gain_ref, wg_ref, wu_ref, wd_ref, o_ref, act_ref):
    attn_dim = attn_ref.shape[1]
    d_ff = wg_ref.shape[1]
    mix = (jnp.dot(attn_ref[...], wo_ref[0:attn_dim, :], preferred_element_type=F32)
           + jnp.dot(conv_ref[...], wo_ref[attn_dim:, :], preferred_element_type=F32))
    x1 = x_ref[...] + mix
    h = ((x1 * _rms_scale(x1)) * gain_ref[...]).astype(BF16)
    for c0 in range(0, d_ff, FF_CHUNK):
        g = jnp.dot(h, wg_ref[:, c0:c0 + FF_CHUNK], preferred_element_type=F32)
        u = jnp.dot(h, wu_ref[:, c0:c0 + FF_CHUNK], preferred_element_type=F32)
        act_ref[:, c0:c0 + FF_CHUNK] = ((g * jax.nn.sigmoid(g)) * u).astype(BF16)
    o_ref[...] = x1 + jnp.dot(act_ref[...], wd_ref[...], preferred_element_type=F32)


def _ffn(x2d, attn, conv, w_out, gain, w_gate, w_up, w_down):
    n, d = x2d.shape
    tm = FFN_ROW_TILE
    d_ff = w_gate.shape[1]
    const = lambda i: (0, 0)
    row = lambda i: (i, 0)
    resident = functools.partial(pl.BlockSpec, index_map=const, pipeline_mode=pl.Buffered(1))
    return pl.pallas_call(
        _ffn_kernel,
        grid=(n // tm,),
        in_specs=[
            pl.BlockSpec((tm, d), row),
            pl.BlockSpec((tm, attn.shape[1]), row),
            pl.BlockSpec((tm, conv.shape[1]), row),
            resident(w_out.shape),
            pl.BlockSpec((1, d), const),
            resident(w_gate.shape),
            resident(w_up.shape),
            resident(w_down.shape),
        ],
        out_specs=pl.BlockSpec((tm, d), row),
        out_shape=jax.ShapeDtypeStruct((n, d), F32),
        scratch_shapes=[pltpu.VMEM((tm, d_ff), BF16)],
        compiler_params=pltpu.CompilerParams(
            dimension_semantics=("arbitrary",), vmem_limit_bytes=VMEM_LIMIT),
    )(x2d, attn, conv, w_out, gain, w_gate, w_up, w_down)


def kernel(x, norm_mix, w_in, q_norm, k_norm, conv_w, w_out, norm_ffn, w_gate, w_up, w_down):
    batch, seq, d = x.shape
    depth = w_in.shape[0]
    conv_dim = conv_w.shape[2]
    attn_dim = w_out.shape[1] - conv_dim
    n_heads = attn_dim // HEAD_DIM
    assert seq % ROW_TILE == 0 and seq % (ATTN_TILES * SUB) == 0 and attn_dim % MXU_DIM == 0
    assert (batch * seq) % FFN_ROW_TILE == 0
    assert w_gate.shape[2] % FF_CHUNK == 0 and conv_w.shape[1] == CONV_SHIFTS + 1

    xf = x.reshape(batch * seq, d)
    for l in range(depth):
        q, kt, v, conv, wo, wg, wu, wd = _inproj(
            xf, norm_mix[l][None, :], w_in, l,
            jnp.tile(q_norm[l], n_heads)[None, :], jnp.tile(k_norm[l], n_heads)[None, :], conv_w[l],
            (w_out, w_gate, w_up, w_down), seq=seq, attn_dim=attn_dim, conv_dim=conv_dim)
        attn = _attention(q, kt, v, batch=batch, seq=seq)
        xf = _ffn(xf, attn, conv, wo, norm_ffn[l][None, :], wg, wu, wd)
    return xf.reshape(batch, seq, d)
```

```python
import functools

import jax
import jax.numpy as jnp
from jax import lax
from jax.experimental import pallas as pl
from jax.experimental.pallas import tpu as pltpu

F32 = jnp.float32
BF16 = jnp.bfloat16

HEAD_DIM = 64
EPS = 1e-6
LANES = 128
SUBLANES = 8
BF16_SUBLANES = 16
W_CAST_ROWS = 128
MXU_DIM = 256
HEADS_PER_BLOCK = LANES // HEAD_DIM
SUB = 128
LOG2E = 1.4426950408889634
LN2 = 0.6931471805599453
SKIP_ABOVE = 104.0

ROW_TILE = 1024
FFN_ROW_TILE = 1024
ATTN_TILES = 8
NEAR_SUBS = 1
FAR_ROWS = 32
WALK_SUBS = 2
FF_CHUNK = 256
CONV_SHIFTS = 2
VMEM_LIMIT = 56 * 1024 * 1024


def _rms_scale(x):
    return lax.rsqrt(jnp.mean(x * x, axis=-1, keepdims=True) + EPS)


def _inproj_kernel(x_ref, gain_ref, w_ref, qg_ref, kg_ref, cw_ref, *rest,
                   tiles_per_seq, attn_dim, conv_dim, n_cast):
    cast_in, (q_ref, k_ref, v_ref, c_ref) = rest[:n_cast], rest[n_cast:n_cast + 4]
    cast_out, (halo_ref, wbf_ref) = rest[n_cast + 4:2 * n_cast + 4], rest[2 * n_cast + 4:]
    tm = x_ref.shape[0]
    i = pl.program_id(0)

    @pl.when(i == 0)
    def _():
        def cast_rows(c, _):
            rows = pl.ds(pl.multiple_of(c * W_CAST_ROWS, W_CAST_ROWS), W_CAST_ROWS)
            wbf_ref[rows, :] = w_ref[rows, :].astype(BF16)
            return 0
        lax.fori_loop(0, w_ref.shape[0] // W_CAST_ROWS, cast_rows, 0)
        halo_ref[...] = jnp.zeros_like(halo_ref)

    for src, dst in zip(cast_in, cast_out):
        dst[...] = src[...].astype(BF16)

    x = x_ref[...]
    h = ((x * _rms_scale(x)) * gain_ref[...]).astype(BF16)

    def proj(c0, width):
        return jnp.dot(h, wbf_ref[:, c0:c0 + width], preferred_element_type=F32)

    r = lax.broadcasted_iota(jnp.int32, (MXU_DIM, MXU_DIM), 0) // HEAD_DIM
    c = lax.broadcasted_iota(jnp.int32, (MXU_DIM, MXU_DIM), 1) // HEAD_DIM
    seg = jnp.where(r == c, 1.0, 0.0).astype(BF16)

    def head_norm(p, gain):
        p2 = (p * p).astype(BF16)
        ssq = jnp.concatenate(
            [jnp.dot(p2[:, s:s + MXU_DIM], seg, preferred_element_type=F32)
             for s in range(0, attn_dim, MXU_DIM)], axis=1)
        return (p * lax.rsqrt(ssq * (1.0 / HEAD_DIM) + EPS)) * gain

    q_ref[...] = (head_norm(proj(0, attn_dim), qg_ref[...]) * (HEAD_DIM ** -0.5)).astype(BF16)
    k_ref[...] = head_norm(proj(attn_dim, attn_dim), kg_ref[...]).astype(BF16)

    cb = proj(3 * attn_dim, conv_dim)
    hh = proj(3 * attn_dim + conv_dim, conv_dim) * proj(3 * attn_dim + 2 * conv_dim, conv_dim)
    row = lax.broadcasted_iota(jnp.int32, (SUBLANES, conv_dim), 0)
    seq_start = i % tiles_per_seq == 0

    def shift_down(a, n):
        prev_last = jnp.where(seq_start, 0.0, halo_ref[n, SUBLANES - 1:SUBLANES, :])
        halo_ref[n] = a[tm - SUBLANES:tm, :]
        rolled = pltpu.roll(a, 1, 0)
        head = jnp.where(row == 0, jnp.broadcast_to(prev_last, (SUBLANES, conv_dim)), rolled[0:SUBLANES])
        return jnp.concatenate([head, rolled[SUBLANES:]], axis=0)

    y = cw_ref[2:3, :] * hh + shift_down(cw_ref[1:2, :] * hh + shift_down(cw_ref[0:1, :] * hh, 0), 1)
    c_ref[...] = (cb * y).astype(BF16)

    v_ref[...] = proj(2 * attn_dim, attn_dim).astype(BF16)


def _cast_specs(weights, layer, n_steps, step_of):
    in_specs, out_specs, shapes = [], [], []
    for w in weights:
        _, rows, cols = w.shape
        chunks = max(c for c in range(1, n_steps + 1) if rows % c == 0 and (rows // c) % BF16_SUBLANES == 0)
        chunk_of = lambda *idx, chunks=chunks: jnp.minimum(step_of(*idx), chunks - 1)
        in_specs.append(pl.BlockSpec((None, rows // chunks, cols),
                                     lambda *idx, chunk_of=chunk_of: (layer, chunk_of(*idx), 0)))
        out_specs.append(pl.BlockSpec((rows // chunks, cols), lambda *idx, chunk_of=chunk_of: (chunk_of(*idx), 0)))
        shapes.append(jax.ShapeDtypeStruct((rows, cols), BF16))
    return in_specs, out_specs, shapes


def _inproj(x2d, gain, w_in, layer, q_gain, k_gain, conv_w, later_weights, *, seq, attn_dim, conv_dim):
    n, d = x2d.shape
    tm = ROW_TILE
    n_steps = n // tm
    cols = w_in.shape[2]
    const = lambda i: (0, 0)
    row = lambda i: (i, 0)
    cast_in_specs, cast_out_specs, cast_shapes = _cast_specs(later_weights, layer, n_steps, lambda i: i)
    out_shapes = [jax.ShapeDtypeStruct((n, attn_dim), BF16)] * 3 + [jax.ShapeDtypeStruct((n, conv_dim), BF16)]
    return pl.pallas_call(
        functools.partial(_inproj_kernel, tiles_per_seq=seq // tm, attn_dim=attn_dim, conv_dim=conv_dim,
                          n_cast=len(later_weights)),
        grid=(n_steps,),
        in_specs=[
            pl.BlockSpec((tm, d), row),
            pl.BlockSpec((1, d), const),
            pl.BlockSpec((None, d, cols), lambda i: (layer, 0, 0), pipeline_mode=pl.Buffered(1)),
            pl.BlockSpec((1, attn_dim), const),
            pl.BlockSpec((1, attn_dim), const),
            pl.BlockSpec(conv_w.shape, const),
        ] + cast_in_specs,
        out_specs=[pl.BlockSpec((tm, attn_dim), row)] * 3 + [pl.BlockSpec((tm, conv_dim), row)] + cast_out_specs,
        out_shape=out_shapes + cast_shapes,
        scratch_shapes=[pltpu.VMEM((CONV_SHIFTS, SUBLANES, conv_dim), F32), pltpu.VMEM((d, cols), BF16)],
        compiler_params=pltpu.CompilerParams(
            dimension_semantics=("arbitrary",), vmem_limit_bytes=VMEM_LIMIT),
    )(x2d, gain, w_in, q_gain, k_gain, conv_w, *later_weights)


def _min_to_vreg(arrays):
    chunks = [a[r:r + SUBLANES] for a in arrays for r in range(0, a.shape[0], SUBLANES)]
    while len(chunks) > 1:
        odd = chunks[-1:] if len(chunks) % 2 else []
        chunks = [jnp.minimum(a, b) for a, b in zip(chunks[0::2], chunks[1::2])] + odd
    return chunks[0]


def _attn_kernel(q_ref, k_ref, v_ref, o_ref, acc_ref, carry_ref, min_ref):
    n_tiles = q_ref.shape[0] // SUB
    n_blocks = q_ref.shape[1] // LANES
    i = pl.program_id(1)
    assert n_tiles >= NEAR_SUBS + 1

    r = lax.broadcasted_iota(jnp.int32, (2 * SUB, 2 * SUB), 0) % SUB
    c = lax.broadcasted_iota(jnp.int32, (2 * SUB, 2 * SUB), 1)
    tri = jnp.where((c >= SUB) | (r >= c), 1.0, 0.0).astype(BF16)
    causal = (lax.broadcasted_iota(jnp.int32, (HEADS_PER_BLOCK * SUB, SUB), 1)
              < lax.broadcasted_iota(jnp.int32, (HEADS_PER_BLOCK * SUB, SUB), 0) % SUB)

    def head_rows(row0, nrows):
        return [slice(h * SUB + row0, h * SUB + row0 + nrows) for h in range(HEADS_PER_BLOCK)]

    def softplus_split(z_s, masked):
        e = jnp.exp2(jnp.abs(z_s) * (-LOG2E))
        sp = jnp.maximum(z_s, 0.0) + jnp.log(1.0 + e)
        if masked:
            sp = jnp.where(causal[:z_s.shape[0]], sp, 0.0)
        hi = sp.astype(BF16)
        lo = (sp - hi.astype(F32)).astype(BF16)
        return jnp.concatenate([hi, lo], axis=1)

    def span(jobs, whole_tiles):
        z_all, cs_all = {}, {}
        for n, (t, row0, nrows, start, nsub, diag, first) in enumerate(jobs):
            assert not diag or (row0 == 0 and nrows == SUB)
            qrows = slice(t * SUB + row0, t * SUB + row0 + nrows)
            lane = lax.broadcasted_iota(jnp.int32, (nrows, LANES), 1)
            for p in range(n_blocks):
                cols = slice(p * LANES, (p + 1) * LANES)
                q = q_ref[qrows, cols]
                q2 = jnp.concatenate([jnp.where((lane // HEAD_DIM) == h, q, jnp.zeros_like(q))
                                      for h in range(HEADS_PER_BLOCK)], axis=0)
                z = lax.dot_general(q2, k_ref[pl.ds(start, nsub * SUB), cols], (((1,), (1,)), ((), ())),
                                    preferred_element_type=F32)
                z_all[n, p] = z
                lhs_parts = [softplus_split(z[:, s * SUB:(s + 1) * SUB], diag and s == nsub - 1)
                             for s in reversed(range(nsub))]
                cs_all[n, p] = jnp.dot(jnp.concatenate(lhs_parts, axis=0), tri, preferred_element_type=F32)
        latest = {}
        for n, (t, row0, nrows, start, nsub, diag, first) in enumerate(jobs):
            qrows = slice(t * SUB + row0, t * SUB + row0 + nrows)
            lane = lax.broadcasted_iota(jnp.int32, (nrows, LANES), 1)
            rows2 = HEADS_PER_BLOCK * nrows
            for p in range(n_blocks):
                cols = slice(p * LANES, (p + 1) * LANES)
                carry = None if first else jnp.concatenate([carry_ref[t, p, rs] for rs in head_rows(row0, nrows)], axis=0)
                a_cols = [None] * nsub
                for idx, s in enumerate(reversed(range(nsub))):
                    cs = cs_all[n, p][idx * rows2:(idx + 1) * rows2]
                    g = cs[:, :SUB] if carry is None else cs[:, :SUB] + carry
                    carry = cs[:, SUB:] if carry is None else carry + cs[:, SUB:]
                    a = jnp.exp2((z_all[n, p][:, s * SUB:(s + 1) * SUB] - g) * LOG2E)
                    if diag and s == nsub - 1:
                        a = jnp.where(causal, a, 0.0)
                    a_cols[s] = a.astype(BF16)
                for h, rs in enumerate(head_rows(row0, nrows)):
                    carry_ref[t, p, rs] = carry[h * nrows:(h + 1) * nrows]
                latest.setdefault(t, {}).setdefault(p, []).append((row0, nrows, carry))
                o2 = jnp.dot(jnp.concatenate(a_cols, axis=1), v_ref[pl.ds(start, nsub * SUB), cols],
                             preferred_element_type=F32)
                out = o2[0:nrows]
                for h in range(1, HEADS_PER_BLOCK):
                    out = jnp.where((lane // HEAD_DIM) == h, o2[h * nrows:(h + 1) * nrows], out)
                if first:
                    acc_ref[qrows, cols] = out
                else:
                    acc_ref[qrows, cols] += out
        for t, by_block in latest.items():
            if whole_tiles:
                assert all(len(e) == 1 and e[0][:2] == (0, SUB) for e in by_block.values())
                min_ref[t] = _min_to_vreg([e[0][2] for e in by_block.values()])
            else:
                min_ref[t] = _min_to_vreg([carry_ref[t, p] for p in range(n_blocks)])

    def window(tiles):
        n_keys = (NEAR_SUBS + 2) * SUB
        far = [slice(h * SUB, h * SUB + FAR_ROWS) for h in range(HEADS_PER_BLOCK)]
        rest = [slice(h * SUB + FAR_ROWS, (h + 1) * SUB) for h in range(HEADS_PER_BLOCK)]
        lane = lax.broadcasted_iota(jnp.int32, (SUB, LANES), 1)
        lane_far = lax.broadcasted_iota(jnp.int32, (FAR_ROWS, LANES), 1)
        ctx = {}
        for t, tile in tiles:
            start = pl.multiple_of((tile - NEAR_SUBS - 1) * SUB, SUB)
            for p in range(n_blocks):
                cols = slice(p * LANES, (p + 1) * LANES)
                q = q_ref[t * SUB:(t + 1) * SUB, cols]
                q2 = jnp.concatenate([jnp.where((lane // HEAD_DIM) == h, q, jnp.zeros_like(q))
                                      for h in range(HEADS_PER_BLOCK)], axis=0)
                z = lax.dot_general(q2, k_ref[pl.ds(start, n_keys), cols], (((1,), (1,)), ((), ())),
                                    preferred_element_type=F32)
                z_far = jnp.concatenate([z[rs, 0:SUB] for rs in far], axis=0)
                lhs = [softplus_split(z[:, s * SUB:(s + 1) * SUB], s == NEAR_SUBS + 1)
                       for s in range(NEAR_SUBS + 1, 0, -1)] + [softplus_split(z_far, False)]
                cs = jnp.dot(jnp.concatenate(lhs, axis=0), tri, preferred_element_type=F32)
                ctx[t, p] = (start, z, z_far, cs)
        for t, tile in tiles:
            parts = []
            for p in range(n_blocks):
                cols = slice(p * LANES, (p + 1) * LANES)
                start, z, z_far, cs = ctx[t, p]
                carry, a_cols = None, []
                for idx, s in enumerate(range(NEAR_SUBS + 1, 0, -1)):
                    blk = cs[idx * 2 * SUB:(idx + 1) * 2 * SUB]
                    g = blk[:, :SUB] if carry is None else blk[:, :SUB] + carry
                    carry = blk[:, SUB:] if carry is None else carry + blk[:, SUB:]
                    a = jnp.exp2((z[:, s * SUB:(s + 1) * SUB] - g) * LOG2E)
                    if s == NEAR_SUBS + 1:
                        a = jnp.where(causal, a, 0.0)
                    a_cols.insert(0, a.astype(BF16))
                blk = cs[(NEAR_SUBS + 1) * 2 * SUB:]
                carry_far = jnp.concatenate([carry[rs] for rs in far], axis=0)
                a_far = jnp.exp2((z_far - (blk[:, :SUB] + carry_far)) * LOG2E).astype(BF16)
                carry_far = carry_far + blk[:, SUB:]
                for h in range(HEADS_PER_BLOCK):
                    carry_ref[t, p, far[h]] = carry_far[h * FAR_ROWS:(h + 1) * FAR_ROWS]
                    carry_ref[t, p, rest[h]] = carry[rest[h]]
                    parts.append(carry[rest[h]])
                parts.append(carry_far)
                o2 = jnp.dot(jnp.concatenate(a_cols, axis=1), v_ref[pl.ds(start + SUB, (NEAR_SUBS + 1) * SUB), cols],
                             preferred_element_type=F32)
                o_far = jnp.dot(a_far, v_ref[pl.ds(start, SUB), cols], preferred_element_type=F32)
                out, out_far = o2[0:SUB], o_far[0:FAR_ROWS]
                for h in range(1, HEADS_PER_BLOCK):
                    out = jnp.where((lane // HEAD_DIM) == h, o2[h * SUB:(h + 1) * SUB], out)
                    out_far = jnp.where((lane_far // HEAD_DIM) == h,
                                        o_far[h * FAR_ROWS:(h + 1) * FAR_ROWS], out_far)
                acc_ref[t * SUB:t * SUB + FAR_ROWS, cols] = out[:FAR_ROWS] + out_far
                acc_ref[t * SUB + FAR_ROWS:(t + 1) * SUB, cols] = out[FAR_ROWS:]
            min_ref[t] = _min_to_vreg(parts)

    @pl.when(i == 0)
    def _():
        early = [t for t in range(n_tiles) if t <= NEAR_SUBS]
        span([(t, 0, SUB, 0, t + 1, True, True) for t in early], True)
        window([(t, t) for t in range(n_tiles) if t > NEAR_SUBS])

    @pl.when(i > 0)
    def _():
        window([(t, i * n_tiles + t) for t in range(n_tiles)])

    @pl.when(jnp.min(min_ref[...]) < SKIP_ABOVE)
    def _():
        for t in range(n_tiles):
            tile = i * n_tiles + t
            if FAR_ROWS < SUB:
                @pl.when(jnp.logical_and(tile > NEAR_SUBS, jnp.min(min_ref[t]) < SKIP_ABOVE))
                def _(t=t, tile=tile):
                    start = pl.multiple_of((tile - NEAR_SUBS - 1) * SUB, SUB)
                    span([(t, FAR_ROWS, SUB - FAR_ROWS, start, 1, False, False)], False)

            def unfinished(t=t):
                return (jnp.min(min_ref[t]) < SKIP_ABOVE).astype(jnp.int32)

            def cond(state):
                rem, go = state
                return jnp.logical_and(rem >= WALK_SUBS * SUB, go > 0)

            def body(state, t=t, unfinished=unfinished):
                rem, _ = state
                start = pl.multiple_of(rem - WALK_SUBS * SUB, SUB)
                span([(t, 0, SUB, start, WALK_SUBS, False, False)], False)
                return start, unfinished()

            remaining = jnp.maximum(tile - NEAR_SUBS - 1, 0) * SUB
            rem, go = lax.while_loop(cond, body, (remaining, unfinished()))
            for tail in range(1, WALK_SUBS):
                @pl.when(jnp.logical_and(rem == tail * SUB, go > 0))
                def _(t=t, tail=tail):
                    span([(t, 0, SUB, 0, tail, False, False)], False)

    o_ref[...] = acc_ref[...].astype(o_ref.dtype)


def _attention(q, k, v, *, batch, seq):
    n, attn_dim = q.shape
    tq = ATTN_TILES * SUB
    nq = seq // tq
    return pl.pallas_call(
        _attn_kernel,
        grid=(batch, nq),
        in_specs=[
            pl.BlockSpec((tq, attn_dim), lambda b, i: (b * nq + i, 0)),
            pl.BlockSpec((seq, attn_dim), lambda b, i: (b, 0)),
            pl.BlockSpec((seq, attn_dim), lambda b, i: (b, 0)),
        ],
        out_specs=pl.BlockSpec((tq, attn_dim), lambda b, i: (b * nq + i, 0)),
        out_shape=jax.ShapeDtypeStruct((n, attn_dim), BF16),
        scratch_shapes=[pltpu.VMEM((tq, attn_dim), F32),
                        pltpu.VMEM((ATTN_TILES, attn_dim // LANES, HEADS_PER_BLOCK * SUB, SUB), F32),
                        pltpu.VMEM((ATTN_TILES, SUBLANES, SUB), F32)],
        compiler_params=pltpu.CompilerParams(
            dimension_semantics=("arbitrary", "arbitrary"), vmem_limit_bytes=VMEM_LIMIT),
    )(q, k, v)


def _ffn_kernel(x_ref, attn_ref, conv_ref, wo_ref, gain_ref, wg_ref, wu_ref, wd_ref, o_ref, act_ref):
    attn_dim = attn_ref.shape[1]
    d_ff = wg_ref.shape[1]
    mix = (jnp.dot(attn_ref[...], wo_ref[0:attn_dim, :], preferred_element_type=F32)
           + jnp.dot(conv_ref[...], wo_ref[attn_dim:, :], preferred_element_type=F32))
    x1 = x_ref[...] + mix
    h = ((x1 * _rms_scale(x1)) * gain_ref[...]).astype(BF16)
    for c0 in range(0, d_ff, FF_CHUNK):
        g = jnp.dot(h, wg_ref[:, c0:c0 + FF_CHUNK], preferred_element_type=F32)
        u = jnp.dot(h, wu_ref[:, c0:c0 + FF_CHUNK], preferred_element_type=F32)
        act_ref[:, c0:c0 + FF_CHUNK] = ((g * jax.nn.sigmoid(g)) * u).astype(BF16)
    o_ref[...] = x1 + jnp.dot(act_ref[...], wd_ref[...], preferred_element_type=F32)


def _ffn(x2d, attn, conv, w_out, gain, w_gate, w_up, w_down):
    n, d = x2d.shape
    tm = FFN_ROW_TILE
    d_ff = w_gate.shape[1]
    const = lambda i: (0, 0)
    row = lambda i: (i, 0)
    resident = functools.partial(pl.BlockSpec, index_map=const, pipeline_mode=pl.Buffered(1))
    return pl.pallas_call(
        _ffn_kernel,
        grid=(n // tm,),
        in_specs=[
            pl.BlockSpec((tm, d), row),
            pl.BlockSpec((tm, attn.shape[1]), row),
            pl.BlockSpec((tm, conv.shape[1]), row),
            resident(w_out.shape),
            pl.BlockSpec((1, d), const),
            resident(w_gate.shape),
            resident(w_up.shape),
            resident(w_down.shape),
        ],
        out_specs=pl.BlockSpec((tm, d), row),
        out_shape=jax.ShapeDtypeStruct((n, d), F32),
        scratch_shapes=[pltpu.VMEM((tm, d_ff), BF16)],
        compiler_params=pltpu.CompilerParams(
            dimension_semantics=("arbitrary",), vmem_limit_bytes=VMEM_LIMIT),
    )(x2d, attn, conv, w_out, gain, w_gate, w_up, w_down)


def kernel(x, norm_mix, w_in, q_norm, k_norm, conv_w, w_out, norm_ffn, w_gate, w_up, w_down):
    batch, seq, d = x.shape
    depth = w_in.shape[0]
    conv_dim = conv_w.shape[2]
    attn_dim = w_out.shape[1] - conv_dim
    n_heads = attn_dim // HEAD_DIM
    assert seq % ROW_TILE == 0 and seq % (ATTN_TILES * SUB) == 0 and attn_dim % MXU_DIM == 0
    assert (batch * seq) % FFN_ROW_TILE == 0
    assert w_gate.shape[2] % FF_CHUNK == 0 and conv_w.shape[1] == CONV_SHIFTS + 1

    xf = x.reshape(batch * seq, d)
    for l in range(depth):
        q, k, v, conv, wo, wg, wu, wd = _inproj(
            xf, norm_mix[l][None, :], w_in, l,
            jnp.tile(q_norm[l], n_heads)[None, :], jnp.tile(k_norm[l], n_heads)[None, :], conv_w[l],
            (w_out, w_gate, w_up, w_down), seq=seq, attn_dim=attn_dim, conv_dim=conv_dim)
        attn = _attention(q, k, v, batch=batch, seq=seq)
        xf = _ffn(xf, attn, conv, wo, norm_ffn[l][None, :], wg, wu, wd)
    return xf.reshape(batch, seq, d)
```

```python
import functools

import jax
import jax.numpy as jnp
from jax import lax
from jax.experimental import pallas as pl
from jax.experimental.pallas import tpu as pltpu

F32 = jnp.float32
BF16 = jnp.bfloat16

HEAD_DIM = 64
EPS = 1e-6
LANES = 128
SUBLANES = 8
BF16_SUBLANES = 16
W_CAST_ROWS = 128
MXU_DIM = 256
HEADS_PER_BLOCK = LANES // HEAD_DIM
SUB = 128
LOG2E = 1.4426950408889634
LN2 = 0.6931471805599453
SKIP_ABOVE = 104.0

ROW_TILE = 1024
FFN_ROW_TILE = 1024
ATTN_TILES = 8
NEAR_SUBS = 1
FAR_ROWS = 48
WALK_SUBS = 2
FF_CHUNK = 256
CONV_SHIFTS = 2
VMEM_LIMIT = 56 * 1024 * 1024


def _rms_scale(x):
    return lax.rsqrt(jnp.mean(x * x, axis=-1, keepdims=True) + EPS)


def _inproj_kernel(x_ref, gain_ref, w_ref, qg_ref, kg_ref, cw_ref, *rest,
                   tiles_per_seq, attn_dim, conv_dim, n_cast):
    cast_in, (q_ref, k_ref, v_ref, c_ref) = rest[:n_cast], rest[n_cast:n_cast + 4]
    cast_out, (halo_ref, wbf_ref) = rest[n_cast + 4:2 * n_cast + 4], rest[2 * n_cast + 4:]
    tm = x_ref.shape[0]
    i = pl.program_id(0)

    @pl.when(i == 0)
    def _():
        def cast_rows(c, _):
            rows = pl.ds(pl.multiple_of(c * W_CAST_ROWS, W_CAST_ROWS), W_CAST_ROWS)
            wbf_ref[rows, :] = w_ref[rows, :].astype(BF16)
            return 0
        lax.fori_loop(0, w_ref.shape[0] // W_CAST_ROWS, cast_rows, 0)
        halo_ref[...] = jnp.zeros_like(halo_ref)

    for src, dst in zip(cast_in, cast_out):
        dst[...] = src[...].astype(BF16)

    x = x_ref[...]
    h = ((x * _rms_scale(x)) * gain_ref[...]).astype(BF16)

    def proj(c0, width):
        return jnp.dot(h, wbf_ref[:, c0:c0 + width], preferred_element_type=F32)

    r = lax.broadcasted_iota(jnp.int32, (MXU_DIM, MXU_DIM), 0) // HEAD_DIM
    c = lax.broadcasted_iota(jnp.int32, (MXU_DIM, MXU_DIM), 1) // HEAD_DIM
    seg = jnp.where(r == c, 1.0, 0.0).astype(BF16)

    def head_norm(p, gain):
        p2 = (p * p).astype(BF16)
        ssq = jnp.concatenate(
            [jnp.dot(p2[:, s:s + MXU_DIM], seg, preferred_element_type=F32)
             for s in range(0, attn_dim, MXU_DIM)], axis=1)
        return (p * lax.rsqrt(ssq * (1.0 / HEAD_DIM) + EPS)) * gain

    q_ref[...] = (head_norm(proj(0, attn_dim), qg_ref[...]) * (HEAD_DIM ** -0.5)).astype(BF16)
    k_ref[...] = head_norm(proj(attn_dim, attn_dim), kg_ref[...]).astype(BF16)

    cb = proj(3 * attn_dim, conv_dim)
    hh = proj(3 * attn_dim + conv_dim, conv_dim) * proj(3 * attn_dim + 2 * conv_dim, conv_dim)
    row = lax.broadcasted_iota(jnp.int32, (SUBLANES, conv_dim), 0)
    seq_start = i % tiles_per_seq == 0

    def shift_down(a, n):
        prev_last = jnp.where(seq_start, 0.0, halo_ref[n, SUBLANES - 1:SUBLANES, :])
        halo_ref[n] = a[tm - SUBLANES:tm, :]
        rolled = pltpu.roll(a, 1, 0)
        head = jnp.where(row == 0, jnp.broadcast_to(prev_last, (SUBLANES, conv_dim)), rolled[0:SUBLANES])
        return jnp.concatenate([head, rolled[SUBLANES:]], axis=0)

    y = cw_ref[2:3, :] * hh + shift_down(cw_ref[1:2, :] * hh + shift_down(cw_ref[0:1, :] * hh, 0), 1)
    c_ref[...] = (cb * y).astype(BF16)

    v_ref[...] = proj(2 * attn_dim, attn_dim).astype(BF16)


def _cast_specs(weights, layer, n_steps, step_of):
    in_specs, out_specs, shapes = [], [], []
    for w in weights:
        _, rows, cols = w.shape
        chunks = max(c for c in range(1, n_steps + 1) if rows % c == 0 and (rows // c) % BF16_SUBLANES == 0)
        chunk_of = lambda *idx, chunks=chunks: jnp.minimum(step_of(*idx), chunks - 1)
        in_specs.append(pl.BlockSpec((None, rows // chunks, cols),
                                     lambda *idx, chunk_of=chunk_of: (layer, chunk_of(*idx), 0)))
        out_specs.append(pl.BlockSpec((rows // chunks, cols), lambda *idx, chunk_of=chunk_of: (chunk_of(*idx), 0)))
        shapes.append(jax.ShapeDtypeStruct((rows, cols), BF16))
    return in_specs, out_specs, shapes


def _inproj(x2d, gain, w_in, layer, q_gain, k_gain, conv_w, later_weights, *, seq, attn_dim, conv_dim):
    n, d = x2d.shape
    tm = ROW_TILE
    n_steps = n // tm
    cols = w_in.shape[2]
    const = lambda i: (0, 0)
    row = lambda i: (i, 0)
    cast_in_specs, cast_out_specs, cast_shapes = _cast_specs(later_weights, layer, n_steps, lambda i: i)
    out_shapes = [jax.ShapeDtypeStruct((n, attn_dim), BF16)] * 3 + [jax.ShapeDtypeStruct((n, conv_dim), BF16)]
    return pl.pallas_call(
        functools.partial(_inproj_kernel, tiles_per_seq=seq // tm, attn_dim=attn_dim, conv_dim=conv_dim,
                          n_cast=len(later_weights)),
        grid=(n_steps,),
        in_specs=[
            pl.BlockSpec((tm, d), row),
            pl.BlockSpec((1, d), const),
            pl.BlockSpec((None, d, cols), lambda i: (layer, 0, 0), pipeline_mode=pl.Buffered(1)),
            pl.BlockSpec((1, attn_dim), const),
            pl.BlockSpec((1, attn_dim), const),
            pl.BlockSpec(conv_w.shape, const),
        ] + cast_in_specs,
        out_specs=[pl.BlockSpec((tm, attn_dim), row)] * 3 + [pl.BlockSpec((tm, conv_dim), row)] + cast_out_specs,
        out_shape=out_shapes + cast_shapes,
        scratch_shapes=[pltpu.VMEM((CONV_SHIFTS, SUBLANES, conv_dim), F32), pltpu.VMEM((d, cols), BF16)],
        compiler_params=pltpu.CompilerParams(
            dimension_semantics=("arbitrary",), vmem_limit_bytes=VMEM_LIMIT),
    )(x2d, gain, w_in, q_gain, k_gain, conv_w, *later_weights)


def _min_to_vreg(arrays):
    chunks = [a[r:r + SUBLANES] for a in arrays for r in range(0, a.shape[0], SUBLANES)]
    while len(chunks) > 1:
        odd = chunks[-1:] if len(chunks) % 2 else []
        chunks = [jnp.minimum(a, b) for a, b in zip(chunks[0::2], chunks[1::2])] + odd
    return chunks[0]


def _attn_kernel(q_ref, k_ref, v_ref, o_ref, acc_ref, carry_ref, min_ref):
    n_tiles = q_ref.shape[0] // SUB
    n_blocks = q_ref.shape[1] // LANES
    i = pl.program_id(1)
    assert n_tiles >= NEAR_SUBS + 1

    r = lax.broadcasted_iota(jnp.int32, (2 * SUB, 2 * SUB), 0) % SUB
    c = lax.broadcasted_iota(jnp.int32, (2 * SUB, 2 * SUB), 1)
    tri = jnp.where((c >= SUB) | (r >= c), 1.0, 0.0).astype(BF16)
    causal = (lax.broadcasted_iota(jnp.int32, (HEADS_PER_BLOCK * SUB, SUB), 1)
              < lax.broadcasted_iota(jnp.int32, (HEADS_PER_BLOCK * SUB, SUB), 0) % SUB)

    def head_rows(row0, nrows):
        return [slice(h * SUB + row0, h * SUB + row0 + nrows) for h in range(HEADS_PER_BLOCK)]

    def softplus_split(z_s, masked):
        e = jnp.exp2(jnp.abs(z_s) * (-LOG2E))
        sp = jnp.maximum(z_s, 0.0) + jnp.log(1.0 + e)
        if masked:
            sp = jnp.where(causal[:z_s.shape[0]], sp, 0.0)
        hi = sp.astype(BF16)
        lo = (sp - hi.astype(F32)).astype(BF16)
        return jnp.concatenate([hi, lo], axis=1)

    def span(jobs, whole_tiles):
        z_all, cs_all = {}, {}
        for n, (t, row0, nrows, start, nsub, diag, first) in enumerate(jobs):
            assert not diag or (row0 == 0 and nrows == SUB)
            qrows = slice(t * SUB + row0, t * SUB + row0 + nrows)
            lane = lax.broadcasted_iota(jnp.int32, (nrows, LANES), 1)
            for p in range(n_blocks):
                cols = slice(p * LANES, (p + 1) * LANES)
                q = q_ref[qrows, cols]
                q2 = jnp.concatenate([jnp.where((lane // HEAD_DIM) == h, q, jnp.zeros_like(q))
                                      for h in range(HEADS_PER_BLOCK)], axis=0)
                z = lax.dot_general(q2, k_ref[pl.ds(start, nsub * SUB), cols], (((1,), (1,)), ((), ())),
                                    preferred_element_type=F32)
                z_all[n, p] = z
                lhs_parts = [softplus_split(z[:, s * SUB:(s + 1) * SUB], diag and s == nsub - 1)
                             for s in reversed(range(nsub))]
                cs_all[n, p] = jnp.dot(jnp.concatenate(lhs_parts, axis=0), tri, preferred_element_type=F32)
        latest = {}
        for n, (t, row0, nrows, start, nsub, diag, first) in enumerate(jobs):
            qrows = slice(t * SUB + row0, t * SUB + row0 + nrows)
            lane = lax.broadcasted_iota(jnp.int32, (nrows, LANES), 1)
            rows2 = HEADS_PER_BLOCK * nrows
            for p in range(n_blocks):
                cols = slice(p * LANES, (p + 1) * LANES)
                carry = None if first else jnp.concatenate([carry_ref[t, p, rs] for rs in head_rows(row0, nrows)], axis=0)
                a_cols = [None] * nsub
                for idx, s in enumerate(reversed(range(nsub))):
                    cs = cs_all[n, p][idx * rows2:(idx + 1) * rows2]
                    g = cs[:, :SUB] if carry is None else cs[:, :SUB] + carry
                    carry = cs[:, SUB:] if carry is None else carry + cs[:, SUB:]
                    a = jnp.exp2((z_all[n, p][:, s * SUB:(s + 1) * SUB] - g) * LOG2E)
                    if diag and s == nsub - 1:
                        a = jnp.where(causal, a, 0.0)
                    a_cols[s] = a.astype(BF16)
                for h, rs in enumerate(head_rows(row0, nrows)):
                    carry_ref[t, p, rs] = carry[h * nrows:(h + 1) * nrows]
                latest.setdefault(t, {}).setdefault(p, []).append((row0, nrows, carry))
                o2 = jnp.dot(jnp.concatenate(a_cols, axis=1), v_ref[pl.ds(start, nsub * SUB), cols],
                             preferred_element_type=F32)
                out = o2[0:nrows]
                for h in range(1, HEADS_PER_BLOCK):
                    out = jnp.where((lane // HEAD_DIM) == h, o2[h * nrows:(h + 1) * nrows], out)
                if first:
                    acc_ref[qrows, cols] = out
                else:
                    acc_ref[qrows, cols] += out
        for t, by_block in latest.items():
            if whole_tiles:
                assert all(len(e) == 1 and e[0][:2] == (0, SUB) for e in by_block.values())
                min_ref[t] = _min_to_vreg([e[0][2] for e in by_block.values()])
            else:
                min_ref[t] = _min_to_vreg([carry_ref[t, p] for p in range(n_blocks)])

    def window(tiles):
        n_keys = (NEAR_SUBS + 2) * SUB
        far = [slice(h * SUB, h * SUB + FAR_ROWS) for h in range(HEADS_PER_BLOCK)]
        rest = [slice(h * SUB + FAR_ROWS, (h + 1) * SUB) for h in range(HEADS_PER_BLOCK)]
        lane = lax.broadcasted_iota(jnp.int32, (SUB, LANES), 1)
        lane_far = lax.broadcasted_iota(jnp.int32, (FAR_ROWS, LANES), 1)
        ctx = {}
        for t, tile in tiles:
            start = pl.multiple_of((tile - NEAR_SUBS - 1) * SUB, SUB)
            for p in range(n_blocks):
                cols = slice(p * LANES, (p + 1) * LANES)
                q = q_ref[t * SUB:(t + 1) * SUB, cols]
                q2 = jnp.concatenate([jnp.where((lane // HEAD_DIM) == h, q, jnp.zeros_like(q))
                                      for h in range(HEADS_PER_BLOCK)], axis=0)
                z = lax.dot_general(q2, k_ref[pl.ds(start, n_keys), cols], (((1,), (1,)), ((), ())),
                                    preferred_element_type=F32)
                z_far = jnp.concatenate([z[rs, 0:SUB] for rs in far], axis=0)
                lhs = [softplus_split(z[:, s * SUB:(s + 1) * SUB], s == NEAR_SUBS + 1)
                       for s in range(NEAR_SUBS + 1, 0, -1)] + [softplus_split(z_far, False)]
                cs = jnp.dot(jnp.concatenate(lhs, axis=0), tri, preferred_element_type=F32)
                ctx[t, p] = (start, z, z_far, cs)
        for t, tile in tiles:
            parts = []
            for p in range(n_blocks):
                cols = slice(p * LANES, (p + 1) * LANES)
                start, z, z_far, cs = ctx[t, p]
                carry, a_cols = None, []
                for idx, s in enumerate(range(NEAR_SUBS + 1, 0, -1)):
                    blk = cs[idx * 2 * SUB:(idx + 1) * 2 * SUB]
                    g = blk[:, :SUB] if carry is None else blk[:, :SUB] + carry
                    carry = blk[:, SUB:] if carry is None else carry + blk[:, SUB:]
                    a = jnp.exp2((z[:, s * SUB:(s + 1) * SUB] - g) * LOG2E)
                    if s == NEAR_SUBS + 1:
                        a = jnp.where(causal, a, 0.0)
                    a_cols.insert(0, a.astype(BF16))
                blk = cs[(NEAR_SUBS + 1) * 2 * SUB:]
                carry_far = jnp.concatenate([carry[rs] for rs in far], axis=0)
                a_far = jnp.exp2((z_far - (blk[:, :SUB] + carry_far)) * LOG2E).astype(BF16)
                carry_far = carry_far + blk[:, SUB:]
                for h in range(HEADS_PER_BLOCK):
                    carry_ref[t, p, far[h]] = carry_far[h * FAR_ROWS:(h + 1) * FAR_ROWS]
                    carry_ref[t, p, rest[h]] = carry[rest[h]]
                    parts.append(carry[rest[h]])
                parts.append(carry_far)
                o2 = jnp.dot(jnp.concatenate(a_cols, axis=1), v_ref[pl.ds(start + SUB, (NEAR_SUBS + 1) * SUB), cols],
                             preferred_element_type=F32)
                o_far = jnp.dot(a_far, v_ref[pl.ds(start, SUB), cols], preferred_element_type=F32)
                out, out_far = o2[0:SUB], o_far[0:FAR_ROWS]
                for h in range(1, HEADS_PER_BLOCK):
                    out = jnp.where((lane // HEAD_DIM) == h, o2[h * SUB:(h + 1) * SUB], out)
                    out_far = jnp.where((lane_far // HEAD_DIM) == h,
                                        o_far[h * FAR_ROWS:(h + 1) * FAR_ROWS], out_far)
                acc_ref[t * SUB:t * SUB + FAR_ROWS, cols] = out[:FAR_ROWS] + out_far
                acc_ref[t * SUB + FAR_ROWS:(t + 1) * SUB, cols] = out[FAR_ROWS:]
            min_ref[t] = _min_to_vreg(parts)

    @pl.when(i == 0)
    def _():
        early = [t for t in range(n_tiles) if t <= NEAR_SUBS]
        span([(t, 0, SUB, 0, t + 1, True, True) for t in early], True)
        window([(t, t) for t in range(n_tiles) if t > NEAR_SUBS])

    @pl.when(i > 0)
    def _():
        window([(t, i * n_tiles + t) for t in range(n_tiles)])

    @pl.when(jnp.min(min_ref[...]) < SKIP_ABOVE)
    def _():
        for t in range(n_tiles):
            tile = i * n_tiles + t
            if FAR_ROWS < SUB:
                @pl.when(jnp.logical_and(tile > NEAR_SUBS, jnp.min(min_ref[t]) < SKIP_ABOVE))
                def _(t=t, tile=tile):
                    start = pl.multiple_of((tile - NEAR_SUBS - 1) * SUB, SUB)
                    span([(t, FAR_ROWS, SUB - FAR_ROWS, start, 1, False, False)], False)

            def unfinished(t=t):
                return (jnp.min(min_ref[t]) < SKIP_ABOVE).astype(jnp.int32)

            def cond(state):
                rem, go = state
                return jnp.logical_and(rem >= WALK_SUBS * SUB, go > 0)

            def body(state, t=t, unfinished=unfinished):
                rem, _ = state
                start = pl.multiple_of(rem - WALK_SUBS * SUB, SUB)
                span([(t, 0, SUB, start, WALK_SUBS, False, False)], False)
                return start, unfinished()

            remaining = jnp.maximum(tile - NEAR_SUBS - 1, 0) * SUB
            rem, go = lax.while_loop(cond, body, (remaining, unfinished()))
            for tail in range(1, WALK_SUBS):
                @pl.when(jnp.logical_and(rem == tail * SUB, go > 0))
                def _(t=t, tail=tail):
                    span([(t, 0, SUB, 0, tail, False, False)], False)

    o_ref[...] = acc_ref[...].astype(o_ref.dtype)


def _attention(q, k, v, *, batch, seq):
    n, attn_dim = q.shape
    tq = ATTN_TILES * SUB
    nq = seq // tq
    return pl.pallas_call(
        _attn_kernel,
        grid=(batch, nq),
        in_specs=[
            pl.BlockSpec((tq, attn_dim), lambda b, i: (b * nq + i, 0)),
            pl.BlockSpec((seq, attn_dim), lambda b, i: (b, 0)),
            pl.BlockSpec((seq, attn_dim), lambda b, i: (b, 0)),
        ],
        out_specs=pl.BlockSpec((tq, attn_dim), lambda b, i: (b * nq + i, 0)),
        out_shape=jax.ShapeDtypeStruct((n, attn_dim), BF16),
        scratch_shapes=[pltpu.VMEM((tq, attn_dim), F32),
                        pltpu.VMEM((ATTN_TILES, attn_dim // LANES, HEADS_PER_BLOCK * SUB, SUB), F32),
                        pltpu.VMEM((ATTN_TILES, SUBLANES, SUB), F32)],
        compiler_params=pltpu.CompilerParams(
            dimension_semantics=("arbitrary", "arbitrary"), vmem_limit_bytes=VMEM_LIMIT),
    )(q, k, v)


def _ffn_kernel(x_ref, attn_ref, conv_ref, wo_ref, gain_ref, wg_ref, wu_ref, wd_ref, o_ref, act_ref):
    attn_dim = attn_ref.shape[1]
    d_ff = wg_ref.shape[1]
    mix = (jnp.dot(attn_ref[...], wo_ref[0:attn_dim, :], preferred_element_type=F32)
           + jnp.dot(conv_ref[...], wo_ref[attn_dim:, :], preferred_element_type=F32))
    x1 = x_ref[...] + mix
    h = ((x1 * _rms_scale(x1)) * gain_ref[...]).astype(BF16)
    for c0 in range(0, d_ff, FF_CHUNK):
        g = jnp.dot(h, wg_ref[:, c0:c0 + FF_CHUNK], preferred_element_type=F32)
        u = jnp.dot(h, wu_ref[:, c0:c0 + FF_CHUNK], preferred_element_type=F32)
        act_ref[:, c0:c0 + FF_CHUNK] = ((g * jax.nn.sigmoid(g)) * u).astype(BF16)
    o_ref[...] = x1 + jnp.dot(act_ref[...], wd_ref[...], preferred_element_type=F32)


def _ffn(x2d, attn, conv, w_out, gain, w_gate, w_up, w_down):
    n, d = x2d.shape
    tm = FFN_ROW_TILE
    d_ff = w_gate.shape[1]
    const = lambda i: (0, 0)
    row = lambda i: (i, 0)
    resident = functools.partial(pl.BlockSpec, index_map=const, pipeline_mode=pl.Buffered(1))
    return pl.pallas_call(
        _ffn_kernel,
        grid=(n // tm,),
        in_specs=[
            pl.BlockSpec((tm, d), row),
            pl.BlockSpec((tm, attn.shape[1]), row),
            pl.BlockSpec((tm, conv.shape[1]), row),
            resident(w_out.shape),
            pl.BlockSpec((1, d), const),
            resident(w_gate.shape),
            resident(w_up.shape),
            resident(w_down.shape),
        ],
        out_specs=pl.BlockSpec((tm, d), row),
        out_shape=jax.ShapeDtypeStruct((n, d), F32),
        scratch_shapes=[pltpu.VMEM((tm, d_ff), BF16)],
        compiler_params=pltpu.CompilerParams(
            dimension_semantics=("arbitrary",), vmem_limit_bytes=VMEM_LIMIT),
    )(x2d, attn, conv, w_out, gain, w_gate, w_up, w_down)


def kernel(x, norm_mix, w_in, q_norm, k_norm, conv_w, w_out, norm_ffn, w_gate, w_up, w_down):
    batch, seq, d = x.shape
    depth = w_in.shape[0]
    conv_dim = conv_w.shape[2]
    attn_dim = w_out.shape[1] - conv_dim
    n_heads = attn_dim // HEAD_DIM
    assert seq % ROW_TILE == 0 and seq % (ATTN_TILES * SUB) == 0 and attn_dim % MXU_DIM == 0
    assert (batch * seq) % FFN_ROW_TILE == 0
    assert w_gate.shape[2] % FF_CHUNK == 0 and conv_w.shape[1] == CONV_SHIFTS + 1

    xf = x.reshape(batch * seq, d)
    for l in range(depth):
        q, k, v, conv, wo, wg, wu, wd = _inproj(
            xf, norm_mix[l][None, :], w_in, l,
            jnp.tile(q_norm[l], n_heads)[None, :], jnp.tile(k_norm[l], n_heads)[None, :], conv_w[l],
            (w_out, w_gate, w_up, w_down), seq=seq, attn_dim=attn_dim, conv_dim=conv_dim)
        attn = _attention(q, k, v, batch=batch, seq=seq)
        xf = _ffn(xf, attn, conv, wo, norm_ffn[l][None, :], wg, wu, wd)
    return xf.reshape(batch, seq, d)
```

```python
import functools

import jax
import jax.numpy as jnp
from jax import lax
from jax.experimental import pallas as pl
from jax.experimental.pallas import tpu as pltpu

F32 = jnp.float32
BF16 = jnp.bfloat16

HEAD_DIM = 64
EPS = 1e-6
LANES = 128
SUBLANES = 8
BF16_SUBLANES = 16
W_CAST_ROWS = 128
MXU_DIM = 256
HEADS_PER_BLOCK = LANES // HEAD_DIM
SUB = 128
LOG2E = 1.4426950408889634
LN2 = 0.6931471805599453
SKIP_ABOVE = 104.0

ROW_TILE = 1024
FFN_ROW_TILE = 1024
ATTN_TILES = 4
NEAR_SUBS = 1
FAR_ROWS = 48
WALK_SUBS = 2
FF_CHUNK = 256
CONV_SHIFTS = 2
VMEM_LIMIT = 56 * 1024 * 1024


def _rms_scale(x):
    return lax.rsqrt(jnp.mean(x * x, axis=-1, keepdims=True) + EPS)


def _inproj_kernel(x_ref, gain_ref, w_ref, qg_ref, kg_ref, cw_ref, *rest,
                   tiles_per_seq, attn_dim, conv_dim, n_cast):
    cast_in, (q_ref, k_ref, v_ref, c_ref) = rest[:n_cast], rest[n_cast:n_cast + 4]
    cast_out, (halo_ref, wbf_ref) = rest[n_cast + 4:2 * n_cast + 4], rest[2 * n_cast + 4:]
    tm = x_ref.shape[0]
    i = pl.program_id(0)

    @pl.when(i == 0)
    def _():
        def cast_rows(c, _):
            rows = pl.ds(pl.multiple_of(c * W_CAST_ROWS, W_CAST_ROWS), W_CAST_ROWS)
            wbf_ref[rows, :] = w_ref[rows, :].astype(BF16)
            return 0
        lax.fori_loop(0, w_ref.shape[0] // W_CAST_ROWS, cast_rows, 0)
        halo_ref[...] = jnp.zeros_like(halo_ref)

    for src, dst in zip(cast_in, cast_out):
        dst[...] = src[...].astype(BF16)

    x = x_ref[...]
    h = ((x * _rms_scale(x)) * gain_ref[...]).astype(BF16)

    def proj(c0, width):
        return jnp.dot(h, wbf_ref[:, c0:c0 + width], preferred_element_type=F32)

    r = lax.broadcasted_iota(jnp.int32, (MXU_DIM, MXU_DIM), 0) // HEAD_DIM
    c = lax.broadcasted_iota(jnp.int32, (MXU_DIM, MXU_DIM), 1) // HEAD_DIM
    seg = jnp.where(r == c, 1.0, 0.0).astype(BF16)

    def head_norm(p, gain):
        p2 = (p * p).astype(BF16)
        ssq = jnp.concatenate(
            [jnp.dot(p2[:, s:s + MXU_DIM], seg, preferred_element_type=F32)
             for s in range(0, attn_dim, MXU_DIM)], axis=1)
        return (p * lax.rsqrt(ssq * (1.0 / HEAD_DIM) + EPS)) * gain

    q_ref[...] = (head_norm(proj(0, attn_dim), qg_ref[...]) * (HEAD_DIM ** -0.5)).astype(BF16)
    k_ref[...] = head_norm(proj(attn_dim, attn_dim), kg_ref[...]).astype(BF16)

    cb = proj(3 * attn_dim, conv_dim)
    hh = proj(3 * attn_dim + conv_dim, conv_dim) * proj(3 * attn_dim + 2 * conv_dim, conv_dim)
    row = lax.broadcasted_iota(jnp.int32, (SUBLANES, conv_dim), 0)
    seq_start = i % tiles_per_seq == 0

    def shift_down(a, n):
        prev_last = jnp.where(seq_start, 0.0, halo_ref[n, SUBLANES - 1:SUBLANES, :])
        halo_ref[n] = a[tm - SUBLANES:tm, :]
        rolled = pltpu.roll(a, 1, 0)
        head = jnp.where(row == 0, jnp.broadcast_to(prev_last, (SUBLANES, conv_dim)), rolled[0:SUBLANES])
        return jnp.concatenate([head, rolled[SUBLANES:]], axis=0)

    y = cw_ref[2:3, :] * hh + shift_down(cw_ref[1:2, :] * hh + shift_down(cw_ref[0:1, :] * hh, 0), 1)
    c_ref[...] = (cb * y).astype(BF16)

    v_ref[...] = proj(2 * attn_dim, attn_dim).astype(BF16)


def _cast_specs(weights, layer, n_steps, step_of):
    in_specs, out_specs, shapes = [], [], []
    for w in weights:
        _, rows, cols = w.shape
        chunks = max(c for c in range(1, n_steps + 1) if rows % c == 0 and (rows // c) % BF16_SUBLANES == 0)
        chunk_of = lambda *idx, chunks=chunks: jnp.minimum(step_of(*idx), chunks - 1)
        in_specs.append(pl.BlockSpec((None, rows // chunks, cols),
                                     lambda *idx, chunk_of=chunk_of: (layer, chunk_of(*idx), 0)))
        out_specs.append(pl.BlockSpec((rows // chunks, cols), lambda *idx, chunk_of=chunk_of: (chunk_of(*idx), 0)))
        shapes.append(jax.ShapeDtypeStruct((rows, cols), BF16))
    return in_specs, out_specs, shapes


def _inproj(x2d, gain, w_in, layer, q_gain, k_gain, conv_w, later_weights, *, seq, attn_dim, conv_dim):
    n, d = x2d.shape
    tm = ROW_TILE
    n_steps = n // tm
    cols = w_in.shape[2]
    const = lambda i: (0, 0)
    row = lambda i: (i, 0)
    cast_in_specs, cast_out_specs, cast_shapes = _cast_specs(later_weights, layer, n_steps, lambda i: i)
    out_shapes = [jax.ShapeDtypeStruct((n, attn_dim), BF16)] * 3 + [jax.ShapeDtypeStruct((n, conv_dim), BF16)]
    return pl.pallas_call(
        functools.partial(_inproj_kernel, tiles_per_seq=seq // tm, attn_dim=attn_dim, conv_dim=conv_dim,
                          n_cast=len(later_weights)),
        grid=(n_steps,),
        in_specs=[
            pl.BlockSpec((tm, d), row),
            pl.BlockSpec((1, d), const),
            pl.BlockSpec((None, d, cols), lambda i: (layer, 0, 0), pipeline_mode=pl.Buffered(1)),
            pl.BlockSpec((1, attn_dim), const),
            pl.BlockSpec((1, attn_dim), const),
            pl.BlockSpec(conv_w.shape, const),
        ] + cast_in_specs,
        out_specs=[pl.BlockSpec((tm, attn_dim), row)] * 3 + [pl.BlockSpec((tm, conv_dim), row)] + cast_out_specs,
        out_shape=out_shapes + cast_shapes,
        scratch_shapes=[pltpu.VMEM((CONV_SHIFTS, SUBLANES, conv_dim), F32), pltpu.VMEM((d, cols), BF16)],
        compiler_params=pltpu.CompilerParams(
            dimension_semantics=("arbitrary",), vmem_limit_bytes=VMEM_LIMIT),
    )(x2d, gain, w_in, q_gain, k_gain, conv_w, *later_weights)


def _min_to_vreg(arrays):
    chunks = [a[r:r + SUBLANES] for a in arrays for r in range(0, a.shape[0], SUBLANES)]
    while len(chunks) > 1:
        odd = chunks[-1:] if len(chunks) % 2 else []
        chunks = [jnp.minimum(a, b) for a, b in zip(chunks[0::2], chunks[1::2])] + odd
    return chunks[0]


def _attn_kernel(q_ref, k_ref, v_ref, o_ref, acc_ref, carry_ref, min_ref):
    n_tiles = q_ref.shape[0] // SUB
    n_blocks = q_ref.shape[1] // LANES
    i = pl.program_id(1)
    assert n_tiles >= NEAR_SUBS + 1

    r = lax.broadcasted_iota(jnp.int32, (2 * SUB, 2 * SUB), 0) % SUB
    c = lax.broadcasted_iota(jnp.int32, (2 * SUB, 2 * SUB), 1)
    tri = jnp.where((c >= SUB) | (r >= c), 1.0, 0.0).astype(BF16)
    causal = (lax.broadcasted_iota(jnp.int32, (HEADS_PER_BLOCK * SUB, SUB), 1)
              < lax.broadcasted_iota(jnp.int32, (HEADS_PER_BLOCK * SUB, SUB), 0) % SUB)

    def head_rows(row0, nrows):
        return [slice(h * SUB + row0, h * SUB + row0 + nrows) for h in range(HEADS_PER_BLOCK)]

    def softplus_split(z_s, masked):
        e = jnp.exp2(jnp.abs(z_s) * (-LOG2E))
        sp = jnp.maximum(z_s, 0.0) + jnp.log(1.0 + e)
        if masked:
            sp = jnp.where(causal[:z_s.shape[0]], sp, 0.0)
        hi = sp.astype(BF16)
        lo = (sp - hi.astype(F32)).astype(BF16)
        return jnp.concatenate([hi, lo], axis=1)

    def span(jobs, whole_tiles):
        z_all, cs_all = {}, {}
        for n, (t, row0, nrows, start, nsub, diag, first) in enumerate(jobs):
            assert not diag or (row0 == 0 and nrows == SUB)
            qrows = slice(t * SUB + row0, t * SUB + row0 + nrows)
            lane = lax.broadcasted_iota(jnp.int32, (nrows, LANES), 1)
            for p in range(n_blocks):
                cols = slice(p * LANES, (p + 1) * LANES)
                q = q_ref[qrows, cols]
                q2 = jnp.concatenate([jnp.where((lane // HEAD_DIM) == h, q, jnp.zeros_like(q))
                                      for h in range(HEADS_PER_BLOCK)], axis=0)
                z = lax.dot_general(q2, k_ref[pl.ds(start, nsub * SUB), cols], (((1,), (1,)), ((), ())),
                                    preferred_element_type=F32)
                z_all[n, p] = z
                lhs_parts = [softplus_split(z[:, s * SUB:(s + 1) * SUB], diag and s == nsub - 1)
                             for s in reversed(range(nsub))]
                cs_all[n, p] = jnp.dot(jnp.concatenate(lhs_parts, axis=0), tri, preferred_element_type=F32)
        latest = {}
        for n, (t, row0, nrows, start, nsub, diag, first) in enumerate(jobs):
            qrows = slice(t * SUB + row0, t * SUB + row0 + nrows)
            lane = lax.broadcasted_iota(jnp.int32, (nrows, LANES), 1)
            rows2 = HEADS_PER_BLOCK * nrows
            for p in range(n_blocks):
                cols = slice(p * LANES, (p + 1) * LANES)
                carry = None if first else jnp.concatenate([carry_ref[t, p, rs] for rs in head_rows(row0, nrows)], axis=0)
                a_cols = [None] * nsub
                for idx, s in enumerate(reversed(range(nsub))):
                    cs = cs_all[n, p][idx * rows2:(idx + 1) * rows2]
                    g = cs[:, :SUB] if carry is None else cs[:, :SUB] + carry
                    carry = cs[:, SUB:] if carry is None else carry + cs[:, SUB:]
                    a = jnp.exp2((z_all[n, p][:, s * SUB:(s + 1) * SUB] - g) * LOG2E)
                    if diag and s == nsub - 1:
                        a = jnp.where(causal, a, 0.0)
                    a_cols[s] = a.astype(BF16)
                for h, rs in enumerate(head_rows(row0, nrows)):
                    carry_ref[t, p, rs] = carry[h * nrows:(h + 1) * nrows]
                latest.setdefault(t, {}).setdefault(p, []).append((row0, nrows, carry))
                o2 = jnp.dot(jnp.concatenate(a_cols, axis=1), v_ref[pl.ds(start, nsub * SUB), cols],
                             preferred_element_type=F32)
                out = o2[0:nrows]
                for h in range(1, HEADS_PER_BLOCK):
                    out = jnp.where((lane // HEAD_DIM) == h, o2[h * nrows:(h + 1) * nrows], out)
                if first:
                    acc_ref[qrows, cols] = out
                else:
                    acc_ref[qrows, cols] += out
        for t, by_block in latest.items():
            if whole_tiles:
                assert all(len(e) == 1 and e[0][:2] == (0, SUB) for e in by_block.values())
                min_ref[t] = _min_to_vreg([e[0][2] for e in by_block.values()])
            else:
                min_ref[t] = _min_to_vreg([carry_ref[t, p] for p in range(n_blocks)])

    def window(tiles):
        n_keys = (NEAR_SUBS + 2) * SUB
        far = [slice(h * SUB, h * SUB + FAR_ROWS) for h in range(HEADS_PER_BLOCK)]
        rest = [slice(h * SUB + FAR_ROWS, (h + 1) * SUB) for h in range(HEADS_PER_BLOCK)]
        lane = lax.broadcasted_iota(jnp.int32, (SUB, LANES), 1)
        lane_far = lax.broadcasted_iota(jnp.int32, (FAR_ROWS, LANES), 1)
        ctx = {}
        for t, tile in tiles:
            start = pl.multiple_of((tile - NEAR_SUBS - 1) * SUB, SUB)
            for p in range(n_blocks):
                cols = slice(p * LANES, (p + 1) * LANES)
                q = q_ref[t * SUB:(t + 1) * SUB, cols]
                q2 = jnp.concatenate([jnp.where((lane // HEAD_DIM) == h, q, jnp.zeros_like(q))
                                      for h in range(HEADS_PER_BLOCK)], axis=0)
                z = lax.dot_general(q2, k_ref[pl.ds(start, n_keys), cols], (((1,), (1,)), ((), ())),
                                    preferred_element_type=F32)
                z_far = jnp.concatenate([z[rs, 0:SUB] for rs in far], axis=0)
                lhs = [softplus_split(z[:, s * SUB:(s + 1) * SUB], s == NEAR_SUBS + 1)
                       for s in range(NEAR_SUBS + 1, 0, -1)] + [softplus_split(z_far, False)]
                cs = jnp.dot(jnp.concatenate(lhs, axis=0), tri, preferred_element_type=F32)
                ctx[t, p] = (start, z, z_far, cs)
        for t, tile in tiles:
            parts = []
            for p in range(n_blocks):
                cols = slice(p * LANES, (p + 1) * LANES)
                start, z, z_far, cs = ctx[t, p]
                carry, a_cols = None, []
                for idx, s in enumerate(range(NEAR_SUBS + 1, 0, -1)):
                    blk = cs[idx * 2 * SUB:(idx + 1) * 2 * SUB]
                    g = blk[:, :SUB] if carry is None else blk[:, :SUB] + carry
                    carry = blk[:, SUB:] if carry is None else carry + blk[:, SUB:]
                    a = jnp.exp2((z[:, s * SUB:(s + 1) * SUB] - g) * LOG2E)
                    if s == NEAR_SUBS + 1:
                        a = jnp.where(causal, a, 0.0)
                    a_cols.insert(0, a.astype(BF16))
                blk = cs[(NEAR_SUBS + 1) * 2 * SUB:]
                carry_far = jnp.concatenate([carry[rs] for rs in far], axis=0)
                a_far = jnp.exp2((z_far - (blk[:, :SUB] + carry_far)) * LOG2E).astype(BF16)
                carry_far = carry_far + blk[:, SUB:]
                for h in range(HEADS_PER_BLOCK):
                    carry_ref[t, p, far[h]] = carry_far[h * FAR_ROWS:(h + 1) * FAR_ROWS]
                    carry_ref[t, p, rest[h]] = carry[rest[h]]
                    parts.append(carry[rest[h]])
                parts.append(carry_far)
                o2 = jnp.dot(jnp.concatenate(a_cols, axis=1), v_ref[pl.ds(start + SUB, (NEAR_SUBS + 1) * SUB), cols],
                             preferred_element_type=F32)
                o_far = jnp.dot(a_far, v_ref[pl.ds(start, SUB), cols], preferred_element_type=F32)
                out, out_far = o2[0:SUB], o_far[0:FAR_ROWS]
                for h in range(1, HEADS_PER_BLOCK):
                    out = jnp.where((lane // HEAD_DIM) == h, o2[h * SUB:(h + 1) * SUB], out)
                    out_far = jnp.where((lane_far // HEAD_DIM) == h,
                                        o_far[h * FAR_ROWS:(h + 1) * FAR_ROWS], out_far)
                acc_ref[t * SUB:t * SUB + FAR_ROWS, cols] = out[:FAR_ROWS] + out_far
                acc_ref[t * SUB + FAR_ROWS:(t + 1) * SUB, cols] = out[FAR_ROWS:]
            min_ref[t] = _min_to_vreg(parts)

    @pl.when(i == 0)
    def _():
        early = [t for t in range(n_tiles) if t <= NEAR_SUBS]
        span([(t, 0, SUB, 0, t + 1, True, True) for t in early], True)
        window([(t, t) for t in range(n_tiles) if t > NEAR_SUBS])

    @pl.when(i > 0)
    def _():
        window([(t, i * n_tiles + t) for t in range(n_tiles)])

    @pl.when(jnp.min(min_ref[...]) < SKIP_ABOVE)
    def _():
        for t in range(n_tiles):
            tile = i * n_tiles + t
            if FAR_ROWS < SUB:
                @pl.when(jnp.logical_and(tile > NEAR_SUBS, jnp.min(min_ref[t]) < SKIP_ABOVE))
                def _(t=t, tile=tile):
                    start = pl.multiple_of((tile - NEAR_SUBS - 1) * SUB, SUB)
                    span([(t, FAR_ROWS, SUB - FAR_ROWS, start, 1, False, False)], False)

            def unfinished(t=t):
                return (jnp.min(min_ref[t]) < SKIP_ABOVE).astype(jnp.int32)

            def cond(state):
                rem, go = state
                return jnp.logical_and(rem >= WALK_SUBS * SUB, go > 0)

            def body(state, t=t, unfinished=unfinished):
                rem, _ = state
                start = pl.multiple_of(rem - WALK_SUBS * SUB, SUB)
                span([(t, 0, SUB, start, WALK_SUBS, False, False)], False)
                return start, unfinished()

            remaining = jnp.maximum(tile - NEAR_SUBS - 1, 0) * SUB
            rem, go = lax.while_loop(cond, body, (remaining, unfinished()))
            for tail in range(1, WALK_SUBS):
                @pl.when(jnp.logical_and(rem == tail * SUB, go > 0))
                def _(t=t, tail=tail):
                    span([(t, 0, SUB, 0, tail, False, False)], False)

    o_ref[...] = acc_ref[...].astype(o_ref.dtype)


def _attention(q, k, v, *, batch, seq):
    n, attn_dim = q.shape
    tq = ATTN_TILES * SUB
    nq = seq // tq
    return pl.pallas_call(
        _attn_kernel,
        grid=(batch, nq),
        in_specs=[
            pl.BlockSpec((tq, attn_dim), lambda b, i: (b * nq + i, 0)),
            pl.BlockSpec((seq, attn_dim), lambda b, i: (b, 0)),
            pl.BlockSpec((seq, attn_dim), lambda b, i: (b, 0)),
        ],
        out_specs=pl.BlockSpec((tq, attn_dim), lambda b, i: (b * nq + i, 0)),
        out_shape=jax.ShapeDtypeStruct((n, attn_dim), BF16),
        scratch_shapes=[pltpu.VMEM((tq, attn_dim), F32),
                        pltpu.VMEM((ATTN_TILES, attn_dim // LANES, HEADS_PER_BLOCK * SUB, SUB), F32),
                        pltpu.VMEM((ATTN_TILES, SUBLANES, SUB), F32)],
        compiler_params=pltpu.CompilerParams(
            dimension_semantics=("arbitrary", "arbitrary"), vmem_limit_bytes=VMEM_LIMIT),
    )(q, k, v)


def _ffn_kernel(x_ref, attn_ref, conv_ref, wo_ref, gain_ref, wg_ref, wu_ref, wd_ref, o_ref, act_ref):
    attn_dim = attn_ref.shape[1]
    d_ff = wg_ref.shape[1]
    mix = (jnp.dot(attn_ref[...], wo_ref[0:attn_dim, :], preferred_element_type=F32)
           + jnp.dot(conv_ref[...], wo_ref[attn_dim:, :], preferred_element_type=F32))
    x1 = x_ref[...] + mix
    h = ((x1 * _rms_scale(x1)) * gain_ref[...]).astype(BF16)
    for c0 in range(0, d_ff, FF_CHUNK):
        g = jnp.dot(h, wg_ref[:, c0:c0 + FF_CHUNK], preferred_element_type=F32)
        u = jnp.dot(h, wu_ref[:, c0:c0 + FF_CHUNK], preferred_element_type=F32)
        act_ref[:, c0:c0 + FF_CHUNK] = ((g * jax.nn.sigmoid(g)) * u).astype(BF16)
    o_ref[...] = x1 + jnp.dot(act_ref[...], wd_ref[...], preferred_element_type=F32)


def _ffn(x2d, attn, conv, w_out, gain, w_gate, w_up, w_down):
    n, d = x2d.shape
    tm = FFN_ROW_TILE
    d_ff = w_gate.shape[1]
    const = lambda i: (0, 0)
    row = lambda i: (i, 0)
    resident = functools.partial(pl.BlockSpec, index_map=const, pipeline_mode=pl.Buffered(1))
    return pl.pallas_call(
        _ffn_kernel,
        grid=(n // tm,),
        in_specs=[
            pl.BlockSpec((tm, d), row),
            pl.BlockSpec((tm, attn.shape[1]), row),
            pl.BlockSpec((tm, conv.shape[1]), row),
            resident(w_out.shape),
            pl.BlockSpec((1, d), const),
            resident(w_gate.shape),
            resident(w_up.shape),
            resident(w_down.shape),
        ],
        out_specs=pl.BlockSpec((tm, d), row),
        out_shape=jax.ShapeDtypeStruct((n, d), F32),
        scratch_shapes=[pltpu.VMEM((tm, d_ff), BF16)],
        compiler_params=pltpu.CompilerParams(
            dimension_semantics=("arbitrary",), vmem_limit_bytes=VMEM_LIMIT),
    )(x2d, attn, conv, w_out, gain, w_gate, w_up, w_down)


def kernel(x, norm_mix, w_in, q_norm, k_norm, conv_w, w_out, norm_ffn, w_gate, w_up, w_down):
    batch, seq, d = x.shape
    depth = w_in.shape[0]
    conv_dim = conv_w.shape[2]
    attn_dim = w_out.shape[1] - conv_dim
    n_heads = attn_dim // HEAD_DIM
    assert seq % ROW_TILE == 0 and seq % (ATTN_TILES * SUB) == 0 and attn_dim % MXU_DIM == 0
    assert (batch * seq) % FFN_ROW_TILE == 0
    assert w_gate.shape[2] % FF_CHUNK == 0 and conv_w.shape[1] == CONV_SHIFTS + 1

    xf = x.reshape(batch * seq, d)
    for l in range(depth):
        q, k, v, conv, wo, wg, wu, wd = _inproj(
            xf, norm_mix[l][None, :], w_in, l,
            jnp.tile(q_norm[l], n_heads)[None, :], jnp.tile(k_norm[l], n_heads)[None, :], conv_w[l],
            (w_out, w_gate, w_up, w_down), seq=seq, attn_dim=attn_dim, conv_dim=conv_dim)
        attn = _attention(q, k, v, batch=batch, seq=seq)
        xf = _ffn(xf, attn, conv, wo, norm_ffn[l][None, :], wg, wu, wd)
    return xf.reshape(batch, seq, d)
```

```python
import functools

import jax
import jax.numpy as jnp
from jax import lax
from jax.experimental import pallas as pl
from jax.experimental.pallas import tpu as pltpu

F32 = jnp.float32
BF16 = jnp.bfloat16

HEAD_DIM = 64
EPS = 1e-6
LANES = 128
SUBLANES = 8
BF16_SUBLANES = 16
W_CAST_ROWS = 128
MXU_DIM = 256
HEADS_PER_BLOCK = LANES // HEAD_DIM
SUB = 128
LOG2E = 1.4426950408889634
LN2 = 0.6931471805599453
SKIP_ABOVE = 104.0

ROW_TILE = 1024
FFN_ROW_TILE = 1024
ATTN_TILES = 4
NEAR_SUBS = 1
FAR_ROWS = 32
WALK_SUBS = 2
FF_CHUNK = 256
CONV_SHIFTS = 2
VMEM_LIMIT = 56 * 1024 * 1024


def _rms_scale(x):
    return lax.rsqrt(jnp.mean(x * x, axis=-1, keepdims=True) + EPS)


def _inproj_kernel(x_ref, gain_ref, w_ref, qg_ref, kg_ref, cw_ref, *rest,
                   tiles_per_seq, attn_dim, conv_dim, n_cast):
    cast_in, (q_ref, k_ref, v_ref, c_ref) = rest[:n_cast], rest[n_cast:n_cast + 4]
    cast_out, (halo_ref, wbf_ref) = rest[n_cast + 4:2 * n_cast + 4], rest[2 * n_cast + 4:]
    tm = x_ref.shape[0]
    i = pl.program_id(0)

    @pl.when(i == 0)
    def _():
        def cast_rows(c, _):
            rows = pl.ds(pl.multiple_of(c * W_CAST_ROWS, W_CAST_ROWS), W_CAST_ROWS)
            wbf_ref[rows, :] = w_ref[rows, :].astype(BF16)
            return 0
        lax.fori_loop(0, w_ref.shape[0] // W_CAST_ROWS, cast_rows, 0)
        halo_ref[...] = jnp.zeros_like(halo_ref)

    for src, dst in zip(cast_in, cast_out):
        dst[...] = src[...].astype(BF16)

    x = x_ref[...]
    h = ((x * _rms_scale(x)) * gain_ref[...]).astype(BF16)

    def proj(c0, width):
        return jnp.dot(h, wbf_ref[:, c0:c0 + width], preferred_element_type=F32)

    r = lax.broadcasted_iota(jnp.int32, (MXU_DIM, MXU_DIM), 0) // HEAD_DIM
    c = lax.broadcasted_iota(jnp.int32, (MXU_DIM, MXU_DIM), 1) // HEAD_DIM
    seg = jnp.where(r == c, 1.0, 0.0).astype(BF16)

    def head_norm(p, gain):
        p2 = (p * p).astype(BF16)
        ssq = jnp.concatenate(
            [jnp.dot(p2[:, s:s + MXU_DIM], seg, preferred_element_type=F32)
             for s in range(0, attn_dim, MXU_DIM)], axis=1)
        return (p * lax.rsqrt(ssq * (1.0 / HEAD_DIM) + EPS)) * gain

    q_ref[...] = (head_norm(proj(0, attn_dim), qg_ref[...]) * (HEAD_DIM ** -0.5)).astype(BF16)
    k_ref[...] = head_norm(proj(attn_dim, attn_dim), kg_ref[...]).astype(BF16)

    cb = proj(3 * attn_dim, conv_dim)
    hh = proj(3 * attn_dim + conv_dim, conv_dim) * proj(3 * attn_dim + 2 * conv_dim, conv_dim)
    row = lax.broadcasted_iota(jnp.int32, (SUBLANES, conv_dim), 0)
    seq_start = i % tiles_per_seq == 0

    def shift_down(a, n):
        prev_last = jnp.where(seq_start, 0.0, halo_ref[n, SUBLANES - 1:SUBLANES, :])
        halo_ref[n] = a[tm - SUBLANES:tm, :]
        rolled = pltpu.roll(a, 1, 0)
        head = jnp.where(row == 0, jnp.broadcast_to(prev_last, (SUBLANES, conv_dim)), rolled[0:SUBLANES])
        return jnp.concatenate([head, rolled[SUBLANES:]], axis=0)

    y = cw_ref[2:3, :] * hh + shift_down(cw_ref[1:2, :] * hh + shift_down(cw_ref[0:1, :] * hh, 0), 1)
    c_ref[...] = (cb * y).astype(BF16)

    v_ref[...] = proj(2 * attn_dim, attn_dim).astype(BF16)


def _cast_specs(weights, layer, n_steps, step_of):
    in_specs, out_specs, shapes = [], [], []
    for w in weights:
        _, rows, cols = w.shape
        chunks = max(c for c in range(1, n_steps + 1) if rows % c == 0 and (rows // c) % BF16_SUBLANES == 0)
        chunk_of = lambda *idx, chunks=chunks: jnp.minimum(step_of(*idx), chunks - 1)
        in_specs.append(pl.BlockSpec((None, rows // chunks, cols),
                                     lambda *idx, chunk_of=chunk_of: (layer, chunk_of(*idx), 0)))
        out_specs.append(pl.BlockSpec((rows // chunks, cols), lambda *idx, chunk_of=chunk_of: (chunk_of(*idx), 0)))
        shapes.append(jax.ShapeDtypeStruct((rows, cols), BF16))
    return in_specs, out_specs, shapes


def _inproj(x2d, gain, w_in, layer, q_gain, k_gain, conv_w, later_weights, *, seq, attn_dim, conv_dim):
    n, d = x2d.shape
    tm = ROW_TILE
    n_steps = n // tm
    cols = w_in.shape[2]
    const = lambda i: (0, 0)
    row = lambda i: (i, 0)
    cast_in_specs, cast_out_specs, cast_shapes = _cast_specs(later_weights, layer, n_steps, lambda i: i)
    out_shapes = [jax.ShapeDtypeStruct((n, attn_dim), BF16)] * 3 + [jax.ShapeDtypeStruct((n, conv_dim), BF16)]
    return pl.pallas_call(
        functools.partial(_inproj_kernel, tiles_per_seq=seq // tm, attn_dim=attn_dim, conv_dim=conv_dim,
                          n_cast=len(later_weights)),
        grid=(n_steps,),
        in_specs=[
            pl.BlockSpec((tm, d), row),
            pl.BlockSpec((1, d), const),
            pl.BlockSpec((None, d, cols), lambda i: (layer, 0, 0), pipeline_mode=pl.Buffered(1)),
            pl.BlockSpec((1, attn_dim), const),
            pl.BlockSpec((1, attn_dim), const),
            pl.BlockSpec(conv_w.shape, const),
        ] + cast_in_specs,
        out_specs=[pl.BlockSpec((tm, attn_dim), row)] * 3 + [pl.BlockSpec((tm, conv_dim), row)] + cast_out_specs,
        out_shape=out_shapes + cast_shapes,
        scratch_shapes=[pltpu.VMEM((CONV_SHIFTS, SUBLANES, conv_dim), F32), pltpu.VMEM((d, cols), BF16)],
        compiler_params=pltpu.CompilerParams(
            dimension_semantics=("arbitrary",), vmem_limit_bytes=VMEM_LIMIT),
    )(x2d, gain, w_in, q_gain, k_gain, conv_w, *later_weights)


def _min_to_vreg(arrays):
    chunks = [a[r:r + SUBLANES] for a in arrays for r in range(0, a.shape[0], SUBLANES)]
    while len(chunks) > 1:
        odd = chunks[-1:] if len(chunks) % 2 else []
        chunks = [jnp.minimum(a, b) for a, b in zip(chunks[0::2], chunks[1::2])] + odd
    return chunks[0]


def _attn_kernel(q_ref, k_ref, v_ref, o_ref, acc_ref, carry_ref, min_ref):
    n_tiles = q_ref.shape[0] // SUB
    n_blocks = q_ref.shape[1] // LANES
    i = pl.program_id(1)
    assert n_tiles >= NEAR_SUBS + 1

    r = lax.broadcasted_iota(jnp.int32, (2 * SUB, 2 * SUB), 0) % SUB
    c = lax.broadcasted_iota(jnp.int32, (2 * SUB, 2 * SUB), 1)
    tri = jnp.where((c >= SUB) | (r >= c), 1.0, 0.0).astype(BF16)
    causal = (lax.broadcasted_iota(jnp.int32, (HEADS_PER_BLOCK * SUB, SUB), 1)
              < lax.broadcasted_iota(jnp.int32, (HEADS_PER_BLOCK * SUB, SUB), 0) % SUB)

    def head_rows(row0, nrows):
        return [slice(h * SUB + row0, h * SUB + row0 + nrows) for h in range(HEADS_PER_BLOCK)]

    def softplus_split(z_s, masked):
        e = jnp.exp2(jnp.abs(z_s) * (-LOG2E))
        sp = jnp.maximum(z_s, 0.0) + jnp.log(1.0 + e)
        if masked:
            sp = jnp.where(causal[:z_s.shape[0]], sp, 0.0)
        hi = sp.astype(BF16)
        lo = (sp - hi.astype(F32)).astype(BF16)
        return jnp.concatenate([hi, lo], axis=1)

    def span(jobs, whole_tiles):
        z_all, cs_all = {}, {}
        for n, (t, row0, nrows, start, nsub, diag, first) in enumerate(jobs):
            assert not diag or (row0 == 0 and nrows == SUB)
            qrows = slice(t * SUB + row0, t * SUB + row0 + nrows)
            lane = lax.broadcasted_iota(jnp.int32, (nrows, LANES), 1)
            for p in range(n_blocks):
                cols = slice(p * LANES, (p + 1) * LANES)
                q = q_ref[qrows, cols]
                q2 = jnp.concatenate([jnp.where((lane // HEAD_DIM) == h, q, jnp.zeros_like(q))
                                      for h in range(HEADS_PER_BLOCK)], axis=0)
                z = lax.dot_general(q2, k_ref[pl.ds(start, nsub * SUB), cols], (((1,), (1,)), ((), ())),
                                    preferred_element_type=F32)
                z_all[n, p] = z
                lhs_parts = [softplus_split(z[:, s * SUB:(s + 1) * SUB], diag and s == nsub - 1)
                             for s in reversed(range(nsub))]
                cs_all[n, p] = jnp.dot(jnp.concatenate(lhs_parts, axis=0), tri, preferred_element_type=F32)
        latest = {}
        for n, (t, row0, nrows, start, nsub, diag, first) in enumerate(jobs):
            qrows = slice(t * SUB + row0, t * SUB + row0 + nrows)
            lane = lax.broadcasted_iota(jnp.int32, (nrows, LANES), 1)
            rows2 = HEADS_PER_BLOCK * nrows
            for p in range(n_blocks):
                cols = slice(p * LANES, (p + 1) * LANES)
                carry = None if first else jnp.concatenate([carry_ref[t, p, rs] for rs in head_rows(row0, nrows)], axis=0)
                a_cols = [None] * nsub
                for idx, s in enumerate(reversed(range(nsub))):
                    cs = cs_all[n, p][idx * rows2:(idx + 1) * rows2]
                    g = cs[:, :SUB] if carry is None else cs[:, :SUB] + carry
                    carry = cs[:, SUB:] if carry is None else carry + cs[:, SUB:]
                    a = jnp.exp2((z_all[n, p][:, s * SUB:(s + 1) * SUB] - g) * LOG2E)
                    if diag and s == nsub - 1:
                        a = jnp.where(causal, a, 0.0)
                    a_cols[s] = a.astype(BF16)
                for h, rs in enumerate(head_rows(row0, nrows)):
                    carry_ref[t, p, rs] = carry[h * nrows:(h + 1) * nrows]
                latest.setdefault(t, {}).setdefault(p, []).append((row0, nrows, carry))
                o2 = jnp.dot(jnp.concatenate(a_cols, axis=1), v_ref[pl.ds(start, nsub * SUB), cols],
                             preferred_element_type=F32)
                out = o2[0:nrows]
                for h in range(1, HEADS_PER_BLOCK):
                    out = jnp.where((lane // HEAD_DIM) == h, o2[h * nrows:(h + 1) * nrows], out)
                if first:
                    acc_ref[qrows, cols] = out
                else:
                    acc_ref[qrows, cols] += out
        for t, by_block in latest.items():
            if whole_tiles:
                assert all(len(e) == 1 and e[0][:2] == (0, SUB) for e in by_block.values())
                min_ref[t] = _min_to_vreg([e[0][2] for e in by_block.values()])
            else:
                min_ref[t] = _min_to_vreg([carry_ref[t, p] for p in range(n_blocks)])

    def window(tiles):
        n_keys = (NEAR_SUBS + 2) * SUB
        far = [slice(h * SUB, h * SUB + FAR_ROWS) for h in range(HEADS_PER_BLOCK)]
        rest = [slice(h * SUB + FAR_ROWS, (h + 1) * SUB) for h in range(HEADS_PER_BLOCK)]
        lane = lax.broadcasted_iota(jnp.int32, (SUB, LANES), 1)
        lane_far = lax.broadcasted_iota(jnp.int32, (FAR_ROWS, LANES), 1)
        ctx = {}
        for t, tile in tiles:
            start = pl.multiple_of((tile - NEAR_SUBS - 1) * SUB, SUB)
            for p in range(n_blocks):
                cols = slice(p * LANES, (p + 1) * LANES)
                q = q_ref[t * SUB:(t + 1) * SUB, cols]
                q2 = jnp.concatenate([jnp.where((lane // HEAD_DIM) == h, q, jnp.zeros_like(q))
                                      for h in range(HEADS_PER_BLOCK)], axis=0)
                z = lax.dot_general(q2, k_ref[pl.ds(start, n_keys), cols], (((1,), (1,)), ((), ())),
                                    preferred_element_type=F32)
                z_far = jnp.concatenate([z[rs, 0:SUB] for rs in far], axis=0)
                lhs = [softplus_split(z[:, s * SUB:(s + 1) * SUB], s == NEAR_SUBS + 1)
                       for s in range(NEAR_SUBS + 1, 0, -1)] + [softplus_split(z_far, False)]
                cs = jnp.dot(jnp.concatenate(lhs, axis=0), tri, preferred_element_type=F32)
                ctx[t, p] = (start, z, z_far, cs)
        for t, tile in tiles:
            parts = []
            for p in range(n_blocks):
                cols = slice(p * LANES, (p + 1) * LANES)
                start, z, z_far, cs = ctx[t, p]
                carry, a_cols = None, []
                for idx, s in enumerate(range(NEAR_SUBS + 1, 0, -1)):
                    blk = cs[idx * 2 * SUB:(idx + 1) * 2 * SUB]
                    g = blk[:, :SUB] if carry is None else blk[:, :SUB] + carry
                    carry = blk[:, SUB:] if carry is None else carry + blk[:, SUB:]
                    a = jnp.exp2((z[:, s * SUB:(s + 1) * SUB] - g) * LOG2E)
                    if s == NEAR_SUBS + 1:
                        a = jnp.where(causal, a, 0.0)
                    a_cols.insert(0, a.astype(BF16))
                blk = cs[(NEAR_SUBS + 1) * 2 * SUB:]
                carry_far = jnp.concatenate([carry[rs] for rs in far], axis=0)
                a_far = jnp.exp2((z_far - (blk[:, :SUB] + carry_far)) * LOG2E).astype(BF16)
                carry_far = carry_far + blk[:, SUB:]
                for h in range(HEADS_PER_BLOCK):
                    carry_ref[t, p, far[h]] = carry_far[h * FAR_ROWS:(h + 1) * FAR_ROWS]
                    carry_ref[t, p, rest[h]] = carry[rest[h]]
                    parts.append(carry[rest[h]])
                parts.append(carry_far)
                o2 = jnp.dot(jnp.concatenate(a_cols, axis=1), v_ref[pl.ds(start + SUB, (NEAR_SUBS + 1) * SUB), cols],
                             preferred_element_type=F32)
                o_far = jnp.dot(a_far, v_ref[pl.ds(start, SUB), cols], preferred_element_type=F32)
                out, out_far = o2[0:SUB], o_far[0:FAR_ROWS]
                for h in range(1, HEADS_PER_BLOCK):
                    out = jnp.where((lane // HEAD_DIM) == h, o2[h * SUB:(h + 1) * SUB], out)
                    out_far = jnp.where((lane_far // HEAD_DIM) == h,
                                        o_far[h * FAR_ROWS:(h + 1) * FAR_ROWS], out_far)
                acc_ref[t * SUB:t * SUB + FAR_ROWS, cols] = out[:FAR_ROWS] + out_far
                acc_ref[t * SUB + FAR_ROWS:(t + 1) * SUB, cols] = out[FAR_ROWS:]
            min_ref[t] = _min_to_vreg(parts)

    @pl.when(i == 0)
    def _():
        early = [t for t in range(n_tiles) if t <= NEAR_SUBS]
        span([(t, 0, SUB, 0, t + 1, True, True) for t in early], True)
        window([(t, t) for t in range(n_tiles) if t > NEAR_SUBS])

    @pl.when(i > 0)
    def _():
        window([(t, i * n_tiles + t) for t in range(n_tiles)])

    @pl.when(jnp.min(min_ref[...]) < SKIP_ABOVE)
    def _():
        for t in range(n_tiles):
            tile = i * n_tiles + t
            if FAR_ROWS < SUB:
                @pl.when(jnp.logical_and(tile > NEAR_SUBS, jnp.min(min_ref[t]) < SKIP_ABOVE))
                def _(t=t, tile=tile):
                    start = pl.multiple_of((tile - NEAR_SUBS - 1) * SUB, SUB)
                    span([(t, FAR_ROWS, SUB - FAR_ROWS, start, 1, False, False)], False)

            def unfinished(t=t):
                return (jnp.min(min_ref[t]) < SKIP_ABOVE).astype(jnp.int32)

            def cond(state):
                rem, go = state
                return jnp.logical_and(rem >= WALK_SUBS * SUB, go > 0)

            def body(state, t=t, unfinished=unfinished):
                rem, _ = state
                start = pl.multiple_of(rem - WALK_SUBS * SUB, SUB)
                span([(t, 0, SUB, start, WALK_SUBS, False, False)], False)
                return start, unfinished()

            remaining = jnp.maximum(tile - NEAR_SUBS - 1, 0) * SUB
            rem, go = lax.while_loop(cond, body, (remaining, unfinished()))
            for tail in range(1, WALK_SUBS):
                @pl.when(jnp.logical_and(rem == tail * SUB, go > 0))
                def _(t=t, tail=tail):
                    span([(t, 0, SUB, 0, tail, False, False)], False)

    o_ref[...] = acc_ref[...].astype(o_ref.dtype)


def _attention(q, k, v, *, batch, seq):
    n, attn_dim = q.shape
    tq = ATTN_TILES * SUB
    nq = seq // tq
    return pl.pallas_call(
        _attn_kernel,
        grid=(batch, nq),
        in_specs=[
            pl.BlockSpec((tq, attn_dim), lambda b, i: (b * nq + i, 0)),
            pl.BlockSpec((seq, attn_dim), lambda b, i: (b, 0)),
            pl.BlockSpec((seq, attn_dim), lambda b, i: (b, 0)),
        ],
        out_specs=pl.BlockSpec((tq, attn_dim), lambda b, i: (b * nq + i, 0)),
        out_shape=jax.ShapeDtypeStruct((n, attn_dim), BF16),
        scratch_shapes=[pltpu.VMEM((tq, attn_dim), F32),
                        pltpu.VMEM((ATTN_TILES, attn_dim // LANES, HEADS_PER_BLOCK * SUB, SUB), F32),
                        pltpu.VMEM((ATTN_TILES, SUBLANES, SUB), F32)],
        compiler_params=pltpu.CompilerParams(
            dimension_semantics=("arbitrary", "arbitrary"), vmem_limit_bytes=VMEM_LIMIT),
    )(q, k, v)


def _ffn_kernel(x_ref, attn_ref, conv_ref, wo_ref, gain_ref, wg_ref, wu_ref, wd_ref, o_ref, act_ref):
    attn_dim = attn_ref.shape[1]
    d_ff = wg_ref.shape[1]
    mix = (jnp.dot(attn_ref[...], wo_ref[0:attn_dim, :], preferred_element_type=F32)
           + jnp.dot(conv_ref[...], wo_ref[attn_dim:, :], preferred_element_type=F32))
    x1 = x_ref[...] + mix
    h = ((x1 * _rms_scale(x1)) * gain_ref[...]).astype(BF16)
    for c0 in range(0, d_ff, FF_CHUNK):
        g = jnp.dot(h, wg_ref[:, c0:c0 + FF_CHUNK], preferred_element_type=F32)
        u = jnp.dot(h, wu_ref[:, c0:c0 + FF_CHUNK], preferred_element_type=F32)
        act_ref[:, c0:c0 + FF_CHUNK] = ((g * jax.nn.sigmoid(g)) * u).astype(BF16)
    o_ref[...] = x1 + jnp.dot(act_ref[...], wd_ref[...], preferred_element_type=F32)


def _ffn(x2d, attn, conv, w_out, gain, w_gate, w_up, w_down):
    n, d = x2d.shape
    tm = FFN_ROW_TILE
    d_ff = w_gate.shape[1]
    const = lambda i: (0, 0)
    row = lambda i: (i, 0)
    resident = functools.partial(pl.BlockSpec, index_map=const, pipeline_mode=pl.Buffered(1))
    return pl.pallas_call(
        _ffn_kernel,
        grid=(n // tm,),
        in_specs=[
            pl.BlockSpec((tm, d), row),
            pl.BlockSpec((tm, attn.shape[1]), row),
            pl.BlockSpec((tm, conv.shape[1]), row),
            resident(w_out.shape),
            pl.BlockSpec((1, d), const),
            resident(w_gate.shape),
            resident(w_up.shape),
            resident(w_down.shape),
        ],
        out_specs=pl.BlockSpec((tm, d), row),
        out_shape=jax.ShapeDtypeStruct((n, d), F32),
        scratch_shapes=[pltpu.VMEM((tm, d_ff), BF16)],
        compiler_params=pltpu.CompilerParams(
            dimension_semantics=("arbitrary",), vmem_limit_bytes=VMEM_LIMIT),
    )(x2d, attn, conv, w_out, gain, w_gate, w_up, w_down)


def kernel(x, norm_mix, w_in, q_norm, k_norm, conv_w, w_out, norm_ffn, w_gate, w_up, w_down):
    batch, seq, d = x.shape
    depth = w_in.shape[0]
    conv_dim = conv_w.shape[2]
    attn_dim = w_out.shape[1] - conv_dim
    n_heads = attn_dim // HEAD_DIM
    assert seq % ROW_TILE == 0 and seq % (ATTN_TILES * SUB) == 0 and attn_dim % MXU_DIM == 0
    assert (batch * seq) % FFN_ROW_TILE == 0
    assert w_gate.shape[2] % FF_CHUNK == 0 and conv_w.shape[1] == CONV_SHIFTS + 1

    xf = x.reshape(batch * seq, d)
    for l in range(depth):
        q, k, v, conv, wo, wg, wu, wd = _inproj(
            xf, norm_mix[l][None, :], w_in, l,
            jnp.tile(q_norm[l], n_heads)[None, :], jnp.tile(k_norm[l], n_heads)[None, :], conv_w[l],
            (w_out, w_gate, w_up, w_down), seq=seq, attn_dim=attn_dim, conv_dim=conv_dim)
        attn = _attention(q, k, v, batch=batch, seq=seq)
        xf = _ffn(xf, attn, conv, wo, norm_ffn[l][None, :], wg, wu, wd)
    return xf.reshape(batch, seq, d)
```

```python
import functools

import jax
import jax.numpy as jnp
from jax import lax
from jax.experimental import pallas as pl
from jax.experimental.pallas import tpu as pltpu

F32 = jnp.float32
BF16 = jnp.bfloat16

HEAD_DIM = 64
EPS = 1e-6
LANES = 128
SUBLANES = 8
BF16_SUBLANES = 16
W_CAST_ROWS = 128
MXU_DIM = 256
HEADS_PER_BLOCK = LANES // HEAD_DIM
SUB = 128
LOG2E = 1.4426950408889634
LN2 = 0.6931471805599453
SKIP_ABOVE = 104.0

ROW_TILE = 1024
FFN_ROW_TILE = 1024
ATTN_TILES = 4
NEAR_SUBS = 1
FAR_ROWS = 32
WALK_SUBS = 2
FF_CHUNK = 256
CONV_SHIFTS = 2
VMEM_LIMIT = 56 * 1024 * 1024


def _rms_scale(x):
    return lax.rsqrt(jnp.mean(x * x, axis=-1, keepdims=True) + EPS)


def _inproj_kernel(x_ref, gain_ref, w_ref, qg_ref, kg_ref, cw_ref, *rest,
                   layer, tiles_per_seq, attn_dim, conv_dim, n_cast):
    cast_in, (q_ref, k_ref, v_ref, c_ref) = rest[:n_cast], rest[n_cast:n_cast + 4]
    cast_out, (halo_ref, wbf_ref) = rest[n_cast + 4:2 * n_cast + 4], rest[2 * n_cast + 4:]
    tm = x_ref.shape[0]
    i = pl.program_id(0)

    @pl.when(i == 0)
    def _():
        def cast_rows(c, _):
            rows = pl.ds(pl.multiple_of(c * W_CAST_ROWS, W_CAST_ROWS), W_CAST_ROWS)
            wbf_ref[rows, :] = w_ref[rows, :].astype(BF16)
            return 0
        lax.fori_loop(0, w_ref.shape[0] // W_CAST_ROWS, cast_rows, 0)
        halo_ref[...] = jnp.zeros_like(halo_ref)

    for src, dst in zip(cast_in, cast_out):
        dst[...] = src[...].astype(BF16)

    x = x_ref[...]
    h = ((x * _rms_scale(x)) * gain_ref[layer:layer + 1, :]).astype(BF16)

    def proj(c0, width):
        return jnp.dot(h, wbf_ref[:, c0:c0 + width], preferred_element_type=F32)

    r = lax.broadcasted_iota(jnp.int32, (MXU_DIM, MXU_DIM), 0) // HEAD_DIM
    c = lax.broadcasted_iota(jnp.int32, (MXU_DIM, MXU_DIM), 1) // HEAD_DIM
    seg = jnp.where(r == c, 1.0, 0.0).astype(BF16)

    def head_norm(p, gain_ref):
        gain = jnp.concatenate([gain_ref[layer:layer + 1, :]] * (attn_dim // HEAD_DIM), axis=1)
        p2 = (p * p).astype(BF16)
        ssq = jnp.concatenate(
            [jnp.dot(p2[:, s:s + MXU_DIM], seg, preferred_element_type=F32)
             for s in range(0, attn_dim, MXU_DIM)], axis=1)
        return (p * lax.rsqrt(ssq * (1.0 / HEAD_DIM) + EPS)) * gain

    q_ref[...] = (head_norm(proj(0, attn_dim), qg_ref) * (HEAD_DIM ** -0.5)).astype(BF16)
    k_ref[...] = head_norm(proj(attn_dim, attn_dim), kg_ref).astype(BF16)

    cb = proj(3 * attn_dim, conv_dim)
    hh = proj(3 * attn_dim + conv_dim, conv_dim) * proj(3 * attn_dim + 2 * conv_dim, conv_dim)
    row = lax.broadcasted_iota(jnp.int32, (SUBLANES, conv_dim), 0)
    seq_start = i % tiles_per_seq == 0

    def shift_down(a, n):
        prev_last = jnp.where(seq_start, 0.0, halo_ref[n, SUBLANES - 1:SUBLANES, :])
        halo_ref[n] = a[tm - SUBLANES:tm, :]
        rolled = pltpu.roll(a, 1, 0)
        head = jnp.where(row == 0, jnp.broadcast_to(prev_last, (SUBLANES, conv_dim)), rolled[0:SUBLANES])
        return jnp.concatenate([head, rolled[SUBLANES:]], axis=0)

    taps = [cw_ref[layer, n:n + 1, :] for n in range(CONV_SHIFTS + 1)]
    y = taps[2] * hh + shift_down(taps[1] * hh + shift_down(taps[0] * hh, 0), 1)
    c_ref[...] = (cb * y).astype(BF16)

    v_ref[...] = proj(2 * attn_dim, attn_dim).astype(BF16)


def _cast_specs(weights, layer, n_steps, step_of):
    in_specs, out_specs, shapes = [], [], []
    for w in weights:
        _, rows, cols = w.shape
        chunks = max(c for c in range(1, n_steps + 1) if rows % c == 0 and (rows // c) % BF16_SUBLANES == 0)
        chunk_of = lambda *idx, chunks=chunks: jnp.minimum(step_of(*idx), chunks - 1)
        in_specs.append(pl.BlockSpec((None, rows // chunks, cols),
                                     lambda *idx, chunk_of=chunk_of: (layer, chunk_of(*idx), 0)))
        out_specs.append(pl.BlockSpec((rows // chunks, cols), lambda *idx, chunk_of=chunk_of: (chunk_of(*idx), 0)))
        shapes.append(jax.ShapeDtypeStruct((rows, cols), BF16))
    return in_specs, out_specs, shapes


def _inproj(x2d, gains, w_in, layer, q_gains, k_gains, conv_w, later_weights, *, seq, attn_dim, conv_dim):
    n, d = x2d.shape
    tm = ROW_TILE
    n_steps = n // tm
    cols = w_in.shape[2]
    const = lambda i: (0, 0)
    row = lambda i: (i, 0)
    cast_in_specs, cast_out_specs, cast_shapes = _cast_specs(later_weights, layer, n_steps, lambda i: i)
    out_shapes = [jax.ShapeDtypeStruct((n, attn_dim), BF16)] * 3 + [jax.ShapeDtypeStruct((n, conv_dim), BF16)]
    return pl.pallas_call(
        functools.partial(_inproj_kernel, layer=layer, tiles_per_seq=seq // tm, attn_dim=attn_dim,
                          conv_dim=conv_dim, n_cast=len(later_weights)),
        grid=(n_steps,),
        in_specs=[
            pl.BlockSpec((tm, d), row),
            pl.BlockSpec(gains.shape, const),
            pl.BlockSpec((None, d, cols), lambda i: (layer, 0, 0), pipeline_mode=pl.Buffered(1)),
            pl.BlockSpec(q_gains.shape, const),
            pl.BlockSpec(k_gains.shape, const),
            pl.BlockSpec(conv_w.shape, lambda i: (0, 0, 0)),
        ] + cast_in_specs,
        out_specs=[pl.BlockSpec((tm, attn_dim), row)] * 3 + [pl.BlockSpec((tm, conv_dim), row)] + cast_out_specs,
        out_shape=out_shapes + cast_shapes,
        scratch_shapes=[pltpu.VMEM((CONV_SHIFTS, SUBLANES, conv_dim), F32), pltpu.VMEM((d, cols), BF16)],
        compiler_params=pltpu.CompilerParams(
            dimension_semantics=("arbitrary",), vmem_limit_bytes=VMEM_LIMIT),
    )(x2d, gains, w_in, q_gains, k_gains, conv_w, *later_weights)


def _min_to_vreg(arrays):
    chunks = [a[r:r + SUBLANES] for a in arrays for r in range(0, a.shape[0], SUBLANES)]
    while len(chunks) > 1:
        odd = chunks[-1:] if len(chunks) % 2 else []
        chunks = [jnp.minimum(a, b) for a, b in zip(chunks[0::2], chunks[1::2])] + odd
    return chunks[0]


def _attn_kernel(q_ref, k_ref, v_ref, o_ref, acc_ref, carry_ref, min_ref):
    n_tiles = q_ref.shape[0] // SUB
    n_blocks = q_ref.shape[1] // LANES
    i = pl.program_id(1)
    assert n_tiles >= NEAR_SUBS + 1

    r = lax.broadcasted_iota(jnp.int32, (2 * SUB, 2 * SUB), 0) % SUB
    c = lax.broadcasted_iota(jnp.int32, (2 * SUB, 2 * SUB), 1)
    tri = jnp.where((c >= SUB) | (r >= c), 1.0, 0.0).astype(BF16)
    causal = (lax.broadcasted_iota(jnp.int32, (HEADS_PER_BLOCK * SUB, SUB), 1)
              < lax.broadcasted_iota(jnp.int32, (HEADS_PER_BLOCK * SUB, SUB), 0) % SUB)

    def head_rows(row0, nrows):
        return [slice(h * SUB + row0, h * SUB + row0 + nrows) for h in range(HEADS_PER_BLOCK)]

    def softplus_split(z_s, masked):
        e = jnp.exp2(jnp.abs(z_s) * (-LOG2E))
        sp = jnp.maximum(z_s, 0.0) + jnp.log(1.0 + e)
        if masked:
            sp = jnp.where(causal[:z_s.shape[0]], sp, 0.0)
        hi = sp.astype(BF16)
        lo = (sp - hi.astype(F32)).astype(BF16)
        return jnp.concatenate([hi, lo], axis=1)

    def span(jobs, whole_tiles):
        z_all, cs_all = {}, {}
        for n, (t, row0, nrows, start, nsub, diag, first) in enumerate(jobs):
            assert not diag or (row0 == 0 and nrows == SUB)
            qrows = slice(t * SUB + row0, t * SUB + row0 + nrows)
            lane = lax.broadcasted_iota(jnp.int32, (nrows, LANES), 1)
            for p in range(n_blocks):
                cols = slice(p * LANES, (p + 1) * LANES)
                q = q_ref[qrows, cols]
                q2 = jnp.concatenate([jnp.where((lane // HEAD_DIM) == h, q, jnp.zeros_like(q))
                                      for h in range(HEADS_PER_BLOCK)], axis=0)
                z = lax.dot_general(q2, k_ref[pl.ds(start, nsub * SUB), cols], (((1,), (1,)), ((), ())),
                                    preferred_element_type=F32)
                z_all[n, p] = z
                lhs_parts = [softplus_split(z[:, s * SUB:(s + 1) * SUB], diag and s == nsub - 1)
                             for s in reversed(range(nsub))]
                cs_all[n, p] = jnp.dot(jnp.concatenate(lhs_parts, axis=0), tri, preferred_element_type=F32)
        latest = {}
        for n, (t, row0, nrows, start, nsub, diag, first) in enumerate(jobs):
            qrows = slice(t * SUB + row0, t * SUB + row0 + nrows)
            lane = lax.broadcasted_iota(jnp.int32, (nrows, LANES), 1)
            rows2 = HEADS_PER_BLOCK * nrows
            for p in range(n_blocks):
                cols = slice(p * LANES, (p + 1) * LANES)
                carry = None if first else jnp.concatenate([carry_ref[t, p, rs] for rs in head_rows(row0, nrows)], axis=0)
                a_cols = [None] * nsub
                for idx, s in enumerate(reversed(range(nsub))):
                    cs = cs_all[n, p][idx * rows2:(idx + 1) * rows2]
                    g = cs[:, :SUB] if carry is None else cs[:, :SUB] + carry
                    carry = cs[:, SUB:] if carry is None else carry + cs[:, SUB:]
                    a = jnp.exp2((z_all[n, p][:, s * SUB:(s + 1) * SUB] - g) * LOG2E)
                    if diag and s == nsub - 1:
                        a = jnp.where(causal, a, 0.0)
                    a_cols[s] = a.astype(BF16)
                for h, rs in enumerate(head_rows(row0, nrows)):
                    carry_ref[t, p, rs] = carry[h * nrows:(h + 1) * nrows]
                latest.setdefault(t, {}).setdefault(p, []).append((row0, nrows, carry))
                o2 = jnp.dot(jnp.concatenate(a_cols, axis=1), v_ref[pl.ds(start, nsub * SUB), cols],
                             preferred_element_type=F32)
                out = o2[0:nrows]
                for h in range(1, HEADS_PER_BLOCK):
                    out = jnp.where((lane // HEAD_DIM) == h, o2[h * nrows:(h + 1) * nrows], out)
                if first:
                    acc_ref[qrows, cols] = out
                else:
                    acc_ref[qrows, cols] += out
        for t, by_block in latest.items():
            if whole_tiles:
                assert all(len(e) == 1 and e[0][:2] == (0, SUB) for e in by_block.values())
                min_ref[t] = _min_to_vreg([e[0][2] for e in by_block.values()])
            else:
                min_ref[t] = _min_to_vreg([carry_ref[t, p] for p in range(n_blocks)])

    def window(tiles):
        n_keys = (NEAR_SUBS + 2) * SUB
        far = [slice(h * SUB, h * SUB + FAR_ROWS) for h in range(HEADS_PER_BLOCK)]
        rest = [slice(h * SUB + FAR_ROWS, (h + 1) * SUB) for h in range(HEADS_PER_BLOCK)]
        lane = lax.broadcasted_iota(jnp.int32, (SUB, LANES), 1)
        lane_far = lax.broadcasted_iota(jnp.int32, (FAR_ROWS, LANES), 1)
        ctx = {}
        for t, tile in tiles:
            start = pl.multiple_of((tile - NEAR_SUBS - 1) * SUB, SUB)
            for p in range(n_blocks):
                cols = slice(p * LANES, (p + 1) * LANES)
                q = q_ref[t * SUB:(t + 1) * SUB, cols]
                q2 = jnp.concatenate([jnp.where((lane // HEAD_DIM) == h, q, jnp.zeros_like(q))
                                      for h in range(HEADS_PER_BLOCK)], axis=0)
                z = lax.dot_general(q2, k_ref[pl.ds(start, n_keys), cols], (((1,), (1,)), ((), ())),
                                    preferred_element_type=F32)
                z_far = jnp.concatenate([z[rs, 0:SUB] for rs in far], axis=0)
                lhs = [softplus_split(z[:, s * SUB:(s + 1) * SUB], s == NEAR_SUBS + 1)
                       for s in range(NEAR_SUBS + 1, 0, -1)] + [softplus_split(z_far, False)]
                cs = jnp.dot(jnp.concatenate(lhs, axis=0), tri, preferred_element_type=F32)
                ctx[t, p] = (start, z, z_far, cs)
        for t, tile in tiles:
            parts = []
            for p in range(n_blocks):
                cols = slice(p * LANES, (p + 1) * LANES)
                start, z, z_far, cs = ctx[t, p]
                carry, a_cols = None, []
                for idx, s in enumerate(range(NEAR_SUBS + 1, 0, -1)):
                    blk = cs[idx * 2 * SUB:(idx + 1) * 2 * SUB]
                    g = blk[:, :SUB] if carry is None else blk[:, :SUB] + carry
                    carry = blk[:, SUB:] if carry is None else carry + blk[:, SUB:]
                    a = jnp.exp2((z[:, s * SUB:(s + 1) * SUB] - g) * LOG2E)
                    if s == NEAR_SUBS + 1:
                        a = jnp.where(causal, a, 0.0)
                    a_cols.insert(0, a.astype(BF16))
                blk = cs[(NEAR_SUBS + 1) * 2 * SUB:]
                carry_far = jnp.concatenate([carry[rs] for rs in far], axis=0)
                a_far = jnp.exp2((z_far - (blk[:, :SUB] + carry_far)) * LOG2E).astype(BF16)
                carry_far = carry_far + blk[:, SUB:]
                for h in range(HEADS_PER_BLOCK):
                    carry_ref[t, p, far[h]] = carry_far[h * FAR_ROWS:(h + 1) * FAR_ROWS]
                    carry_ref[t, p, rest[h]] = carry[rest[h]]
                    parts.append(carry[rest[h]])
                parts.append(carry_far)
                o2 = jnp.dot(jnp.concatenate(a_cols, axis=1), v_ref[pl.ds(start + SUB, (NEAR_SUBS + 1) * SUB), cols],
                             preferred_element_type=F32)
                o_far = jnp.dot(a_far, v_ref[pl.ds(start, SUB), cols], preferred_element_type=F32)
                out, out_far = o2[0:SUB], o_far[0:FAR_ROWS]
                for h in range(1, HEADS_PER_BLOCK):
                    out = jnp.where((lane // HEAD_DIM) == h, o2[h * SUB:(h + 1) * SUB], out)
                    out_far = jnp.where((lane_far // HEAD_DIM) == h,
                                        o_far[h * FAR_ROWS:(h + 1) * FAR_ROWS], out_far)
                acc_ref[t * SUB:t * SUB + FAR_ROWS, cols] = out[:FAR_ROWS] + out_far
                acc_ref[t * SUB + FAR_ROWS:(t + 1) * SUB, cols] = out[FAR_ROWS:]
            min_ref[t] = _min_to_vreg(parts)

    @pl.when(i == 0)
    def _():
        early = [t for t in range(n_tiles) if t <= NEAR_SUBS]
        span([(t, 0, SUB, 0, t + 1, True, True) for t in early], True)
        window([(t, t) for t in range(n_tiles) if t > NEAR_SUBS])

    @pl.when(i > 0)
    def _():
        window([(t, i * n_tiles + t) for t in range(n_tiles)])

    @pl.when(jnp.min(min_ref[...]) < SKIP_ABOVE)
    def _():
        for t in range(n_tiles):
            tile = i * n_tiles + t
            if FAR_ROWS < SUB:
                @pl.when(jnp.logical_and(tile > NEAR_SUBS, jnp.min(min_ref[t]) < SKIP_ABOVE))
                def _(t=t, tile=tile):
                    start = pl.multiple_of((tile - NEAR_SUBS - 1) * SUB, SUB)
                    span([(t, FAR_ROWS, SUB - FAR_ROWS, start, 1, False, False)], False)

            def unfinished(t=t):
                return (jnp.min(min_ref[t]) < SKIP_ABOVE).astype(jnp.int32)

            def cond(state):
                rem, go = state
                return jnp.logical_and(rem >= WALK_SUBS * SUB, go > 0)

            def body(state, t=t, unfinished=unfinished):
                rem, _ = state
                start = pl.multiple_of(rem - WALK_SUBS * SUB, SUB)
                span([(t, 0, SUB, start, WALK_SUBS, False, False)], False)
                return start, unfinished()

            remaining = jnp.maximum(tile - NEAR_SUBS - 1, 0) * SUB
            rem, go = lax.while_loop(cond, body, (remaining, unfinished()))
            for tail in range(1, WALK_SUBS):
                @pl.when(jnp.logical_and(rem == tail * SUB, go > 0))
                def _(t=t, tail=tail):
                    span([(t, 0, SUB, 0, tail, False, False)], False)

    o_ref[...] = acc_ref[...].astype(o_ref.dtype)


def _attention(q, k, v, *, batch, seq):
    n, attn_dim = q.shape
    tq = ATTN_TILES * SUB
    nq = seq // tq
    return pl.pallas_call(
        _attn_kernel,
        grid=(batch, nq),
        in_specs=[
            pl.BlockSpec((tq, attn_dim), lambda b, i: (b * nq + i, 0)),
            pl.BlockSpec((seq, attn_dim), lambda b, i: (b, 0)),
            pl.BlockSpec((seq, attn_dim), lambda b, i: (b, 0)),
        ],
        out_specs=pl.BlockSpec((tq, attn_dim), lambda b, i: (b * nq + i, 0)),
        out_shape=jax.ShapeDtypeStruct((n, attn_dim), BF16),
        scratch_shapes=[pltpu.VMEM((tq, attn_dim), F32),
                        pltpu.VMEM((ATTN_TILES, attn_dim // LANES, HEADS_PER_BLOCK * SUB, SUB), F32),
                        pltpu.VMEM((ATTN_TILES, SUBLANES, SUB), F32)],
        compiler_params=pltpu.CompilerParams(
            dimension_semantics=("arbitrary", "arbitrary"), vmem_limit_bytes=VMEM_LIMIT),
    )(q, k, v)


def _ffn_kernel(x_ref, attn_ref, conv_ref, wo_ref, gain_ref, wg_ref, wu_ref, wd_ref, o_ref, act_ref, *, layer):
    attn_dim = attn_ref.shape[1]
    d_ff = wg_ref.shape[1]
    mix = (jnp.dot(attn_ref[...], wo_ref[0:attn_dim, :], preferred_element_type=F32)
           + jnp.dot(conv_ref[...], wo_ref[attn_dim:, :], preferred_element_type=F32))
    x1 = x_ref[...] + mix
    h = ((x1 * _rms_scale(x1)) * gain_ref[layer:layer + 1, :]).astype(BF16)
    for c0 in range(0, d_ff, FF_CHUNK):
        g = jnp.dot(h, wg_ref[:, c0:c0 + FF_CHUNK], preferred_element_type=F32)
        u = jnp.dot(h, wu_ref[:, c0:c0 + FF_CHUNK], preferred_element_type=F32)
        act_ref[:, c0:c0 + FF_CHUNK] = ((g * jax.nn.sigmoid(g)) * u).astype(BF16)
    o_ref[...] = x1 + jnp.dot(act_ref[...], wd_ref[...], preferred_element_type=F32)


def _ffn(x2d, attn, conv, w_out, gains, layer, w_gate, w_up, w_down):
    n, d = x2d.shape
    tm = FFN_ROW_TILE
    d_ff = w_gate.shape[1]
    const = lambda i: (0, 0)
    row = lambda i: (i, 0)
    resident = functools.partial(pl.BlockSpec, index_map=const, pipeline_mode=pl.Buffered(1))
    return pl.pallas_call(
        functools.partial(_ffn_kernel, layer=layer),
        grid=(n // tm,),
        in_specs=[
            pl.BlockSpec((tm, d), row),
            pl.BlockSpec((tm, attn.shape[1]), row),
            pl.BlockSpec((tm, conv.shape[1]), row),
            resident(w_out.shape),
            pl.BlockSpec(gains.shape, const),
            resident(w_gate.shape),
            resident(w_up.shape),
            resident(w_down.shape),
        ],
        out_specs=pl.BlockSpec((tm, d), row),
        out_shape=jax.ShapeDtypeStruct((n, d), F32),
        scratch_shapes=[pltpu.VMEM((tm, d_ff), BF16)],
        compiler_params=pltpu.CompilerParams(
            dimension_semantics=("arbitrary",), vmem_limit_bytes=VMEM_LIMIT),
    )(x2d, attn, conv, w_out, gains, w_gate, w_up, w_down)


def kernel(x, norm_mix, w_in, q_norm, k_norm, conv_w, w_out, norm_ffn, w_gate, w_up, w_down):
    batch, seq, d = x.shape
    depth = w_in.shape[0]
    conv_dim = conv_w.shape[2]
    attn_dim = w_out.shape[1] - conv_dim
    assert seq % ROW_TILE == 0 and seq % (ATTN_TILES * SUB) == 0 and attn_dim % MXU_DIM == 0
    assert (batch * seq) % FFN_ROW_TILE == 0
    assert w_gate.shape[2] % FF_CHUNK == 0 and conv_w.shape[1] == CONV_SHIFTS + 1

    xf = x.reshape(batch * seq, d)
    for l in range(depth):
        q, k, v, conv, wo, wg, wu, wd = _inproj(
            xf, norm_mix, w_in, l, q_norm, k_norm, conv_w,
            (w_out, w_gate, w_up, w_down), seq=seq, attn_dim=attn_dim, conv_dim=conv_dim)
        attn = _attention(q, k, v, batch=batch, seq=seq)
        xf = _ffn(xf, attn, conv, wo, norm_ffn, l, wg, wu, wd)
    return xf.reshape(batch, seq, d)
```

```python
import functools

import jax
import jax.numpy as jnp
from jax import lax
from jax.experimental import pallas as pl
from jax.experimental.pallas import tpu as pltpu

F32 = jnp.float32
BF16 = jnp.bfloat16

HEAD_DIM = 64
EPS = 1e-6
LANES = 128
SUBLANES = 8
BF16_SUBLANES = 16
W_CAST_ROWS = 128
MXU_DIM = 256
HEADS_PER_BLOCK = LANES // HEAD_DIM
SUB = 128
LOG2E = 1.4426950408889634
LN2 = 0.6931471805599453
SKIP_ABOVE = 104.0

ROW_TILE = 1024
FFN_ROW_TILE = 1024
ATTN_TILES = 4
NEAR_SUBS = 1
FAR_ROWS = 32
OUTPUT_LAG = 2
WALK_SUBS = 2
FF_CHUNK = 256
CONV_SHIFTS = 2
VMEM_LIMIT = 56 * 1024 * 1024


def _rms_scale(x):
    return lax.rsqrt(jnp.mean(x * x, axis=-1, keepdims=True) + EPS)


def _inproj_kernel(x_ref, gain_ref, w_ref, qg_ref, kg_ref, cw_ref, *rest,
                   layer, tiles_per_seq, attn_dim, conv_dim, n_cast):
    cast_in, (q_ref, k_ref, v_ref, c_ref) = rest[:n_cast], rest[n_cast:n_cast + 4]
    cast_out, (halo_ref, wbf_ref) = rest[n_cast + 4:2 * n_cast + 4], rest[2 * n_cast + 4:]
    tm = x_ref.shape[0]
    i = pl.program_id(0)

    @pl.when(i == 0)
    def _():
        def cast_rows(c, _):
            rows = pl.ds(pl.multiple_of(c * W_CAST_ROWS, W_CAST_ROWS), W_CAST_ROWS)
            wbf_ref[rows, :] = w_ref[rows, :].astype(BF16)
            return 0
        lax.fori_loop(0, w_ref.shape[0] // W_CAST_ROWS, cast_rows, 0)
        halo_ref[...] = jnp.zeros_like(halo_ref)

    for src, dst in zip(cast_in, cast_out):
        dst[...] = src[...].astype(BF16)

    x = x_ref[...]
    h = ((x * _rms_scale(x)) * gain_ref[layer:layer + 1, :]).astype(BF16)

    def proj(c0, width):
        return jnp.dot(h, wbf_ref[:, c0:c0 + width], preferred_element_type=F32)

    r = lax.broadcasted_iota(jnp.int32, (MXU_DIM, MXU_DIM), 0) // HEAD_DIM
    c = lax.broadcasted_iota(jnp.int32, (MXU_DIM, MXU_DIM), 1) // HEAD_DIM
    seg = jnp.where(r == c, 1.0, 0.0).astype(BF16)

    def head_norm(p, gain_ref):
        gain = jnp.concatenate([gain_ref[layer:layer + 1, :]] * (attn_dim // HEAD_DIM), axis=1)
        p2 = (p * p).astype(BF16)
        ssq = jnp.concatenate(
            [jnp.dot(p2[:, s:s + MXU_DIM], seg, preferred_element_type=F32)
             for s in range(0, attn_dim, MXU_DIM)], axis=1)
        return (p * lax.rsqrt(ssq * (1.0 / HEAD_DIM) + EPS)) * gain

    q_ref[...] = (head_norm(proj(0, attn_dim), qg_ref) * (HEAD_DIM ** -0.5)).astype(BF16)
    k_ref[...] = head_norm(proj(attn_dim, attn_dim), kg_ref).astype(BF16)

    cb = proj(3 * attn_dim, conv_dim)
    hh = proj(3 * attn_dim + conv_dim, conv_dim) * proj(3 * attn_dim + 2 * conv_dim, conv_dim)
    row = lax.broadcasted_iota(jnp.int32, (SUBLANES, conv_dim), 0)
    seq_start = i % tiles_per_seq == 0

    def shift_down(a, n):
        prev_last = jnp.where(seq_start, 0.0, halo_ref[n, SUBLANES - 1:SUBLANES, :])
        halo_ref[n] = a[tm - SUBLANES:tm, :]
        rolled = pltpu.roll(a, 1, 0)
        head = jnp.where(row == 0, jnp.broadcast_to(prev_last, (SUBLANES, conv_dim)), rolled[0:SUBLANES])
        return jnp.concatenate([head, rolled[SUBLANES:]], axis=0)

    taps = [cw_ref[layer, n:n + 1, :] for n in range(CONV_SHIFTS + 1)]
    y = taps[2] * hh + shift_down(taps[1] * hh + shift_down(taps[0] * hh, 0), 1)
    c_ref[...] = (cb * y).astype(BF16)

    v_ref[...] = proj(2 * attn_dim, attn_dim).astype(BF16)


def _cast_specs(weights, layer, n_steps, step_of):
    in_specs, out_specs, shapes = [], [], []
    for w in weights:
        _, rows, cols = w.shape
        chunks = max(c for c in range(1, n_steps + 1) if rows % c == 0 and (rows // c) % BF16_SUBLANES == 0)
        chunk_of = lambda *idx, chunks=chunks: jnp.minimum(step_of(*idx), chunks - 1)
        in_specs.append(pl.BlockSpec((None, rows // chunks, cols),
                                     lambda *idx, chunk_of=chunk_of: (layer, chunk_of(*idx), 0)))
        out_specs.append(pl.BlockSpec((rows // chunks, cols), lambda *idx, chunk_of=chunk_of: (chunk_of(*idx), 0)))
        shapes.append(jax.ShapeDtypeStruct((rows, cols), BF16))
    return in_specs, out_specs, shapes


def _inproj(x2d, gains, w_in, layer, q_gains, k_gains, conv_w, later_weights, *, seq, attn_dim, conv_dim):
    n, d = x2d.shape
    tm = ROW_TILE
    n_steps = n // tm
    cols = w_in.shape[2]
    const = lambda i: (0, 0)
    row = lambda i: (i, 0)
    cast_in_specs, cast_out_specs, cast_shapes = _cast_specs(later_weights, layer, n_steps, lambda i: i)
    out_shapes = [jax.ShapeDtypeStruct((n, attn_dim), BF16)] * 3 + [jax.ShapeDtypeStruct((n, conv_dim), BF16)]
    return pl.pallas_call(
        functools.partial(_inproj_kernel, layer=layer, tiles_per_seq=seq // tm, attn_dim=attn_dim,
                          conv_dim=conv_dim, n_cast=len(later_weights)),
        grid=(n_steps,),
        in_specs=[
            pl.BlockSpec((tm, d), row),
            pl.BlockSpec(gains.shape, const),
            pl.BlockSpec((None, d, cols), lambda i: (layer, 0, 0), pipeline_mode=pl.Buffered(1)),
            pl.BlockSpec(q_gains.shape, const),
            pl.BlockSpec(k_gains.shape, const),
            pl.BlockSpec(conv_w.shape, lambda i: (0, 0, 0)),
        ] + cast_in_specs,
        out_specs=[pl.BlockSpec((tm, attn_dim), row)] * 3 + [pl.BlockSpec((tm, conv_dim), row)] + cast_out_specs,
        out_shape=out_shapes + cast_shapes,
        scratch_shapes=[pltpu.VMEM((CONV_SHIFTS, SUBLANES, conv_dim), F32), pltpu.VMEM((d, cols), BF16)],
        compiler_params=pltpu.CompilerParams(
            dimension_semantics=("arbitrary",), vmem_limit_bytes=VMEM_LIMIT),
    )(x2d, gains, w_in, q_gains, k_gains, conv_w, *later_weights)


def _min_to_vreg(arrays):
    chunks = [a[r:r + SUBLANES] for a in arrays for r in range(0, a.shape[0], SUBLANES)]
    while len(chunks) > 1:
        odd = chunks[-1:] if len(chunks) % 2 else []
        chunks = [jnp.minimum(a, b) for a, b in zip(chunks[0::2], chunks[1::2])] + odd
    return chunks[0]


def _attn_kernel(q_ref, k_ref, v_ref, o_ref, acc_ref, carry_ref, min_ref):
    n_tiles = q_ref.shape[0] // SUB
    n_blocks = q_ref.shape[1] // LANES
    i = pl.program_id(1)
    assert n_tiles >= NEAR_SUBS + 1

    r = lax.broadcasted_iota(jnp.int32, (2 * SUB, 2 * SUB), 0) % SUB
    c = lax.broadcasted_iota(jnp.int32, (2 * SUB, 2 * SUB), 1)
    tri = jnp.where((c >= SUB) | (r >= c), 1.0, 0.0).astype(BF16)
    causal = (lax.broadcasted_iota(jnp.int32, (HEADS_PER_BLOCK * SUB, SUB), 1)
              < lax.broadcasted_iota(jnp.int32, (HEADS_PER_BLOCK * SUB, SUB), 0) % SUB)

    def head_rows(row0, nrows):
        return [slice(h * SUB + row0, h * SUB + row0 + nrows) for h in range(HEADS_PER_BLOCK)]

    def softplus_split(z_s, masked):
        e = jnp.exp2(jnp.abs(z_s) * (-LOG2E))
        sp = jnp.maximum(z_s, 0.0) + jnp.log(1.0 + e)
        if masked:
            sp = jnp.where(causal[:z_s.shape[0]], sp, 0.0)
        hi = sp.astype(BF16)
        lo = (sp - hi.astype(F32)).astype(BF16)
        return jnp.concatenate([hi, lo], axis=1)

    def span(jobs, whole_tiles):
        z_all, cs_all = {}, {}
        for n, (t, row0, nrows, start, nsub, diag, first) in enumerate(jobs):
            assert not diag or (row0 == 0 and nrows == SUB)
            qrows = slice(t * SUB + row0, t * SUB + row0 + nrows)
            lane = lax.broadcasted_iota(jnp.int32, (nrows, LANES), 1)
            for p in range(n_blocks):
                cols = slice(p * LANES, (p + 1) * LANES)
                q = q_ref[qrows, cols]
                q2 = jnp.concatenate([jnp.where((lane // HEAD_DIM) == h, q, jnp.zeros_like(q))
                                      for h in range(HEADS_PER_BLOCK)], axis=0)
                z = lax.dot_general(q2, k_ref[pl.ds(start, nsub * SUB), cols], (((1,), (1,)), ((), ())),
                                    preferred_element_type=F32)
                z_all[n, p] = z
                lhs_parts = [softplus_split(z[:, s * SUB:(s + 1) * SUB], diag and s == nsub - 1)
                             for s in reversed(range(nsub))]
                cs_all[n, p] = jnp.dot(jnp.concatenate(lhs_parts, axis=0), tri, preferred_element_type=F32)
        latest = {}
        for n, (t, row0, nrows, start, nsub, diag, first) in enumerate(jobs):
            qrows = slice(t * SUB + row0, t * SUB + row0 + nrows)
            lane = lax.broadcasted_iota(jnp.int32, (nrows, LANES), 1)
            rows2 = HEADS_PER_BLOCK * nrows
            for p in range(n_blocks):
                cols = slice(p * LANES, (p + 1) * LANES)
                carry = None if first else jnp.concatenate([carry_ref[t, p, rs] for rs in head_rows(row0, nrows)], axis=0)
                a_cols = [None] * nsub
                for idx, s in enumerate(reversed(range(nsub))):
                    cs = cs_all[n, p][idx * rows2:(idx + 1) * rows2]
                    g = cs[:, :SUB] if carry is None else cs[:, :SUB] + carry
                    carry = cs[:, SUB:] if carry is None else carry + cs[:, SUB:]
                    a = jnp.exp2((z_all[n, p][:, s * SUB:(s + 1) * SUB] - g) * LOG2E)
                    if diag and s == nsub - 1:
                        a = jnp.where(causal, a, 0.0)
                    a_cols[s] = a.astype(BF16)
                for h, rs in enumerate(head_rows(row0, nrows)):
                    carry_ref[t, p, rs] = carry[h * nrows:(h + 1) * nrows]
                latest.setdefault(t, {}).setdefault(p, []).append((row0, nrows, carry))
                o2 = jnp.dot(jnp.concatenate(a_cols, axis=1), v_ref[pl.ds(start, nsub * SUB), cols],
                             preferred_element_type=F32)
                out = o2[0:nrows]
                for h in range(1, HEADS_PER_BLOCK):
                    out = jnp.where((lane // HEAD_DIM) == h, o2[h * nrows:(h + 1) * nrows], out)
                if first:
                    acc_ref[qrows, cols] = out
                else:
                    acc_ref[qrows, cols] += out
        for t, by_block in latest.items():
            if whole_tiles:
                assert all(len(e) == 1 and e[0][:2] == (0, SUB) for e in by_block.values())
                min_ref[t] = _min_to_vreg([e[0][2] for e in by_block.values()])
            else:
                min_ref[t] = _min_to_vreg([carry_ref[t, p] for p in range(n_blocks)])

    def window(tiles):
        n_keys = (NEAR_SUBS + 2) * SUB
        far = [slice(h * SUB, h * SUB + FAR_ROWS) for h in range(HEADS_PER_BLOCK)]
        rest = [slice(h * SUB + FAR_ROWS, (h + 1) * SUB) for h in range(HEADS_PER_BLOCK)]
        lane = lax.broadcasted_iota(jnp.int32, (SUB, LANES), 1)
        lane_far = lax.broadcasted_iota(jnp.int32, (FAR_ROWS, LANES), 1)
        ctx, scores, suffix_sums, outputs = {}, [], [], []

        def score(t, tile, p):
            start = pl.multiple_of((tile - NEAR_SUBS - 1) * SUB, SUB)
            cols = slice(p * LANES, (p + 1) * LANES)
            q = q_ref[t * SUB:(t + 1) * SUB, cols]
            q2 = jnp.concatenate([jnp.where((lane // HEAD_DIM) == h, q, jnp.zeros_like(q))
                                  for h in range(HEADS_PER_BLOCK)], axis=0)
            z = lax.dot_general(q2, k_ref[pl.ds(start, n_keys), cols], (((1,), (1,)), ((), ())),
                                preferred_element_type=F32)
            ctx[t, p] = (start, z)

        def sums(t, p):
            start, z = ctx[t, p]
            z_far = jnp.concatenate([z[rs, 0:SUB] for rs in far], axis=0)
            lhs = [softplus_split(z[:, s * SUB:(s + 1) * SUB], s == NEAR_SUBS + 1)
                   for s in range(NEAR_SUBS + 1, 0, -1)] + [softplus_split(z_far, False)]
            cs = jnp.dot(jnp.concatenate(lhs, axis=0), tri, preferred_element_type=F32)
            ctx[t, p] = (start, z, z_far, cs)

        def output(t, p, parts):
            cols = slice(p * LANES, (p + 1) * LANES)
            start, z, z_far, cs = ctx[t, p]
            carry, a_cols = None, []
            for idx, s in enumerate(range(NEAR_SUBS + 1, 0, -1)):
                blk = cs[idx * 2 * SUB:(idx + 1) * 2 * SUB]
                g = blk[:, :SUB] if carry is None else blk[:, :SUB] + carry
                carry = blk[:, SUB:] if carry is None else carry + blk[:, SUB:]
                a = jnp.exp2((z[:, s * SUB:(s + 1) * SUB] - g) * LOG2E)
                if s == NEAR_SUBS + 1:
                    a = jnp.where(causal, a, 0.0)
                a_cols.insert(0, a.astype(BF16))
            blk = cs[(NEAR_SUBS + 1) * 2 * SUB:]
            carry_far = jnp.concatenate([carry[rs] for rs in far], axis=0)
            a_far = jnp.exp2((z_far - (blk[:, :SUB] + carry_far)) * LOG2E).astype(BF16)
            carry_far = carry_far + blk[:, SUB:]
            for h in range(HEADS_PER_BLOCK):
                carry_ref[t, p, far[h]] = carry_far[h * FAR_ROWS:(h + 1) * FAR_ROWS]
                carry_ref[t, p, rest[h]] = carry[rest[h]]
                parts.append(carry[rest[h]])
            parts.append(carry_far)
            o2 = jnp.dot(jnp.concatenate(a_cols, axis=1), v_ref[pl.ds(start + SUB, (NEAR_SUBS + 1) * SUB), cols],
                         preferred_element_type=F32)
            o_far = jnp.dot(a_far, v_ref[pl.ds(start, SUB), cols], preferred_element_type=F32)
            out, out_far = o2[0:SUB], o_far[0:FAR_ROWS]
            for h in range(1, HEADS_PER_BLOCK):
                out = jnp.where((lane // HEAD_DIM) == h, o2[h * SUB:(h + 1) * SUB], out)
                out_far = jnp.where((lane_far // HEAD_DIM) == h,
                                    o_far[h * FAR_ROWS:(h + 1) * FAR_ROWS], out_far)
            acc_ref[t * SUB:t * SUB + FAR_ROWS, cols] = out[:FAR_ROWS] + out_far
            acc_ref[t * SUB + FAR_ROWS:(t + 1) * SUB, cols] = out[FAR_ROWS:]
            if p == n_blocks - 1:
                min_ref[t] = _min_to_vreg(parts)

        for t, tile in tiles:
            parts = []
            for p in range(n_blocks):
                scores.append(functools.partial(score, t, tile, p))
                suffix_sums.append(functools.partial(sums, t, p))
                outputs.append(functools.partial(output, t, p, parts))
        for stage in scores:
            stage()
        for n in range(len(suffix_sums) + OUTPUT_LAG):
            if n < len(suffix_sums):
                suffix_sums[n]()
            if n >= OUTPUT_LAG:
                outputs[n - OUTPUT_LAG]()

    @pl.when(i == 0)
    def _():
        early = [t for t in range(n_tiles) if t <= NEAR_SUBS]
        span([(t, 0, SUB, 0, t + 1, True, True) for t in early], True)
        window([(t, t) for t in range(n_tiles) if t > NEAR_SUBS])

    @pl.when(i > 0)
    def _():
        window([(t, i * n_tiles + t) for t in range(n_tiles)])

    @pl.when(jnp.min(min_ref[...]) < SKIP_ABOVE)
    def _():
        for t in range(n_tiles):
            tile = i * n_tiles + t
            if FAR_ROWS < SUB:
                @pl.when(jnp.logical_and(tile > NEAR_SUBS, jnp.min(min_ref[t]) < SKIP_ABOVE))
                def _(t=t, tile=tile):
                    start = pl.multiple_of((tile - NEAR_SUBS - 1) * SUB, SUB)
                    span([(t, FAR_ROWS, SUB - FAR_ROWS, start, 1, False, False)], False)

            def unfinished(t=t):
                return (jnp.min(min_ref[t]) < SKIP_ABOVE).astype(jnp.int32)

            def cond(state):
                rem, go = state
                return jnp.logical_and(rem >= WALK_SUBS * SUB, go > 0)

            def body(state, t=t, unfinished=unfinished):
                rem, _ = state
                start = pl.multiple_of(rem - WALK_SUBS * SUB, SUB)
                span([(t, 0, SUB, start, WALK_SUBS, False, False)], False)
                return start, unfinished()

            remaining = jnp.maximum(tile - NEAR_SUBS - 1, 0) * SUB
            rem, go = lax.while_loop(cond, body, (remaining, unfinished()))
            for tail in range(1, WALK_SUBS):
                @pl.when(jnp.logical_and(rem == tail * SUB, go > 0))
                def _(t=t, tail=tail):
                    span([(t, 0, SUB, 0, tail, False, False)], False)

    o_ref[...] = acc_ref[...].astype(o_ref.dtype)


def _attention(q, k, v, *, batch, seq):
    n, attn_dim = q.shape
    tq = ATTN_TILES * SUB
    nq = seq // tq
    return pl.pallas_call(
        _attn_kernel,
        grid=(batch, nq),
        in_specs=[
            pl.BlockSpec((tq, attn_dim), lambda b, i: (b * nq + i, 0)),
            pl.BlockSpec((seq, attn_dim), lambda b, i: (b, 0)),
            pl.BlockSpec((seq, attn_dim), lambda b, i: (b, 0)),
        ],
        out_specs=pl.BlockSpec((tq, attn_dim), lambda b, i: (b * nq + i, 0)),
        out_shape=jax.ShapeDtypeStruct((n, attn_dim), BF16),
        scratch_shapes=[pltpu.VMEM((tq, attn_dim), F32),
                        pltpu.VMEM((ATTN_TILES, attn_dim // LANES, HEADS_PER_BLOCK * SUB, SUB), F32),
                        pltpu.VMEM((ATTN_TILES, SUBLANES, SUB), F32)],
        compiler_params=pltpu.CompilerParams(
            dimension_semantics=("arbitrary", "arbitrary"), vmem_limit_bytes=VMEM_LIMIT),
    )(q, k, v)


def _ffn_kernel(x_ref, attn_ref, conv_ref, wo_ref, gain_ref, wg_ref, wu_ref, wd_ref, o_ref, act_ref, *, layer):
    attn_dim = attn_ref.shape[1]
    d_ff = wg_ref.shape[1]
    mix = (jnp.dot(attn_ref[...], wo_ref[0:attn_dim, :], preferred_element_type=F32)
           + jnp.dot(conv_ref[...], wo_ref[attn_dim:, :], preferred_element_type=F32))
    x1 = x_ref[...] + mix
    h = ((x1 * _rms_scale(x1)) * gain_ref[layer:layer + 1, :]).astype(BF16)
    for c0 in range(0, d_ff, FF_CHUNK):
        g = jnp.dot(h, wg_ref[:, c0:c0 + FF_CHUNK], preferred_element_type=F32)
        u = jnp.dot(h, wu_ref[:, c0:c0 + FF_CHUNK], preferred_element_type=F32)
        act_ref[:, c0:c0 + FF_CHUNK] = ((g * jax.nn.sigmoid(g)) * u).astype(BF16)
    o_ref[...] = x1 + jnp.dot(act_ref[...], wd_ref[...], preferred_element_type=F32)


def _ffn(x2d, attn, conv, w_out, gains, layer, w_gate, w_up, w_down):
    n, d = x2d.shape
    tm = FFN_ROW_TILE
    d_ff = w_gate.shape[1]
    const = lambda i: (0, 0)
    row = lambda i: (i, 0)
    resident = functools.partial(pl.BlockSpec, index_map=const, pipeline_mode=pl.Buffered(1))
    return pl.pallas_call(
        functools.partial(_ffn_kernel, layer=layer),
        grid=(n // tm,),
        in_specs=[
            pl.BlockSpec((tm, d), row),
            pl.BlockSpec((tm, attn.shape[1]), row),
            pl.BlockSpec((tm, conv.shape[1]), row),
            resident(w_out.shape),
            pl.BlockSpec(gains.shape, const),
            resident(w_gate.shape),
            resident(w_up.shape),
            resident(w_down.shape),
        ],
        out_specs=pl.BlockSpec((tm, d), row),
        out_shape=jax.ShapeDtypeStruct((n, d), F32),
        scratch_shapes=[pltpu.VMEM((tm, d_ff), BF16)],
        compiler_params=pltpu.CompilerParams(
            dimension_semantics=("arbitrary",), vmem_limit_bytes=VMEM_LIMIT),
    )(x2d, attn, conv, w_out, gains, w_gate, w_up, w_down)


def kernel(x, norm_mix, w_in, q_norm, k_norm, conv_w, w_out, norm_ffn, w_gate, w_up, w_down):
    batch, seq, d = x.shape
    depth = w_in.shape[0]
    conv_dim = conv_w.shape[2]
    attn_dim = w_out.shape[1] - conv_dim
    assert seq % ROW_TILE == 0 and seq % (ATTN_TILES * SUB) == 0 and attn_dim % MXU_DIM == 0
    assert (batch * seq) % FFN_ROW_TILE == 0
    assert w_gate.shape[2] % FF_CHUNK == 0 and conv_w.shape[1] == CONV_SHIFTS + 1

    xf = x.reshape(batch * seq, d)
    for l in range(depth):
        q, k, v, conv, wo, wg, wu, wd = _inproj(
            xf, norm_mix, w_in, l, q_norm, k_norm, conv_w,
            (w_out, w_gate, w_up, w_down), seq=seq, attn_dim=attn_dim, conv_dim=conv_dim)
        attn = _attention(q, k, v, batch=batch, seq=seq)
        xf = _ffn(xf, attn, conv, wo, norm_ffn, l, wg, wu, wd)
    return xf.reshape(batch, seq, d)
```

```python
import functools

import jax
import jax.numpy as jnp
from jax import lax
from jax.experimental import pallas as pl
from jax.experimental.pallas import tpu as pltpu

F32 = jnp.float32
BF16 = jnp.bfloat16

HEAD_DIM = 64
EPS = 1e-6
LANES = 128
SUBLANES = 8
BF16_SUBLANES = 16
W_CAST_ROWS = 128
MXU_DIM = 256
HEADS_PER_BLOCK = LANES // HEAD_DIM
SUB = 128
LOG2E = 1.4426950408889634
LN2 = 0.6931471805599453
SKIP_ABOVE = 104.0

ROW_TILE = 1024
ROW_SPLIT = 2
FFN_ROW_TILE = 1024
ATTN_TILES = 4
NEAR_SUBS = 1
FAR_ROWS = 32
SCORE_LEAD = 16
OUTPUT_LAG = 2
MIX_LAG = 0
WALK_SUBS = 2
FF_CHUNK = 256
CONV_SHIFTS = 2
VMEM_LIMIT = 56 * 1024 * 1024


def _rms_scale(x):
    return lax.rsqrt(jnp.mean(x * x, axis=-1, keepdims=True) + EPS)


def _inproj_kernel(x_ref, gain_ref, w_ref, qg_ref, kg_ref, cw_ref, *rest,
                   layer, tiles_per_seq, attn_dim, conv_dim, n_cast):
    cast_in, (q_ref, k_ref, v_ref, c_ref) = rest[:n_cast], rest[n_cast:n_cast + 4]
    cast_out, (halo_ref, wbf_ref) = rest[n_cast + 4:2 * n_cast + 4], rest[2 * n_cast + 4:]
    tm = x_ref.shape[0]
    i = pl.program_id(0)

    @pl.when(i == 0)
    def _():
        def cast_rows(c, _):
            rows = pl.ds(pl.multiple_of(c * W_CAST_ROWS, W_CAST_ROWS), W_CAST_ROWS)
            wbf_ref[rows, :] = w_ref[rows, :].astype(BF16)
            return 0
        lax.fori_loop(0, w_ref.shape[0] // W_CAST_ROWS, cast_rows, 0)
        halo_ref[...] = jnp.zeros_like(halo_ref)

    r = lax.broadcasted_iota(jnp.int32, (MXU_DIM, MXU_DIM), 0) // HEAD_DIM
    c = lax.broadcasted_iota(jnp.int32, (MXU_DIM, MXU_DIM), 1) // HEAD_DIM
    seg = jnp.where(r == c, 1.0, 0.0).astype(BF16)
    taps = [cw_ref[layer, n:n + 1, :] for n in range(CONV_SHIFTS + 1)]
    row = lax.broadcasted_iota(jnp.int32, (SUBLANES, conv_dim), 0)
    hm = tm // ROW_SPLIT

    def head_norm(p, gain_ref):
        gain = jnp.concatenate([gain_ref[layer:layer + 1, :]] * (attn_dim // HEAD_DIM), axis=1)
        p2 = (p * p).astype(BF16)
        ssq = jnp.concatenate(
            [jnp.dot(p2[:, s:s + MXU_DIM], seg, preferred_element_type=F32)
             for s in range(0, attn_dim, MXU_DIM)], axis=1)
        return (p * lax.rsqrt(ssq * (1.0 / HEAD_DIM) + EPS)) * gain

    for part in range(ROW_SPLIT):
        rows = slice(part * hm, (part + 1) * hm)
        x = x_ref[rows, :]
        h = ((x * _rms_scale(x)) * gain_ref[layer:layer + 1, :]).astype(BF16)

        def proj(c0, width, h=h):
            return jnp.dot(h, wbf_ref[:, c0:c0 + width], preferred_element_type=F32)

        cb = proj(3 * attn_dim, conv_dim)
        hh = proj(3 * attn_dim + conv_dim, conv_dim) * proj(3 * attn_dim + 2 * conv_dim, conv_dim)

        q_ref[rows, :] = (head_norm(proj(0, attn_dim), qg_ref) * (HEAD_DIM ** -0.5)).astype(BF16)
        k_ref[rows, :] = head_norm(proj(attn_dim, attn_dim), kg_ref).astype(BF16)
        v_ref[rows, :] = proj(2 * attn_dim, attn_dim).astype(BF16)

        def shift_down(a, n, part=part):
            prev_last = halo_ref[n, SUBLANES - 1:SUBLANES, :]
            if part == 0:
                prev_last = jnp.where(i % tiles_per_seq == 0, 0.0, prev_last)
            halo_ref[n] = a[hm - SUBLANES:hm, :]
            rolled = pltpu.roll(a, 1, 0)
            head = jnp.where(row == 0, jnp.broadcast_to(prev_last, (SUBLANES, conv_dim)), rolled[0:SUBLANES])
            return jnp.concatenate([head, rolled[SUBLANES:]], axis=0)

        y = taps[2] * hh + shift_down(taps[1] * hh + shift_down(taps[0] * hh, 0), 1)
        c_ref[rows, :] = (cb * y).astype(BF16)

    for src, dst in zip(cast_in, cast_out):
        dst[...] = src[...].astype(BF16)


def _cast_specs(weights, layer, n_steps, step_of):
    in_specs, out_specs, shapes = [], [], []
    for w in weights:
        _, rows, cols = w.shape
        chunks = max(c for c in range(1, n_steps + 1) if rows % c == 0 and (rows // c) % BF16_SUBLANES == 0)
        chunk_of = lambda *idx, chunks=chunks: jnp.minimum(step_of(*idx), chunks - 1)
        in_specs.append(pl.BlockSpec((None, rows // chunks, cols),
                                     lambda *idx, chunk_of=chunk_of: (layer, chunk_of(*idx), 0)))
        out_specs.append(pl.BlockSpec((rows // chunks, cols), lambda *idx, chunk_of=chunk_of: (chunk_of(*idx), 0)))
        shapes.append(jax.ShapeDtypeStruct((rows, cols), BF16))
    return in_specs, out_specs, shapes


def _inproj(x2d, gains, w_in, layer, q_gains, k_gains, conv_w, later_weights, *, seq, attn_dim, conv_dim):
    n, d = x2d.shape
    tm = ROW_TILE
    n_steps = n // tm
    cols = w_in.shape[2]
    const = lambda i: (0, 0)
    row = lambda i: (i, 0)
    cast_in_specs, cast_out_specs, cast_shapes = _cast_specs(later_weights, layer, n_steps, lambda i: i)
    out_shapes = [jax.ShapeDtypeStruct((n, attn_dim), BF16)] * 3 + [jax.ShapeDtypeStruct((n, conv_dim), BF16)]
    return pl.pallas_call(
        functools.partial(_inproj_kernel, layer=layer, tiles_per_seq=seq // tm, attn_dim=attn_dim,
                          conv_dim=conv_dim, n_cast=len(later_weights)),
        grid=(n_steps,),
        in_specs=[
            pl.BlockSpec((tm, d), row),
            pl.BlockSpec(gains.shape, const),
            pl.BlockSpec((None, d, cols), lambda i: (layer, 0, 0), pipeline_mode=pl.Buffered(1)),
            pl.BlockSpec(q_gains.shape, const),
            pl.BlockSpec(k_gains.shape, const),
            pl.BlockSpec(conv_w.shape, lambda i: (0, 0, 0)),
        ] + cast_in_specs,
        out_specs=[pl.BlockSpec((tm, attn_dim), row)] * 3 + [pl.BlockSpec((tm, conv_dim), row)] + cast_out_specs,
        out_shape=out_shapes + cast_shapes,
        scratch_shapes=[pltpu.VMEM((CONV_SHIFTS, SUBLANES, conv_dim), F32), pltpu.VMEM((d, cols), BF16)],
        compiler_params=pltpu.CompilerParams(
            dimension_semantics=("arbitrary",), vmem_limit_bytes=VMEM_LIMIT),
    )(x2d, gains, w_in, q_gains, k_gains, conv_w, *later_weights)


def _min_to_vreg(arrays):
    chunks = [a[r:r + SUBLANES] for a in arrays for r in range(0, a.shape[0], SUBLANES)]
    while len(chunks) > 1:
        odd = chunks[-1:] if len(chunks) % 2 else []
        chunks = [jnp.minimum(a, b) for a, b in zip(chunks[0::2], chunks[1::2])] + odd
    return chunks[0]


def _attn_kernel(q_ref, k_ref, v_ref, o_ref, acc_ref, carry_ref, min_ref):
    n_tiles = q_ref.shape[0] // SUB
    n_blocks = q_ref.shape[1] // LANES
    i = pl.program_id(1)
    assert n_tiles >= NEAR_SUBS + 1

    r = lax.broadcasted_iota(jnp.int32, (2 * SUB, 2 * SUB), 0) % SUB
    c = lax.broadcasted_iota(jnp.int32, (2 * SUB, 2 * SUB), 1)
    tri = jnp.where((c >= SUB) | (r >= c), 1.0, 0.0).astype(BF16)
    causal = (lax.broadcasted_iota(jnp.int32, (HEADS_PER_BLOCK * SUB, SUB), 1)
              < lax.broadcasted_iota(jnp.int32, (HEADS_PER_BLOCK * SUB, SUB), 0) % SUB)

    def head_rows(row0, nrows):
        return [slice(h * SUB + row0, h * SUB + row0 + nrows) for h in range(HEADS_PER_BLOCK)]

    def softplus_split(z_s, masked):
        e = jnp.exp2(jnp.abs(z_s) * (-LOG2E))
        sp = jnp.maximum(z_s, 0.0) + jnp.log(1.0 + e)
        if masked:
            sp = jnp.where(causal[:z_s.shape[0]], sp, 0.0)
        hi = sp.astype(BF16)
        lo = (sp - hi.astype(F32)).astype(BF16)
        return jnp.concatenate([hi, lo], axis=1)

    def span(jobs, whole_tiles):
        z_all, cs_all = {}, {}
        for n, (t, row0, nrows, start, nsub, diag, first) in enumerate(jobs):
            assert not diag or (row0 == 0 and nrows == SUB)
            qrows = slice(t * SUB + row0, t * SUB + row0 + nrows)
            lane = lax.broadcasted_iota(jnp.int32, (nrows, LANES), 1)
            for p in range(n_blocks):
                cols = slice(p * LANES, (p + 1) * LANES)
                q = q_ref[qrows, cols]
                q2 = jnp.concatenate([jnp.where((lane // HEAD_DIM) == h, q, jnp.zeros_like(q))
                                      for h in range(HEADS_PER_BLOCK)], axis=0)
                z = lax.dot_general(q2, k_ref[pl.ds(start, nsub * SUB), cols], (((1,), (1,)), ((), ())),
                                    preferred_element_type=F32)
                z_all[n, p] = z
                lhs_parts = [softplus_split(z[:, s * SUB:(s + 1) * SUB], diag and s == nsub - 1)
                             for s in reversed(range(nsub))]
                cs_all[n, p] = jnp.dot(jnp.concatenate(lhs_parts, axis=0), tri, preferred_element_type=F32)
        latest = {}
        for n, (t, row0, nrows, start, nsub, diag, first) in enumerate(jobs):
            qrows = slice(t * SUB + row0, t * SUB + row0 + nrows)
            lane = lax.broadcasted_iota(jnp.int32, (nrows, LANES), 1)
            rows2 = HEADS_PER_BLOCK * nrows
            for p in range(n_blocks):
                cols = slice(p * LANES, (p + 1) * LANES)
                carry = None if first else jnp.concatenate([carry_ref[t, p, rs] for rs in head_rows(row0, nrows)], axis=0)
                a_cols = [None] * nsub
                for idx, s in enumerate(reversed(range(nsub))):
                    cs = cs_all[n, p][idx * rows2:(idx + 1) * rows2]
                    g = cs[:, :SUB] if carry is None else cs[:, :SUB] + carry
                    carry = cs[:, SUB:] if carry is None else carry + cs[:, SUB:]
                    a = jnp.exp2((z_all[n, p][:, s * SUB:(s + 1) * SUB] - g) * LOG2E)
                    if diag and s == nsub - 1:
                        a = jnp.where(causal, a, 0.0)
                    a_cols[s] = a.astype(BF16)
                for h, rs in enumerate(head_rows(row0, nrows)):
                    carry_ref[t, p, rs] = carry[h * nrows:(h + 1) * nrows]
                latest.setdefault(t, {}).setdefault(p, []).append((row0, nrows, carry))
                o2 = jnp.dot(jnp.concatenate(a_cols, axis=1), v_ref[pl.ds(start, nsub * SUB), cols],
                             preferred_element_type=F32)
                out = o2[0:nrows]
                for h in range(1, HEADS_PER_BLOCK):
                    out = jnp.where((lane // HEAD_DIM) == h, o2[h * nrows:(h + 1) * nrows], out)
                if first:
                    acc_ref[qrows, cols] = out
                else:
                    acc_ref[qrows, cols] += out
        for t, by_block in latest.items():
            if whole_tiles:
                assert all(len(e) == 1 and e[0][:2] == (0, SUB) for e in by_block.values())
                min_ref[t] = _min_to_vreg([e[0][2] for e in by_block.values()])
            else:
                min_ref[t] = _min_to_vreg([carry_ref[t, p] for p in range(n_blocks)])

    def window(tiles):
        n_keys = (NEAR_SUBS + 2) * SUB
        far = [slice(h * SUB, h * SUB + FAR_ROWS) for h in range(HEADS_PER_BLOCK)]
        rest = [slice(h * SUB + FAR_ROWS, (h + 1) * SUB) for h in range(HEADS_PER_BLOCK)]
        lane = lax.broadcasted_iota(jnp.int32, (SUB, LANES), 1)
        lane_far = lax.broadcasted_iota(jnp.int32, (FAR_ROWS, LANES), 1)
        ctx, scores, suffix_sums, outputs, mixes = {}, [], [], [], []

        def score(t, tile, p):
            start = pl.multiple_of((tile - NEAR_SUBS - 1) * SUB, SUB)
            cols = slice(p * LANES, (p + 1) * LANES)
            q = q_ref[t * SUB:(t + 1) * SUB, cols]
            q2 = jnp.concatenate([jnp.where((lane // HEAD_DIM) == h, q, jnp.zeros_like(q))
                                  for h in range(HEADS_PER_BLOCK)], axis=0)
            z = lax.dot_general(q2, k_ref[pl.ds(start, n_keys), cols], (((1,), (1,)), ((), ())),
                                preferred_element_type=F32)
            ctx[t, p] = (start, z)

        def sums(t, p):
            start, z = ctx[t, p]
            z_far = jnp.concatenate([z[rs, 0:SUB] for rs in far], axis=0)
            lhs = [softplus_split(z[:, s * SUB:(s + 1) * SUB], s == NEAR_SUBS + 1)
                   for s in range(NEAR_SUBS + 1, 0, -1)] + [softplus_split(z_far, False)]
            cs = jnp.dot(jnp.concatenate(lhs, axis=0), tri, preferred_element_type=F32)
            ctx[t, p] = (start, z, z_far, cs)

        def output(t, p, parts):
            cols = slice(p * LANES, (p + 1) * LANES)
            start, z, z_far, cs = ctx[t, p]
            carry, a_cols = None, []
            for idx, s in enumerate(range(NEAR_SUBS + 1, 0, -1)):
                blk = cs[idx * 2 * SUB:(idx + 1) * 2 * SUB]
                g = blk[:, :SUB] if carry is None else blk[:, :SUB] + carry
                carry = blk[:, SUB:] if carry is None else carry + blk[:, SUB:]
                a = jnp.exp2((z[:, s * SUB:(s + 1) * SUB] - g) * LOG2E)
                if s == NEAR_SUBS + 1:
                    a = jnp.where(causal, a, 0.0)
                a_cols.insert(0, a.astype(BF16))
            blk = cs[(NEAR_SUBS + 1) * 2 * SUB:]
            carry_far = jnp.concatenate([carry[rs] for rs in far], axis=0)
            a_far = jnp.exp2((z_far - (blk[:, :SUB] + carry_far)) * LOG2E).astype(BF16)
            carry_far = carry_far + blk[:, SUB:]
            for h in range(HEADS_PER_BLOCK):
                carry_ref[t, p, far[h]] = carry_far[h * FAR_ROWS:(h + 1) * FAR_ROWS]
                carry_ref[t, p, rest[h]] = carry[rest[h]]
                parts.append(carry[rest[h]])
            parts.append(carry_far)
            if p == n_blocks - 1:
                min_ref[t] = _min_to_vreg(parts)
            ctx[t, p] = (start, a_cols, a_far)

        def mix(t, p):
            cols = slice(p * LANES, (p + 1) * LANES)
            start, a_cols, a_far = ctx[t, p]
            o2 = jnp.dot(jnp.concatenate(a_cols, axis=1), v_ref[pl.ds(start + SUB, (NEAR_SUBS + 1) * SUB), cols],
                         preferred_element_type=F32)
            o_far = jnp.dot(a_far, v_ref[pl.ds(start, SUB), cols], preferred_element_type=F32)
            out, out_far = o2[0:SUB], o_far[0:FAR_ROWS]
            for h in range(1, HEADS_PER_BLOCK):
                out = jnp.where((lane // HEAD_DIM) == h, o2[h * SUB:(h + 1) * SUB], out)
                out_far = jnp.where((lane_far // HEAD_DIM) == h,
                                    o_far[h * FAR_ROWS:(h + 1) * FAR_ROWS], out_far)
            acc_ref[t * SUB:t * SUB + FAR_ROWS, cols] = out[:FAR_ROWS] + out_far
            acc_ref[t * SUB + FAR_ROWS:(t + 1) * SUB, cols] = out[FAR_ROWS:]

        parts = {t: [] for t, _ in tiles}
        for p in range(n_blocks):
            for t, tile in tiles:
                scores.append(functools.partial(score, t, tile, p))
                suffix_sums.append(functools.partial(sums, t, p))
                outputs.append(functools.partial(output, t, p, parts[t]))
                mixes.append(functools.partial(mix, t, p))
        for n in range(len(scores) + SCORE_LEAD + OUTPUT_LAG + MIX_LAG):
            for stages, lead in ((scores, 0), (suffix_sums, SCORE_LEAD), (outputs, SCORE_LEAD + OUTPUT_LAG),
                                 (mixes, SCORE_LEAD + OUTPUT_LAG + MIX_LAG)):
                if 0 <= n - lead < len(stages):
                    stages[n - lead]()

    @pl.when(i == 0)
    def _():
        early = [t for t in range(n_tiles) if t <= NEAR_SUBS]
        span([(t, 0, SUB, 0, t + 1, True, True) for t in early], True)
        window([(t, t) for t in range(n_tiles) if t > NEAR_SUBS])

    @pl.when(i > 0)
    def _():
        window([(t, i * n_tiles + t) for t in range(n_tiles)])

    @pl.when(jnp.min(min_ref[...]) < SKIP_ABOVE)
    def _():
        for t in range(n_tiles):
            tile = i * n_tiles + t
            if FAR_ROWS < SUB:
                @pl.when(jnp.logical_and(tile > NEAR_SUBS, jnp.min(min_ref[t]) < SKIP_ABOVE))
                def _(t=t, tile=tile):
                    start = pl.multiple_of((tile - NEAR_SUBS - 1) * SUB, SUB)
                    span([(t, FAR_ROWS, SUB - FAR_ROWS, start, 1, False, False)], False)

            def unfinished(t=t):
                return (jnp.min(min_ref[t]) < SKIP_ABOVE).astype(jnp.int32)

            def cond(state):
                rem, go = state
                return jnp.logical_and(rem >= WALK_SUBS * SUB, go > 0)

            def body(state, t=t, unfinished=unfinished):
                rem, _ = state
                start = pl.multiple_of(rem - WALK_SUBS * SUB, SUB)
                span([(t, 0, SUB, start, WALK_SUBS, False, False)], False)
                return start, unfinished()

            remaining = jnp.maximum(tile - NEAR_SUBS - 1, 0) * SUB
            rem, go = lax.while_loop(cond, body, (remaining, unfinished()))
            for tail in range(1, WALK_SUBS):
                @pl.when(jnp.logical_and(rem == tail * SUB, go > 0))
                def _(t=t, tail=tail):
                    span([(t, 0, SUB, 0, tail, False, False)], False)

    o_ref[...] = acc_ref[...].astype(o_ref.dtype)


def _attention(q, k, v, *, batch, seq):
    n, attn_dim = q.shape
    tq = ATTN_TILES * SUB
    nq = seq // tq
    return pl.pallas_call(
        _attn_kernel,
        grid=(batch, nq),
        in_specs=[
            pl.BlockSpec((tq, attn_dim), lambda b, i: (b * nq + i, 0)),
            pl.BlockSpec((seq, attn_dim), lambda b, i: (b, 0)),
            pl.BlockSpec((seq, attn_dim), lambda b, i: (b, 0)),
        ],
        out_specs=pl.BlockSpec((tq, attn_dim), lambda b, i: (b * nq + i, 0)),
        out_shape=jax.ShapeDtypeStruct((n, attn_dim), BF16),
        scratch_shapes=[pltpu.VMEM((tq, attn_dim), F32),
                        pltpu.VMEM((ATTN_TILES, attn_dim // LANES, HEADS_PER_BLOCK * SUB, SUB), F32),
                        pltpu.VMEM((ATTN_TILES, SUBLANES, SUB), F32)],
        compiler_params=pltpu.CompilerParams(
            dimension_semantics=("arbitrary", "arbitrary"), vmem_limit_bytes=VMEM_LIMIT),
    )(q, k, v)


def _ffn_kernel(x_ref, attn_ref, conv_ref, wo_ref, gain_ref, wg_ref, wu_ref, wd_ref, o_ref, act_ref, *, layer):
    attn_dim = attn_ref.shape[1]
    d_ff = wg_ref.shape[1]
    mix = (jnp.dot(attn_ref[...], wo_ref[0:attn_dim, :], preferred_element_type=F32)
           + jnp.dot(conv_ref[...], wo_ref[attn_dim:, :], preferred_element_type=F32))
    x1 = x_ref[...] + mix
    h = ((x1 * _rms_scale(x1)) * gain_ref[layer:layer + 1, :]).astype(BF16)
    for c0 in range(0, d_ff, FF_CHUNK):
        g = jnp.dot(h, wg_ref[:, c0:c0 + FF_CHUNK], preferred_element_type=F32)
        u = jnp.dot(h, wu_ref[:, c0:c0 + FF_CHUNK], preferred_element_type=F32)
        act_ref[:, c0:c0 + FF_CHUNK] = ((g * jax.nn.sigmoid(g)) * u).astype(BF16)
    o_ref[...] = x1 + jnp.dot(act_ref[...], wd_ref[...], preferred_element_type=F32)


def _ffn(x2d, attn, conv, w_out, gains, layer, w_gate, w_up, w_down):
    n, d = x2d.shape
    tm = FFN_ROW_TILE
    d_ff = w_gate.shape[1]
    const = lambda i: (0, 0)
    row = lambda i: (i, 0)
    resident = functools.partial(pl.BlockSpec, index_map=const, pipeline_mode=pl.Buffered(1))
    return pl.pallas_call(
        functools.partial(_ffn_kernel, layer=layer),
        grid=(n // tm,),
        in_specs=[
            pl.BlockSpec((tm, d), row),
            pl.BlockSpec((tm, attn.shape[1]), row),
            pl.BlockSpec((tm, conv.shape[1]), row),
            resident(w_out.shape),
            pl.BlockSpec(gains.shape, const),
            resident(w_gate.shape),
            resident(w_up.shape),
            resident(w_down.shape),
        ],
        out_specs=pl.BlockSpec((tm, d), row),
        out_shape=jax.ShapeDtypeStruct((n, d), F32),
        scratch_shapes=[pltpu.VMEM((tm, d_ff), BF16)],
        compiler_params=pltpu.CompilerParams(
            dimension_semantics=("arbitrary",), vmem_limit_bytes=VMEM_LIMIT),
    )(x2d, attn, conv, w_out, gains, w_gate, w_up, w_down)


def kernel(x, norm_mix, w_in, q_norm, k_norm, conv_w, w_out, norm_ffn, w_gate, w_up, w_down):
    batch, seq, d = x.shape
    depth = w_in.shape[0]
    conv_dim = conv_w.shape[2]
    attn_dim = w_out.shape[1] - conv_dim
    assert seq % ROW_TILE == 0 and seq % (ATTN_TILES * SUB) == 0 and attn_dim % MXU_DIM == 0
    assert (batch * seq) % FFN_ROW_TILE == 0
    assert w_gate.shape[2] % FF_CHUNK == 0 and conv_w.shape[1] == CONV_SHIFTS + 1

    xf = x.reshape(batch * seq, d)
    for l in range(depth):
        q, k, v, conv, wo, wg, wu, wd = _inproj(
            xf, norm_mix, w_in, l, q_norm, k_norm, conv_w,
            (w_out, w_gate, w_up, w_down), seq=seq, attn_dim=attn_dim, conv_dim=conv_dim)
        attn = _attention(q, k, v, batch=batch, seq=seq)
        xf = _ffn(xf, attn, conv, wo, norm_ffn, l, wg, wu, wd)
    return xf.reshape(batch, seq, d)
```

```python
import functools

import jax
import jax.numpy as jnp
from jax import lax
from jax.experimental import pallas as pl
from jax.experimental.pallas import tpu as pltpu

F32 = jnp.float32
BF16 = jnp.bfloat16

HEAD_DIM = 64
EPS = 1e-6
LANES = 128
SUBLANES = 8
BF16_SUBLANES = 16
W_CAST_ROWS = 128
MXU_DIM = 256
HEADS_PER_BLOCK = LANES // HEAD_DIM
SUB = 128
LOG2E = 1.4426950408889634
LN2 = 0.6931471805599453
SKIP_ABOVE = 104.0

ROW_TILE = 1024
ROW_SPLIT = 1
FFN_ROW_TILE = 1024
ATTN_TILES = 4
NEAR_SUBS = 1
FAR_ROWS = 32
SCORE_LEAD = 16
OUTPUT_LAG = 2
MIX_LAG = 0
WALK_SUBS = 2
FF_CHUNK = 256
CONV_SHIFTS = 2
VMEM_LIMIT = 56 * 1024 * 1024


def _rms_scale(x):
    return lax.rsqrt(jnp.mean(x * x, axis=-1, keepdims=True) + EPS)


def _inproj_kernel(x_ref, gain_ref, w_ref, qg_ref, kg_ref, cw_ref, *rest,
                   layer, tiles_per_seq, attn_dim, conv_dim, n_cast):
    cast_in, (q_ref, k_ref, v_ref, c_ref) = rest[:n_cast], rest[n_cast:n_cast + 4]
    cast_out, (halo_ref, wbf_ref) = rest[n_cast + 4:2 * n_cast + 4], rest[2 * n_cast + 4:]
    tm = x_ref.shape[0]
    i = pl.program_id(0)

    @pl.when(i == 0)
    def _():
        def cast_rows(c, _):
            rows = pl.ds(pl.multiple_of(c * W_CAST_ROWS, W_CAST_ROWS), W_CAST_ROWS)
            wbf_ref[rows, :] = w_ref[rows, :].astype(BF16)
            return 0
        lax.fori_loop(0, w_ref.shape[0] // W_CAST_ROWS, cast_rows, 0)
        halo_ref[...] = jnp.zeros_like(halo_ref)

    r = lax.broadcasted_iota(jnp.int32, (MXU_DIM, MXU_DIM), 0) // HEAD_DIM
    c = lax.broadcasted_iota(jnp.int32, (MXU_DIM, MXU_DIM), 1) // HEAD_DIM
    seg = jnp.where(r == c, 1.0, 0.0).astype(BF16)
    taps = [cw_ref[layer, n:n + 1, :] for n in range(CONV_SHIFTS + 1)]
    row = lax.broadcasted_iota(jnp.int32, (SUBLANES, conv_dim), 0)
    hm = tm // ROW_SPLIT

    def head_norm(p, gain_ref):
        gain = jnp.concatenate([gain_ref[layer:layer + 1, :]] * (attn_dim // HEAD_DIM), axis=1)
        p2 = (p * p).astype(BF16)
        ssq = jnp.concatenate(
            [jnp.dot(p2[:, s:s + MXU_DIM], seg, preferred_element_type=F32)
             for s in range(0, attn_dim, MXU_DIM)], axis=1)
        return (p * lax.rsqrt(ssq * (1.0 / HEAD_DIM) + EPS)) * gain

    for part in range(ROW_SPLIT):
        rows = slice(part * hm, (part + 1) * hm)
        x = x_ref[rows, :]
        h = ((x * _rms_scale(x)) * gain_ref[layer:layer + 1, :]).astype(BF16)

        def proj(c0, width, h=h):
            return jnp.dot(h, wbf_ref[:, c0:c0 + width], preferred_element_type=F32)

        cb = proj(3 * attn_dim, conv_dim)
        hh = proj(3 * attn_dim + conv_dim, conv_dim) * proj(3 * attn_dim + 2 * conv_dim, conv_dim)

        q_ref[rows, :] = (head_norm(proj(0, attn_dim), qg_ref) * (HEAD_DIM ** -0.5)).astype(BF16)
        k_ref[rows, :] = head_norm(proj(attn_dim, attn_dim), kg_ref).astype(BF16)
        v_ref[rows, :] = proj(2 * attn_dim, attn_dim).astype(BF16)

        def shift_down(a, n, part=part):
            prev_last = halo_ref[n, SUBLANES - 1:SUBLANES, :]
            if part == 0:
                prev_last = jnp.where(i % tiles_per_seq == 0, 0.0, prev_last)
            halo_ref[n] = a[hm - SUBLANES:hm, :]
            rolled = pltpu.roll(a, 1, 0)
            head = jnp.where(row == 0, jnp.broadcast_to(prev_last, (SUBLANES, conv_dim)), rolled[0:SUBLANES])
            return jnp.concatenate([head, rolled[SUBLANES:]], axis=0)

        y = taps[2] * hh + shift_down(taps[1] * hh + shift_down(taps[0] * hh, 0), 1)
        c_ref[rows, :] = (cb * y).astype(BF16)

    for src, dst in zip(cast_in, cast_out):
        dst[...] = src[...].astype(BF16)


def _cast_specs(weights, layer, n_steps, step_of):
    in_specs, out_specs, shapes = [], [], []
    for w in weights:
        _, rows, cols = w.shape
        chunks = max(c for c in range(1, n_steps + 1) if rows % c == 0 and (rows // c) % BF16_SUBLANES == 0)
        chunk_of = lambda *idx, chunks=chunks: jnp.minimum(step_of(*idx), chunks - 1)
        in_specs.append(pl.BlockSpec((None, rows // chunks, cols),
                                     lambda *idx, chunk_of=chunk_of: (layer, chunk_of(*idx), 0)))
        out_specs.append(pl.BlockSpec((rows // chunks, cols), lambda *idx, chunk_of=chunk_of: (chunk_of(*idx), 0)))
        shapes.append(jax.ShapeDtypeStruct((rows, cols), BF16))
    return in_specs, out_specs, shapes


def _inproj(x2d, gains, w_in, layer, q_gains, k_gains, conv_w, later_weights, *, seq, attn_dim, conv_dim):
    n, d = x2d.shape
    tm = ROW_TILE
    n_steps = n // tm
    cols = w_in.shape[2]
    const = lambda i: (0, 0)
    row = lambda i: (i, 0)
    cast_in_specs, cast_out_specs, cast_shapes = _cast_specs(later_weights, layer, n_steps, lambda i: i)
    out_shapes = [jax.ShapeDtypeStruct((n, attn_dim), BF16)] * 3 + [jax.ShapeDtypeStruct((n, conv_dim), BF16)]
    return pl.pallas_call(
        functools.partial(_inproj_kernel, layer=layer, tiles_per_seq=seq // tm, attn_dim=attn_dim,
                          conv_dim=conv_dim, n_cast=len(later_weights)),
        grid=(n_steps,),
        in_specs=[
            pl.BlockSpec((tm, d), row),
            pl.BlockSpec(gains.shape, const),
            pl.BlockSpec((None, d, cols), lambda i: (layer, 0, 0), pipeline_mode=pl.Buffered(1)),
            pl.BlockSpec(q_gains.shape, const),
            pl.BlockSpec(k_gains.shape, const),
            pl.BlockSpec(conv_w.shape, lambda i: (0, 0, 0)),
        ] + cast_in_specs,
        out_specs=[pl.BlockSpec((tm, attn_dim), row)] * 3 + [pl.BlockSpec((tm, conv_dim), row)] + cast_out_specs,
        out_shape=out_shapes + cast_shapes,
        scratch_shapes=[pltpu.VMEM((CONV_SHIFTS, SUBLANES, conv_dim), F32), pltpu.VMEM((d, cols), BF16)],
        compiler_params=pltpu.CompilerParams(
            dimension_semantics=("arbitrary",), vmem_limit_bytes=VMEM_LIMIT),
    )(x2d, gains, w_in, q_gains, k_gains, conv_w, *later_weights)


def _min_to_vreg(arrays):
    chunks = [a[r:r + SUBLANES] for a in arrays for r in range(0, a.shape[0], SUBLANES)]
    while len(chunks) > 1:
        odd = chunks[-1:] if len(chunks) % 2 else []
        chunks = [jnp.minimum(a, b) for a, b in zip(chunks[0::2], chunks[1::2])] + odd
    return chunks[0]


def _attn_kernel(q_ref, k_ref, v_ref, o_ref, acc_ref, carry_ref, min_ref):
    n_tiles = q_ref.shape[0] // SUB
    n_blocks = q_ref.shape[1] // LANES
    i = pl.program_id(1)
    assert n_tiles >= NEAR_SUBS + 1

    r = lax.broadcasted_iota(jnp.int32, (2 * SUB, 2 * SUB), 0) % SUB
    c = lax.broadcasted_iota(jnp.int32, (2 * SUB, 2 * SUB), 1)
    tri = jnp.where((c >= SUB) | (r >= c), 1.0, 0.0).astype(BF16)
    causal = (lax.broadcasted_iota(jnp.int32, (HEADS_PER_BLOCK * SUB, SUB), 1)
              < lax.broadcasted_iota(jnp.int32, (HEADS_PER_BLOCK * SUB, SUB), 0) % SUB)

    def head_rows(row0, nrows):
        return [slice(h * SUB + row0, h * SUB + row0 + nrows) for h in range(HEADS_PER_BLOCK)]

    def softplus_split(z_s, masked):
        e = jnp.exp2(jnp.abs(z_s) * (-LOG2E))
        sp = jnp.maximum(z_s, 0.0) + jnp.log(1.0 + e)
        if masked:
            sp = jnp.where(causal[:z_s.shape[0]], sp, 0.0)
        hi = sp.astype(BF16)
        lo = (sp - hi.astype(F32)).astype(BF16)
        return jnp.concatenate([hi, lo], axis=1)

    def span(jobs, whole_tiles):
        z_all, cs_all = {}, {}
        for n, (t, row0, nrows, start, nsub, diag, first) in enumerate(jobs):
            assert not diag or (row0 == 0 and nrows == SUB)
            qrows = slice(t * SUB + row0, t * SUB + row0 + nrows)
            lane = lax.broadcasted_iota(jnp.int32, (nrows, LANES), 1)
            for p in range(n_blocks):
                cols = slice(p * LANES, (p + 1) * LANES)
                q = q_ref[qrows, cols]
                q2 = jnp.concatenate([jnp.where((lane // HEAD_DIM) == h, q, jnp.zeros_like(q))
                                      for h in range(HEADS_PER_BLOCK)], axis=0)
                z = lax.dot_general(q2, k_ref[pl.ds(start, nsub * SUB), cols], (((1,), (1,)), ((), ())),
                                    preferred_element_type=F32)
                z_all[n, p] = z
                lhs_parts = [softplus_split(z[:, s * SUB:(s + 1) * SUB], diag and s == nsub - 1)
                             for s in reversed(range(nsub))]
                cs_all[n, p] = jnp.dot(jnp.concatenate(lhs_parts, axis=0), tri, preferred_element_type=F32)
        latest = {}
        for n, (t, row0, nrows, start, nsub, diag, first) in enumerate(jobs):
            qrows = slice(t * SUB + row0, t * SUB + row0 + nrows)
            lane = lax.broadcasted_iota(jnp.int32, (nrows, LANES), 1)
            rows2 = HEADS_PER_BLOCK * nrows
            for p in range(n_blocks):
                cols = slice(p * LANES, (p + 1) * LANES)
                carry = None if first else jnp.concatenate([carry_ref[t, p, rs] for rs in head_rows(row0, nrows)], axis=0)
                a_cols = [None] * nsub
                for idx, s in enumerate(reversed(range(nsub))):
                    cs = cs_all[n, p][idx * rows2:(idx + 1) * rows2]
                    g = cs[:, :SUB] if carry is None else cs[:, :SUB] + carry
                    carry = cs[:, SUB:] if carry is None else carry + cs[:, SUB:]
                    a = jnp.exp2((z_all[n, p][:, s * SUB:(s + 1) * SUB] - g) * LOG2E)
                    if diag and s == nsub - 1:
                        a = jnp.where(causal, a, 0.0)
                    a_cols[s] = a.astype(BF16)
                for h, rs in enumerate(head_rows(row0, nrows)):
                    carry_ref[t, p, rs] = carry[h * nrows:(h + 1) * nrows]
                latest.setdefault(t, {}).setdefault(p, []).append((row0, nrows, carry))
                o2 = jnp.dot(jnp.concatenate(a_cols, axis=1), v_ref[pl.ds(start, nsub * SUB), cols],
                             preferred_element_type=F32)
                out = o2[0:nrows]
                for h in range(1, HEADS_PER_BLOCK):
                    out = jnp.where((lane // HEAD_DIM) == h, o2[h * nrows:(h + 1) * nrows], out)
                if first:
                    acc_ref[qrows, cols] = out
                else:
                    acc_ref[qrows, cols] += out
        for t, by_block in latest.items():
            if whole_tiles:
                assert all(len(e) == 1 and e[0][:2] == (0, SUB) for e in by_block.values())
                min_ref[t] = _min_to_vreg([e[0][2] for e in by_block.values()])
            else:
                min_ref[t] = _min_to_vreg([carry_ref[t, p] for p in range(n_blocks)])

    def window(tiles):
        n_keys = (NEAR_SUBS + 2) * SUB
        far = [slice(h * SUB, h * SUB + FAR_ROWS) for h in range(HEADS_PER_BLOCK)]
        rest = [slice(h * SUB + FAR_ROWS, (h + 1) * SUB) for h in range(HEADS_PER_BLOCK)]
        lane = lax.broadcasted_iota(jnp.int32, (SUB, LANES), 1)
        lane_far = lax.broadcasted_iota(jnp.int32, (FAR_ROWS, LANES), 1)
        ctx, scores, suffix_sums, outputs, mixes = {}, [], [], [], []

        def score(t, tile, p):
            start = pl.multiple_of((tile - NEAR_SUBS - 1) * SUB, SUB)
            cols = slice(p * LANES, (p + 1) * LANES)
            q = q_ref[t * SUB:(t + 1) * SUB, cols]
            q2 = jnp.concatenate([jnp.where((lane // HEAD_DIM) == h, q, jnp.zeros_like(q))
                                  for h in range(HEADS_PER_BLOCK)], axis=0)
            z = lax.dot_general(q2, k_ref[pl.ds(start, n_keys), cols], (((1,), (1,)), ((), ())),
                                preferred_element_type=F32)
            ctx[t, p] = (start, z)

        def sums(t, p):
            start, z = ctx[t, p]
            z_far = jnp.concatenate([z[rs, 0:SUB] for rs in far], axis=0)
            lhs = [softplus_split(z[:, s * SUB:(s + 1) * SUB], s == NEAR_SUBS + 1)
                   for s in range(NEAR_SUBS + 1, 0, -1)] + [softplus_split(z_far, False)]
            cs = jnp.dot(jnp.concatenate(lhs, axis=0), tri, preferred_element_type=F32)
            ctx[t, p] = (start, z, z_far, cs)

        def output(t, p, parts):
            cols = slice(p * LANES, (p + 1) * LANES)
            start, z, z_far, cs = ctx[t, p]
            carry, a_cols = None, []
            for idx, s in enumerate(range(NEAR_SUBS + 1, 0, -1)):
                blk = cs[idx * 2 * SUB:(idx + 1) * 2 * SUB]
                g = blk[:, :SUB] if carry is None else blk[:, :SUB] + carry
                carry = blk[:, SUB:] if carry is None else carry + blk[:, SUB:]
                a = jnp.exp2((z[:, s * SUB:(s + 1) * SUB] - g) * LOG2E)
                if s == NEAR_SUBS + 1:
                    a = jnp.where(causal, a, 0.0)
                a_cols.insert(0, a.astype(BF16))
            blk = cs[(NEAR_SUBS + 1) * 2 * SUB:]
            carry_far = jnp.concatenate([carry[rs] for rs in far], axis=0)
            a_far = jnp.exp2((z_far - (blk[:, :SUB] + carry_far)) * LOG2E).astype(BF16)
            carry_far = carry_far + blk[:, SUB:]
            for h in range(HEADS_PER_BLOCK):
                carry_ref[t, p, far[h]] = carry_far[h * FAR_ROWS:(h + 1) * FAR_ROWS]
                carry_ref[t, p, rest[h]] = carry[rest[h]]
                parts.append(carry[rest[h]])
            parts.append(carry_far)
            if p == n_blocks - 1:
                min_ref[t] = _min_to_vreg(parts)
            ctx[t, p] = (start, a_cols, a_far)

        def mix(t, p):
            cols = slice(p * LANES, (p + 1) * LANES)
            start, a_cols, a_far = ctx[t, p]
            o2 = jnp.dot(jnp.concatenate(a_cols, axis=1), v_ref[pl.ds(start + SUB, (NEAR_SUBS + 1) * SUB), cols],
                         preferred_element_type=F32)
            o_far = jnp.dot(a_far, v_ref[pl.ds(start, SUB), cols], preferred_element_type=F32)
            out, out_far = o2[0:SUB], o_far[0:FAR_ROWS]
            for h in range(1, HEADS_PER_BLOCK):
                out = jnp.where((lane // HEAD_DIM) == h, o2[h * SUB:(h + 1) * SUB], out)
                out_far = jnp.where((lane_far // HEAD_DIM) == h,
                                    o_far[h * FAR_ROWS:(h + 1) * FAR_ROWS], out_far)
            acc_ref[t * SUB:t * SUB + FAR_ROWS, cols] = out[:FAR_ROWS] + out_far
            acc_ref[t * SUB + FAR_ROWS:(t + 1) * SUB, cols] = out[FAR_ROWS:]

        parts = {t: [] for t, _ in tiles}
        for p in range(n_blocks):
            for t, tile in tiles:
                scores.append(functools.partial(score, t, tile, p))
                suffix_sums.append(functools.partial(sums, t, p))
                outputs.append(functools.partial(output, t, p, parts[t]))
                mixes.append(functools.partial(mix, t, p))
        for n in range(len(scores) + SCORE_LEAD + OUTPUT_LAG + MIX_LAG):
            for stages, lead in ((scores, 0), (suffix_sums, SCORE_LEAD), (outputs, SCORE_LEAD + OUTPUT_LAG),
                                 (mixes, SCORE_LEAD + OUTPUT_LAG + MIX_LAG)):
                if 0 <= n - lead < len(stages):
                    stages[n - lead]()

    @pl.when(i == 0)
    def _():
        early = [t for t in range(n_tiles) if t <= NEAR_SUBS]
        span([(t, 0, SUB, 0, t + 1, True, True) for t in early], True)
        window([(t, t) for t in range(n_tiles) if t > NEAR_SUBS])

    @pl.when(i > 0)
    def _():
        window([(t, i * n_tiles + t) for t in range(n_tiles)])

    @pl.when(jnp.min(min_ref[...]) < SKIP_ABOVE)
    def _():
        for t in range(n_tiles):
            tile = i * n_tiles + t
            if FAR_ROWS < SUB:
                @pl.when(jnp.logical_and(tile > NEAR_SUBS, jnp.min(min_ref[t]) < SKIP_ABOVE))
                def _(t=t, tile=tile):
                    start = pl.multiple_of((tile - NEAR_SUBS - 1) * SUB, SUB)
                    span([(t, FAR_ROWS, SUB - FAR_ROWS, start, 1, False, False)], False)

            def unfinished(t=t):
                return (jnp.min(min_ref[t]) < SKIP_ABOVE).astype(jnp.int32)

            def cond(state):
                rem, go = state
                return jnp.logical_and(rem >= WALK_SUBS * SUB, go > 0)

            def body(state, t=t, unfinished=unfinished):
                rem, _ = state
                start = pl.multiple_of(rem - WALK_SUBS * SUB, SUB)
                span([(t, 0, SUB, start, WALK_SUBS, False, False)], False)
                return start, unfinished()

            remaining = jnp.maximum(tile - NEAR_SUBS - 1, 0) * SUB
            rem, go = lax.while_loop(cond, body, (remaining, unfinished()))
            for tail in range(1, WALK_SUBS):
                @pl.when(jnp.logical_and(rem == tail * SUB, go > 0))
                def _(t=t, tail=tail):
                    span([(t, 0, SUB, 0, tail, False, False)], False)

    o_ref[...] = acc_ref[...].astype(o_ref.dtype)


def _attention(q, k, v, *, batch, seq):
    n, attn_dim = q.shape
    tq = ATTN_TILES * SUB
    nq = seq // tq
    return pl.pallas_call(
        _attn_kernel,
        grid=(batch, nq),
        in_specs=[
            pl.BlockSpec((tq, attn_dim), lambda b, i: (b * nq + i, 0)),
            pl.BlockSpec((seq, attn_dim), lambda b, i: (b, 0)),
            pl.BlockSpec((seq, attn_dim), lambda b, i: (b, 0)),
        ],
        out_specs=pl.BlockSpec((tq, attn_dim), lambda b, i: (b * nq + i, 0)),
        out_shape=jax.ShapeDtypeStruct((n, attn_dim), BF16),
        scratch_shapes=[pltpu.VMEM((tq, attn_dim), F32),
                        pltpu.VMEM((ATTN_TILES, attn_dim // LANES, HEADS_PER_BLOCK * SUB, SUB), F32),
                        pltpu.VMEM((ATTN_TILES, SUBLANES, SUB), F32)],
        compiler_params=pltpu.CompilerParams(
            dimension_semantics=("arbitrary", "arbitrary"), vmem_limit_bytes=VMEM_LIMIT),
    )(q, k, v)


def _ffn_kernel(x_ref, attn_ref, conv_ref, wo_ref, gain_ref, wg_ref, wu_ref, wd_ref, o_ref, act_ref, *, layer):
    attn_dim = attn_ref.shape[1]
    d_ff = wg_ref.shape[1]
    mix = (jnp.dot(attn_ref[...], wo_ref[0:attn_dim, :], preferred_element_type=F32)
           + jnp.dot(conv_ref[...], wo_ref[attn_dim:, :], preferred_element_type=F32))
    x1 = x_ref[...] + mix
    h = ((x1 * _rms_scale(x1)) * gain_ref[layer:layer + 1, :]).astype(BF16)
    for c0 in range(0, d_ff, FF_CHUNK):
        g = jnp.dot(h, wg_ref[:, c0:c0 + FF_CHUNK], preferred_element_type=F32)
        u = jnp.dot(h, wu_ref[:, c0:c0 + FF_CHUNK], preferred_element_type=F32)
        act_ref[:, c0:c0 + FF_CHUNK] = ((g * jax.nn.sigmoid(g)) * u).astype(BF16)
    o_ref[...] = x1 + jnp.dot(act_ref[...], wd_ref[...], preferred_element_type=F32)


def _ffn(x2d, attn, conv, w_out, gains, layer, w_gate, w_up, w_down):
    n, d = x2d.shape
    tm = FFN_ROW_TILE
    d_ff = w_gate.shape[1]
    const = lambda i: (0, 0)
    row = lambda i: (i, 0)
    resident = functools.partial(pl.BlockSpec, index_map=const, pipeline_mode=pl.Buffered(1))
    return pl.pallas_call(
        functools.partial(_ffn_kernel, layer=layer),
        grid=(n // tm,),
        in_specs=[
            pl.BlockSpec((tm, d), row),
            pl.BlockSpec((tm, attn.shape[1]), row),
            pl.BlockSpec((tm, conv.shape[1]), row),
            resident(w_out.shape),
            pl.BlockSpec(gains.shape, const),
            resident(w_gate.shape),
            resident(w_up.shape),
            resident(w_down.shape),
        ],
        out_specs=pl.BlockSpec((tm, d), row),
        out_shape=jax.ShapeDtypeStruct((n, d), F32),
        scratch_shapes=[pltpu.VMEM((tm, d_ff), BF16)],
        compiler_params=pltpu.CompilerParams(
            dimension_semantics=("arbitrary",), vmem_limit_bytes=VMEM_LIMIT),
    )(x2d, attn, conv, w_out, gains, w_gate, w_up, w_down)


def kernel(x, norm_mix, w_in, q_norm, k_norm, conv_w, w_out, norm_ffn, w_gate, w_up, w_down):
    batch, seq, d = x.shape
    depth = w_in.shape[0]
    conv_dim = conv_w.shape[2]
    attn_dim = w_out.shape[1] - conv_dim
    assert seq % ROW_TILE == 0 and seq % (ATTN_TILES * SUB) == 0 and attn_dim % MXU_DIM == 0
    assert (batch * seq) % FFN_ROW_TILE == 0
    assert w_gate.shape[2] % FF_CHUNK == 0 and conv_w.shape[1] == CONV_SHIFTS + 1

    xf = x.reshape(batch * seq, d)
    for l in range(depth):
        q, k, v, conv, wo, wg, wu, wd = _inproj(
            xf, norm_mix, w_in, l, q_norm, k_norm, conv_w,
            (w_out, w_gate, w_up, w_down), seq=seq, attn_dim=attn_dim, conv_dim=conv_dim)
        attn = _attention(q, k, v, batch=batch, seq=seq)
        xf = _ffn(xf, attn, conv, wo, norm_ffn, l, wg, wu, wd)
    return xf.reshape(batch, seq, d)
```

```python
import functools

import jax
import jax.numpy as jnp
from jax import lax
from jax.experimental import pallas as pl
from jax.experimental.pallas import tpu as pltpu

F32 = jnp.float32
BF16 = jnp.bfloat16

HEAD_DIM = 64
EPS = 1e-6
LANES = 128
SUBLANES = 8
BF16_SUBLANES = 16
W_CAST_ROWS = 128
MXU_DIM = 256
HEADS_PER_BLOCK = LANES // HEAD_DIM
SUB = 128
LOG2E = 1.4426950408889634
LN2 = 0.6931471805599453
SKIP_ABOVE = 104.0

ROW_TILE = 1024
ROW_SPLIT = 1
FFN_ROW_TILE = 1024
ATTN_TILES = 4
NEAR_SUBS = 1
FAR_ROWS = 32
SCORE_LEAD = 16
OUTPUT_LAG = 2
MIX_LAG = 0
WALK_SUBS = 2
FF_CHUNK = 256
CONV_SHIFTS = 2
VMEM_LIMIT = 56 * 1024 * 1024


def _rms_scale(x):
    return lax.rsqrt(jnp.mean(x * x, axis=-1, keepdims=True) + EPS)


def _inproj_kernel(x_ref, gain_ref, w_ref, qg_ref, kg_ref, cw_ref, *rest,
                   layer, tiles_per_seq, attn_dim, conv_dim, n_cast):
    cast_in, (q_ref, k_ref, v_ref, c_ref) = rest[:n_cast], rest[n_cast:n_cast + 4]
    cast_out, (halo_ref, wbf_ref) = rest[n_cast + 4:2 * n_cast + 4], rest[2 * n_cast + 4:]
    tm = x_ref.shape[0]
    i = pl.program_id(0)

    @pl.when(i == 0)
    def _():
        def cast_rows(c, _):
            rows = pl.ds(pl.multiple_of(c * W_CAST_ROWS, W_CAST_ROWS), W_CAST_ROWS)
            wbf_ref[rows, :] = w_ref[rows, :].astype(BF16)
            return 0
        lax.fori_loop(0, w_ref.shape[0] // W_CAST_ROWS, cast_rows, 0)
        halo_ref[...] = jnp.zeros_like(halo_ref)

    r = lax.broadcasted_iota(jnp.int32, (MXU_DIM, MXU_DIM), 0) // HEAD_DIM
    c = lax.broadcasted_iota(jnp.int32, (MXU_DIM, MXU_DIM), 1) // HEAD_DIM
    seg = jnp.where(r == c, 1.0, 0.0).astype(BF16)
    taps = [cw_ref[layer, n:n + 1, :] for n in range(CONV_SHIFTS + 1)]
    row = lax.broadcasted_iota(jnp.int32, (SUBLANES, conv_dim), 0)
    hm = tm // ROW_SPLIT

    def head_norm(p, gain_ref):
        gain = jnp.concatenate([gain_ref[layer:layer + 1, :]] * (attn_dim // HEAD_DIM), axis=1)
        p2 = (p * p).astype(BF16)
        ssq = jnp.concatenate(
            [jnp.dot(p2[:, s:s + MXU_DIM], seg, preferred_element_type=F32)
             for s in range(0, attn_dim, MXU_DIM)], axis=1)
        return (p * lax.rsqrt(ssq * (1.0 / HEAD_DIM) + EPS)) * gain

    for part in range(ROW_SPLIT):
        rows = slice(part * hm, (part + 1) * hm)
        x = x_ref[rows, :]
        h = ((x * _rms_scale(x)) * gain_ref[layer:layer + 1, :]).astype(BF16)

        def proj(c0, width, h=h):
            return jnp.dot(h, wbf_ref[:, c0:c0 + width], preferred_element_type=F32)

        cb = proj(3 * attn_dim, conv_dim)
        hh = proj(3 * attn_dim + conv_dim, conv_dim) * proj(3 * attn_dim + 2 * conv_dim, conv_dim)

        q_ref[rows, :] = (head_norm(proj(0, attn_dim), qg_ref) * (HEAD_DIM ** -0.5)).astype(BF16)
        k_ref[rows, :] = head_norm(proj(attn_dim, attn_dim), kg_ref).astype(BF16)
        v_ref[rows, :] = proj(2 * attn_dim, attn_dim).astype(BF16)

        def shift_down(a, n, part=part):
            prev_last = halo_ref[n, SUBLANES - 1:SUBLANES, :]
            if part == 0:
                prev_last = jnp.where(i % tiles_per_seq == 0, 0.0, prev_last)
            halo_ref[n] = a[hm - SUBLANES:hm, :]
            rolled = pltpu.roll(a, 1, 0)
            head = jnp.where(row == 0, jnp.broadcast_to(prev_last, (SUBLANES, conv_dim)), rolled[0:SUBLANES])
            return jnp.concatenate([head, rolled[SUBLANES:]], axis=0)

        y = taps[2] * hh + shift_down(taps[1] * hh + shift_down(taps[0] * hh, 0), 1)
        c_ref[rows, :] = (cb * y).astype(BF16)

    for src, dst in zip(cast_in, cast_out):
        dst[...] = src[...].astype(BF16)


def _cast_specs(weights, layer, n_steps, step_of):
    in_specs, out_specs, shapes = [], [], []
    for w in weights:
        _, rows, cols = w.shape
        chunks = max(c for c in range(1, n_steps + 1) if rows % c == 0 and (rows // c) % BF16_SUBLANES == 0)
        chunk_of = lambda *idx, chunks=chunks: jnp.minimum(step_of(*idx), chunks - 1)
        in_specs.append(pl.BlockSpec((None, rows // chunks, cols),
                                     lambda *idx, chunk_of=chunk_of: (layer, chunk_of(*idx), 0)))
        out_specs.append(pl.BlockSpec((rows // chunks, cols), lambda *idx, chunk_of=chunk_of: (chunk_of(*idx), 0)))
        shapes.append(jax.ShapeDtypeStruct((rows, cols), BF16))
    return in_specs, out_specs, shapes


def _inproj(x2d, gains, w_in, layer, q_gains, k_gains, conv_w, later_weights, *, seq, attn_dim, conv_dim):
    n, d = x2d.shape
    tm = ROW_TILE
    n_steps = n // tm
    cols = w_in.shape[2]
    const = lambda i: (0, 0)
    row = lambda i: (i, 0)
    cast_in_specs, cast_out_specs, cast_shapes = _cast_specs(later_weights, layer, n_steps, lambda i: i)
    out_shapes = [jax.ShapeDtypeStruct((n, attn_dim), BF16)] * 3 + [jax.ShapeDtypeStruct((n, conv_dim), BF16)]
    return pl.pallas_call(
        functools.partial(_inproj_kernel, layer=layer, tiles_per_seq=seq // tm, attn_dim=attn_dim,
                          conv_dim=conv_dim, n_cast=len(later_weights)),
        grid=(n_steps,),
        in_specs=[
            pl.BlockSpec((tm, d), row),
            pl.BlockSpec(gains.shape, const),
            pl.BlockSpec((None, d, cols), lambda i: (layer, 0, 0), pipeline_mode=pl.Buffered(1)),
            pl.BlockSpec(q_gains.shape, const),
            pl.BlockSpec(k_gains.shape, const),
            pl.BlockSpec(conv_w.shape, lambda i: (0, 0, 0)),
        ] + cast_in_specs,
        out_specs=[pl.BlockSpec((tm, attn_dim), row)] * 3 + [pl.BlockSpec((tm, conv_dim), row)] + cast_out_specs,
        out_shape=out_shapes + cast_shapes,
        scratch_shapes=[pltpu.VMEM((CONV_SHIFTS, SUBLANES, conv_dim), F32), pltpu.VMEM((d, cols), BF16)],
        compiler_params=pltpu.CompilerParams(
            dimension_semantics=("arbitrary",), vmem_limit_bytes=VMEM_LIMIT),
    )(x2d, gains, w_in, q_gains, k_gains, conv_w, *later_weights)


def _min_to_vreg(arrays):
    chunks = [a[r:r + SUBLANES] for a in arrays for r in range(0, a.shape[0], SUBLANES)]
    while len(chunks) > 1:
        odd = chunks[-1:] if len(chunks) % 2 else []
        chunks = [jnp.minimum(a, b) for a, b in zip(chunks[0::2], chunks[1::2])] + odd
    return chunks[0]


def _attn_kernel(q_ref, k_ref, v_ref, o_ref, acc_ref, carry_ref, min_ref):
    n_tiles = q_ref.shape[0] // SUB
    n_blocks = q_ref.shape[1] // LANES
    i = pl.program_id(1)
    assert n_tiles >= NEAR_SUBS + 1

    r = lax.broadcasted_iota(jnp.int32, (2 * SUB, 2 * SUB), 0) % SUB
    c = lax.broadcasted_iota(jnp.int32, (2 * SUB, 2 * SUB), 1)
    tri = jnp.where((c >= SUB) | (r >= c), 1.0, 0.0).astype(BF16)
    causal = (lax.broadcasted_iota(jnp.int32, (HEADS_PER_BLOCK * SUB, SUB), 1)
              < lax.broadcasted_iota(jnp.int32, (HEADS_PER_BLOCK * SUB, SUB), 0) % SUB)

    def head_rows(row0, nrows):
        return [slice(h * SUB + row0, h * SUB + row0 + nrows) for h in range(HEADS_PER_BLOCK)]

    def softplus_split(z_s, masked):
        e = jnp.exp2(jnp.abs(z_s) * (-LOG2E))
        sp = jnp.maximum(z_s, 0.0) + jnp.log(1.0 + e)
        if masked:
            sp = jnp.where(causal[:z_s.shape[0]], sp, 0.0)
        hi = sp.astype(BF16)
        lo = (sp - hi.astype(F32)).astype(BF16)
        return jnp.concatenate([hi, lo], axis=1)

    def span(jobs, whole_tiles):
        z_all, cs_all = {}, {}
        for n, (t, row0, nrows, start, nsub, diag, first) in enumerate(jobs):
            assert not diag or (row0 == 0 and nrows == SUB)
            qrows = slice(t * SUB + row0, t * SUB + row0 + nrows)
            lane = lax.broadcasted_iota(jnp.int32, (nrows, LANES), 1)
            for p in range(n_blocks):
                cols = slice(p * LANES, (p + 1) * LANES)
                q = q_ref[qrows, cols]
                q2 = jnp.concatenate([jnp.where((lane // HEAD_DIM) == h, q, jnp.zeros_like(q))
                                      for h in range(HEADS_PER_BLOCK)], axis=0)
                z = lax.dot_general(q2, k_ref[pl.ds(start, nsub * SUB), cols], (((1,), (1,)), ((), ())),
                                    preferred_element_type=F32)
                z_all[n, p] = z
                lhs_parts = [softplus_split(z[:, s * SUB:(s + 1) * SUB], diag and s == nsub - 1)
                             for s in reversed(range(nsub))]
                cs_all[n, p] = jnp.dot(jnp.concatenate(lhs_parts, axis=0), tri, preferred_element_type=F32)
        latest = {}
        for n, (t, row0, nrows, start, nsub, diag, first) in enumerate(jobs):
            qrows = slice(t * SUB + row0, t * SUB + row0 + nrows)
            lane = lax.broadcasted_iota(jnp.int32, (nrows, LANES), 1)
            rows2 = HEADS_PER_BLOCK * nrows
            for p in range(n_blocks):
                cols = slice(p * LANES, (p + 1) * LANES)
                carry = None if first else jnp.concatenate([carry_ref[t, p, rs] for rs in head_rows(row0, nrows)], axis=0)
                a_cols = [None] * nsub
                for idx, s in enumerate(reversed(range(nsub))):
                    cs = cs_all[n, p][idx * rows2:(idx + 1) * rows2]
                    g = cs[:, :SUB] if carry is None else cs[:, :SUB] + carry
                    carry = cs[:, SUB:] if carry is None else carry + cs[:, SUB:]
                    a = jnp.exp2((z_all[n, p][:, s * SUB:(s + 1) * SUB] - g) * LOG2E)
                    if diag and s == nsub - 1:
                        a = jnp.where(causal, a, 0.0)
                    a_cols[s] = a.astype(BF16)
                for h, rs in enumerate(head_rows(row0, nrows)):
                    carry_ref[t, p, rs] = carry[h * nrows:(h + 1) * nrows]
                latest.setdefault(t, {}).setdefault(p, []).append((row0, nrows, carry))
                o2 = jnp.dot(jnp.concatenate(a_cols, axis=1), v_ref[pl.ds(start, nsub * SUB), cols],
                             preferred_element_type=F32)
                out = o2[0:nrows]
                for h in range(1, HEADS_PER_BLOCK):
                    out = jnp.where((lane // HEAD_DIM) == h, o2[h * nrows:(h + 1) * nrows], out)
                if first:
                    acc_ref[qrows, cols] = out
                else:
                    acc_ref[qrows, cols] += out
        for t, by_block in latest.items():
            if whole_tiles:
                assert all(len(e) == 1 and e[0][:2] == (0, SUB) for e in by_block.values())
                min_ref[t] = _min_to_vreg([e[0][2] for e in by_block.values()])
            else:
                min_ref[t] = _min_to_vreg([carry_ref[t, p] for p in range(n_blocks)])

    def window(tiles):
        n_keys = (NEAR_SUBS + 2) * SUB
        far = [slice(h * SUB, h * SUB + FAR_ROWS) for h in range(HEADS_PER_BLOCK)]
        rest = [slice(h * SUB + FAR_ROWS, (h + 1) * SUB) for h in range(HEADS_PER_BLOCK)]
        lane = lax.broadcasted_iota(jnp.int32, (SUB, LANES), 1)
        lane_far = lax.broadcasted_iota(jnp.int32, (FAR_ROWS, LANES), 1)
        ctx, scores, suffix_sums, outputs, mixes = {}, [], [], [], []

        def score(t, tile, p):
            start = pl.multiple_of((tile - NEAR_SUBS - 1) * SUB, SUB)
            cols = slice(p * LANES, (p + 1) * LANES)
            q = q_ref[t * SUB:(t + 1) * SUB, cols]
            q2 = jnp.concatenate([jnp.where((lane // HEAD_DIM) == h, q, jnp.zeros_like(q))
                                  for h in range(HEADS_PER_BLOCK)], axis=0)
            z = lax.dot_general(q2, k_ref[pl.ds(start, n_keys), cols], (((1,), (1,)), ((), ())),
                                preferred_element_type=F32)
            ctx[t, p] = (start, z)

        def sums(t, p):
            start, z = ctx[t, p]
            z_far = jnp.concatenate([z[rs, 0:SUB] for rs in far], axis=0)
            lhs = [softplus_split(z[:, s * SUB:(s + 1) * SUB], s == NEAR_SUBS + 1)
                   for s in range(NEAR_SUBS + 1, 0, -1)] + [softplus_split(z_far, False)]
            cs = jnp.dot(jnp.concatenate(lhs, axis=0), tri, preferred_element_type=F32)
            ctx[t, p] = (start, z, z_far, cs)

        def output(t, p, parts):
            cols = slice(p * LANES, (p + 1) * LANES)
            start, z, z_far, cs = ctx[t, p]
            carry, a_cols = None, []
            for idx, s in enumerate(range(NEAR_SUBS + 1, 0, -1)):
                blk = cs[idx * 2 * SUB:(idx + 1) * 2 * SUB]
                g = blk[:, :SUB] if carry is None else blk[:, :SUB] + carry
                carry = blk[:, SUB:] if carry is None else carry + blk[:, SUB:]
                a = jnp.exp2((z[:, s * SUB:(s + 1) * SUB] - g) * LOG2E)
                if s == NEAR_SUBS + 1:
                    a = jnp.where(causal, a, 0.0)
                a_cols.insert(0, a.astype(BF16))
            blk = cs[(NEAR_SUBS + 1) * 2 * SUB:]
            carry_far = jnp.concatenate([carry[rs] for rs in far], axis=0)
            a_far = jnp.exp2((z_far - (blk[:, :SUB] + carry_far)) * LOG2E).astype(BF16)
            carry_far = carry_far + blk[:, SUB:]
            for h in range(HEADS_PER_BLOCK):
                carry_ref[t, p, far[h]] = carry_far[h * FAR_ROWS:(h + 1) * FAR_ROWS]
                carry_ref[t, p, rest[h]] = carry[rest[h]]
                parts.append(carry[rest[h]])
            parts.append(carry_far)
            if p == n_blocks - 1:
                min_ref[t] = _min_to_vreg(parts)
            ctx[t, p] = (start, a_cols, a_far)

        def mix(t, p):
            cols = slice(p * LANES, (p + 1) * LANES)
            start, a_cols, a_far = ctx[t, p]
            o2 = jnp.dot(jnp.concatenate(a_cols, axis=1), v_ref[pl.ds(start + SUB, (NEAR_SUBS + 1) * SUB), cols],
                         preferred_element_type=F32)
            o_far = jnp.dot(a_far, v_ref[pl.ds(start, SUB), cols], preferred_element_type=F32)
            out, out_far = o2[0:SUB], o_far[0:FAR_ROWS]
            for h in range(1, HEADS_PER_BLOCK):
                out = jnp.where((lane // HEAD_DIM) == h, o2[h * SUB:(h + 1) * SUB], out)
                out_far = jnp.where((lane_far // HEAD_DIM) == h,
                                    o_far[h * FAR_ROWS:(h + 1) * FAR_ROWS], out_far)
            acc_ref[t * SUB:t * SUB + FAR_ROWS, cols] = out[:FAR_ROWS] + out_far
            acc_ref[t * SUB + FAR_ROWS:(t + 1) * SUB, cols] = out[FAR_ROWS:]

        parts = {t: [] for t, _ in tiles}
        for p in range(n_blocks):
            for t, tile in tiles:
                scores.append(functools.partial(score, t, tile, p))
                suffix_sums.append(functools.partial(sums, t, p))
                outputs.append(functools.partial(output, t, p, parts[t]))
                mixes.append(functools.partial(mix, t, p))
        for n in range(len(scores) + SCORE_LEAD + OUTPUT_LAG + MIX_LAG):
            for stages, lead in ((scores, 0), (suffix_sums, SCORE_LEAD), (outputs, SCORE_LEAD + OUTPUT_LAG),
                                 (mixes, SCORE_LEAD + OUTPUT_LAG + MIX_LAG)):
                if 0 <= n - lead < len(stages):
                    stages[n - lead]()

    @pl.when(i == 0)
    def _():
        early = [t for t in range(n_tiles) if t <= NEAR_SUBS]
        span([(t, 0, SUB, 0, t + 1, True, True) for t in early], True)
        window([(t, t) for t in range(n_tiles) if t > NEAR_SUBS])

    @pl.when(i > 0)
    def _():
        window([(t, i * n_tiles + t) for t in range(n_tiles)])

    @pl.when(jnp.min(min_ref[...], axis=(0, 1))[0] < SKIP_ABOVE)
    def _():
        for t in range(n_tiles):
            tile = i * n_tiles + t
            if FAR_ROWS < SUB:
                @pl.when(jnp.logical_and(tile > NEAR_SUBS, jnp.min(min_ref[t]) < SKIP_ABOVE))
                def _(t=t, tile=tile):
                    start = pl.multiple_of((tile - NEAR_SUBS - 1) * SUB, SUB)
                    span([(t, FAR_ROWS, SUB - FAR_ROWS, start, 1, False, False)], False)

            def unfinished(t=t):
                return (jnp.min(min_ref[t]) < SKIP_ABOVE).astype(jnp.int32)

            def cond(state):
                rem, go = state
                return jnp.logical_and(rem >= WALK_SUBS * SUB, go > 0)

            def body(state, t=t, unfinished=unfinished):
                rem, _ = state
                start = pl.multiple_of(rem - WALK_SUBS * SUB, SUB)
                span([(t, 0, SUB, start, WALK_SUBS, False, False)], False)
                return start, unfinished()

            remaining = jnp.maximum(tile - NEAR_SUBS - 1, 0) * SUB
            rem, go = lax.while_loop(cond, body, (remaining, unfinished()))
            for tail in range(1, WALK_SUBS):
                @pl.when(jnp.logical_and(rem == tail * SUB, go > 0))
                def _(t=t, tail=tail):
                    span([(t, 0, SUB, 0, tail, False, False)], False)

    o_ref[...] = acc_ref[...].astype(o_ref.dtype)


def _attention(q, k, v, *, batch, seq):
    n, attn_dim = q.shape
    tq = ATTN_TILES * SUB
    nq = seq // tq
    return pl.pallas_call(
        _attn_kernel,
        grid=(batch, nq),
        in_specs=[
            pl.BlockSpec((tq, attn_dim), lambda b, i: (b * nq + i, 0)),
            pl.BlockSpec((seq, attn_dim), lambda b, i: (b, 0)),
            pl.BlockSpec((seq, attn_dim), lambda b, i: (b, 0)),
        ],
        out_specs=pl.BlockSpec((tq, attn_dim), lambda b, i: (b * nq + i, 0)),
        out_shape=jax.ShapeDtypeStruct((n, attn_dim), BF16),
        scratch_shapes=[pltpu.VMEM((tq, attn_dim), F32),
                        pltpu.VMEM((ATTN_TILES, attn_dim // LANES, HEADS_PER_BLOCK * SUB, SUB), F32),
                        pltpu.VMEM((ATTN_TILES, SUBLANES, SUB), F32)],
        compiler_params=pltpu.CompilerParams(
            dimension_semantics=("arbitrary", "arbitrary"), vmem_limit_bytes=VMEM_LIMIT),
    )(q, k, v)


def _ffn_kernel(x_ref, attn_ref, conv_ref, wo_ref, gain_ref, wg_ref, wu_ref, wd_ref, o_ref, act_ref, *, layer):
    attn_dim = attn_ref.shape[1]
    d_ff = wg_ref.shape[1]
    mix = (jnp.dot(attn_ref[...], wo_ref[0:attn_dim, :], preferred_element_type=F32)
           + jnp.dot(conv_ref[...], wo_ref[attn_dim:, :], preferred_element_type=F32))
    x1 = x_ref[...] + mix
    h = ((x1 * _rms_scale(x1)) * gain_ref[layer:layer + 1, :]).astype(BF16)
    for c0 in range(0, d_ff, FF_CHUNK):
        g = jnp.dot(h, wg_ref[:, c0:c0 + FF_CHUNK], preferred_element_type=F32)
        u = jnp.dot(h, wu_ref[:, c0:c0 + FF_CHUNK], preferred_element_type=F32)
        act_ref[:, c0:c0 + FF_CHUNK] = ((g * jax.nn.sigmoid(g)) * u).astype(BF16)
    o_ref[...] = x1 + jnp.dot(act_ref[...], wd_ref[...], preferred_element_type=F32)


def _ffn(x2d, attn, conv, w_out, gains, layer, w_gate, w_up, w_down):
    n, d = x2d.shape
    tm = FFN_ROW_TILE
    d_ff = w_gate.shape[1]
    const = lambda i: (0, 0)
    row = lambda i: (i, 0)
    resident = functools.partial(pl.BlockSpec, index_map=const, pipeline_mode=pl.Buffered(1))
    return pl.pallas_call(
        functools.partial(_ffn_kernel, layer=layer),
        grid=(n // tm,),
        in_specs=[
            pl.BlockSpec((tm, d), row),
            pl.BlockSpec((tm, attn.shape[1]), row),
            pl.BlockSpec((tm, conv.shape[1]), row),
            resident(w_out.shape),
            pl.BlockSpec(gains.shape, const),
            resident(w_gate.shape),
            resident(w_up.shape),
            resident(w_down.shape),
        ],
        out_specs=pl.BlockSpec((tm, d), row),
        out_shape=jax.ShapeDtypeStruct((n, d), F32),
        scratch_shapes=[pltpu.VMEM((tm, d_ff), BF16)],
        compiler_params=pltpu.CompilerParams(
            dimension_semantics=("arbitrary",), vmem_limit_bytes=VMEM_LIMIT),
    )(x2d, attn, conv, w_out, gains, w_gate, w_up, w_down)


def kernel(x, norm_mix, w_in, q_norm, k_norm, conv_w, w_out, norm_ffn, w_gate, w_up, w_down):
    batch, seq, d = x.shape
    depth = w_in.shape[0]
    conv_dim = conv_w.shape[2]
    attn_dim = w_out.shape[1] - conv_dim
    assert seq % ROW_TILE == 0 and seq % (ATTN_TILES * SUB) == 0 and attn_dim % MXU_DIM == 0
    assert (batch * seq) % FFN_ROW_TILE == 0
    assert w_gate.shape[2] % FF_CHUNK == 0 and conv_w.shape[1] == CONV_SHIFTS + 1

    xf = x.reshape(batch * seq, d)
    for l in range(depth):
        q, k, v, conv, wo, wg, wu, wd = _inproj(
            xf, norm_mix, w_in, l, q_norm, k_norm, conv_w,
            (w_out, w_gate, w_up, w_down), seq=seq, attn_dim=attn_dim, conv_dim=conv_dim)
        attn = _attention(q, k, v, batch=batch, seq=seq)
        xf = _ffn(xf, attn, conv, wo, norm_ffn, l, wg, wu, wd)
    return xf.reshape(batch, seq, d)
```

```python
import functools

import jax
import jax.numpy as jnp
from jax import lax
from jax.experimental import pallas as pl
from jax.experimental.pallas import tpu as pltpu

F32 = jnp.float32
BF16 = jnp.bfloat16

HEAD_DIM = 64
EPS = 1e-6
LANES = 128
SUBLANES = 8
BF16_SUBLANES = 16
W_CAST_ROWS = 128
MXU_DIM = 256
HEADS_PER_BLOCK = LANES // HEAD_DIM
SUB = 128
LOG2E = 1.4426950408889634
LN2 = 0.6931471805599453
SKIP_ABOVE = 104.0

ROW_TILE = 1024
ROW_SPLIT = 1
FFN_ROW_TILE = 1024
ATTN_TILES = 4
NEAR_SUBS = 1
FAR_ROWS = 32
SCORE_LEAD = 16
OUTPUT_LAG = 2
MIX_LAG = 0
WALK_SUBS = 2
FF_CHUNK = 256
CONV_SHIFTS = 2
VMEM_LIMIT = 56 * 1024 * 1024


def _rms_scale(x):
    return lax.rsqrt(jnp.mean(x * x, axis=-1, keepdims=True) + EPS)


def _inproj_kernel(x_ref, gain_ref, w_ref, qg_ref, kg_ref, cw_ref, *rest,
                   layer, tiles_per_seq, attn_dim, conv_dim, n_cast):
    cast_in, (q_ref, k_ref, v_ref, c_ref) = rest[:n_cast], rest[n_cast:n_cast + 4]
    cast_out, (halo_ref, wbf_ref) = rest[n_cast + 4:2 * n_cast + 4], rest[2 * n_cast + 4:]
    tm = x_ref.shape[0]
    i = pl.program_id(0)

    @pl.when(i == 0)
    def _():
        def cast_rows(c, _):
            rows = pl.ds(pl.multiple_of(c * W_CAST_ROWS, W_CAST_ROWS), W_CAST_ROWS)
            wbf_ref[rows, :] = w_ref[rows, :].astype(BF16)
            return 0
        lax.fori_loop(0, w_ref.shape[0] // W_CAST_ROWS, cast_rows, 0)
        halo_ref[...] = jnp.zeros_like(halo_ref)

    r = lax.broadcasted_iota(jnp.int32, (MXU_DIM, MXU_DIM), 0) // HEAD_DIM
    c = lax.broadcasted_iota(jnp.int32, (MXU_DIM, MXU_DIM), 1) // HEAD_DIM
    seg = jnp.where(r == c, 1.0, 0.0).astype(BF16)
    taps = [cw_ref[layer, n:n + 1, :] for n in range(CONV_SHIFTS + 1)]
    row = lax.broadcasted_iota(jnp.int32, (SUBLANES, conv_dim), 0)
    hm = tm // ROW_SPLIT

    def head_norm(p, gain_ref):
        gain = jnp.concatenate([gain_ref[layer:layer + 1, :]] * (attn_dim // HEAD_DIM), axis=1)
        p2 = (p * p).astype(BF16)
        ssq = jnp.concatenate(
            [jnp.dot(p2[:, s:s + MXU_DIM], seg, preferred_element_type=F32)
             for s in range(0, attn_dim, MXU_DIM)], axis=1)
        return (p * lax.rsqrt(ssq * (1.0 / HEAD_DIM) + EPS)) * gain

    for part in range(ROW_SPLIT):
        rows = slice(part * hm, (part + 1) * hm)
        x = x_ref[rows, :]
        h = ((x * _rms_scale(x)) * gain_ref[layer:layer + 1, :]).astype(BF16)

        def proj(c0, width, h=h):
            return jnp.dot(h, wbf_ref[:, c0:c0 + width], preferred_element_type=F32)

        cb = proj(3 * attn_dim, conv_dim)
        hh = proj(3 * attn_dim + conv_dim, conv_dim) * proj(3 * attn_dim + 2 * conv_dim, conv_dim)

        q_ref[rows, :] = (head_norm(proj(0, attn_dim), qg_ref) * (HEAD_DIM ** -0.5)).astype(BF16)
        k_ref[rows, :] = head_norm(proj(attn_dim, attn_dim), kg_ref).astype(BF16)
        v_ref[rows, :] = proj(2 * attn_dim, attn_dim).astype(BF16)

        def shift_down(a, n, part=part):
            prev_last = halo_ref[n, SUBLANES - 1:SUBLANES, :]
            if part == 0:
                prev_last = jnp.where(i % tiles_per_seq == 0, 0.0, prev_last)
            halo_ref[n] = a[hm - SUBLANES:hm, :]
            rolled = pltpu.roll(a, 1, 0)
            head = jnp.where(row == 0, jnp.broadcast_to(prev_last, (SUBLANES, conv_dim)), rolled[0:SUBLANES])
            return jnp.concatenate([head, rolled[SUBLANES:]], axis=0)

        y = taps[2] * hh + shift_down(taps[1] * hh + shift_down(taps[0] * hh, 0), 1)
        c_ref[rows, :] = (cb * y).astype(BF16)

    for src, dst in zip(cast_in, cast_out):
        dst[...] = src[...].astype(BF16)


def _cast_specs(weights, layer, n_steps, step_of):
    in_specs, out_specs, shapes = [], [], []
    for w in weights:
        _, rows, cols = w.shape
        chunks = max(c for c in range(1, n_steps + 1) if rows % c == 0 and (rows // c) % BF16_SUBLANES == 0)
        chunk_of = lambda *idx, chunks=chunks: jnp.minimum(step_of(*idx), chunks - 1)
        in_specs.append(pl.BlockSpec((None, rows // chunks, cols),
                                     lambda *idx, chunk_of=chunk_of: (layer, chunk_of(*idx), 0)))
        out_specs.append(pl.BlockSpec((rows // chunks, cols), lambda *idx, chunk_of=chunk_of: (chunk_of(*idx), 0)))
        shapes.append(jax.ShapeDtypeStruct((rows, cols), BF16))
    return in_specs, out_specs, shapes


def _inproj(x2d, gains, w_in, layer, q_gains, k_gains, conv_w, later_weights, *, seq, attn_dim, conv_dim):
    n, d = x2d.shape
    tm = ROW_TILE
    n_steps = n // tm
    cols = w_in.shape[2]
    const = lambda i: (0, 0)
    row = lambda i: (i, 0)
    cast_in_specs, cast_out_specs, cast_shapes = _cast_specs(later_weights, layer, n_steps, lambda i: i)
    out_shapes = [jax.ShapeDtypeStruct((n, attn_dim), BF16)] * 3 + [jax.ShapeDtypeStruct((n, conv_dim), BF16)]
    return pl.pallas_call(
        functools.partial(_inproj_kernel, layer=layer, tiles_per_seq=seq // tm, attn_dim=attn_dim,
                          conv_dim=conv_dim, n_cast=len(later_weights)),
        grid=(n_steps,),
        in_specs=[
            pl.BlockSpec((tm, d), row),
            pl.BlockSpec(gains.shape, const),
            pl.BlockSpec((None, d, cols), lambda i: (layer, 0, 0), pipeline_mode=pl.Buffered(1)),
            pl.BlockSpec(q_gains.shape, const),
            pl.BlockSpec(k_gains.shape, const),
            pl.BlockSpec(conv_w.shape, lambda i: (0, 0, 0)),
        ] + cast_in_specs,
        out_specs=[pl.BlockSpec((tm, attn_dim), row)] * 3 + [pl.BlockSpec((tm, conv_dim), row)] + cast_out_specs,
        out_shape=out_shapes + cast_shapes,
        scratch_shapes=[pltpu.VMEM((CONV_SHIFTS, SUBLANES, conv_dim), F32), pltpu.VMEM((d, cols), BF16)],
        compiler_params=pltpu.CompilerParams(
            dimension_semantics=("arbitrary",), vmem_limit_bytes=VMEM_LIMIT),
    )(x2d, gains, w_in, q_gains, k_gains, conv_w, *later_weights)


def _min_to_vreg(arrays):
    chunks = [a[r:r + SUBLANES] for a in arrays for r in range(0, a.shape[0], SUBLANES)]
    while len(chunks) > 1:
        odd = chunks[-1:] if len(chunks) % 2 else []
        chunks = [jnp.minimum(a, b) for a, b in zip(chunks[0::2], chunks[1::2])] + odd
    return chunks[0]


def _attn_kernel(q_ref, k_ref, v_ref, o_ref, acc_ref, carry_ref, min_ref):
    n_tiles = q_ref.shape[0] // SUB
    n_blocks = q_ref.shape[1] // LANES
    i = pl.program_id(1)
    assert n_tiles >= NEAR_SUBS + 1

    r = lax.broadcasted_iota(jnp.int32, (2 * SUB, 2 * SUB), 0) % SUB
    c = lax.broadcasted_iota(jnp.int32, (2 * SUB, 2 * SUB), 1)
    tri = jnp.where((c >= SUB) | (r >= c), 1.0, 0.0).astype(BF16)
    causal = (lax.broadcasted_iota(jnp.int32, (HEADS_PER_BLOCK * SUB, SUB), 1)
              < lax.broadcasted_iota(jnp.int32, (HEADS_PER_BLOCK * SUB, SUB), 0) % SUB)

    def head_rows(row0, nrows):
        return [slice(h * SUB + row0, h * SUB + row0 + nrows) for h in range(HEADS_PER_BLOCK)]

    def softplus_split(z_s, masked):
        e = jnp.exp2(jnp.abs(z_s) * (-LOG2E))
        sp = jnp.maximum(z_s, 0.0) + jnp.log(1.0 + e)
        if masked:
            sp = jnp.where(causal[:z_s.shape[0]], sp, 0.0)
        hi = sp.astype(BF16)
        lo = (sp - hi.astype(F32)).astype(BF16)
        return jnp.concatenate([hi, lo], axis=1)

    def span(jobs, whole_tiles):
        z_all, cs_all = {}, {}
        for n, (t, row0, nrows, start, nsub, diag, first) in enumerate(jobs):
            assert not diag or (row0 == 0 and nrows == SUB)
            qrows = slice(t * SUB + row0, t * SUB + row0 + nrows)
            lane = lax.broadcasted_iota(jnp.int32, (nrows, LANES), 1)
            for p in range(n_blocks):
                cols = slice(p * LANES, (p + 1) * LANES)
                q = q_ref[qrows, cols]
                q2 = jnp.concatenate([jnp.where((lane // HEAD_DIM) == h, q, jnp.zeros_like(q))
                                      for h in range(HEADS_PER_BLOCK)], axis=0)
                z = lax.dot_general(q2, k_ref[pl.ds(start, nsub * SUB), cols], (((1,), (1,)), ((), ())),
                                    preferred_element_type=F32)
                z_all[n, p] = z
                lhs_parts = [softplus_split(z[:, s * SUB:(s + 1) * SUB], diag and s == nsub - 1)
                             for s in reversed(range(nsub))]
                cs_all[n, p] = jnp.dot(jnp.concatenate(lhs_parts, axis=0), tri, preferred_element_type=F32)
        latest = {}
        for n, (t, row0, nrows, start, nsub, diag, first) in enumerate(jobs):
            qrows = slice(t * SUB + row0, t * SUB + row0 + nrows)
            lane = lax.broadcasted_iota(jnp.int32, (nrows, LANES), 1)
            rows2 = HEADS_PER_BLOCK * nrows
            for p in range(n_blocks):
                cols = slice(p * LANES, (p + 1) * LANES)
                carry = None if first else jnp.concatenate([carry_ref[t, p, rs] for rs in head_rows(row0, nrows)], axis=0)
                a_cols = [None] * nsub
                for idx, s in enumerate(reversed(range(nsub))):
                    cs = cs_all[n, p][idx * rows2:(idx + 1) * rows2]
                    g = cs[:, :SUB] if carry is None else cs[:, :SUB] + carry
                    carry = cs[:, SUB:] if carry is None else carry + cs[:, SUB:]
                    a = jnp.exp2((z_all[n, p][:, s * SUB:(s + 1) * SUB] - g) * LOG2E)
                    if diag and s == nsub - 1:
                        a = jnp.where(causal, a, 0.0)
                    a_cols[s] = a.astype(BF16)
                for h, rs in enumerate(head_rows(row0, nrows)):
                    carry_ref[t, p, rs] = carry[h * nrows:(h + 1) * nrows]
                latest.setdefault(t, {}).setdefault(p, []).append((row0, nrows, carry))
                o2 = jnp.dot(jnp.concatenate(a_cols, axis=1), v_ref[pl.ds(start, nsub * SUB), cols],
                             preferred_element_type=F32)
                out = o2[0:nrows]
                for h in range(1, HEADS_PER_BLOCK):
                    out = jnp.where((lane // HEAD_DIM) == h, o2[h * nrows:(h + 1) * nrows], out)
                if first:
                    acc_ref[qrows, cols] = out
                else:
                    acc_ref[qrows, cols] += out
        for t, by_block in latest.items():
            if whole_tiles:
                assert all(len(e) == 1 and e[0][:2] == (0, SUB) for e in by_block.values())
                min_ref[t] = _min_to_vreg([e[0][2] for e in by_block.values()])
            else:
                min_ref[t] = _min_to_vreg([carry_ref[t, p] for p in range(n_blocks)])

    def window(tiles):
        n_keys = (NEAR_SUBS + 2) * SUB
        far = [slice(h * SUB, h * SUB + FAR_ROWS) for h in range(HEADS_PER_BLOCK)]
        rest = [slice(h * SUB + FAR_ROWS, (h + 1) * SUB) for h in range(HEADS_PER_BLOCK)]
        lane = lax.broadcasted_iota(jnp.int32, (SUB, LANES), 1)
        lane_far = lax.broadcasted_iota(jnp.int32, (FAR_ROWS, LANES), 1)
        ctx, scores, suffix_sums, outputs, mixes = {}, [], [], [], []

        def score(t, tile, p):
            start = pl.multiple_of((tile - NEAR_SUBS - 1) * SUB, SUB)
            cols = slice(p * LANES, (p + 1) * LANES)
            q = q_ref[t * SUB:(t + 1) * SUB, cols]
            q2 = jnp.concatenate([jnp.where((lane // HEAD_DIM) == h, q, jnp.zeros_like(q))
                                  for h in range(HEADS_PER_BLOCK)], axis=0)
            z = lax.dot_general(q2, k_ref[pl.ds(start, n_keys), cols], (((1,), (1,)), ((), ())),
                                preferred_element_type=F32)
            ctx[t, p] = (start, z)

        def sums(t, p):
            start, z = ctx[t, p]
            z_far = jnp.concatenate([z[rs, 0:SUB] for rs in far], axis=0)
            lhs = [softplus_split(z[:, s * SUB:(s + 1) * SUB], s == NEAR_SUBS + 1)
                   for s in range(NEAR_SUBS + 1, 0, -1)] + [softplus_split(z_far, False)]
            cs = jnp.dot(jnp.concatenate(lhs, axis=0), tri, preferred_element_type=F32)
            ctx[t, p] = (start, z, z_far, cs)

        def output(t, p, parts):
            cols = slice(p * LANES, (p + 1) * LANES)
            start, z, z_far, cs = ctx[t, p]
            carry, a_cols = None, []
            for idx, s in enumerate(range(NEAR_SUBS + 1, 0, -1)):
                blk = cs[idx * 2 * SUB:(idx + 1) * 2 * SUB]
                g = blk[:, :SUB] if carry is None else blk[:, :SUB] + carry
                carry = blk[:, SUB:] if carry is None else carry + blk[:, SUB:]
                a = jnp.exp2((z[:, s * SUB:(s + 1) * SUB] - g) * LOG2E)
                if s == NEAR_SUBS + 1:
                    a = jnp.where(causal, a, 0.0)
                a_cols.insert(0, a.astype(BF16))
            blk = cs[(NEAR_SUBS + 1) * 2 * SUB:]
            carry_far = jnp.concatenate([carry[rs] for rs in far], axis=0)
            a_far = jnp.exp2((z_far - (blk[:, :SUB] + carry_far)) * LOG2E).astype(BF16)
            carry_far = carry_far + blk[:, SUB:]
            for h in range(HEADS_PER_BLOCK):
                carry_ref[t, p, far[h]] = carry_far[h * FAR_ROWS:(h + 1) * FAR_ROWS]
                carry_ref[t, p, rest[h]] = carry[rest[h]]
                parts.append(carry[rest[h]])
            parts.append(carry_far)
            if p == n_blocks - 1:
                min_ref[t] = _min_to_vreg(parts)
            ctx[t, p] = (start, a_cols, a_far)

        def mix(t, p):
            cols = slice(p * LANES, (p + 1) * LANES)
            start, a_cols, a_far = ctx[t, p]
            o2 = jnp.dot(jnp.concatenate(a_cols, axis=1), v_ref[pl.ds(start + SUB, (NEAR_SUBS + 1) * SUB), cols],
                         preferred_element_type=F32)
            o_far = jnp.dot(a_far, v_ref[pl.ds(start, SUB), cols], preferred_element_type=F32)
            out, out_far = o2[0:SUB], o_far[0:FAR_ROWS]
            for h in range(1, HEADS_PER_BLOCK):
                out = jnp.where((lane // HEAD_DIM) == h, o2[h * SUB:(h + 1) * SUB], out)
                out_far = jnp.where((lane_far // HEAD_DIM) == h,
                                    o_far[h * FAR_ROWS:(h + 1) * FAR_ROWS], out_far)
            out = jnp.concatenate([out[:FAR_ROWS] + out_far, out[FAR_ROWS:]], axis=0)
            acc_ref[t * SUB:(t + 1) * SUB, cols] = out
            o_ref[t * SUB:(t + 1) * SUB, cols] = out.astype(o_ref.dtype)

        parts = {t: [] for t, _ in tiles}
        for p in range(n_blocks):
            for t, tile in tiles:
                scores.append(functools.partial(score, t, tile, p))
                suffix_sums.append(functools.partial(sums, t, p))
                outputs.append(functools.partial(output, t, p, parts[t]))
                mixes.append(functools.partial(mix, t, p))
        for n in range(len(scores) + SCORE_LEAD + OUTPUT_LAG + MIX_LAG):
            for stages, lead in ((scores, 0), (suffix_sums, SCORE_LEAD), (outputs, SCORE_LEAD + OUTPUT_LAG),
                                 (mixes, SCORE_LEAD + OUTPUT_LAG + MIX_LAG)):
                if 0 <= n - lead < len(stages):
                    stages[n - lead]()

    @pl.when(i == 0)
    def _():
        early = [t for t in range(n_tiles) if t <= NEAR_SUBS]
        span([(t, 0, SUB, 0, t + 1, True, True) for t in early], True)
        window([(t, t) for t in range(n_tiles) if t > NEAR_SUBS])
        for t in early:
            o_ref[t * SUB:(t + 1) * SUB, :] = acc_ref[t * SUB:(t + 1) * SUB, :].astype(o_ref.dtype)

    @pl.when(i > 0)
    def _():
        window([(t, i * n_tiles + t) for t in range(n_tiles)])

    @pl.when(jnp.min(min_ref[...], axis=(0, 1))[0] < SKIP_ABOVE)
    def _():
        for t in range(n_tiles):
            tile = i * n_tiles + t
            if FAR_ROWS < SUB:
                @pl.when(jnp.logical_and(tile > NEAR_SUBS, jnp.min(min_ref[t]) < SKIP_ABOVE))
                def _(t=t, tile=tile):
                    start = pl.multiple_of((tile - NEAR_SUBS - 1) * SUB, SUB)
                    span([(t, FAR_ROWS, SUB - FAR_ROWS, start, 1, False, False)], False)

            def unfinished(t=t):
                return (jnp.min(min_ref[t]) < SKIP_ABOVE).astype(jnp.int32)

            def cond(state):
                rem, go = state
                return jnp.logical_and(rem >= WALK_SUBS * SUB, go > 0)

            def body(state, t=t, unfinished=unfinished):
                rem, _ = state
                start = pl.multiple_of(rem - WALK_SUBS * SUB, SUB)
                span([(t, 0, SUB, start, WALK_SUBS, False, False)], False)
                return start, unfinished()

            remaining = jnp.maximum(tile - NEAR_SUBS - 1, 0) * SUB
            rem, go = lax.while_loop(cond, body, (remaining, unfinished()))
            for tail in range(1, WALK_SUBS):
                @pl.when(jnp.logical_and(rem == tail * SUB, go > 0))
                def _(t=t, tail=tail):
                    span([(t, 0, SUB, 0, tail, False, False)], False)
        o_ref[...] = acc_ref[...].astype(o_ref.dtype)


def _attention(q, k, v, *, batch, seq):
    n, attn_dim = q.shape
    tq = ATTN_TILES * SUB
    nq = seq // tq
    return pl.pallas_call(
        _attn_kernel,
        grid=(batch, nq),
        in_specs=[
            pl.BlockSpec((tq, attn_dim), lambda b, i: (b * nq + i, 0)),
            pl.BlockSpec((seq, attn_dim), lambda b, i: (b, 0)),
            pl.BlockSpec((seq, attn_dim), lambda b, i: (b, 0)),
        ],
        out_specs=pl.BlockSpec((tq, attn_dim), lambda b, i: (b * nq + i, 0)),
        out_shape=jax.ShapeDtypeStruct((n, attn_dim), BF16),
        scratch_shapes=[pltpu.VMEM((tq, attn_dim), F32),
                        pltpu.VMEM((ATTN_TILES, attn_dim // LANES, HEADS_PER_BLOCK * SUB, SUB), F32),
                        pltpu.VMEM((ATTN_TILES, SUBLANES, SUB), F32)],
        compiler_params=pltpu.CompilerParams(
            dimension_semantics=("arbitrary", "arbitrary"), vmem_limit_bytes=VMEM_LIMIT),
    )(q, k, v)


def _ffn_kernel(x_ref, attn_ref, conv_ref, wo_ref, gain_ref, wg_ref, wu_ref, wd_ref, o_ref, act_ref, *, layer):
    attn_dim = attn_ref.shape[1]
    d_ff = wg_ref.shape[1]
    mix = (jnp.dot(attn_ref[...], wo_ref[0:attn_dim, :], preferred_element_type=F32)
           + jnp.dot(conv_ref[...], wo_ref[attn_dim:, :], preferred_element_type=F32))
    x1 = x_ref[...] + mix
    h = ((x1 * _rms_scale(x1)) * gain_ref[layer:layer + 1, :]).astype(BF16)
    for c0 in range(0, d_ff, FF_CHUNK):
        g = jnp.dot(h, wg_ref[:, c0:c0 + FF_CHUNK], preferred_element_type=F32)
        u = jnp.dot(h, wu_ref[:, c0:c0 + FF_CHUNK], preferred_element_type=F32)
        act_ref[:, c0:c0 + FF_CHUNK] = ((g * jax.nn.sigmoid(g)) * u).astype(BF16)
    o_ref[...] = x1 + jnp.dot(act_ref[...], wd_ref[...], preferred_element_type=F32)


def _ffn(x2d, attn, conv, w_out, gains, layer, w_gate, w_up, w_down):
    n, d = x2d.shape
    tm = FFN_ROW_TILE
    d_ff = w_gate.shape[1]
    const = lambda i: (0, 0)
    row = lambda i: (i, 0)
    resident = functools.partial(pl.BlockSpec, index_map=const, pipeline_mode=pl.Buffered(1))
    return pl.pallas_call(
        functools.partial(_ffn_kernel, layer=layer),
        grid=(n // tm,),
        in_specs=[
            pl.BlockSpec((tm, d), row),
            pl.BlockSpec((tm, attn.shape[1]), row),
            pl.BlockSpec((tm, conv.shape[1]), row),
            resident(w_out.shape),
            pl.BlockSpec(gains.shape, const),
            resident(w_gate.shape),
            resident(w_up.shape),
            resident(w_down.shape),
        ],
        out_specs=pl.BlockSpec((tm, d), row),
        out_shape=jax.ShapeDtypeStruct((n, d), F32),
        scratch_shapes=[pltpu.VMEM((tm, d_ff), BF16)],
        compiler_params=pltpu.CompilerParams(
            dimension_semantics=("arbitrary",), vmem_limit_bytes=VMEM_LIMIT),
    )(x2d, attn, conv, w_out, gains, w_gate, w_up, w_down)


def kernel(x, norm_mix, w_in, q_norm, k_norm, conv_w, w_out, norm_ffn, w_gate, w_up, w_down):
    batch, seq, d = x.shape
    depth = w_in.shape[0]
    conv_dim = conv_w.shape[2]
    attn_dim = w_out.shape[1] - conv_dim
    assert seq % ROW_TILE == 0 and seq % (ATTN_TILES * SUB) == 0 and attn_dim % MXU_DIM == 0
    assert (batch * seq) % FFN_ROW_TILE == 0
    assert w_gate.shape[2] % FF_CHUNK == 0 and conv_w.shape[1] == CONV_SHIFTS + 1

    xf = x.reshape(batch * seq, d)
    for l in range(depth):
        q, k, v, conv, wo, wg, wu, wd = _inproj(
            xf, norm_mix, w_in, l, q_norm, k_norm, conv_w,
            (w_out, w_gate, w_up, w_down), seq=seq, attn_dim=attn_dim, conv_dim=conv_dim)
        attn = _attention(q, k, v, batch=batch, seq=seq)
        xf = _ffn(xf, attn, conv, wo, norm_ffn, l, wg, wu, wd)
    return xf.reshape(batch, seq, d)
```

```python
import functools

import jax
import jax.numpy as jnp
from jax import lax
from jax.experimental import pallas as pl
from jax.experimental.pallas import tpu as pltpu

F32 = jnp.float32
BF16 = jnp.bfloat16

HEAD_DIM = 64
EPS = 1e-6
LANES = 128
SUBLANES = 8
BF16_SUBLANES = 16
W_CAST_ROWS = 128
MXU_DIM = 256
HEADS_PER_BLOCK = LANES // HEAD_DIM
SUB = 128
LOG2E = 1.4426950408889634
SKIP_ABOVE = 104.0

ROW_TILE = 1024
FFN_ROW_TILE = 1024
ATTN_TILES = 4
NEAR_SUBS = 1
FAR_ROWS = 32
OUTPUT_LAG = 2
WALK_SUBS = 2
FF_CHUNK = 256
CONV_SHIFTS = 2
VMEM_LIMIT = 56 * 1024 * 1024


def _rms_scale(x):
    return lax.rsqrt(jnp.mean(x * x, axis=-1, keepdims=True) + EPS)


def _inproj_kernel(x_ref, gain_ref, w_ref, qg_ref, kg_ref, cw_ref, *rest,
                   layer, tiles_per_seq, attn_dim, conv_dim, n_cast):
    cast_in, (q_ref, k_ref, v_ref, c_ref) = rest[:n_cast], rest[n_cast:n_cast + 4]
    cast_out, (halo_ref, wbf_ref) = rest[n_cast + 4:2 * n_cast + 4], rest[2 * n_cast + 4:]
    tm = x_ref.shape[0]
    i = pl.program_id(0)

    @pl.when(i == 0)
    def _():
        def cast_rows(c, _):
            rows = pl.ds(pl.multiple_of(c * W_CAST_ROWS, W_CAST_ROWS), W_CAST_ROWS)
            wbf_ref[rows, :] = w_ref[rows, :].astype(BF16)
            return 0
        lax.fori_loop(0, w_ref.shape[0] // W_CAST_ROWS, cast_rows, 0)
        halo_ref[...] = jnp.zeros_like(halo_ref)

    r = lax.broadcasted_iota(jnp.int32, (MXU_DIM, MXU_DIM), 0) // HEAD_DIM
    c = lax.broadcasted_iota(jnp.int32, (MXU_DIM, MXU_DIM), 1) // HEAD_DIM
    seg = jnp.where(r == c, 1.0, 0.0).astype(BF16)
    taps = [cw_ref[layer, n:n + 1, :] for n in range(CONV_SHIFTS + 1)]
    row = lax.broadcasted_iota(jnp.int32, (SUBLANES, conv_dim), 0)

    def head_norm(p, gain_ref):
        gain = jnp.concatenate([gain_ref[layer:layer + 1, :]] * (attn_dim // HEAD_DIM), axis=1)
        p2 = (p * p).astype(BF16)
        ssq = jnp.concatenate(
            [jnp.dot(p2[:, s:s + MXU_DIM], seg, preferred_element_type=F32)
             for s in range(0, attn_dim, MXU_DIM)], axis=1)
        return (p * lax.rsqrt(ssq * (1.0 / HEAD_DIM) + EPS)) * gain

    x = x_ref[...]
    h = ((x * _rms_scale(x)) * gain_ref[layer:layer + 1, :]).astype(BF16)

    def proj(c0, width):
        return jnp.dot(h, wbf_ref[:, c0:c0 + width], preferred_element_type=F32)

    cb = proj(3 * attn_dim, conv_dim)
    hh = proj(3 * attn_dim + conv_dim, conv_dim) * proj(3 * attn_dim + 2 * conv_dim, conv_dim)

    q_ref[...] = (head_norm(proj(0, attn_dim), qg_ref) * (HEAD_DIM ** -0.5)).astype(BF16)
    k_ref[...] = head_norm(proj(attn_dim, attn_dim), kg_ref).astype(BF16)
    v_ref[...] = proj(2 * attn_dim, attn_dim).astype(BF16)

    def shift_down(a, n):
        prev_last = jnp.where(i % tiles_per_seq == 0, 0.0, halo_ref[n, SUBLANES - 1:SUBLANES, :])
        halo_ref[n] = a[tm - SUBLANES:tm, :]
        rolled = pltpu.roll(a, 1, 0)
        head = jnp.where(row == 0, jnp.broadcast_to(prev_last, (SUBLANES, conv_dim)), rolled[0:SUBLANES])
        return jnp.concatenate([head, rolled[SUBLANES:]], axis=0)

    y = taps[2] * hh + shift_down(taps[1] * hh + shift_down(taps[0] * hh, 0), 1)
    c_ref[...] = (cb * y).astype(BF16)

    for src, dst in zip(cast_in, cast_out):
        dst[...] = src[...].astype(BF16)


def _cast_specs(weights, layer, n_steps, step_of):
    in_specs, out_specs, shapes = [], [], []
    for w in weights:
        _, rows, cols = w.shape
        chunks = max(c for c in range(1, n_steps + 1) if rows % c == 0 and (rows // c) % BF16_SUBLANES == 0)
        chunk_of = lambda *idx, chunks=chunks: jnp.minimum(step_of(*idx), chunks - 1)
        in_specs.append(pl.BlockSpec((None, rows // chunks, cols),
                                     lambda *idx, chunk_of=chunk_of: (layer, chunk_of(*idx), 0)))
        out_specs.append(pl.BlockSpec((rows // chunks, cols), lambda *idx, chunk_of=chunk_of: (chunk_of(*idx), 0)))
        shapes.append(jax.ShapeDtypeStruct((rows, cols), BF16))
    return in_specs, out_specs, shapes


def _inproj(x2d, gains, w_in, layer, q_gains, k_gains, conv_w, later_weights, *, seq, attn_dim, conv_dim):
    n, d = x2d.shape
    tm = ROW_TILE
    n_steps = n // tm
    cols = w_in.shape[2]
    const = lambda i: (0, 0)
    row = lambda i: (i, 0)
    cast_in_specs, cast_out_specs, cast_shapes = _cast_specs(later_weights, layer, n_steps, lambda i: i)
    out_shapes = [jax.ShapeDtypeStruct((n, attn_dim), BF16)] * 3 + [jax.ShapeDtypeStruct((n, conv_dim), BF16)]
    return pl.pallas_call(
        functools.partial(_inproj_kernel, layer=layer, tiles_per_seq=seq // tm, attn_dim=attn_dim,
                          conv_dim=conv_dim, n_cast=len(later_weights)),
        grid=(n_steps,),
        in_specs=[
            pl.BlockSpec((tm, d), row),
            pl.BlockSpec(gains.shape, const),
            pl.BlockSpec((None, d, cols), lambda i: (layer, 0, 0), pipeline_mode=pl.Buffered(1)),
            pl.BlockSpec(q_gains.shape, const),
            pl.BlockSpec(k_gains.shape, const),
            pl.BlockSpec(conv_w.shape, lambda i: (0, 0, 0)),
        ] + cast_in_specs,
        out_specs=[pl.BlockSpec((tm, attn_dim), row)] * 3 + [pl.BlockSpec((tm, conv_dim), row)] + cast_out_specs,
        out_shape=out_shapes + cast_shapes,
        scratch_shapes=[pltpu.VMEM((CONV_SHIFTS, SUBLANES, conv_dim), F32), pltpu.VMEM((d, cols), BF16)],
        compiler_params=pltpu.CompilerParams(
            dimension_semantics=("arbitrary",), vmem_limit_bytes=VMEM_LIMIT),
    )(x2d, gains, w_in, q_gains, k_gains, conv_w, *later_weights)


def _min_to_vreg(arrays):
    chunks = [a[r:r + SUBLANES] for a in arrays for r in range(0, a.shape[0], SUBLANES)]
    while len(chunks) > 1:
        odd = chunks[-1:] if len(chunks) % 2 else []
        chunks = [jnp.minimum(a, b) for a, b in zip(chunks[0::2], chunks[1::2])] + odd
    return chunks[0]


def _attn_kernel(q_ref, k_ref, v_ref, o_ref, acc_ref, carry_ref, min_ref):
    n_tiles = q_ref.shape[0] // SUB
    n_blocks = q_ref.shape[1] // LANES
    i = pl.program_id(1)
    assert n_tiles >= NEAR_SUBS + 1

    r = lax.broadcasted_iota(jnp.int32, (2 * SUB, 2 * SUB), 0) % SUB
    c = lax.broadcasted_iota(jnp.int32, (2 * SUB, 2 * SUB), 1)
    tri = jnp.where((c >= SUB) | (r >= c), 1.0, 0.0).astype(BF16)
    causal = (lax.broadcasted_iota(jnp.int32, (HEADS_PER_BLOCK * SUB, SUB), 1)
              < lax.broadcasted_iota(jnp.int32, (HEADS_PER_BLOCK * SUB, SUB), 0) % SUB)

    def head_rows(row0, nrows):
        return [slice(h * SUB + row0, h * SUB + row0 + nrows) for h in range(HEADS_PER_BLOCK)]

    def softplus_split(z_s, masked):
        e = jnp.exp2(jnp.abs(z_s) * (-LOG2E))
        sp = jnp.maximum(z_s, 0.0) + jnp.log(1.0 + e)
        if masked:
            sp = jnp.where(causal[:z_s.shape[0]], sp, 0.0)
        hi = sp.astype(BF16)
        lo = (sp - hi.astype(F32)).astype(BF16)
        return jnp.concatenate([hi, lo], axis=1)

    def span(jobs, whole_tiles):
        z_all, cs_all = {}, {}
        for n, (t, row0, nrows, start, nsub, diag, first) in enumerate(jobs):
            assert not diag or (row0 == 0 and nrows == SUB)
            qrows = slice(t * SUB + row0, t * SUB + row0 + nrows)
            lane = lax.broadcasted_iota(jnp.int32, (nrows, LANES), 1)
            for p in range(n_blocks):
                cols = slice(p * LANES, (p + 1) * LANES)
                q = q_ref[qrows, cols]
                q2 = jnp.concatenate([jnp.where((lane // HEAD_DIM) == h, q, jnp.zeros_like(q))
                                      for h in range(HEADS_PER_BLOCK)], axis=0)
                z = lax.dot_general(q2, k_ref[pl.ds(start, nsub * SUB), cols], (((1,), (1,)), ((), ())),
                                    preferred_element_type=F32)
                z_all[n, p] = z
                lhs_parts = [softplus_split(z[:, s * SUB:(s + 1) * SUB], diag and s == nsub - 1)
                             for s in reversed(range(nsub))]
                cs_all[n, p] = jnp.dot(jnp.concatenate(lhs_parts, axis=0), tri, preferred_element_type=F32)
        latest = {}
        for n, (t, row0, nrows, start, nsub, diag, first) in enumerate(jobs):
            qrows = slice(t * SUB + row0, t * SUB + row0 + nrows)
            lane = lax.broadcasted_iota(jnp.int32, (nrows, LANES), 1)
            rows2 = HEADS_PER_BLOCK * nrows
            for p in range(n_blocks):
                cols = slice(p * LANES, (p + 1) * LANES)
                carry = None if first else jnp.concatenate([carry_ref[t, p, rs] for rs in head_rows(row0, nrows)], axis=0)
                a_cols = [None] * nsub
                for idx, s in enumerate(reversed(range(nsub))):
                    cs = cs_all[n, p][idx * rows2:(idx + 1) * rows2]
                    g = cs[:, :SUB] if carry is None else cs[:, :SUB] + carry
                    carry = cs[:, SUB:] if carry is None else carry + cs[:, SUB:]
                    a = jnp.exp2((z_all[n, p][:, s * SUB:(s + 1) * SUB] - g) * LOG2E)
                    if diag and s == nsub - 1:
                        a = jnp.where(causal, a, 0.0)
                    a_cols[s] = a.astype(BF16)
                for h, rs in enumerate(head_rows(row0, nrows)):
                    carry_ref[t, p, rs] = carry[h * nrows:(h + 1) * nrows]
                latest.setdefault(t, {}).setdefault(p, []).append((row0, nrows, carry))
                o2 = jnp.dot(jnp.concatenate(a_cols, axis=1), v_ref[pl.ds(start, nsub * SUB), cols],
                             preferred_element_type=F32)
                out = o2[0:nrows]
                for h in range(1, HEADS_PER_BLOCK):
                    out = jnp.where((lane // HEAD_DIM) == h, o2[h * nrows:(h + 1) * nrows], out)
                if first:
                    acc_ref[qrows, cols] = out
                else:
                    acc_ref[qrows, cols] += out
        for t, by_block in latest.items():
            if whole_tiles:
                assert all(len(e) == 1 and e[0][:2] == (0, SUB) for e in by_block.values())
                min_ref[t] = _min_to_vreg([e[0][2] for e in by_block.values()])
            else:
                min_ref[t] = _min_to_vreg([carry_ref[t, p] for p in range(n_blocks)])

    def window(tiles):
        n_keys = (NEAR_SUBS + 2) * SUB
        far = [slice(h * SUB, h * SUB + FAR_ROWS) for h in range(HEADS_PER_BLOCK)]
        rest = [slice(h * SUB + FAR_ROWS, (h + 1) * SUB) for h in range(HEADS_PER_BLOCK)]
        lane = lax.broadcasted_iota(jnp.int32, (SUB, LANES), 1)
        lane_far = lax.broadcasted_iota(jnp.int32, (FAR_ROWS, LANES), 1)
        ctx, scores, suffix_sums, outputs = {}, [], [], []

        def score(t, tile, p):
            start = pl.multiple_of((tile - NEAR_SUBS - 1) * SUB, SUB)
            cols = slice(p * LANES, (p + 1) * LANES)
            q = q_ref[t * SUB:(t + 1) * SUB, cols]
            q2 = jnp.concatenate([jnp.where((lane // HEAD_DIM) == h, q, jnp.zeros_like(q))
                                  for h in range(HEADS_PER_BLOCK)], axis=0)
            z = lax.dot_general(q2, k_ref[pl.ds(start, n_keys), cols], (((1,), (1,)), ((), ())),
                                preferred_element_type=F32)
            ctx[t, p] = (start, z)

        def sums(t, p):
            start, z = ctx[t, p]
            z_far = jnp.concatenate([z[rs, 0:SUB] for rs in far], axis=0)
            lhs = [softplus_split(z[:, s * SUB:(s + 1) * SUB], s == NEAR_SUBS + 1)
                   for s in range(NEAR_SUBS + 1, 0, -1)] + [softplus_split(z_far, False)]
            cs = jnp.dot(jnp.concatenate(lhs, axis=0), tri, preferred_element_type=F32)
            ctx[t, p] = (start, z, z_far, cs)

        def output(t, p, parts):
            cols = slice(p * LANES, (p + 1) * LANES)
            start, z, z_far, cs = ctx[t, p]
            carry, a_cols = None, []
            for idx, s in enumerate(range(NEAR_SUBS + 1, 0, -1)):
                blk = cs[idx * 2 * SUB:(idx + 1) * 2 * SUB]
                g = blk[:, :SUB] if carry is None else blk[:, :SUB] + carry
                carry = blk[:, SUB:] if carry is None else carry + blk[:, SUB:]
                a = jnp.exp2((z[:, s * SUB:(s + 1) * SUB] - g) * LOG2E)
                if s == NEAR_SUBS + 1:
                    a = jnp.where(causal, a, 0.0)
                a_cols.insert(0, a.astype(BF16))
            blk = cs[(NEAR_SUBS + 1) * 2 * SUB:]
            carry_far = jnp.concatenate([carry[rs] for rs in far], axis=0)
            a_far = jnp.exp2((z_far - (blk[:, :SUB] + carry_far)) * LOG2E).astype(BF16)
            carry_far = carry_far + blk[:, SUB:]
            for h in range(HEADS_PER_BLOCK):
                carry_ref[t, p, far[h]] = carry_far[h * FAR_ROWS:(h + 1) * FAR_ROWS]
                carry_ref[t, p, rest[h]] = carry[rest[h]]
                parts.append(carry[rest[h]])
            parts.append(carry_far)
            if p == n_blocks - 1:
                min_ref[t] = _min_to_vreg(parts)
            o2 = jnp.dot(jnp.concatenate(a_cols, axis=1), v_ref[pl.ds(start + SUB, (NEAR_SUBS + 1) * SUB), cols],
                         preferred_element_type=F32)
            o_far = jnp.dot(a_far, v_ref[pl.ds(start, SUB), cols], preferred_element_type=F32)
            out, out_far = o2[0:SUB], o_far[0:FAR_ROWS]
            for h in range(1, HEADS_PER_BLOCK):
                out = jnp.where((lane // HEAD_DIM) == h, o2[h * SUB:(h + 1) * SUB], out)
                out_far = jnp.where((lane_far // HEAD_DIM) == h,
                                    o_far[h * FAR_ROWS:(h + 1) * FAR_ROWS], out_far)
            out = jnp.concatenate([out[:FAR_ROWS] + out_far, out[FAR_ROWS:]], axis=0)
            acc_ref[t * SUB:(t + 1) * SUB, cols] = out
            o_ref[t * SUB:(t + 1) * SUB, cols] = out.astype(o_ref.dtype)

        parts = {t: [] for t, _ in tiles}
        for p in range(n_blocks):
            for t, tile in tiles:
                scores.append(functools.partial(score, t, tile, p))
                suffix_sums.append(functools.partial(sums, t, p))
                outputs.append(functools.partial(output, t, p, parts[t]))
        for stage in scores:
            stage()
        for n in range(len(suffix_sums) + OUTPUT_LAG):
            if n < len(suffix_sums):
                suffix_sums[n]()
            if n >= OUTPUT_LAG:
                outputs[n - OUTPUT_LAG]()

    @pl.when(i == 0)
    def _():
        early = [t for t in range(n_tiles) if t <= NEAR_SUBS]
        span([(t, 0, SUB, 0, t + 1, True, True) for t in early], True)
        window([(t, t) for t in range(n_tiles) if t > NEAR_SUBS])
        for t in early:
            o_ref[t * SUB:(t + 1) * SUB, :] = acc_ref[t * SUB:(t + 1) * SUB, :].astype(o_ref.dtype)

    @pl.when(i > 0)
    def _():
        window([(t, i * n_tiles + t) for t in range(n_tiles)])

    @pl.when(jnp.min(min_ref[...], axis=(0, 1))[0] < SKIP_ABOVE)
    def _():
        for t in range(n_tiles):
            tile = i * n_tiles + t
            if FAR_ROWS < SUB:
                @pl.when(jnp.logical_and(tile > NEAR_SUBS, jnp.min(min_ref[t]) < SKIP_ABOVE))
                def _(t=t, tile=tile):
                    start = pl.multiple_of((tile - NEAR_SUBS - 1) * SUB, SUB)
                    span([(t, FAR_ROWS, SUB - FAR_ROWS, start, 1, False, False)], False)

            def unfinished(t=t):
                return (jnp.min(min_ref[t]) < SKIP_ABOVE).astype(jnp.int32)

            def cond(state):
                rem, go = state
                return jnp.logical_and(rem >= WALK_SUBS * SUB, go > 0)

            def body(state, t=t, unfinished=unfinished):
                rem, _ = state
                start = pl.multiple_of(rem - WALK_SUBS * SUB, SUB)
                span([(t, 0, SUB, start, WALK_SUBS, False, False)], False)
                return start, unfinished()

            remaining = jnp.maximum(tile - NEAR_SUBS - 1, 0) * SUB
            rem, go = lax.while_loop(cond, body, (remaining, unfinished()))
            for tail in range(1, WALK_SUBS):
                @pl.when(jnp.logical_and(rem == tail * SUB, go > 0))
                def _(t=t, tail=tail):
                    span([(t, 0, SUB, 0, tail, False, False)], False)
        o_ref[...] = acc_ref[...].astype(o_ref.dtype)


def _attention(q, k, v, *, batch, seq):
    n, attn_dim = q.shape
    tq = ATTN_TILES * SUB
    nq = seq // tq
    return pl.pallas_call(
        _attn_kernel,
        grid=(batch, nq),
        in_specs=[
            pl.BlockSpec((tq, attn_dim), lambda b, i: (b * nq + i, 0)),
            pl.BlockSpec((seq, attn_dim), lambda b, i: (b, 0)),
            pl.BlockSpec((seq, attn_dim), lambda b, i: (b, 0)),
        ],
        out_specs=pl.BlockSpec((tq, attn_dim), lambda b, i: (b * nq + i, 0)),
        out_shape=jax.ShapeDtypeStruct((n, attn_dim), BF16),
        scratch_shapes=[pltpu.VMEM((tq, attn_dim), F32),
                        pltpu.VMEM((ATTN_TILES, attn_dim // LANES, HEADS_PER_BLOCK * SUB, SUB), F32),
                        pltpu.VMEM((ATTN_TILES, SUBLANES, SUB), F32)],
        compiler_params=pltpu.CompilerParams(
            dimension_semantics=("arbitrary", "arbitrary"), vmem_limit_bytes=VMEM_LIMIT),
    )(q, k, v)


def _ffn_kernel(x_ref, attn_ref, conv_ref, wo_ref, gain_ref, wg_ref, wu_ref, wd_ref, o_ref, act_ref, *, layer):
    attn_dim = attn_ref.shape[1]
    d_ff = wg_ref.shape[1]
    mix = (jnp.dot(attn_ref[...], wo_ref[0:attn_dim, :], preferred_element_type=F32)
           + jnp.dot(conv_ref[...], wo_ref[attn_dim:, :], preferred_element_type=F32))
    x1 = x_ref[...] + mix
    h = ((x1 * _rms_scale(x1)) * gain_ref[layer:layer + 1, :]).astype(BF16)
    for c0 in range(0, d_ff, FF_CHUNK):
        g = jnp.dot(h, wg_ref[:, c0:c0 + FF_CHUNK], preferred_element_type=F32)
        u = jnp.dot(h, wu_ref[:, c0:c0 + FF_CHUNK], preferred_element_type=F32)
        act_ref[:, c0:c0 + FF_CHUNK] = ((g * jax.nn.sigmoid(g)) * u).astype(BF16)
    o_ref[...] = x1 + jnp.dot(act_ref[...], wd_ref[...], preferred_element_type=F32)


def _ffn(x2d, attn, conv, w_out, gains, layer, w_gate, w_up, w_down):
    n, d = x2d.shape
    tm = FFN_ROW_TILE
    d_ff = w_gate.shape[1]
    const = lambda i: (0, 0)
    row = lambda i: (i, 0)
    resident = functools.partial(pl.BlockSpec, index_map=const, pipeline_mode=pl.Buffered(1))
    return pl.pallas_call(
        functools.partial(_ffn_kernel, layer=layer),
        grid=(n // tm,),
        in_specs=[
            pl.BlockSpec((tm, d), row),
            pl.BlockSpec((tm, attn.shape[1]), row),
            pl.BlockSpec((tm, conv.shape[1]), row),
            resident(w_out.shape),
            pl.BlockSpec(gains.shape, const),
            resident(w_gate.shape),
            resident(w_up.shape),
            resident(w_down.shape),
        ],
        out_specs=pl.BlockSpec((tm, d), row),
        out_shape=jax.ShapeDtypeStruct((n, d), F32),
        scratch_shapes=[pltpu.VMEM((tm, d_ff), BF16)],
        compiler_params=pltpu.CompilerParams(
            dimension_semantics=("arbitrary",), vmem_limit_bytes=VMEM_LIMIT),
    )(x2d, attn, conv, w_out, gains, w_gate, w_up, w_down)


def kernel(x, norm_mix, w_in, q_norm, k_norm, conv_w, w_out, norm_ffn, w_gate, w_up, w_down):
    batch, seq, d = x.shape
    depth = w_in.shape[0]
    conv_dim = conv_w.shape[2]
    attn_dim = w_out.shape[1] - conv_dim
    assert seq % ROW_TILE == 0 and seq % (ATTN_TILES * SUB) == 0 and attn_dim % MXU_DIM == 0
    assert (batch * seq) % FFN_ROW_TILE == 0
    assert w_gate.shape[2] % FF_CHUNK == 0 and conv_w.shape[1] == CONV_SHIFTS + 1

    xf = x.reshape(batch * seq, d)
    for l in range(depth):
        q, k, v, conv, wo, wg, wu, wd = _inproj(
            xf, norm_mix, w_in, l, q_norm, k_norm, conv_w,
            (w_out, w_gate, w_up, w_down), seq=seq, attn_dim=attn_dim, conv_dim=conv_dim)
        attn = _attention(q, k, v, batch=batch, seq=seq)
        xf = _ffn(xf, attn, conv, wo, norm_ffn, l, wg, wu, wd)
    return xf.reshape(batch, seq, d)
```

```python
import functools

import jax
import jax.numpy as jnp
from jax import lax
from jax.experimental import pallas as pl
from jax.experimental.pallas import tpu as pltpu

F32 = jnp.float32
BF16 = jnp.bfloat16

HEAD_DIM = 64
EPS = 1e-6
LANES = 128
SUBLANES = 8
BF16_SUBLANES = 16
W_CAST_ROWS = 128
MXU_DIM = 256
HEADS_PER_BLOCK = LANES // HEAD_DIM
SUB = 128
LOG2E = 1.4426950408889634
SKIP_ABOVE = 104.0

ROW_TILE = 1024
FFN_ROW_TILE = 1024
ATTN_TILES = 4
NEAR_SUBS = 1
FAR_ROWS = 32
OUTPUT_LAG = 2
WALK_SUBS = 2
FFN_ROW_GROUPS = 2
FF_CHUNK = 256
CONV_SHIFTS = 2
VMEM_LIMIT = 56 * 1024 * 1024


def _rms_scale(x):
    return lax.rsqrt(jnp.mean(x * x, axis=-1, keepdims=True) + EPS)


def _inproj_kernel(x_ref, gain_ref, w_ref, qg_ref, kg_ref, cw_ref, *rest,
                   layer, tiles_per_seq, attn_dim, conv_dim, n_cast):
    cast_in, (q_ref, k_ref, v_ref, c_ref) = rest[:n_cast], rest[n_cast:n_cast + 4]
    cast_out, (halo_ref, wbf_ref) = rest[n_cast + 4:2 * n_cast + 4], rest[2 * n_cast + 4:]
    tm = x_ref.shape[0]
    i = pl.program_id(0)

    @pl.when(i == 0)
    def _():
        def cast_rows(c, _):
            rows = pl.ds(pl.multiple_of(c * W_CAST_ROWS, W_CAST_ROWS), W_CAST_ROWS)
            wbf_ref[rows, :] = w_ref[rows, :].astype(BF16)
            return 0
        lax.fori_loop(0, w_ref.shape[0] // W_CAST_ROWS, cast_rows, 0)
        halo_ref[...] = jnp.zeros_like(halo_ref)

    r = lax.broadcasted_iota(jnp.int32, (MXU_DIM, MXU_DIM), 0) // HEAD_DIM
    c = lax.broadcasted_iota(jnp.int32, (MXU_DIM, MXU_DIM), 1) // HEAD_DIM
    seg = jnp.where(r == c, 1.0, 0.0).astype(BF16)
    taps = [cw_ref[layer, n:n + 1, :] for n in range(CONV_SHIFTS + 1)]
    row = lax.broadcasted_iota(jnp.int32, (SUBLANES, conv_dim), 0)

    def head_norm(p, gain_ref):
        gain = jnp.concatenate([gain_ref[layer:layer + 1, :]] * (attn_dim // HEAD_DIM), axis=1)
        p2 = (p * p).astype(BF16)
        ssq = jnp.concatenate(
            [jnp.dot(p2[:, s:s + MXU_DIM], seg, preferred_element_type=F32)
             for s in range(0, attn_dim, MXU_DIM)], axis=1)
        return (p * lax.rsqrt(ssq * (1.0 / HEAD_DIM) + EPS)) * gain

    x = x_ref[...]
    h = ((x * _rms_scale(x)) * gain_ref[layer:layer + 1, :]).astype(BF16)

    def proj(c0, width):
        return jnp.dot(h, wbf_ref[:, c0:c0 + width], preferred_element_type=F32)

    cb = proj(3 * attn_dim, conv_dim)
    hh = proj(3 * attn_dim + conv_dim, conv_dim) * proj(3 * attn_dim + 2 * conv_dim, conv_dim)

    q_ref[...] = (head_norm(proj(0, attn_dim), qg_ref) * (HEAD_DIM ** -0.5)).astype(BF16)
    k_ref[...] = head_norm(proj(attn_dim, attn_dim), kg_ref).astype(BF16)
    v_ref[...] = proj(2 * attn_dim, attn_dim).astype(BF16)

    def shift_down(a, n):
        prev_last = jnp.where(i % tiles_per_seq == 0, 0.0, halo_ref[n, SUBLANES - 1:SUBLANES, :])
        halo_ref[n] = a[tm - SUBLANES:tm, :]
        rolled = pltpu.roll(a, 1, 0)
        head = jnp.where(row == 0, jnp.broadcast_to(prev_last, (SUBLANES, conv_dim)), rolled[0:SUBLANES])
        return jnp.concatenate([head, rolled[SUBLANES:]], axis=0)

    y = taps[2] * hh + shift_down(taps[1] * hh + shift_down(taps[0] * hh, 0), 1)
    c_ref[...] = (cb * y).astype(BF16)

    for src, dst in zip(cast_in, cast_out):
        dst[...] = src[...].astype(BF16)


def _cast_specs(weights, layer, n_steps, step_of):
    in_specs, out_specs, shapes = [], [], []
    for w in weights:
        _, rows, cols = w.shape
        chunks = max(c for c in range(1, n_steps + 1) if rows % c == 0 and (rows // c) % BF16_SUBLANES == 0)
        chunk_of = lambda *idx, chunks=chunks: jnp.minimum(step_of(*idx), chunks - 1)
        in_specs.append(pl.BlockSpec((None, rows // chunks, cols),
                                     lambda *idx, chunk_of=chunk_of: (layer, chunk_of(*idx), 0)))
        out_specs.append(pl.BlockSpec((rows // chunks, cols), lambda *idx, chunk_of=chunk_of: (chunk_of(*idx), 0)))
        shapes.append(jax.ShapeDtypeStruct((rows, cols), BF16))
    return in_specs, out_specs, shapes


def _inproj(x2d, gains, w_in, layer, q_gains, k_gains, conv_w, later_weights, *, seq, attn_dim, conv_dim):
    n, d = x2d.shape
    tm = ROW_TILE
    n_steps = n // tm
    cols = w_in.shape[2]
    const = lambda i: (0, 0)
    row = lambda i: (i, 0)
    cast_in_specs, cast_out_specs, cast_shapes = _cast_specs(later_weights, layer, n_steps, lambda i: i)
    out_shapes = [jax.ShapeDtypeStruct((n, attn_dim), BF16)] * 3 + [jax.ShapeDtypeStruct((n, conv_dim), BF16)]
    return pl.pallas_call(
        functools.partial(_inproj_kernel, layer=layer, tiles_per_seq=seq // tm, attn_dim=attn_dim,
                          conv_dim=conv_dim, n_cast=len(later_weights)),
        grid=(n_steps,),
        in_specs=[
            pl.BlockSpec((tm, d), row),
            pl.BlockSpec(gains.shape, const),
            pl.BlockSpec((None, d, cols), lambda i: (layer, 0, 0), pipeline_mode=pl.Buffered(1)),
            pl.BlockSpec(q_gains.shape, const),
            pl.BlockSpec(k_gains.shape, const),
            pl.BlockSpec(conv_w.shape, lambda i: (0, 0, 0)),
        ] + cast_in_specs,
        out_specs=[pl.BlockSpec((tm, attn_dim), row)] * 3 + [pl.BlockSpec((tm, conv_dim), row)] + cast_out_specs,
        out_shape=out_shapes + cast_shapes,
        scratch_shapes=[pltpu.VMEM((CONV_SHIFTS, SUBLANES, conv_dim), F32), pltpu.VMEM((d, cols), BF16)],
        compiler_params=pltpu.CompilerParams(
            dimension_semantics=("arbitrary",), vmem_limit_bytes=VMEM_LIMIT),
    )(x2d, gains, w_in, q_gains, k_gains, conv_w, *later_weights)


def _min_to_vreg(arrays):
    chunks = [a[r:r + SUBLANES] for a in arrays for r in range(0, a.shape[0], SUBLANES)]
    while len(chunks) > 1:
        odd = chunks[-1:] if len(chunks) % 2 else []
        chunks = [jnp.minimum(a, b) for a, b in zip(chunks[0::2], chunks[1::2])] + odd
    return chunks[0]


def _attn_kernel(q_ref, k_ref, v_ref, o_ref, acc_ref, carry_ref, min_ref):
    n_tiles = q_ref.shape[0] // SUB
    n_blocks = q_ref.shape[1] // LANES
    i = pl.program_id(1)
    assert n_tiles >= NEAR_SUBS + 1

    r = lax.broadcasted_iota(jnp.int32, (2 * SUB, 2 * SUB), 0) % SUB
    c = lax.broadcasted_iota(jnp.int32, (2 * SUB, 2 * SUB), 1)
    tri = jnp.where((c >= SUB) | (r >= c), 1.0, 0.0).astype(BF16)
    causal = (lax.broadcasted_iota(jnp.int32, (HEADS_PER_BLOCK * SUB, SUB), 1)
              < lax.broadcasted_iota(jnp.int32, (HEADS_PER_BLOCK * SUB, SUB), 0) % SUB)

    def head_rows(row0, nrows):
        return [slice(h * SUB + row0, h * SUB + row0 + nrows) for h in range(HEADS_PER_BLOCK)]

    def softplus_split(z_s, masked):
        e = jnp.exp2(jnp.abs(z_s) * (-LOG2E))
        sp = jnp.maximum(z_s, 0.0) + jnp.log(1.0 + e)
        if masked:
            sp = jnp.where(causal[:z_s.shape[0]], sp, 0.0)
        hi = sp.astype(BF16)
        lo = (sp - hi.astype(F32)).astype(BF16)
        return jnp.concatenate([hi, lo], axis=1)

    def span(jobs, whole_tiles):
        z_all, cs_all = {}, {}
        for n, (t, row0, nrows, start, nsub, diag, first) in enumerate(jobs):
            assert not diag or (row0 == 0 and nrows == SUB)
            qrows = slice(t * SUB + row0, t * SUB + row0 + nrows)
            lane = lax.broadcasted_iota(jnp.int32, (nrows, LANES), 1)
            for p in range(n_blocks):
                cols = slice(p * LANES, (p + 1) * LANES)
                q = q_ref[qrows, cols]
                q2 = jnp.concatenate([jnp.where((lane // HEAD_DIM) == h, q, jnp.zeros_like(q))
                                      for h in range(HEADS_PER_BLOCK)], axis=0)
                z = lax.dot_general(q2, k_ref[pl.ds(start, nsub * SUB), cols], (((1,), (1,)), ((), ())),
                                    preferred_element_type=F32)
                z_all[n, p] = z
                lhs_parts = [softplus_split(z[:, s * SUB:(s + 1) * SUB], diag and s == nsub - 1)
                             for s in reversed(range(nsub))]
                cs_all[n, p] = jnp.dot(jnp.concatenate(lhs_parts, axis=0), tri, preferred_element_type=F32)
        latest = {}
        for n, (t, row0, nrows, start, nsub, diag, first) in enumerate(jobs):
            qrows = slice(t * SUB + row0, t * SUB + row0 + nrows)
            lane = lax.broadcasted_iota(jnp.int32, (nrows, LANES), 1)
            rows2 = HEADS_PER_BLOCK * nrows
            for p in range(n_blocks):
                cols = slice(p * LANES, (p + 1) * LANES)
                carry = None if first else jnp.concatenate([carry_ref[t, p, rs] for rs in head_rows(row0, nrows)], axis=0)
                a_cols = [None] * nsub
                for idx, s in enumerate(reversed(range(nsub))):
                    cs = cs_all[n, p][idx * rows2:(idx + 1) * rows2]
                    g = cs[:, :SUB] if carry is None else cs[:, :SUB] + carry
                    carry = cs[:, SUB:] if carry is None else carry + cs[:, SUB:]
                    a = jnp.exp2((z_all[n, p][:, s * SUB:(s + 1) * SUB] - g) * LOG2E)
                    if diag and s == nsub - 1:
                        a = jnp.where(causal, a, 0.0)
                    a_cols[s] = a.astype(BF16)
                for h, rs in enumerate(head_rows(row0, nrows)):
                    carry_ref[t, p, rs] = carry[h * nrows:(h + 1) * nrows]
                latest.setdefault(t, {}).setdefault(p, []).append((row0, nrows, carry))
                o2 = jnp.dot(jnp.concatenate(a_cols, axis=1), v_ref[pl.ds(start, nsub * SUB), cols],
                             preferred_element_type=F32)
                out = o2[0:nrows]
                for h in range(1, HEADS_PER_BLOCK):
                    out = jnp.where((lane // HEAD_DIM) == h, o2[h * nrows:(h + 1) * nrows], out)
                if first:
                    acc_ref[qrows, cols] = out
                else:
                    acc_ref[qrows, cols] += out
        for t, by_block in latest.items():
            if whole_tiles:
                assert all(len(e) == 1 and e[0][:2] == (0, SUB) for e in by_block.values())
                min_ref[t] = _min_to_vreg([e[0][2] for e in by_block.values()])
            else:
                min_ref[t] = _min_to_vreg([carry_ref[t, p] for p in range(n_blocks)])

    def window(tiles):
        n_keys = (NEAR_SUBS + 2) * SUB
        far = [slice(h * SUB, h * SUB + FAR_ROWS) for h in range(HEADS_PER_BLOCK)]
        rest = [slice(h * SUB + FAR_ROWS, (h + 1) * SUB) for h in range(HEADS_PER_BLOCK)]
        lane = lax.broadcasted_iota(jnp.int32, (SUB, LANES), 1)
        lane_far = lax.broadcasted_iota(jnp.int32, (FAR_ROWS, LANES), 1)
        ctx, scores, suffix_sums, outputs = {}, [], [], []

        def score(t, tile, p):
            start = pl.multiple_of((tile - NEAR_SUBS - 1) * SUB, SUB)
            cols = slice(p * LANES, (p + 1) * LANES)
            q = q_ref[t * SUB:(t + 1) * SUB, cols]
            q2 = jnp.concatenate([jnp.where((lane // HEAD_DIM) == h, q, jnp.zeros_like(q))
                                  for h in range(HEADS_PER_BLOCK)], axis=0)
            z = lax.dot_general(q2, k_ref[pl.ds(start, n_keys), cols], (((1,), (1,)), ((), ())),
                                preferred_element_type=F32)
            ctx[t, p] = (start, z)

        def sums(t, p):
            start, z = ctx[t, p]
            z_far = jnp.concatenate([z[rs, 0:SUB] for rs in far], axis=0)
            lhs = [softplus_split(z[:, s * SUB:(s + 1) * SUB], s == NEAR_SUBS + 1)
                   for s in range(NEAR_SUBS + 1, 0, -1)] + [softplus_split(z_far, False)]
            cs = jnp.dot(jnp.concatenate(lhs, axis=0), tri, preferred_element_type=F32)
            ctx[t, p] = (start, z, z_far, cs)

        def output(t, p, parts):
            cols = slice(p * LANES, (p + 1) * LANES)
            start, z, z_far, cs = ctx[t, p]
            carry, a_cols = None, []
            for idx, s in enumerate(range(NEAR_SUBS + 1, 0, -1)):
                blk = cs[idx * 2 * SUB:(idx + 1) * 2 * SUB]
                g = blk[:, :SUB] if carry is None else blk[:, :SUB] + carry
                carry = blk[:, SUB:] if carry is None else carry + blk[:, SUB:]
                a = jnp.exp2((z[:, s * SUB:(s + 1) * SUB] - g) * LOG2E)
                if s == NEAR_SUBS + 1:
                    a = jnp.where(causal, a, 0.0)
                a_cols.insert(0, a.astype(BF16))
            blk = cs[(NEAR_SUBS + 1) * 2 * SUB:]
            carry_far = jnp.concatenate([carry[rs] for rs in far], axis=0)
            a_far = jnp.exp2((z_far - (blk[:, :SUB] + carry_far)) * LOG2E).astype(BF16)
            carry_far = carry_far + blk[:, SUB:]
            for h in range(HEADS_PER_BLOCK):
                carry_ref[t, p, far[h]] = carry_far[h * FAR_ROWS:(h + 1) * FAR_ROWS]
                carry_ref[t, p, rest[h]] = carry[rest[h]]
                parts.append(carry[rest[h]])
            parts.append(carry_far)
            if p == n_blocks - 1:
                min_ref[t] = _min_to_vreg(parts)
            o2 = jnp.dot(jnp.concatenate(a_cols, axis=1), v_ref[pl.ds(start + SUB, (NEAR_SUBS + 1) * SUB), cols],
                         preferred_element_type=F32)
            o_far = jnp.dot(a_far, v_ref[pl.ds(start, SUB), cols], preferred_element_type=F32)
            out, out_far = o2[0:SUB], o_far[0:FAR_ROWS]
            for h in range(1, HEADS_PER_BLOCK):
                out = jnp.where((lane // HEAD_DIM) == h, o2[h * SUB:(h + 1) * SUB], out)
                out_far = jnp.where((lane_far // HEAD_DIM) == h,
                                    o_far[h * FAR_ROWS:(h + 1) * FAR_ROWS], out_far)
            out = jnp.concatenate([out[:FAR_ROWS] + out_far, out[FAR_ROWS:]], axis=0)
            acc_ref[t * SUB:(t + 1) * SUB, cols] = out
            o_ref[t * SUB:(t + 1) * SUB, cols] = out.astype(o_ref.dtype)

        parts = {t: [] for t, _ in tiles}
        for p in range(n_blocks):
            for t, tile in tiles:
                scores.append(functools.partial(score, t, tile, p))
                suffix_sums.append(functools.partial(sums, t, p))
                outputs.append(functools.partial(output, t, p, parts[t]))
        for stage in scores:
            stage()
        for n in range(len(suffix_sums) + OUTPUT_LAG):
            if n < len(suffix_sums):
                suffix_sums[n]()
            if n >= OUTPUT_LAG:
                outputs[n - OUTPUT_LAG]()

    @pl.when(i == 0)
    def _():
        early = [t for t in range(n_tiles) if t <= NEAR_SUBS]
        span([(t, 0, SUB, 0, t + 1, True, True) for t in early], True)
        window([(t, t) for t in range(n_tiles) if t > NEAR_SUBS])
        for t in early:
            o_ref[t * SUB:(t + 1) * SUB, :] = acc_ref[t * SUB:(t + 1) * SUB, :].astype(o_ref.dtype)

    @pl.when(i > 0)
    def _():
        window([(t, i * n_tiles + t) for t in range(n_tiles)])

    @pl.when(jnp.min(min_ref[...], axis=(0, 1))[0] < SKIP_ABOVE)
    def _():
        for t in range(n_tiles):
            tile = i * n_tiles + t
            if FAR_ROWS < SUB:
                @pl.when(jnp.logical_and(tile > NEAR_SUBS, jnp.min(min_ref[t]) < SKIP_ABOVE))
                def _(t=t, tile=tile):
                    start = pl.multiple_of((tile - NEAR_SUBS - 1) * SUB, SUB)
                    span([(t, FAR_ROWS, SUB - FAR_ROWS, start, 1, False, False)], False)

            def unfinished(t=t):
                return (jnp.min(min_ref[t]) < SKIP_ABOVE).astype(jnp.int32)

            def cond(state):
                rem, go = state
                return jnp.logical_and(rem >= WALK_SUBS * SUB, go > 0)

            def body(state, t=t, unfinished=unfinished):
                rem, _ = state
                start = pl.multiple_of(rem - WALK_SUBS * SUB, SUB)
                span([(t, 0, SUB, start, WALK_SUBS, False, False)], False)
                return start, unfinished()

            remaining = jnp.maximum(tile - NEAR_SUBS - 1, 0) * SUB
            rem, go = lax.while_loop(cond, body, (remaining, unfinished()))
            for tail in range(1, WALK_SUBS):
                @pl.when(jnp.logical_and(rem == tail * SUB, go > 0))
                def _(t=t, tail=tail):
                    span([(t, 0, SUB, 0, tail, False, False)], False)
        o_ref[...] = acc_ref[...].astype(o_ref.dtype)


def _attention(q, k, v, *, batch, seq):
    n, attn_dim = q.shape
    tq = ATTN_TILES * SUB
    nq = seq // tq
    return pl.pallas_call(
        _attn_kernel,
        grid=(batch, nq),
        in_specs=[
            pl.BlockSpec((tq, attn_dim), lambda b, i: (b * nq + i, 0)),
            pl.BlockSpec((seq, attn_dim), lambda b, i: (b, 0)),
            pl.BlockSpec((seq, attn_dim), lambda b, i: (b, 0)),
        ],
        out_specs=pl.BlockSpec((tq, attn_dim), lambda b, i: (b * nq + i, 0)),
        out_shape=jax.ShapeDtypeStruct((n, attn_dim), BF16),
        scratch_shapes=[pltpu.VMEM((tq, attn_dim), F32),
                        pltpu.VMEM((ATTN_TILES, attn_dim // LANES, HEADS_PER_BLOCK * SUB, SUB), F32),
                        pltpu.VMEM((ATTN_TILES, SUBLANES, SUB), F32)],
        compiler_params=pltpu.CompilerParams(
            dimension_semantics=("arbitrary", "arbitrary"), vmem_limit_bytes=VMEM_LIMIT),
    )(q, k, v)


def _ffn_kernel(x_ref, attn_ref, conv_ref, wo_ref, gain_ref, wg_ref, wu_ref, wd_ref, o_ref, act_ref, *, layer):
    attn_dim = attn_ref.shape[1]
    d_ff = wg_ref.shape[1]
    tm = x_ref.shape[0]
    gm = tm // FFN_ROW_GROUPS
    x1s, hs = [], []
    for part in range(FFN_ROW_GROUPS):
        rows = slice(part * gm, (part + 1) * gm)
        mix = (jnp.dot(attn_ref[rows, :], wo_ref[0:attn_dim, :], preferred_element_type=F32)
               + jnp.dot(conv_ref[rows, :], wo_ref[attn_dim:, :], preferred_element_type=F32))
        x1 = x_ref[rows, :] + mix
        x1s.append(x1)
        hs.append(((x1 * _rms_scale(x1)) * gain_ref[layer:layer + 1, :]).astype(BF16))
    for part in range(FFN_ROW_GROUPS):
        rows = slice(part * gm, (part + 1) * gm)
        for c0 in range(0, d_ff, FF_CHUNK):
            g = jnp.dot(hs[part], wg_ref[:, c0:c0 + FF_CHUNK], preferred_element_type=F32)
            u = jnp.dot(hs[part], wu_ref[:, c0:c0 + FF_CHUNK], preferred_element_type=F32)
            act_ref[rows, c0:c0 + FF_CHUNK] = ((g * jax.nn.sigmoid(g)) * u).astype(BF16)
        o_ref[rows, :] = x1s[part] + jnp.dot(act_ref[rows, :], wd_ref[...], preferred_element_type=F32)


def _ffn(x2d, attn, conv, w_out, gains, layer, w_gate, w_up, w_down):
    n, d = x2d.shape
    tm = FFN_ROW_TILE
    d_ff = w_gate.shape[1]
    const = lambda i: (0, 0)
    row = lambda i: (i, 0)
    resident = functools.partial(pl.BlockSpec, index_map=const, pipeline_mode=pl.Buffered(1))
    return pl.pallas_call(
        functools.partial(_ffn_kernel, layer=layer),
        grid=(n // tm,),
        in_specs=[
            pl.BlockSpec((tm, d), row),
            pl.BlockSpec((tm, attn.shape[1]), row),
            pl.BlockSpec((tm, conv.shape[1]), row),
            resident(w_out.shape),
            pl.BlockSpec(gains.shape, const),
            resident(w_gate.shape),
            resident(w_up.shape),
            resident(w_down.shape),
        ],
        out_specs=pl.BlockSpec((tm, d), row),
        out_shape=jax.ShapeDtypeStruct((n, d), F32),
        scratch_shapes=[pltpu.VMEM((tm, d_ff), BF16)],
        compiler_params=pltpu.CompilerParams(
            dimension_semantics=("arbitrary",), vmem_limit_bytes=VMEM_LIMIT),
    )(x2d, attn, conv, w_out, gains, w_gate, w_up, w_down)


def kernel(x, norm_mix, w_in, q_norm, k_norm, conv_w, w_out, norm_ffn, w_gate, w_up, w_down):
    batch, seq, d = x.shape
    depth = w_in.shape[0]
    conv_dim = conv_w.shape[2]
    attn_dim = w_out.shape[1] - conv_dim
    assert seq % ROW_TILE == 0 and seq % (ATTN_TILES * SUB) == 0 and attn_dim % MXU_DIM == 0
    assert (batch * seq) % FFN_ROW_TILE == 0
    assert w_gate.shape[2] % FF_CHUNK == 0 and conv_w.shape[1] == CONV_SHIFTS + 1

    xf = x.reshape(batch * seq, d)
    for l in range(depth):
        q, k, v, conv, wo, wg, wu, wd = _inproj(
            xf, norm_mix, w_in, l, q_norm, k_norm, conv_w,
            (w_out, w_gate, w_up, w_down), seq=seq, attn_dim=attn_dim, conv_dim=conv_dim)
        attn = _attention(q, k, v, batch=batch, seq=seq)
        xf = _ffn(xf, attn, conv, wo, norm_ffn, l, wg, wu, wd)
    return xf.reshape(batch, seq, d)
```

```python
import functools

import jax
import jax.numpy as jnp
from jax import lax
from jax.experimental import pallas as pl
from jax.experimental.pallas import tpu as pltpu

F32 = jnp.float32
BF16 = jnp.bfloat16

HEAD_DIM = 64
EPS = 1e-6
LANES = 128
SUBLANES = 8
BF16_SUBLANES = 16
W_CAST_ROWS = 128
MXU_DIM = 256
HEADS_PER_BLOCK = LANES // HEAD_DIM
SUB = 128
LOG2E = 1.4426950408889634
SKIP_ABOVE = 104.0

ROW_TILE = 1024
FFN_ROW_TILE = 1024
ATTN_TILES = 4
NEAR_SUBS = 1
FAR_ROWS = 32
OUTPUT_LAG = 2
WALK_SUBS = 2
FFN_ROW_GROUPS = 4
FF_CHUNK = 256
CONV_SHIFTS = 2
VMEM_LIMIT = 56 * 1024 * 1024


def _rms_scale(x):
    return lax.rsqrt(jnp.mean(x * x, axis=-1, keepdims=True) + EPS)


def _inproj_kernel(x_ref, gain_ref, w_ref, qg_ref, kg_ref, cw_ref, *rest,
                   layer, tiles_per_seq, attn_dim, conv_dim, n_cast):
    cast_in, (q_ref, k_ref, v_ref, c_ref) = rest[:n_cast], rest[n_cast:n_cast + 4]
    cast_out, (halo_ref, wbf_ref) = rest[n_cast + 4:2 * n_cast + 4], rest[2 * n_cast + 4:]
    tm = x_ref.shape[0]
    i = pl.program_id(0)

    @pl.when(i == 0)
    def _():
        def cast_rows(c, _):
            rows = pl.ds(pl.multiple_of(c * W_CAST_ROWS, W_CAST_ROWS), W_CAST_ROWS)
            wbf_ref[rows, :] = w_ref[rows, :].astype(BF16)
            return 0
        lax.fori_loop(0, w_ref.shape[0] // W_CAST_ROWS, cast_rows, 0)
        halo_ref[...] = jnp.zeros_like(halo_ref)

    r = lax.broadcasted_iota(jnp.int32, (MXU_DIM, MXU_DIM), 0) // HEAD_DIM
    c = lax.broadcasted_iota(jnp.int32, (MXU_DIM, MXU_DIM), 1) // HEAD_DIM
    seg = jnp.where(r == c, 1.0, 0.0).astype(BF16)
    taps = [cw_ref[layer, n:n + 1, :] for n in range(CONV_SHIFTS + 1)]
    row = lax.broadcasted_iota(jnp.int32, (SUBLANES, conv_dim), 0)

    def head_norm(p, gain_ref):
        gain = jnp.concatenate([gain_ref[layer:layer + 1, :]] * (attn_dim // HEAD_DIM), axis=1)
        p2 = (p * p).astype(BF16)
        ssq = jnp.concatenate(
            [jnp.dot(p2[:, s:s + MXU_DIM], seg, preferred_element_type=F32)
             for s in range(0, attn_dim, MXU_DIM)], axis=1)
        return (p * lax.rsqrt(ssq * (1.0 / HEAD_DIM) + EPS)) * gain

    x = x_ref[...]
    h = ((x * _rms_scale(x)) * gain_ref[layer:layer + 1, :]).astype(BF16)

    def proj(c0, width):
        return jnp.dot(h, wbf_ref[:, c0:c0 + width], preferred_element_type=F32)

    cb = proj(3 * attn_dim, conv_dim)
    hh = proj(3 * attn_dim + conv_dim, conv_dim) * proj(3 * attn_dim + 2 * conv_dim, conv_dim)

    q_ref[...] = (head_norm(proj(0, attn_dim), qg_ref) * (HEAD_DIM ** -0.5)).astype(BF16)
    k_ref[...] = head_norm(proj(attn_dim, attn_dim), kg_ref).astype(BF16)
    v_ref[...] = proj(2 * attn_dim, attn_dim).astype(BF16)

    def shift_down(a, n):
        prev_last = jnp.where(i % tiles_per_seq == 0, 0.0, halo_ref[n, SUBLANES - 1:SUBLANES, :])
        halo_ref[n] = a[tm - SUBLANES:tm, :]
        rolled = pltpu.roll(a, 1, 0)
        head = jnp.where(row == 0, jnp.broadcast_to(prev_last, (SUBLANES, conv_dim)), rolled[0:SUBLANES])
        return jnp.concatenate([head, rolled[SUBLANES:]], axis=0)

    y = taps[2] * hh + shift_down(taps[1] * hh + shift_down(taps[0] * hh, 0), 1)
    c_ref[...] = (cb * y).astype(BF16)

    for src, dst in zip(cast_in, cast_out):
        dst[...] = src[...].astype(BF16)


def _cast_specs(weights, layer, n_steps, step_of):
    in_specs, out_specs, shapes = [], [], []
    for w in weights:
        _, rows, cols = w.shape
        chunks = max(c for c in range(1, n_steps + 1) if rows % c == 0 and (rows // c) % BF16_SUBLANES == 0)
        chunk_of = lambda *idx, chunks=chunks: jnp.minimum(step_of(*idx), chunks - 1)
        in_specs.append(pl.BlockSpec((None, rows // chunks, cols),
                                     lambda *idx, chunk_of=chunk_of: (layer, chunk_of(*idx), 0)))
        out_specs.append(pl.BlockSpec((rows // chunks, cols), lambda *idx, chunk_of=chunk_of: (chunk_of(*idx), 0)))
        shapes.append(jax.ShapeDtypeStruct((rows, cols), BF16))
    return in_specs, out_specs, shapes


def _inproj(x2d, gains, w_in, layer, q_gains, k_gains, conv_w, later_weights, *, seq, attn_dim, conv_dim):
    n, d = x2d.shape
    tm = ROW_TILE
    n_steps = n // tm
    cols = w_in.shape[2]
    const = lambda i: (0, 0)
    row = lambda i: (i, 0)
    cast_in_specs, cast_out_specs, cast_shapes = _cast_specs(later_weights, layer, n_steps, lambda i: i)
    out_shapes = [jax.ShapeDtypeStruct((n, attn_dim), BF16)] * 3 + [jax.ShapeDtypeStruct((n, conv_dim), BF16)]
    return pl.pallas_call(
        functools.partial(_inproj_kernel, layer=layer, tiles_per_seq=seq // tm, attn_dim=attn_dim,
                          conv_dim=conv_dim, n_cast=len(later_weights)),
        grid=(n_steps,),
        in_specs=[
            pl.BlockSpec((tm, d), row),
            pl.BlockSpec(gains.shape, const),
            pl.BlockSpec((None, d, cols), lambda i: (layer, 0, 0), pipeline_mode=pl.Buffered(1)),
            pl.BlockSpec(q_gains.shape, const),
            pl.BlockSpec(k_gains.shape, const),
            pl.BlockSpec(conv_w.shape, lambda i: (0, 0, 0)),
        ] + cast_in_specs,
        out_specs=[pl.BlockSpec((tm, attn_dim), row)] * 3 + [pl.BlockSpec((tm, conv_dim), row)] + cast_out_specs,
        out_shape=out_shapes + cast_shapes,
        scratch_shapes=[pltpu.VMEM((CONV_SHIFTS, SUBLANES, conv_dim), F32), pltpu.VMEM((d, cols), BF16)],
        compiler_params=pltpu.CompilerParams(
            dimension_semantics=("arbitrary",), vmem_limit_bytes=VMEM_LIMIT),
    )(x2d, gains, w_in, q_gains, k_gains, conv_w, *later_weights)


def _min_to_vreg(arrays):
    chunks = [a[r:r + SUBLANES] for a in arrays for r in range(0, a.shape[0], SUBLANES)]
    while len(chunks) > 1:
        odd = chunks[-1:] if len(chunks) % 2 else []
        chunks = [jnp.minimum(a, b) for a, b in zip(chunks[0::2], chunks[1::2])] + odd
    return chunks[0]


def _attn_kernel(q_ref, k_ref, v_ref, o_ref, acc_ref, carry_ref, min_ref):
    n_tiles = q_ref.shape[0] // SUB
    n_blocks = q_ref.shape[1] // LANES
    i = pl.program_id(1)
    assert n_tiles >= NEAR_SUBS + 1

    r = lax.broadcasted_iota(jnp.int32, (2 * SUB, 2 * SUB), 0) % SUB
    c = lax.broadcasted_iota(jnp.int32, (2 * SUB, 2 * SUB), 1)
    tri = jnp.where((c >= SUB) | (r >= c), 1.0, 0.0).astype(BF16)
    causal = (lax.broadcasted_iota(jnp.int32, (HEADS_PER_BLOCK * SUB, SUB), 1)
              < lax.broadcasted_iota(jnp.int32, (HEADS_PER_BLOCK * SUB, SUB), 0) % SUB)

    def head_rows(row0, nrows):
        return [slice(h * SUB + row0, h * SUB + row0 + nrows) for h in range(HEADS_PER_BLOCK)]

    def softplus_split(z_s, masked):
        e = jnp.exp2(jnp.abs(z_s) * (-LOG2E))
        sp = jnp.maximum(z_s, 0.0) + jnp.log(1.0 + e)
        if masked:
            sp = jnp.where(causal[:z_s.shape[0]], sp, 0.0)
        hi = sp.astype(BF16)
        lo = (sp - hi.astype(F32)).astype(BF16)
        return jnp.concatenate([hi, lo], axis=1)

    def span(jobs, whole_tiles):
        z_all, cs_all = {}, {}
        for n, (t, row0, nrows, start, nsub, diag, first) in enumerate(jobs):
            assert not diag or (row0 == 0 and nrows == SUB)
            qrows = slice(t * SUB + row0, t * SUB + row0 + nrows)
            lane = lax.broadcasted_iota(jnp.int32, (nrows, LANES), 1)
            for p in range(n_blocks):
                cols = slice(p * LANES, (p + 1) * LANES)
                q = q_ref[qrows, cols]
                q2 = jnp.concatenate([jnp.where((lane // HEAD_DIM) == h, q, jnp.zeros_like(q))
                                      for h in range(HEADS_PER_BLOCK)], axis=0)
                z = lax.dot_general(q2, k_ref[pl.ds(start, nsub * SUB), cols], (((1,), (1,)), ((), ())),
                                    preferred_element_type=F32)
                z_all[n, p] = z
                lhs_parts = [softplus_split(z[:, s * SUB:(s + 1) * SUB], diag and s == nsub - 1)
                             for s in reversed(range(nsub))]
                cs_all[n, p] = jnp.dot(jnp.concatenate(lhs_parts, axis=0), tri, preferred_element_type=F32)
        latest = {}
        for n, (t, row0, nrows, start, nsub, diag, first) in enumerate(jobs):
            qrows = slice(t * SUB + row0, t * SUB + row0 + nrows)
            lane = lax.broadcasted_iota(jnp.int32, (nrows, LANES), 1)
            rows2 = HEADS_PER_BLOCK * nrows
            for p in range(n_blocks):
                cols = slice(p * LANES, (p + 1) * LANES)
                carry = None if first else jnp.concatenate([carry_ref[t, p, rs] for rs in head_rows(row0, nrows)], axis=0)
                a_cols = [None] * nsub
                for idx, s in enumerate(reversed(range(nsub))):
                    cs = cs_all[n, p][idx * rows2:(idx + 1) * rows2]
                    g = cs[:, :SUB] if carry is None else cs[:, :SUB] + carry
                    carry = cs[:, SUB:] if carry is None else carry + cs[:, SUB:]
                    a = jnp.exp2((z_all[n, p][:, s * SUB:(s + 1) * SUB] - g) * LOG2E)
                    if diag and s == nsub - 1:
                        a = jnp.where(causal, a, 0.0)
                    a_cols[s] = a.astype(BF16)
                for h, rs in enumerate(head_rows(row0, nrows)):
                    carry_ref[t, p, rs] = carry[h * nrows:(h + 1) * nrows]
                latest.setdefault(t, {}).setdefault(p, []).append((row0, nrows, carry))
                o2 = jnp.dot(jnp.concatenate(a_cols, axis=1), v_ref[pl.ds(start, nsub * SUB), cols],
                             preferred_element_type=F32)
                out = o2[0:nrows]
                for h in range(1, HEADS_PER_BLOCK):
                    out = jnp.where((lane // HEAD_DIM) == h, o2[h * nrows:(h + 1) * nrows], out)
                if first:
                    acc_ref[qrows, cols] = out
                else:
                    acc_ref[qrows, cols] += out
        for t, by_block in latest.items():
            if whole_tiles:
                assert all(len(e) == 1 and e[0][:2] == (0, SUB) for e in by_block.values())
                min_ref[t] = _min_to_vreg([e[0][2] for e in by_block.values()])
            else:
                min_ref[t] = _min_to_vreg([carry_ref[t, p] for p in range(n_blocks)])

    def window(tiles):
        n_keys = (NEAR_SUBS + 2) * SUB
        far = [slice(h * SUB, h * SUB + FAR_ROWS) for h in range(HEADS_PER_BLOCK)]
        rest = [slice(h * SUB + FAR_ROWS, (h + 1) * SUB) for h in range(HEADS_PER_BLOCK)]
        lane = lax.broadcasted_iota(jnp.int32, (SUB, LANES), 1)
        lane_far = lax.broadcasted_iota(jnp.int32, (FAR_ROWS, LANES), 1)
        ctx, scores, suffix_sums, outputs = {}, [], [], []

        def score(t, tile, p):
            start = pl.multiple_of((tile - NEAR_SUBS - 1) * SUB, SUB)
            cols = slice(p * LANES, (p + 1) * LANES)
            q = q_ref[t * SUB:(t + 1) * SUB, cols]
            q2 = jnp.concatenate([jnp.where((lane // HEAD_DIM) == h, q, jnp.zeros_like(q))
                                  for h in range(HEADS_PER_BLOCK)], axis=0)
            z = lax.dot_general(q2, k_ref[pl.ds(start, n_keys), cols], (((1,), (1,)), ((), ())),
                                preferred_element_type=F32)
            ctx[t, p] = (start, z)

        def sums(t, p):
            start, z = ctx[t, p]
            z_far = jnp.concatenate([z[rs, 0:SUB] for rs in far], axis=0)
            lhs = [softplus_split(z[:, s * SUB:(s + 1) * SUB], s == NEAR_SUBS + 1)
                   for s in range(NEAR_SUBS + 1, 0, -1)] + [softplus_split(z_far, False)]
            cs = jnp.dot(jnp.concatenate(lhs, axis=0), tri, preferred_element_type=F32)
            ctx[t, p] = (start, z, z_far, cs)

        def output(t, p, parts):
            cols = slice(p * LANES, (p + 1) * LANES)
            start, z, z_far, cs = ctx[t, p]
            carry, a_cols = None, []
            for idx, s in enumerate(range(NEAR_SUBS + 1, 0, -1)):
                blk = cs[idx * 2 * SUB:(idx + 1) * 2 * SUB]
                g = blk[:, :SUB] if carry is None else blk[:, :SUB] + carry
                carry = blk[:, SUB:] if carry is None else carry + blk[:, SUB:]
                a = jnp.exp2((z[:, s * SUB:(s + 1) * SUB] - g) * LOG2E)
                if s == NEAR_SUBS + 1:
                    a = jnp.where(causal, a, 0.0)
                a_cols.insert(0, a.astype(BF16))
            blk = cs[(NEAR_SUBS + 1) * 2 * SUB:]
            carry_far = jnp.concatenate([carry[rs] for rs in far], axis=0)
            a_far = jnp.exp2((z_far - (blk[:, :SUB] + carry_far)) * LOG2E).astype(BF16)
            carry_far = carry_far + blk[:, SUB:]
            for h in range(HEADS_PER_BLOCK):
                carry_ref[t, p, far[h]] = carry_far[h * FAR_ROWS:(h + 1) * FAR_ROWS]
                carry_ref[t, p, rest[h]] = carry[rest[h]]
                parts.append(carry[rest[h]])
            parts.append(carry_far)
            if p == n_blocks - 1:
                min_ref[t] = _min_to_vreg(parts)
            o2 = jnp.dot(jnp.concatenate(a_cols, axis=1), v_ref[pl.ds(start + SUB, (NEAR_SUBS + 1) * SUB), cols],
                         preferred_element_type=F32)
            o_far = jnp.dot(a_far, v_ref[pl.ds(start, SUB), cols], preferred_element_type=F32)
            out, out_far = o2[0:SUB], o_far[0:FAR_ROWS]
            for h in range(1, HEADS_PER_BLOCK):
                out = jnp.where((lane // HEAD_DIM) == h, o2[h * SUB:(h + 1) * SUB], out)
                out_far = jnp.where((lane_far // HEAD_DIM) == h,
                                    o_far[h * FAR_ROWS:(h + 1) * FAR_ROWS], out_far)
            out = jnp.concatenate([out[:FAR_ROWS] + out_far, out[FAR_ROWS:]], axis=0)
            acc_ref[t * SUB:(t + 1) * SUB, cols] = out
            o_ref[t * SUB:(t + 1) * SUB, cols] = out.astype(o_ref.dtype)

        parts = {t: [] for t, _ in tiles}
        for p in range(n_blocks):
            for t, tile in tiles:
                scores.append(functools.partial(score, t, tile, p))
                suffix_sums.append(functools.partial(sums, t, p))
                outputs.append(functools.partial(output, t, p, parts[t]))
        for stage in scores:
            stage()
        for n in range(len(suffix_sums) + OUTPUT_LAG):
            if n < len(suffix_sums):
                suffix_sums[n]()
            if n >= OUTPUT_LAG:
                outputs[n - OUTPUT_LAG]()

    @pl.when(i == 0)
    def _():
        early = [t for t in range(n_tiles) if t <= NEAR_SUBS]
        span([(t, 0, SUB, 0, t + 1, True, True) for t in early], True)
        window([(t, t) for t in range(n_tiles) if t > NEAR_SUBS])
        for t in early:
            o_ref[t * SUB:(t + 1) * SUB, :] = acc_ref[t * SUB:(t + 1) * SUB, :].astype(o_ref.dtype)

    @pl.when(i > 0)
    def _():
        window([(t, i * n_tiles + t) for t in range(n_tiles)])

    @pl.when(jnp.min(min_ref[...], axis=(0, 1))[0] < SKIP_ABOVE)
    def _():
        for t in range(n_tiles):
            tile = i * n_tiles + t
            if FAR_ROWS < SUB:
                @pl.when(jnp.logical_and(tile > NEAR_SUBS, jnp.min(min_ref[t]) < SKIP_ABOVE))
                def _(t=t, tile=tile):
                    start = pl.multiple_of((tile - NEAR_SUBS - 1) * SUB, SUB)
                    span([(t, FAR_ROWS, SUB - FAR_ROWS, start, 1, False, False)], False)

            def unfinished(t=t):
                return (jnp.min(min_ref[t]) < SKIP_ABOVE).astype(jnp.int32)

            def cond(state):
                rem, go = state
                return jnp.logical_and(rem >= WALK_SUBS * SUB, go > 0)

            def body(state, t=t, unfinished=unfinished):
                rem, _ = state
                start = pl.multiple_of(rem - WALK_SUBS * SUB, SUB)
                span([(t, 0, SUB, start, WALK_SUBS, False, False)], False)
                return start, unfinished()

            remaining = jnp.maximum(tile - NEAR_SUBS - 1, 0) * SUB
            rem, go = lax.while_loop(cond, body, (remaining, unfinished()))
            for tail in range(1, WALK_SUBS):
                @pl.when(jnp.logical_and(rem == tail * SUB, go > 0))
                def _(t=t, tail=tail):
                    span([(t, 0, SUB, 0, tail, False, False)], False)
        o_ref[...] = acc_ref[...].astype(o_ref.dtype)


def _attention(q, k, v, *, batch, seq):
    n, attn_dim = q.shape
    tq = ATTN_TILES * SUB
    nq = seq // tq
    return pl.pallas_call(
        _attn_kernel,
        grid=(batch, nq),
        in_specs=[
            pl.BlockSpec((tq, attn_dim), lambda b, i: (b * nq + i, 0)),
            pl.BlockSpec((seq, attn_dim), lambda b, i: (b, 0)),
            pl.BlockSpec((seq, attn_dim), lambda b, i: (b, 0)),
        ],
        out_specs=pl.BlockSpec((tq, attn_dim), lambda b, i: (b * nq + i, 0)),
        out_shape=jax.ShapeDtypeStruct((n, attn_dim), BF16),
        scratch_shapes=[pltpu.VMEM((tq, attn_dim), F32),
                        pltpu.VMEM((ATTN_TILES, attn_dim // LANES, HEADS_PER_BLOCK * SUB, SUB), F32),
                        pltpu.VMEM((ATTN_TILES, SUBLANES, SUB), F32)],
        compiler_params=pltpu.CompilerParams(
            dimension_semantics=("arbitrary", "arbitrary"), vmem_limit_bytes=VMEM_LIMIT),
    )(q, k, v)


def _ffn_kernel(x_ref, attn_ref, conv_ref, wo_ref, gain_ref, wg_ref, wu_ref, wd_ref, o_ref, act_ref, *, layer):
    attn_dim = attn_ref.shape[1]
    d_ff = wg_ref.shape[1]
    tm = x_ref.shape[0]
    gm = tm // FFN_ROW_GROUPS
    x1s, hs = [], []
    for part in range(FFN_ROW_GROUPS):
        rows = slice(part * gm, (part + 1) * gm)
        mix = (jnp.dot(attn_ref[rows, :], wo_ref[0:attn_dim, :], preferred_element_type=F32)
               + jnp.dot(conv_ref[rows, :], wo_ref[attn_dim:, :], preferred_element_type=F32))
        x1 = x_ref[rows, :] + mix
        x1s.append(x1)
        hs.append(((x1 * _rms_scale(x1)) * gain_ref[layer:layer + 1, :]).astype(BF16))
    for part in range(FFN_ROW_GROUPS):
        rows = slice(part * gm, (part + 1) * gm)
        for c0 in range(0, d_ff, FF_CHUNK):
            g = jnp.dot(hs[part], wg_ref[:, c0:c0 + FF_CHUNK], preferred_element_type=F32)
            u = jnp.dot(hs[part], wu_ref[:, c0:c0 + FF_CHUNK], preferred_element_type=F32)
            act_ref[rows, c0:c0 + FF_CHUNK] = ((g * jax.nn.sigmoid(g)) * u).astype(BF16)
        o_ref[rows, :] = x1s[part] + jnp.dot(act_ref[rows, :], wd_ref[...], preferred_element_type=F32)


def _ffn(x2d, attn, conv, w_out, gains, layer, w_gate, w_up, w_down):
    n, d = x2d.shape
    tm = FFN_ROW_TILE
    d_ff = w_gate.shape[1]
    const = lambda i: (0, 0)
    row = lambda i: (i, 0)
    resident = functools.partial(pl.BlockSpec, index_map=const, pipeline_mode=pl.Buffered(1))
    return pl.pallas_call(
        functools.partial(_ffn_kernel, layer=layer),
        grid=(n // tm,),
        in_specs=[
            pl.BlockSpec((tm, d), row),
            pl.BlockSpec((tm, attn.shape[1]), row),
            pl.BlockSpec((tm, conv.shape[1]), row),
            resident(w_out.shape),
            pl.BlockSpec(gains.shape, const),
            resident(w_gate.shape),
            resident(w_up.shape),
            resident(w_down.shape),
        ],
        out_specs=pl.BlockSpec((tm, d), row),
        out_shape=jax.ShapeDtypeStruct((n, d), F32),
        scratch_shapes=[pltpu.VMEM((tm, d_ff), BF16)],
        compiler_params=pltpu.CompilerParams(
            dimension_semantics=("arbitrary",), vmem_limit_bytes=VMEM_LIMIT),
    )(x2d, attn, conv, w_out, gains, w_gate, w_up, w_down)


def kernel(x, norm_mix, w_in, q_norm, k_norm, conv_w, w_out, norm_ffn, w_gate, w_up, w_down):
    batch, seq, d = x.shape
    depth = w_in.shape[0]
    conv_dim = conv_w.shape[2]
    attn_dim = w_out.shape[1] - conv_dim
    assert seq % ROW_TILE == 0 and seq % (ATTN_TILES * SUB) == 0 and attn_dim % MXU_DIM == 0
    assert (batch * seq) % FFN_ROW_TILE == 0
    assert w_gate.shape[2] % FF_CHUNK == 0 and conv_w.shape[1] == CONV_SHIFTS + 1

    xf = x.reshape(batch * seq, d)
    for l in range(depth):
        q, k, v, conv, wo, wg, wu, wd = _inproj(
            xf, norm_mix, w_in, l, q_norm, k_norm, conv_w,
            (w_out, w_gate, w_up, w_down), seq=seq, attn_dim=attn_dim, conv_dim=conv_dim)
        attn = _attention(q, k, v, batch=batch, seq=seq)
        xf = _ffn(xf, attn, conv, wo, norm_ffn, l, wg, wu, wd)
    return xf.reshape(batch, seq, d)
```

```python
import functools

import jax
import jax.numpy as jnp
from jax import lax
from jax.experimental import pallas as pl
from jax.experimental.pallas import tpu as pltpu

F32 = jnp.float32
BF16 = jnp.bfloat16

HEAD_DIM = 64
EPS = 1e-6
LANES = 128
SUBLANES = 8
BF16_SUBLANES = 16
W_CAST_ROWS = 128
MXU_DIM = 256
HEADS_PER_BLOCK = LANES // HEAD_DIM
SUB = 128
LOG2E = 1.4426950408889634
SKIP_ABOVE = 104.0

ROW_TILE = 1024
FFN_ROW_TILE = 1024
ATTN_TILES = 4
NEAR_SUBS = 1
FAR_ROWS = 32
OUTPUT_LAG = 2
WALK_SUBS = 2
FF_CHUNK = 256
CONV_SHIFTS = 2
VMEM_LIMIT = 56 * 1024 * 1024


def _rms_scale(x):
    return lax.rsqrt(jnp.mean(x * x, axis=-1, keepdims=True) + EPS)


def _inproj_kernel(x_ref, gain_ref, w_ref, qg_ref, kg_ref, cw_ref, *rest,
                   layer, tiles_per_seq, attn_dim, conv_dim, n_cast):
    cast_in, (q_ref, k_ref, v_ref, c_ref) = rest[:n_cast], rest[n_cast:n_cast + 4]
    cast_out, (halo_ref, wbf_ref) = rest[n_cast + 4:2 * n_cast + 4], rest[2 * n_cast + 4:]
    tm = x_ref.shape[0]
    i = pl.program_id(0)

    @pl.when(i == 0)
    def _():
        def cast_rows(c, _):
            rows = pl.ds(pl.multiple_of(c * W_CAST_ROWS, W_CAST_ROWS), W_CAST_ROWS)
            wbf_ref[rows, :] = w_ref[rows, :].astype(BF16)
            return 0
        lax.fori_loop(0, w_ref.shape[0] // W_CAST_ROWS, cast_rows, 0)
        halo_ref[...] = jnp.zeros_like(halo_ref)

    r = lax.broadcasted_iota(jnp.int32, (MXU_DIM, MXU_DIM), 0) // HEAD_DIM
    c = lax.broadcasted_iota(jnp.int32, (MXU_DIM, MXU_DIM), 1) // HEAD_DIM
    seg = jnp.where(r == c, 1.0, 0.0).astype(BF16)
    taps = [cw_ref[layer, n:n + 1, :] for n in range(CONV_SHIFTS + 1)]
    row = lax.broadcasted_iota(jnp.int32, (SUBLANES, conv_dim), 0)

    def head_norm(p, gain_ref):
        gain = jnp.concatenate([gain_ref[layer:layer + 1, :]] * (attn_dim // HEAD_DIM), axis=1)
        p2 = (p * p).astype(BF16)
        ssq = jnp.concatenate(
            [jnp.dot(p2[:, s:s + MXU_DIM], seg, preferred_element_type=F32)
             for s in range(0, attn_dim, MXU_DIM)], axis=1)
        return (p * lax.rsqrt(ssq * (1.0 / HEAD_DIM) + EPS)) * gain

    x = x_ref[...]
    h = ((x * _rms_scale(x)) * gain_ref[layer:layer + 1, :]).astype(BF16)

    def proj(c0, width):
        return jnp.dot(h, wbf_ref[:, c0:c0 + width], preferred_element_type=F32)

    cb = proj(3 * attn_dim, conv_dim)
    hh = proj(3 * attn_dim + conv_dim, conv_dim) * proj(3 * attn_dim + 2 * conv_dim, conv_dim)

    q_ref[...] = (head_norm(proj(0, attn_dim), qg_ref) * (HEAD_DIM ** -0.5)).astype(BF16)
    k_ref[...] = head_norm(proj(attn_dim, attn_dim), kg_ref).astype(BF16)
    v_ref[...] = proj(2 * attn_dim, attn_dim).astype(BF16)

    def shift_down(a, n):
        prev_last = jnp.where(i % tiles_per_seq == 0, 0.0, halo_ref[n, SUBLANES - 1:SUBLANES, :])
        halo_ref[n] = a[tm - SUBLANES:tm, :]
        rolled = pltpu.roll(a, 1, 0)
        head = jnp.where(row == 0, jnp.broadcast_to(prev_last, (SUBLANES, conv_dim)), rolled[0:SUBLANES])
        return jnp.concatenate([head, rolled[SUBLANES:]], axis=0)

    y = taps[2] * hh + shift_down(taps[1] * hh + shift_down(taps[0] * hh, 0), 1)
    c_ref[...] = (cb * y).astype(BF16)

    for src, dst in zip(cast_in, cast_out):
        dst[...] = src[...].astype(BF16)


def _cast_specs(weights, layer, n_steps, step_of):
    in_specs, out_specs, shapes = [], [], []
    for w in weights:
        _, rows, cols = w.shape
        chunks = max(c for c in range(1, n_steps + 1) if rows % c == 0 and (rows // c) % BF16_SUBLANES == 0)
        chunk_of = lambda *idx, chunks=chunks: jnp.minimum(step_of(*idx), chunks - 1)
        in_specs.append(pl.BlockSpec((None, rows // chunks, cols),
                                     lambda *idx, chunk_of=chunk_of: (layer, chunk_of(*idx), 0)))
        out_specs.append(pl.BlockSpec((rows // chunks, cols), lambda *idx, chunk_of=chunk_of: (chunk_of(*idx), 0)))
        shapes.append(jax.ShapeDtypeStruct((rows, cols), BF16))
    return in_specs, out_specs, shapes


def _inproj(x2d, gains, w_in, layer, q_gains, k_gains, conv_w, later_weights, *, seq, attn_dim, conv_dim):
    n, d = x2d.shape
    tm = ROW_TILE
    n_steps = n // tm
    cols = w_in.shape[2]
    const = lambda i: (0, 0)
    row = lambda i: (i, 0)
    cast_in_specs, cast_out_specs, cast_shapes = _cast_specs(later_weights, layer, n_steps, lambda i: i)
    out_shapes = [jax.ShapeDtypeStruct((n, attn_dim), BF16)] * 3 + [jax.ShapeDtypeStruct((n, conv_dim), BF16)]
    return pl.pallas_call(
        functools.partial(_inproj_kernel, layer=layer, tiles_per_seq=seq // tm, attn_dim=attn_dim,
                          conv_dim=conv_dim, n_cast=len(later_weights)),
        grid=(n_steps,),
        in_specs=[
            pl.BlockSpec((tm, d), row),
            pl.BlockSpec(gains.shape, const),
            pl.BlockSpec((None, d, cols), lambda i: (layer, 0, 0), pipeline_mode=pl.Buffered(1)),
            pl.BlockSpec(q_gains.shape, const),
            pl.BlockSpec(k_gains.shape, const),
            pl.BlockSpec(conv_w.shape, lambda i: (0, 0, 0)),
        ] + cast_in_specs,
        out_specs=[pl.BlockSpec((tm, attn_dim), row)] * 3 + [pl.BlockSpec((tm, conv_dim), row)] + cast_out_specs,
        out_shape=out_shapes + cast_shapes,
        scratch_shapes=[pltpu.VMEM((CONV_SHIFTS, SUBLANES, conv_dim), F32), pltpu.VMEM((d, cols), BF16)],
        compiler_params=pltpu.CompilerParams(
            dimension_semantics=("arbitrary",), vmem_limit_bytes=VMEM_LIMIT),
    )(x2d, gains, w_in, q_gains, k_gains, conv_w, *later_weights)


def _min_to_vreg(arrays):
    chunks = [a[r:r + SUBLANES] for a in arrays for r in range(0, a.shape[0], SUBLANES)]
    while len(chunks) > 1:
        odd = chunks[-1:] if len(chunks) % 2 else []
        chunks = [jnp.minimum(a, b) for a, b in zip(chunks[0::2], chunks[1::2])] + odd
    return chunks[0]


def _attn_kernel(q_ref, k_ref, v_ref, o_ref, acc_ref, carry_ref, min_ref):
    n_tiles = q_ref.shape[0] // SUB
    n_blocks = q_ref.shape[1] // LANES
    i = pl.program_id(1)
    assert n_tiles >= NEAR_SUBS + 1

    r = lax.broadcasted_iota(jnp.int32, (2 * SUB, 2 * SUB), 0) % SUB
    c = lax.broadcasted_iota(jnp.int32, (2 * SUB, 2 * SUB), 1)
    tri = jnp.where((c >= SUB) | (r >= c), 1.0, 0.0).astype(BF16)
    causal = (lax.broadcasted_iota(jnp.int32, (HEADS_PER_BLOCK * SUB, SUB), 1)
              < lax.broadcasted_iota(jnp.int32, (HEADS_PER_BLOCK * SUB, SUB), 0) % SUB)

    def tile_rows(t, row0, nrows):
        start = t * SUB + row0
        return pl.ds(start if isinstance(start, int) else pl.multiple_of(start, BF16_SUBLANES), nrows)

    def head_rows(row0, nrows):
        return [slice(h * SUB + row0, h * SUB + row0 + nrows) for h in range(HEADS_PER_BLOCK)]

    def softplus_split(z_s, masked):
        e = jnp.exp2(jnp.abs(z_s) * (-LOG2E))
        sp = jnp.maximum(z_s, 0.0) + jnp.log(1.0 + e)
        if masked:
            sp = jnp.where(causal[:z_s.shape[0]], sp, 0.0)
        hi = sp.astype(BF16)
        lo = (sp - hi.astype(F32)).astype(BF16)
        return jnp.concatenate([hi, lo], axis=1)

    def span(jobs, whole_tiles):
        z_all, cs_all = {}, {}
        for n, (t, row0, nrows, start, nsub, diag, first) in enumerate(jobs):
            assert not diag or (row0 == 0 and nrows == SUB)
            qrows = tile_rows(t, row0, nrows)
            lane = lax.broadcasted_iota(jnp.int32, (nrows, LANES), 1)
            for p in range(n_blocks):
                cols = slice(p * LANES, (p + 1) * LANES)
                q = q_ref[qrows, cols]
                q2 = jnp.concatenate([jnp.where((lane // HEAD_DIM) == h, q, jnp.zeros_like(q))
                                      for h in range(HEADS_PER_BLOCK)], axis=0)
                z = lax.dot_general(q2, k_ref[pl.ds(start, nsub * SUB), cols], (((1,), (1,)), ((), ())),
                                    preferred_element_type=F32)
                z_all[n, p] = z
                lhs_parts = [softplus_split(z[:, s * SUB:(s + 1) * SUB], diag and s == nsub - 1)
                             for s in reversed(range(nsub))]
                cs_all[n, p] = jnp.dot(jnp.concatenate(lhs_parts, axis=0), tri, preferred_element_type=F32)
        latest = {}
        for n, (t, row0, nrows, start, nsub, diag, first) in enumerate(jobs):
            qrows = tile_rows(t, row0, nrows)
            lane = lax.broadcasted_iota(jnp.int32, (nrows, LANES), 1)
            rows2 = HEADS_PER_BLOCK * nrows
            for p in range(n_blocks):
                cols = slice(p * LANES, (p + 1) * LANES)
                carry = None if first else jnp.concatenate([carry_ref[t, p, rs] for rs in head_rows(row0, nrows)], axis=0)
                a_cols = [None] * nsub
                for idx, s in enumerate(reversed(range(nsub))):
                    cs = cs_all[n, p][idx * rows2:(idx + 1) * rows2]
                    g = cs[:, :SUB] if carry is None else cs[:, :SUB] + carry
                    carry = cs[:, SUB:] if carry is None else carry + cs[:, SUB:]
                    a = jnp.exp2((z_all[n, p][:, s * SUB:(s + 1) * SUB] - g) * LOG2E)
                    if diag and s == nsub - 1:
                        a = jnp.where(causal, a, 0.0)
                    a_cols[s] = a.astype(BF16)
                for h, rs in enumerate(head_rows(row0, nrows)):
                    carry_ref[t, p, rs] = carry[h * nrows:(h + 1) * nrows]
                latest.setdefault(n, (t, {}))[1].setdefault(p, []).append((row0, nrows, carry))
                o2 = jnp.dot(jnp.concatenate(a_cols, axis=1), v_ref[pl.ds(start, nsub * SUB), cols],
                             preferred_element_type=F32)
                out = o2[0:nrows]
                for h in range(1, HEADS_PER_BLOCK):
                    out = jnp.where((lane // HEAD_DIM) == h, o2[h * nrows:(h + 1) * nrows], out)
                if first:
                    acc_ref[qrows, cols] = out
                else:
                    acc_ref[qrows, cols] += out
        for t, by_block in latest.values():
            if whole_tiles:
                assert all(len(e) == 1 and e[0][:2] == (0, SUB) for e in by_block.values())
                min_ref[t] = _min_to_vreg([e[0][2] for e in by_block.values()])
            else:
                min_ref[t] = _min_to_vreg([carry_ref[t, p] for p in range(n_blocks)])

    def window(tiles):
        n_keys = (NEAR_SUBS + 2) * SUB
        far = [slice(h * SUB, h * SUB + FAR_ROWS) for h in range(HEADS_PER_BLOCK)]
        rest = [slice(h * SUB + FAR_ROWS, (h + 1) * SUB) for h in range(HEADS_PER_BLOCK)]
        lane = lax.broadcasted_iota(jnp.int32, (SUB, LANES), 1)
        lane_far = lax.broadcasted_iota(jnp.int32, (FAR_ROWS, LANES), 1)
        ctx, scores, suffix_sums, outputs = {}, [], [], []

        def score(t, tile, p):
            start = pl.multiple_of((tile - NEAR_SUBS - 1) * SUB, SUB)
            cols = slice(p * LANES, (p + 1) * LANES)
            q = q_ref[t * SUB:(t + 1) * SUB, cols]
            q2 = jnp.concatenate([jnp.where((lane // HEAD_DIM) == h, q, jnp.zeros_like(q))
                                  for h in range(HEADS_PER_BLOCK)], axis=0)
            z = lax.dot_general(q2, k_ref[pl.ds(start, n_keys), cols], (((1,), (1,)), ((), ())),
                                preferred_element_type=F32)
            ctx[t, p] = (start, z)

        def sums(t, p):
            start, z = ctx[t, p]
            z_far = jnp.concatenate([z[rs, 0:SUB] for rs in far], axis=0)
            lhs = [softplus_split(z[:, s * SUB:(s + 1) * SUB], s == NEAR_SUBS + 1)
                   for s in range(NEAR_SUBS + 1, 0, -1)] + [softplus_split(z_far, False)]
            cs = jnp.dot(jnp.concatenate(lhs, axis=0), tri, preferred_element_type=F32)
            ctx[t, p] = (start, z, z_far, cs)

        def output(t, p, parts):
            cols = slice(p * LANES, (p + 1) * LANES)
            start, z, z_far, cs = ctx[t, p]
            carry, a_cols = None, []
            for idx, s in enumerate(range(NEAR_SUBS + 1, 0, -1)):
                blk = cs[idx * 2 * SUB:(idx + 1) * 2 * SUB]
                g = blk[:, :SUB] if carry is None else blk[:, :SUB] + carry
                carry = blk[:, SUB:] if carry is None else carry + blk[:, SUB:]
                a = jnp.exp2((z[:, s * SUB:(s + 1) * SUB] - g) * LOG2E)
                if s == NEAR_SUBS + 1:
                    a = jnp.where(causal, a, 0.0)
                a_cols.insert(0, a.astype(BF16))
            blk = cs[(NEAR_SUBS + 1) * 2 * SUB:]
            carry_far = jnp.concatenate([carry[rs] for rs in far], axis=0)
            a_far = jnp.exp2((z_far - (blk[:, :SUB] + carry_far)) * LOG2E).astype(BF16)
            carry_far = carry_far + blk[:, SUB:]
            for h in range(HEADS_PER_BLOCK):
                carry_ref[t, p, far[h]] = carry_far[h * FAR_ROWS:(h + 1) * FAR_ROWS]
                carry_ref[t, p, rest[h]] = carry[rest[h]]
                parts.append(carry[rest[h]])
            parts.append(carry_far)
            if p == n_blocks - 1:
                min_ref[t] = _min_to_vreg(parts)
            o2 = jnp.dot(jnp.concatenate(a_cols, axis=1), v_ref[pl.ds(start + SUB, (NEAR_SUBS + 1) * SUB), cols],
                         preferred_element_type=F32)
            o_far = jnp.dot(a_far, v_ref[pl.ds(start, SUB), cols], preferred_element_type=F32)
            out, out_far = o2[0:SUB], o_far[0:FAR_ROWS]
            for h in range(1, HEADS_PER_BLOCK):
                out = jnp.where((lane // HEAD_DIM) == h, o2[h * SUB:(h + 1) * SUB], out)
                out_far = jnp.where((lane_far // HEAD_DIM) == h,
                                    o_far[h * FAR_ROWS:(h + 1) * FAR_ROWS], out_far)
            out = jnp.concatenate([out[:FAR_ROWS] + out_far, out[FAR_ROWS:]], axis=0)
            acc_ref[t * SUB:(t + 1) * SUB, cols] = out
            o_ref[t * SUB:(t + 1) * SUB, cols] = out.astype(o_ref.dtype)

        parts = {t: [] for t, _ in tiles}
        for p in range(n_blocks):
            for t, tile in tiles:
                scores.append(functools.partial(score, t, tile, p))
                suffix_sums.append(functools.partial(sums, t, p))
                outputs.append(functools.partial(output, t, p, parts[t]))
        for stage in scores:
            stage()
        for n in range(len(suffix_sums) + OUTPUT_LAG):
            if n < len(suffix_sums):
                suffix_sums[n]()
            if n >= OUTPUT_LAG:
                outputs[n - OUTPUT_LAG]()

    @pl.when(i == 0)
    def _():
        early = [t for t in range(n_tiles) if t <= NEAR_SUBS]
        span([(t, 0, SUB, 0, t + 1, True, True) for t in early], True)
        window([(t, t) for t in range(n_tiles) if t > NEAR_SUBS])
        for t in early:
            o_ref[t * SUB:(t + 1) * SUB, :] = acc_ref[t * SUB:(t + 1) * SUB, :].astype(o_ref.dtype)

    @pl.when(i > 0)
    def _():
        window([(t, i * n_tiles + t) for t in range(n_tiles)])

    @pl.when(jnp.min(min_ref[...], axis=(0, 1))[0] < SKIP_ABOVE)
    def _():
        def walk_tile(t, _):
            tile = i * n_tiles + t
            if FAR_ROWS < SUB:
                @pl.when(jnp.logical_and(tile > NEAR_SUBS, jnp.min(min_ref[t]) < SKIP_ABOVE))
                def _():
                    start = pl.multiple_of((tile - NEAR_SUBS - 1) * SUB, SUB)
                    span([(t, FAR_ROWS, SUB - FAR_ROWS, start, 1, False, False)], False)

            def unfinished():
                return (jnp.min(min_ref[t]) < SKIP_ABOVE).astype(jnp.int32)

            def cond(state):
                rem, go = state
                return jnp.logical_and(rem >= WALK_SUBS * SUB, go > 0)

            def body(state):
                rem, _ = state
                start = pl.multiple_of(rem - WALK_SUBS * SUB, SUB)
                span([(t, 0, SUB, start, WALK_SUBS, False, False)], False)
                return start, unfinished()

            remaining = jnp.maximum(tile - NEAR_SUBS - 1, 0) * SUB
            rem, go = lax.while_loop(cond, body, (remaining, unfinished()))
            for tail in range(1, WALK_SUBS):
                @pl.when(jnp.logical_and(rem == tail * SUB, go > 0))
                def _(tail=tail):
                    span([(t, 0, SUB, 0, tail, False, False)], False)
            return 0

        lax.fori_loop(0, n_tiles, walk_tile, 0)
        o_ref[...] = acc_ref[...].astype(o_ref.dtype)


def _attention(q, k, v, *, batch, seq):
    n, attn_dim = q.shape
    tq = ATTN_TILES * SUB
    nq = seq // tq
    return pl.pallas_call(
        _attn_kernel,
        grid=(batch, nq),
        in_specs=[
            pl.BlockSpec((tq, attn_dim), lambda b, i: (b * nq + i, 0)),
            pl.BlockSpec((seq, attn_dim), lambda b, i: (b, 0)),
            pl.BlockSpec((seq, attn_dim), lambda b, i: (b, 0)),
        ],
        out_specs=pl.BlockSpec((tq, attn_dim), lambda b, i: (b * nq + i, 0)),
        out_shape=jax.ShapeDtypeStruct((n, attn_dim), BF16),
        scratch_shapes=[pltpu.VMEM((tq, attn_dim), F32),
                        pltpu.VMEM((ATTN_TILES, attn_dim // LANES, HEADS_PER_BLOCK * SUB, SUB), F32),
                        pltpu.VMEM((ATTN_TILES, SUBLANES, SUB), F32)],
        compiler_params=pltpu.CompilerParams(
            dimension_semantics=("arbitrary", "arbitrary"), vmem_limit_bytes=VMEM_LIMIT),
    )(q, k, v)


def _ffn_kernel(x_ref, attn_ref, conv_ref, wo_ref, gain_ref, wg_ref, wu_ref, wd_ref, o_ref, act_ref, *, layer):
    attn_dim = attn_ref.shape[1]
    d_ff = wg_ref.shape[1]
    mix = (jnp.dot(attn_ref[...], wo_ref[0:attn_dim, :], preferred_element_type=F32)
           + jnp.dot(conv_ref[...], wo_ref[attn_dim:, :], preferred_element_type=F32))
    x1 = x_ref[...] + mix
    h = ((x1 * _rms_scale(x1)) * gain_ref[layer:layer + 1, :]).astype(BF16)
    for c0 in range(0, d_ff, FF_CHUNK):
        g = jnp.dot(h, wg_ref[:, c0:c0 + FF_CHUNK], preferred_element_type=F32)
        u = jnp.dot(h, wu_ref[:, c0:c0 + FF_CHUNK], preferred_element_type=F32)
        act_ref[:, c0:c0 + FF_CHUNK] = ((g * jax.nn.sigmoid(g)) * u).astype(BF16)
    o_ref[...] = x1 + jnp.dot(act_ref[...], wd_ref[...], preferred_element_type=F32)


def _ffn(x2d, attn, conv, w_out, gains, layer, w_gate, w_up, w_down):
    n, d = x2d.shape
    tm = FFN_ROW_TILE
    d_ff = w_gate.shape[1]
    const = lambda i: (0, 0)
    row = lambda i: (i, 0)
    resident = functools.partial(pl.BlockSpec, index_map=const, pipeline_mode=pl.Buffered(1))
    return pl.pallas_call(
        functools.partial(_ffn_kernel, layer=layer),
        grid=(n // tm,),
        in_specs=[
            pl.BlockSpec((tm, d), row),
            pl.BlockSpec((tm, attn.shape[1]), row),
            pl.BlockSpec((tm, conv.shape[1]), row),
            resident(w_out.shape),
            pl.BlockSpec(gains.shape, const),
            resident(w_gate.shape),
            resident(w_up.shape),
            resident(w_down.shape),
        ],
        out_specs=pl.BlockSpec((tm, d), row),
        out_shape=jax.ShapeDtypeStruct((n, d), F32),
        scratch_shapes=[pltpu.VMEM((tm, d_ff), BF16)],
        compiler_params=pltpu.CompilerParams(
            dimension_semantics=("arbitrary",), vmem_limit_bytes=VMEM_LIMIT),
    )(x2d, attn, conv, w_out, gains, w_gate, w_up, w_down)


def kernel(x, norm_mix, w_in, q_norm, k_norm, conv_w, w_out, norm_ffn, w_gate, w_up, w_down):
    batch, seq, d = x.shape
    depth = w_in.shape[0]
    conv_dim = conv_w.shape[2]
    attn_dim = w_out.shape[1] - conv_dim
    assert seq % ROW_TILE == 0 and seq % (ATTN_TILES * SUB) == 0 and attn_dim % MXU_DIM == 0
    assert (batch * seq) % FFN_ROW_TILE == 0
    assert w_gate.shape[2] % FF_CHUNK == 0 and conv_w.shape[1] == CONV_SHIFTS + 1

    xf = x.reshape(batch * seq, d)
    for l in range(depth):
        q, k, v, conv, wo, wg, wu, wd = _inproj(
            xf, norm_mix, w_in, l, q_norm, k_norm, conv_w,
            (w_out, w_gate, w_up, w_down), seq=seq, attn_dim=attn_dim, conv_dim=conv_dim)
        attn = _attention(q, k, v, batch=batch, seq=seq)
        xf = _ffn(xf, attn, conv, wo, norm_ffn, l, wg, wu, wd)
    return xf.reshape(batch, seq, d)
```

```python
import functools

import jax
import jax.numpy as jnp
from jax import lax
from jax.experimental import pallas as pl
from jax.experimental.pallas import tpu as pltpu

F32 = jnp.float32
BF16 = jnp.bfloat16

HEAD_DIM = 64
EPS = 1e-6
LANES = 128
SUBLANES = 8
BF16_SUBLANES = 16
W_CAST_ROWS = 128
MXU_DIM = 256
HEADS_PER_BLOCK = LANES // HEAD_DIM
SUB = 128
LOG2E = 1.4426950408889634
SKIP_ABOVE = 104.0

ROW_TILE = 1024
FFN_ROW_TILE = 1024
ATTN_TILES = 4
NEAR_SUBS = 1
FAR_ROWS = 32
OUTPUT_LAG = 2
WALK_SUBS = 2
FF_CHUNK = 256
CONV_SHIFTS = 2
VMEM_LIMIT = 56 * 1024 * 1024


def _rms_scale(x):
    return lax.rsqrt(jnp.mean(x * x, axis=-1, keepdims=True) + EPS)


def _inproj_kernel(x_ref, gain_ref, w_ref, qg_ref, kg_ref, cw_ref, *rest,
                   layer, tiles_per_seq, attn_dim, conv_dim, n_cast):
    cast_in, (q_ref, k_ref, v_ref, c_ref) = rest[:n_cast], rest[n_cast:n_cast + 4]
    cast_out, (halo_ref, wbf_ref) = rest[n_cast + 4:2 * n_cast + 4], rest[2 * n_cast + 4:]
    tm = x_ref.shape[0]
    i = pl.program_id(0)

    @pl.when(i == 0)
    def _():
        def cast_rows(c, _):
            rows = pl.ds(pl.multiple_of(c * W_CAST_ROWS, W_CAST_ROWS), W_CAST_ROWS)
            wbf_ref[rows, :] = w_ref[rows, :].astype(BF16)
            return 0
        lax.fori_loop(0, w_ref.shape[0] // W_CAST_ROWS, cast_rows, 0)
        halo_ref[...] = jnp.zeros_like(halo_ref)

    r = lax.broadcasted_iota(jnp.int32, (MXU_DIM, MXU_DIM), 0) // HEAD_DIM
    c = lax.broadcasted_iota(jnp.int32, (MXU_DIM, MXU_DIM), 1) // HEAD_DIM
    seg = jnp.where(r == c, 1.0, 0.0).astype(BF16)
    taps = [cw_ref[layer, n:n + 1, :] for n in range(CONV_SHIFTS + 1)]
    row = lax.broadcasted_iota(jnp.int32, (SUBLANES, conv_dim), 0)

    def head_norm(p, gain_ref):
        gain = jnp.concatenate([gain_ref[layer:layer + 1, :]] * (attn_dim // HEAD_DIM), axis=1)
        p2 = (p * p).astype(BF16)
        ssq = jnp.concatenate(
            [jnp.dot(p2[:, s:s + MXU_DIM], seg, preferred_element_type=F32)
             for s in range(0, attn_dim, MXU_DIM)], axis=1)
        return (p * lax.rsqrt(ssq * (1.0 / HEAD_DIM) + EPS)) * gain

    x = x_ref[...]
    h = ((x * _rms_scale(x)) * gain_ref[layer:layer + 1, :]).astype(BF16)

    def proj(c0, width):
        return jnp.dot(h, wbf_ref[:, c0:c0 + width], preferred_element_type=F32)

    cb = proj(3 * attn_dim, conv_dim)
    hh = proj(3 * attn_dim + conv_dim, conv_dim) * proj(3 * attn_dim + 2 * conv_dim, conv_dim)

    q_ref[...] = (head_norm(proj(0, attn_dim), qg_ref) * (HEAD_DIM ** -0.5)).astype(BF16)
    k_ref[...] = head_norm(proj(attn_dim, attn_dim), kg_ref).astype(BF16)
    v_ref[...] = proj(2 * attn_dim, attn_dim).astype(BF16)

    def shift_down(a, n):
        prev_last = jnp.where(i % tiles_per_seq == 0, 0.0, halo_ref[n, SUBLANES - 1:SUBLANES, :])
        halo_ref[n] = a[tm - SUBLANES:tm, :]
        rolled = pltpu.roll(a, 1, 0)
        head = jnp.where(row == 0, jnp.broadcast_to(prev_last, (SUBLANES, conv_dim)), rolled[0:SUBLANES])
        return jnp.concatenate([head, rolled[SUBLANES:]], axis=0)

    y = taps[2] * hh + shift_down(taps[1] * hh + shift_down(taps[0] * hh, 0), 1)
    c_ref[...] = (cb * y).astype(BF16)

    for src, dst in zip(cast_in, cast_out):
        dst[...] = src[...].astype(BF16)


def _cast_specs(weights, layer, n_steps, step_of):
    in_specs, out_specs, shapes = [], [], []
    for w in weights:
        _, rows, cols = w.shape
        chunks = max(c for c in range(1, n_steps + 1) if rows % c == 0 and (rows // c) % BF16_SUBLANES == 0)
        chunk_of = lambda *idx, chunks=chunks: jnp.minimum(step_of(*idx), chunks - 1)
        in_specs.append(pl.BlockSpec((None, rows // chunks, cols),
                                     lambda *idx, chunk_of=chunk_of: (layer, chunk_of(*idx), 0)))
        out_specs.append(pl.BlockSpec((rows // chunks, cols), lambda *idx, chunk_of=chunk_of: (chunk_of(*idx), 0)))
        shapes.append(jax.ShapeDtypeStruct((rows, cols), BF16))
    return in_specs, out_specs, shapes


def _inproj(x2d, gains, w_in, layer, q_gains, k_gains, conv_w, later_weights, *, seq, attn_dim, conv_dim):
    n, d = x2d.shape
    tm = ROW_TILE
    n_steps = n // tm
    cols = w_in.shape[2]
    const = lambda i: (0, 0)
    row = lambda i: (i, 0)
    cast_in_specs, cast_out_specs, cast_shapes = _cast_specs(later_weights, layer, n_steps, lambda i: i)
    out_shapes = [jax.ShapeDtypeStruct((n, attn_dim), BF16)] * 3 + [jax.ShapeDtypeStruct((n, conv_dim), BF16)]
    return pl.pallas_call(
        functools.partial(_inproj_kernel, layer=layer, tiles_per_seq=seq // tm, attn_dim=attn_dim,
                          conv_dim=conv_dim, n_cast=len(later_weights)),
        grid=(n_steps,),
        in_specs=[
            pl.BlockSpec((tm, d), row),
            pl.BlockSpec(gains.shape, const),
            pl.BlockSpec((None, d, cols), lambda i: (layer, 0, 0), pipeline_mode=pl.Buffered(1)),
            pl.BlockSpec(q_gains.shape, const),
            pl.BlockSpec(k_gains.shape, const),
            pl.BlockSpec(conv_w.shape, lambda i: (0, 0, 0)),
        ] + cast_in_specs,
        out_specs=[pl.BlockSpec((tm, attn_dim), row)] * 3 + [pl.BlockSpec((tm, conv_dim), row)] + cast_out_specs,
        out_shape=out_shapes + cast_shapes,
        scratch_shapes=[pltpu.VMEM((CONV_SHIFTS, SUBLANES, conv_dim), F32), pltpu.VMEM((d, cols), BF16)],
        compiler_params=pltpu.CompilerParams(
            dimension_semantics=("arbitrary",), vmem_limit_bytes=VMEM_LIMIT),
    )(x2d, gains, w_in, q_gains, k_gains, conv_w, *later_weights)


def _min_to_vreg(arrays):
    chunks = [a[r:r + SUBLANES] for a in arrays for r in range(0, a.shape[0], SUBLANES)]
    while len(chunks) > 1:
        odd = chunks[-1:] if len(chunks) % 2 else []
        chunks = [jnp.minimum(a, b) for a, b in zip(chunks[0::2], chunks[1::2])] + odd
    return chunks[0]


def _attn_kernel(q_ref, k_ref, v_ref, o_ref, acc_ref, carry_ref, min_ref):
    n_tiles = q_ref.shape[0] // SUB
    n_blocks = q_ref.shape[1] // LANES
    i = pl.program_id(1)
    assert n_tiles >= NEAR_SUBS + 1

    r = lax.broadcasted_iota(jnp.int32, (2 * SUB, 2 * SUB), 0) % SUB
    c = lax.broadcasted_iota(jnp.int32, (2 * SUB, 2 * SUB), 1)
    tri = jnp.where((c >= SUB) | (r >= c), 1.0, 0.0).astype(BF16)
    causal = (lax.broadcasted_iota(jnp.int32, (HEADS_PER_BLOCK * SUB, SUB), 1)
              < lax.broadcasted_iota(jnp.int32, (HEADS_PER_BLOCK * SUB, SUB), 0) % SUB)

    def tile_rows(t, row0, nrows):
        start = t * SUB + row0
        return pl.ds(start if isinstance(start, int) else pl.multiple_of(start, BF16_SUBLANES), nrows)

    def head_rows(row0, nrows):
        return [slice(h * SUB + row0, h * SUB + row0 + nrows) for h in range(HEADS_PER_BLOCK)]

    def softplus_split(z_s, masked):
        e = jnp.exp2(jnp.abs(z_s) * (-LOG2E))
        sp = jnp.maximum(z_s, 0.0) + jnp.log(1.0 + e)
        if masked:
            sp = jnp.where(causal[:z_s.shape[0]], sp, 0.0)
        hi = sp.astype(BF16)
        lo = (sp - hi.astype(F32)).astype(BF16)
        return jnp.concatenate([hi, lo], axis=1)

    def span(jobs, whole_tiles):
        z_all, cs_all = {}, {}
        for n, (t, row0, nrows, start, nsub, diag, first) in enumerate(jobs):
            assert not diag or (row0 == 0 and nrows == SUB)
            qrows = tile_rows(t, row0, nrows)
            lane = lax.broadcasted_iota(jnp.int32, (nrows, LANES), 1)
            for p in range(n_blocks):
                cols = slice(p * LANES, (p + 1) * LANES)
                q = q_ref[qrows, cols]
                q2 = jnp.concatenate([jnp.where((lane // HEAD_DIM) == h, q, jnp.zeros_like(q))
                                      for h in range(HEADS_PER_BLOCK)], axis=0)
                z = lax.dot_general(q2, k_ref[pl.ds(start, nsub * SUB), cols], (((1,), (1,)), ((), ())),
                                    preferred_element_type=F32)
                z_all[n, p] = z
                lhs_parts = [softplus_split(z[:, s * SUB:(s + 1) * SUB], diag and s == nsub - 1)
                             for s in reversed(range(nsub))]
                cs_all[n, p] = jnp.dot(jnp.concatenate(lhs_parts, axis=0), tri, preferred_element_type=F32)
        latest = {}
        for n, (t, row0, nrows, start, nsub, diag, first) in enumerate(jobs):
            qrows = tile_rows(t, row0, nrows)
            lane = lax.broadcasted_iota(jnp.int32, (nrows, LANES), 1)
            rows2 = HEADS_PER_BLOCK * nrows
            for p in range(n_blocks):
                cols = slice(p * LANES, (p + 1) * LANES)
                carry = None if first else jnp.concatenate([carry_ref[t, p, rs] for rs in head_rows(row0, nrows)], axis=0)
                a_cols = [None] * nsub
                for idx, s in enumerate(reversed(range(nsub))):
                    cs = cs_all[n, p][idx * rows2:(idx + 1) * rows2]
                    g = cs[:, :SUB] if carry is None else cs[:, :SUB] + carry
                    carry = cs[:, SUB:] if carry is None else carry + cs[:, SUB:]
                    a = jnp.exp2((z_all[n, p][:, s * SUB:(s + 1) * SUB] - g) * LOG2E)
                    if diag and s == nsub - 1:
                        a = jnp.where(causal, a, 0.0)
                    a_cols[s] = a.astype(BF16)
                for h, rs in enumerate(head_rows(row0, nrows)):
                    carry_ref[t, p, rs] = carry[h * nrows:(h + 1) * nrows]
                latest.setdefault(n, (t, {}))[1].setdefault(p, []).append((row0, nrows, carry))
                o2 = jnp.dot(jnp.concatenate(a_cols, axis=1), v_ref[pl.ds(start, nsub * SUB), cols],
                             preferred_element_type=F32)
                out = o2[0:nrows]
                for h in range(1, HEADS_PER_BLOCK):
                    out = jnp.where((lane // HEAD_DIM) == h, o2[h * nrows:(h + 1) * nrows], out)
                if first:
                    acc_ref[qrows, cols] = out
                else:
                    acc_ref[qrows, cols] += out
        for t, by_block in latest.values():
            if whole_tiles:
                assert all(len(e) == 1 and e[0][:2] == (0, SUB) for e in by_block.values())
                min_ref[t] = _min_to_vreg([e[0][2] for e in by_block.values()])
            else:
                min_ref[t] = _min_to_vreg([carry_ref[t, p] for p in range(n_blocks)])

    def window(tiles):
        n_keys = (NEAR_SUBS + 2) * SUB
        far = [slice(h * SUB, h * SUB + FAR_ROWS) for h in range(HEADS_PER_BLOCK)]
        rest = [slice(h * SUB + FAR_ROWS, (h + 1) * SUB) for h in range(HEADS_PER_BLOCK)]
        lane = lax.broadcasted_iota(jnp.int32, (SUB, LANES), 1)
        lane_far = lax.broadcasted_iota(jnp.int32, (FAR_ROWS, LANES), 1)
        ctx, scores, suffix_sums, outputs = {}, [], [], []

        def score(t, tile, p):
            start = pl.multiple_of((tile - NEAR_SUBS - 1) * SUB, SUB)
            cols = slice(p * LANES, (p + 1) * LANES)
            q = q_ref[t * SUB:(t + 1) * SUB, cols]
            q2 = jnp.concatenate([jnp.where((lane // HEAD_DIM) == h, q, jnp.zeros_like(q))
                                  for h in range(HEADS_PER_BLOCK)], axis=0)
            z = lax.dot_general(q2, k_ref[pl.ds(start, n_keys), cols], (((1,), (1,)), ((), ())),
                                preferred_element_type=F32)
            ctx[t, p] = (start, z)

        def sums(t, p):
            start, z = ctx[t, p]
            z_far = jnp.concatenate([z[rs, 0:SUB] for rs in far], axis=0)
            lhs = [softplus_split(z[:, s * SUB:(s + 1) * SUB], s == NEAR_SUBS + 1)
                   for s in range(NEAR_SUBS + 1, 0, -1)] + [softplus_split(z_far, False)]
            cs = jnp.dot(jnp.concatenate(lhs, axis=0), tri, preferred_element_type=F32)
            ctx[t, p] = (start, z, z_far, cs)

        def output(t, p, parts):
            cols = slice(p * LANES, (p + 1) * LANES)
            start, z, z_far, cs = ctx[t, p]
            carry, a_cols = None, []
            for idx, s in enumerate(range(NEAR_SUBS + 1, 0, -1)):
                blk = cs[idx * 2 * SUB:(idx + 1) * 2 * SUB]
                g = blk[:, :SUB] if carry is None else blk[:, :SUB] + carry
                carry = blk[:, SUB:] if carry is None else carry + blk[:, SUB:]
                a = jnp.exp2((z[:, s * SUB:(s + 1) * SUB] - g) * LOG2E)
                if s == NEAR_SUBS + 1:
                    a = jnp.where(causal, a, 0.0)
                a_cols.insert(0, a.astype(BF16))
            blk = cs[(NEAR_SUBS + 1) * 2 * SUB:]
            carry_far = jnp.concatenate([carry[rs] for rs in far], axis=0)
            a_far = jnp.exp2((z_far - (blk[:, :SUB] + carry_far)) * LOG2E).astype(BF16)
            carry_far = carry_far + blk[:, SUB:]
            for h in range(HEADS_PER_BLOCK):
                carry_ref[t, p, far[h]] = carry_far[h * FAR_ROWS:(h + 1) * FAR_ROWS]
                carry_ref[t, p, rest[h]] = carry[rest[h]]
                parts.append(carry[rest[h]])
            parts.append(carry_far)
            if p == n_blocks - 1:
                min_ref[t] = _min_to_vreg(parts)
            o2 = jnp.dot(jnp.concatenate(a_cols, axis=1), v_ref[pl.ds(start + SUB, (NEAR_SUBS + 1) * SUB), cols],
                         preferred_element_type=F32)
            o_far = jnp.dot(a_far, v_ref[pl.ds(start, SUB), cols], preferred_element_type=F32)
            out, out_far = o2[0:SUB], o_far[0:FAR_ROWS]
            for h in range(1, HEADS_PER_BLOCK):
                out = jnp.where((lane // HEAD_DIM) == h, o2[h * SUB:(h + 1) * SUB], out)
                out_far = jnp.where((lane_far // HEAD_DIM) == h,
                                    o_far[h * FAR_ROWS:(h + 1) * FAR_ROWS], out_far)
            out = jnp.concatenate([out[:FAR_ROWS] + out_far, out[FAR_ROWS:]], axis=0)
            acc_ref[t * SUB:(t + 1) * SUB, cols] = out
            o_ref[t * SUB:(t + 1) * SUB, cols] = out.astype(o_ref.dtype)

        parts = {t: [] for t, _ in tiles}
        for p in range(n_blocks):
            for t, tile in tiles:
                scores.append(functools.partial(score, t, tile, p))
                suffix_sums.append(functools.partial(sums, t, p))
                outputs.append(functools.partial(output, t, p, parts[t]))
        for stage in scores:
            stage()
        for n in range(len(suffix_sums) + OUTPUT_LAG):
            if n < len(suffix_sums):
                suffix_sums[n]()
            if n >= OUTPUT_LAG:
                outputs[n - OUTPUT_LAG]()

    @pl.when(i == 0)
    def _():
        early = [t for t in range(n_tiles) if t <= NEAR_SUBS]
        span([(t, 0, SUB, 0, t + 1, True, True) for t in early], True)
        window([(t, t) for t in range(n_tiles) if t > NEAR_SUBS])
        for t in early:
            o_ref[t * SUB:(t + 1) * SUB, :] = acc_ref[t * SUB:(t + 1) * SUB, :].astype(o_ref.dtype)

    @pl.when(i > 0)
    def _():
        window([(t, i * n_tiles + t) for t in range(n_tiles)])

    @pl.when(jnp.min(min_ref[...], axis=(0, 1))[0] < SKIP_ABOVE)
    def _():
        def walk_tile(t, _):
            tile = i * n_tiles + t
            if FAR_ROWS < SUB:
                @pl.when(jnp.logical_and(tile > NEAR_SUBS, jnp.min(min_ref[t]) < SKIP_ABOVE))
                def _():
                    start = pl.multiple_of((tile - NEAR_SUBS - 1) * SUB, SUB)
                    span([(t, FAR_ROWS, SUB - FAR_ROWS, start, 1, False, False)], False)

            def unfinished():
                return (jnp.min(min_ref[t]) < SKIP_ABOVE).astype(jnp.int32)

            def cond(state):
                rem, go = state
                return jnp.logical_and(rem >= WALK_SUBS * SUB, go > 0)

            def body(state):
                rem, _ = state
                start = pl.multiple_of(rem - WALK_SUBS * SUB, SUB)
                span([(t, 0, SUB, start, WALK_SUBS, False, False)], False)
                return start, unfinished()

            remaining = jnp.maximum(tile - NEAR_SUBS - 1, 0) * SUB
            rem, go = lax.while_loop(cond, body, (remaining, unfinished()))
            for tail in range(1, WALK_SUBS):
                @pl.when(jnp.logical_and(rem == tail * SUB, go > 0))
                def _(tail=tail):
                    span([(t, 0, SUB, 0, tail, False, False)], False)
            return 0

        lax.fori_loop(0, n_tiles, walk_tile, 0)
        o_ref[...] = acc_ref[...].astype(o_ref.dtype)


def _attention(q, k, v, *, batch, seq):
    n, attn_dim = q.shape
    tq = ATTN_TILES * SUB
    nq = seq // tq
    return pl.pallas_call(
        _attn_kernel,
        grid=(batch, nq),
        in_specs=[
            pl.BlockSpec((tq, attn_dim), lambda b, i: (b * nq + i, 0)),
            pl.BlockSpec((seq, attn_dim), lambda b, i: (b, 0)),
            pl.BlockSpec((seq, attn_dim), lambda b, i: (b, 0)),
        ],
        out_specs=pl.BlockSpec((tq, attn_dim), lambda b, i: (b * nq + i, 0)),
        out_shape=jax.ShapeDtypeStruct((n, attn_dim), BF16),
        scratch_shapes=[pltpu.VMEM((tq, attn_dim), F32),
                        pltpu.VMEM((ATTN_TILES, attn_dim // LANES, HEADS_PER_BLOCK * SUB, SUB), F32),
                        pltpu.VMEM((ATTN_TILES, SUBLANES, SUB), F32)],
        compiler_params=pltpu.CompilerParams(
            dimension_semantics=("parallel", "parallel"), vmem_limit_bytes=VMEM_LIMIT),
    )(q, k, v)


def _ffn_kernel(x_ref, attn_ref, conv_ref, wo_ref, gain_ref, wg_ref, wu_ref, wd_ref, o_ref, act_ref, *, layer):
    attn_dim = attn_ref.shape[1]
    d_ff = wg_ref.shape[1]
    mix = (jnp.dot(attn_ref[...], wo_ref[0:attn_dim, :], preferred_element_type=F32)
           + jnp.dot(conv_ref[...], wo_ref[attn_dim:, :], preferred_element_type=F32))
    x1 = x_ref[...] + mix
    h = ((x1 * _rms_scale(x1)) * gain_ref[layer:layer + 1, :]).astype(BF16)
    for c0 in range(0, d_ff, FF_CHUNK):
        g = jnp.dot(h, wg_ref[:, c0:c0 + FF_CHUNK], preferred_element_type=F32)
        u = jnp.dot(h, wu_ref[:, c0:c0 + FF_CHUNK], preferred_element_type=F32)
        act_ref[:, c0:c0 + FF_CHUNK] = ((g * jax.nn.sigmoid(g)) * u).astype(BF16)
    o_ref[...] = x1 + jnp.dot(act_ref[...], wd_ref[...], preferred_element_type=F32)


def _ffn(x2d, attn, conv, w_out, gains, layer, w_gate, w_up, w_down):
    n, d = x2d.shape
    tm = FFN_ROW_TILE
    d_ff = w_gate.shape[1]
    const = lambda i: (0, 0)
    row = lambda i: (i, 0)
    resident = functools.partial(pl.BlockSpec, index_map=const, pipeline_mode=pl.Buffered(1))
    return pl.pallas_call(
        functools.partial(_ffn_kernel, layer=layer),
        grid=(n // tm,),
        in_specs=[
            pl.BlockSpec((tm, d), row),
            pl.BlockSpec((tm, attn.shape[1]), row),
            pl.BlockSpec((tm, conv.shape[1]), row),
            resident(w_out.shape),
            pl.BlockSpec(gains.shape, const),
            resident(w_gate.shape),
            resident(w_up.shape),
            resident(w_down.shape),
        ],
        out_specs=pl.BlockSpec((tm, d), row),
        out_shape=jax.ShapeDtypeStruct((n, d), F32),
        scratch_shapes=[pltpu.VMEM((tm, d_ff), BF16)],
        compiler_params=pltpu.CompilerParams(
            dimension_semantics=("parallel",), vmem_limit_bytes=VMEM_LIMIT),
    )(x2d, attn, conv, w_out, gains, w_gate, w_up, w_down)


def kernel(x, norm_mix, w_in, q_norm, k_norm, conv_w, w_out, norm_ffn, w_gate, w_up, w_down):
    batch, seq, d = x.shape
    depth = w_in.shape[0]
    conv_dim = conv_w.shape[2]
    attn_dim = w_out.shape[1] - conv_dim
    assert seq % ROW_TILE == 0 and seq % (ATTN_TILES * SUB) == 0 and attn_dim % MXU_DIM == 0
    assert (batch * seq) % FFN_ROW_TILE == 0
    assert w_gate.shape[2] % FF_CHUNK == 0 and conv_w.shape[1] == CONV_SHIFTS + 1

    xf = x.reshape(batch * seq, d)
    for l in range(depth):
        q, k, v, conv, wo, wg, wu, wd = _inproj(
            xf, norm_mix, w_in, l, q_norm, k_norm, conv_w,
            (w_out, w_gate, w_up, w_down), seq=seq, attn_dim=attn_dim, conv_dim=conv_dim)
        attn = _attention(q, k, v, batch=batch, seq=seq)
        xf = _ffn(xf, attn, conv, wo, norm_ffn, l, wg, wu, wd)
    return xf.reshape(batch, seq, d)
```

```python
import functools

import jax
import jax.numpy as jnp
from jax import lax
from jax.experimental import pallas as pl
from jax.experimental.pallas import tpu as pltpu

F32 = jnp.float32
BF16 = jnp.bfloat16

HEAD_DIM = 64
EPS = 1e-6
LANES = 128
SUBLANES = 8
BF16_SUBLANES = 16
W_CAST_ROWS = 128
MXU_DIM = 256
HEADS_PER_BLOCK = LANES // HEAD_DIM
SUB = 128
LOG2E = 1.4426950408889634
SKIP_ABOVE = 104.0

ROW_TILE = 1024
FFN_ROW_TILE = 1024
ATTN_TILES = 4
NEAR_SUBS = 1
FAR_ROWS = 32
OUTPUT_LAG = 2
WALK_SUBS = 2
FF_CHUNK = 256
CONV_SHIFTS = 2
VMEM_LIMIT = 56 * 1024 * 1024
SMALL_VMEM_LIMIT = 48 * 1024 * 1024


def _rms_scale(x):
    return lax.rsqrt(jnp.mean(x * x, axis=-1, keepdims=True) + EPS)


def _inproj_kernel(x_ref, gain_ref, w_ref, qg_ref, kg_ref, cw_ref, *rest,
                   layer, tiles_per_seq, attn_dim, conv_dim, n_cast):
    cast_in, (q_ref, k_ref, v_ref, c_ref) = rest[:n_cast], rest[n_cast:n_cast + 4]
    cast_out, (halo_ref, wbf_ref) = rest[n_cast + 4:2 * n_cast + 4], rest[2 * n_cast + 4:]
    tm = x_ref.shape[0]
    i = pl.program_id(0)

    @pl.when(i == 0)
    def _():
        def cast_rows(c, _):
            rows = pl.ds(pl.multiple_of(c * W_CAST_ROWS, W_CAST_ROWS), W_CAST_ROWS)
            wbf_ref[rows, :] = w_ref[rows, :].astype(BF16)
            return 0
        lax.fori_loop(0, w_ref.shape[0] // W_CAST_ROWS, cast_rows, 0)
        halo_ref[...] = jnp.zeros_like(halo_ref)

    r = lax.broadcasted_iota(jnp.int32, (MXU_DIM, MXU_DIM), 0) // HEAD_DIM
    c = lax.broadcasted_iota(jnp.int32, (MXU_DIM, MXU_DIM), 1) // HEAD_DIM
    seg = jnp.where(r == c, 1.0, 0.0).astype(BF16)
    taps = [cw_ref[layer, n:n + 1, :] for n in range(CONV_SHIFTS + 1)]
    row = lax.broadcasted_iota(jnp.int32, (SUBLANES, conv_dim), 0)

    def head_norm(p, gain_ref):
        gain = jnp.concatenate([gain_ref[layer:layer + 1, :]] * (attn_dim // HEAD_DIM), axis=1)
        p2 = (p * p).astype(BF16)
        ssq = jnp.concatenate(
            [jnp.dot(p2[:, s:s + MXU_DIM], seg, preferred_element_type=F32)
             for s in range(0, attn_dim, MXU_DIM)], axis=1)
        return (p * lax.rsqrt(ssq * (1.0 / HEAD_DIM) + EPS)) * gain

    x = x_ref[...]
    h = ((x * _rms_scale(x)) * gain_ref[layer:layer + 1, :]).astype(BF16)

    def proj(c0, width):
        return jnp.dot(h, wbf_ref[:, c0:c0 + width], preferred_element_type=F32)

    cb = proj(3 * attn_dim, conv_dim)
    hh = proj(3 * attn_dim + conv_dim, conv_dim) * proj(3 * attn_dim + 2 * conv_dim, conv_dim)

    q_ref[...] = (head_norm(proj(0, attn_dim), qg_ref) * (HEAD_DIM ** -0.5)).astype(BF16)
    k_ref[...] = head_norm(proj(attn_dim, attn_dim), kg_ref).astype(BF16)
    v_ref[...] = proj(2 * attn_dim, attn_dim).astype(BF16)

    def shift_down(a, n):
        prev_last = jnp.where(i % tiles_per_seq == 0, 0.0, halo_ref[n, SUBLANES - 1:SUBLANES, :])
        halo_ref[n] = a[tm - SUBLANES:tm, :]
        rolled = pltpu.roll(a, 1, 0)
        head = jnp.where(row == 0, jnp.broadcast_to(prev_last, (SUBLANES, conv_dim)), rolled[0:SUBLANES])
        return jnp.concatenate([head, rolled[SUBLANES:]], axis=0)

    y = taps[2] * hh + shift_down(taps[1] * hh + shift_down(taps[0] * hh, 0), 1)
    c_ref[...] = (cb * y).astype(BF16)

    for src, dst in zip(cast_in, cast_out):
        dst[...] = src[...].astype(BF16)


def _cast_specs(weights, layer, n_steps, step_of):
    in_specs, out_specs, shapes = [], [], []
    for w in weights:
        _, rows, cols = w.shape
        chunks = max(c for c in range(1, n_steps + 1) if rows % c == 0 and (rows // c) % BF16_SUBLANES == 0)
        chunk_of = lambda *idx, chunks=chunks: jnp.minimum(step_of(*idx), chunks - 1)
        in_specs.append(pl.BlockSpec((None, rows // chunks, cols),
                                     lambda *idx, chunk_of=chunk_of: (layer, chunk_of(*idx), 0)))
        out_specs.append(pl.BlockSpec((rows // chunks, cols), lambda *idx, chunk_of=chunk_of: (chunk_of(*idx), 0)))
        shapes.append(jax.ShapeDtypeStruct((rows, cols), BF16))
    return in_specs, out_specs, shapes


def _inproj(x2d, gains, w_in, layer, q_gains, k_gains, conv_w, later_weights, *, seq, attn_dim, conv_dim):
    n, d = x2d.shape
    tm = ROW_TILE
    n_steps = n // tm
    cols = w_in.shape[2]
    const = lambda i: (0, 0)
    row = lambda i: (i, 0)
    cast_in_specs, cast_out_specs, cast_shapes = _cast_specs(later_weights, layer, n_steps, lambda i: i)
    out_shapes = [jax.ShapeDtypeStruct((n, attn_dim), BF16)] * 3 + [jax.ShapeDtypeStruct((n, conv_dim), BF16)]
    return pl.pallas_call(
        functools.partial(_inproj_kernel, layer=layer, tiles_per_seq=seq // tm, attn_dim=attn_dim,
                          conv_dim=conv_dim, n_cast=len(later_weights)),
        grid=(n_steps,),
        in_specs=[
            pl.BlockSpec((tm, d), row),
            pl.BlockSpec(gains.shape, const),
            pl.BlockSpec((None, d, cols), lambda i: (layer, 0, 0), pipeline_mode=pl.Buffered(1)),
            pl.BlockSpec(q_gains.shape, const),
            pl.BlockSpec(k_gains.shape, const),
            pl.BlockSpec(conv_w.shape, lambda i: (0, 0, 0)),
        ] + cast_in_specs,
        out_specs=[pl.BlockSpec((tm, attn_dim), row)] * 3 + [pl.BlockSpec((tm, conv_dim), row)] + cast_out_specs,
        out_shape=out_shapes + cast_shapes,
        scratch_shapes=[pltpu.VMEM((CONV_SHIFTS, SUBLANES, conv_dim), F32), pltpu.VMEM((d, cols), BF16)],
        compiler_params=pltpu.CompilerParams(
            dimension_semantics=("arbitrary",), vmem_limit_bytes=SMALL_VMEM_LIMIT),
    )(x2d, gains, w_in, q_gains, k_gains, conv_w, *later_weights)


def _min_to_vreg(arrays):
    chunks = [a[r:r + SUBLANES] for a in arrays for r in range(0, a.shape[0], SUBLANES)]
    while len(chunks) > 1:
        odd = chunks[-1:] if len(chunks) % 2 else []
        chunks = [jnp.minimum(a, b) for a, b in zip(chunks[0::2], chunks[1::2])] + odd
    return chunks[0]


def _attn_kernel(q_ref, k_ref, v_ref, o_ref, acc_ref, carry_ref, min_ref):
    n_tiles = q_ref.shape[0] // SUB
    n_blocks = q_ref.shape[1] // LANES
    i = pl.program_id(1)
    assert n_tiles >= NEAR_SUBS + 1

    r = lax.broadcasted_iota(jnp.int32, (2 * SUB, 2 * SUB), 0) % SUB
    c = lax.broadcasted_iota(jnp.int32, (2 * SUB, 2 * SUB), 1)
    tri = jnp.where((c >= SUB) | (r >= c), 1.0, 0.0).astype(BF16)
    causal = (lax.broadcasted_iota(jnp.int32, (HEADS_PER_BLOCK * SUB, SUB), 1)
              < lax.broadcasted_iota(jnp.int32, (HEADS_PER_BLOCK * SUB, SUB), 0) % SUB)

    def tile_rows(t, row0, nrows):
        start = t * SUB + row0
        return pl.ds(start if isinstance(start, int) else pl.multiple_of(start, BF16_SUBLANES), nrows)

    def head_rows(row0, nrows):
        return [slice(h * SUB + row0, h * SUB + row0 + nrows) for h in range(HEADS_PER_BLOCK)]

    def softplus_split(z_s, masked):
        e = jnp.exp2(jnp.abs(z_s) * (-LOG2E))
        sp = jnp.maximum(z_s, 0.0) + jnp.log(1.0 + e)
        if masked:
            sp = jnp.where(causal[:z_s.shape[0]], sp, 0.0)
        hi = sp.astype(BF16)
        lo = (sp - hi.astype(F32)).astype(BF16)
        return jnp.concatenate([hi, lo], axis=1)

    def span(jobs, whole_tiles):
        z_all, cs_all = {}, {}
        for n, (t, row0, nrows, start, nsub, diag, first) in enumerate(jobs):
            assert not diag or (row0 == 0 and nrows == SUB)
            qrows = tile_rows(t, row0, nrows)
            lane = lax.broadcasted_iota(jnp.int32, (nrows, LANES), 1)
            for p in range(n_blocks):
                cols = slice(p * LANES, (p + 1) * LANES)
                q = q_ref[qrows, cols]
                q2 = jnp.concatenate([jnp.where((lane // HEAD_DIM) == h, q, jnp.zeros_like(q))
                                      for h in range(HEADS_PER_BLOCK)], axis=0)
                z = lax.dot_general(q2, k_ref[pl.ds(start, nsub * SUB), cols], (((1,), (1,)), ((), ())),
                                    preferred_element_type=F32)
                z_all[n, p] = z
                lhs_parts = [softplus_split(z[:, s * SUB:(s + 1) * SUB], diag and s == nsub - 1)
                             for s in reversed(range(nsub))]
                cs_all[n, p] = jnp.dot(jnp.concatenate(lhs_parts, axis=0), tri, preferred_element_type=F32)
        latest = {}
        for n, (t, row0, nrows, start, nsub, diag, first) in enumerate(jobs):
            qrows = tile_rows(t, row0, nrows)
            lane = lax.broadcasted_iota(jnp.int32, (nrows, LANES), 1)
            rows2 = HEADS_PER_BLOCK * nrows
            for p in range(n_blocks):
                cols = slice(p * LANES, (p + 1) * LANES)
                carry = None if first else jnp.concatenate([carry_ref[t, p, rs] for rs in head_rows(row0, nrows)], axis=0)
                a_cols = [None] * nsub
                for idx, s in enumerate(reversed(range(nsub))):
                    cs = cs_all[n, p][idx * rows2:(idx + 1) * rows2]
                    g = cs[:, :SUB] if carry is None else cs[:, :SUB] + carry
                    carry = cs[:, SUB:] if carry is None else carry + cs[:, SUB:]
                    a = jnp.exp2((z_all[n, p][:, s * SUB:(s + 1) * SUB] - g) * LOG2E)
                    if diag and s == nsub - 1:
                        a = jnp.where(causal, a, 0.0)
                    a_cols[s] = a.astype(BF16)
                for h, rs in enumerate(head_rows(row0, nrows)):
                    carry_ref[t, p, rs] = carry[h * nrows:(h + 1) * nrows]
                latest.setdefault(n, (t, {}))[1].setdefault(p, []).append((row0, nrows, carry))
                o2 = jnp.dot(jnp.concatenate(a_cols, axis=1), v_ref[pl.ds(start, nsub * SUB), cols],
                             preferred_element_type=F32)
                out = o2[0:nrows]
                for h in range(1, HEADS_PER_BLOCK):
                    out = jnp.where((lane // HEAD_DIM) == h, o2[h * nrows:(h + 1) * nrows], out)
                if first:
                    acc_ref[qrows, cols] = out
                else:
                    acc_ref[qrows, cols] += out
        for t, by_block in latest.values():
            if whole_tiles:
                assert all(len(e) == 1 and e[0][:2] == (0, SUB) for e in by_block.values())
                min_ref[t] = _min_to_vreg([e[0][2] for e in by_block.values()])
            else:
                min_ref[t] = _min_to_vreg([carry_ref[t, p] for p in range(n_blocks)])

    def window(tiles):
        n_keys = (NEAR_SUBS + 2) * SUB
        far = [slice(h * SUB, h * SUB + FAR_ROWS) for h in range(HEADS_PER_BLOCK)]
        rest = [slice(h * SUB + FAR_ROWS, (h + 1) * SUB) for h in range(HEADS_PER_BLOCK)]
        lane = lax.broadcasted_iota(jnp.int32, (SUB, LANES), 1)
        lane_far = lax.broadcasted_iota(jnp.int32, (FAR_ROWS, LANES), 1)
        ctx, scores, suffix_sums, outputs = {}, [], [], []

        def score(t, tile, p):
            start = pl.multiple_of((tile - NEAR_SUBS - 1) * SUB, SUB)
            cols = slice(p * LANES, (p + 1) * LANES)
            q = q_ref[t * SUB:(t + 1) * SUB, cols]
            q2 = jnp.concatenate([jnp.where((lane // HEAD_DIM) == h, q, jnp.zeros_like(q))
                                  for h in range(HEADS_PER_BLOCK)], axis=0)
            z = lax.dot_general(q2, k_ref[pl.ds(start, n_keys), cols], (((1,), (1,)), ((), ())),
                                preferred_element_type=F32)
            ctx[t, p] = (start, z)

        def sums(t, p):
            start, z = ctx[t, p]
            z_far = jnp.concatenate([z[rs, 0:SUB] for rs in far], axis=0)
            lhs = [softplus_split(z[:, s * SUB:(s + 1) * SUB], s == NEAR_SUBS + 1)
                   for s in range(NEAR_SUBS + 1, 0, -1)] + [softplus_split(z_far, False)]
            cs = jnp.dot(jnp.concatenate(lhs, axis=0), tri, preferred_element_type=F32)
            ctx[t, p] = (start, z, z_far, cs)

        def output(t, p, parts):
            cols = slice(p * LANES, (p + 1) * LANES)
            start, z, z_far, cs = ctx[t, p]
            carry, a_cols = None, []
            for idx, s in enumerate(range(NEAR_SUBS + 1, 0, -1)):
                blk = cs[idx * 2 * SUB:(idx + 1) * 2 * SUB]
                g = blk[:, :SUB] if carry is None else blk[:, :SUB] + carry
                carry = blk[:, SUB:] if carry is None else carry + blk[:, SUB:]
                a = jnp.exp2((z[:, s * SUB:(s + 1) * SUB] - g) * LOG2E)
                if s == NEAR_SUBS + 1:
                    a = jnp.where(causal, a, 0.0)
                a_cols.insert(0, a.astype(BF16))
            blk = cs[(NEAR_SUBS + 1) * 2 * SUB:]
            carry_far = jnp.concatenate([carry[rs] for rs in far], axis=0)
            a_far = jnp.exp2((z_far - (blk[:, :SUB] + carry_far)) * LOG2E).astype(BF16)
            carry_far = carry_far + blk[:, SUB:]
            for h in range(HEADS_PER_BLOCK):
                carry_ref[t, p, far[h]] = carry_far[h * FAR_ROWS:(h + 1) * FAR_ROWS]
                carry_ref[t, p, rest[h]] = carry[rest[h]]
                parts.append(carry[rest[h]])
            parts.append(carry_far)
            if p == n_blocks - 1:
                min_ref[t] = _min_to_vreg(parts)
            o2 = jnp.dot(jnp.concatenate(a_cols, axis=1), v_ref[pl.ds(start + SUB, (NEAR_SUBS + 1) * SUB), cols],
                         preferred_element_type=F32)
            o_far = jnp.dot(a_far, v_ref[pl.ds(start, SUB), cols], preferred_element_type=F32)
            out, out_far = o2[0:SUB], o_far[0:FAR_ROWS]
            for h in range(1, HEADS_PER_BLOCK):
                out = jnp.where((lane // HEAD_DIM) == h, o2[h * SUB:(h + 1) * SUB], out)
                out_far = jnp.where((lane_far // HEAD_DIM) == h,
                                    o_far[h * FAR_ROWS:(h + 1) * FAR_ROWS], out_far)
            out = jnp.concatenate([out[:FAR_ROWS] + out_far, out[FAR_ROWS:]], axis=0)
            acc_ref[t * SUB:(t + 1) * SUB, cols] = out
            o_ref[t * SUB:(t + 1) * SUB, cols] = out.astype(o_ref.dtype)

        parts = {t: [] for t, _ in tiles}
        for p in range(n_blocks):
            for t, tile in tiles:
                scores.append(functools.partial(score, t, tile, p))
                suffix_sums.append(functools.partial(sums, t, p))
                outputs.append(functools.partial(output, t, p, parts[t]))
        for stage in scores:
            stage()
        for n in range(len(suffix_sums) + OUTPUT_LAG):
            if n < len(suffix_sums):
                suffix_sums[n]()
            if n >= OUTPUT_LAG:
                outputs[n - OUTPUT_LAG]()

    @pl.when(i == 0)
    def _():
        early = [t for t in range(n_tiles) if t <= NEAR_SUBS]
        span([(t, 0, SUB, 0, t + 1, True, True) for t in early], True)
        window([(t, t) for t in range(n_tiles) if t > NEAR_SUBS])
        for t in early:
            o_ref[t * SUB:(t + 1) * SUB, :] = acc_ref[t * SUB:(t + 1) * SUB, :].astype(o_ref.dtype)

    @pl.when(i > 0)
    def _():
        window([(t, i * n_tiles + t) for t in range(n_tiles)])

    @pl.when(jnp.min(min_ref[...], axis=(0, 1))[0] < SKIP_ABOVE)
    def _():
        def walk_tile(t, _):
            tile = i * n_tiles + t
            if FAR_ROWS < SUB:
                @pl.when(jnp.logical_and(tile > NEAR_SUBS, jnp.min(min_ref[t]) < SKIP_ABOVE))
                def _():
                    start = pl.multiple_of((tile - NEAR_SUBS - 1) * SUB, SUB)
                    span([(t, FAR_ROWS, SUB - FAR_ROWS, start, 1, False, False)], False)

            def unfinished():
                return (jnp.min(min_ref[t]) < SKIP_ABOVE).astype(jnp.int32)

            def cond(state):
                rem, go = state
                return jnp.logical_and(rem >= WALK_SUBS * SUB, go > 0)

            def body(state):
                rem, _ = state
                start = pl.multiple_of(rem - WALK_SUBS * SUB, SUB)
                span([(t, 0, SUB, start, WALK_SUBS, False, False)], False)
                return start, unfinished()

            remaining = jnp.maximum(tile - NEAR_SUBS - 1, 0) * SUB
            rem, go = lax.while_loop(cond, body, (remaining, unfinished()))
            for tail in range(1, WALK_SUBS):
                @pl.when(jnp.logical_and(rem == tail * SUB, go > 0))
                def _(tail=tail):
                    span([(t, 0, SUB, 0, tail, False, False)], False)
            return 0

        lax.fori_loop(0, n_tiles, walk_tile, 0)
        o_ref[...] = acc_ref[...].astype(o_ref.dtype)


def _attention(q, k, v, *, batch, seq):
    n, attn_dim = q.shape
    tq = ATTN_TILES * SUB
    nq = seq // tq
    return pl.pallas_call(
        _attn_kernel,
        grid=(batch, nq),
        in_specs=[
            pl.BlockSpec((tq, attn_dim), lambda b, i: (b * nq + i, 0)),
            pl.BlockSpec((seq, attn_dim), lambda b, i: (b, 0)),
            pl.BlockSpec((seq, attn_dim), lambda b, i: (b, 0)),
        ],
        out_specs=pl.BlockSpec((tq, attn_dim), lambda b, i: (b * nq + i, 0)),
        out_shape=jax.ShapeDtypeStruct((n, attn_dim), BF16),
        scratch_shapes=[pltpu.VMEM((tq, attn_dim), F32),
                        pltpu.VMEM((ATTN_TILES, attn_dim // LANES, HEADS_PER_BLOCK * SUB, SUB), F32),
                        pltpu.VMEM((ATTN_TILES, SUBLANES, SUB), F32)],
        compiler_params=pltpu.CompilerParams(
            dimension_semantics=("arbitrary", "arbitrary"), vmem_limit_bytes=SMALL_VMEM_LIMIT),
    )(q, k, v)


def _ffn_kernel(x_ref, attn_ref, conv_ref, wo_ref, gain_ref, wg_ref, wu_ref, wd_ref, o_ref, act_ref, *, layer):
    attn_dim = attn_ref.shape[1]
    d_ff = wg_ref.shape[1]
    mix = (jnp.dot(attn_ref[...], wo_ref[0:attn_dim, :], preferred_element_type=F32)
           + jnp.dot(conv_ref[...], wo_ref[attn_dim:, :], preferred_element_type=F32))
    x1 = x_ref[...] + mix
    h = ((x1 * _rms_scale(x1)) * gain_ref[layer:layer + 1, :]).astype(BF16)
    for c0 in range(0, d_ff, FF_CHUNK):
        g = jnp.dot(h, wg_ref[:, c0:c0 + FF_CHUNK], preferred_element_type=F32)
        u = jnp.dot(h, wu_ref[:, c0:c0 + FF_CHUNK], preferred_element_type=F32)
        act_ref[:, c0:c0 + FF_CHUNK] = ((g * jax.nn.sigmoid(g)) * u).astype(BF16)
    o_ref[...] = x1 + jnp.dot(act_ref[...], wd_ref[...], preferred_element_type=F32)


def _ffn(x2d, attn, conv, w_out, gains, layer, w_gate, w_up, w_down):
    n, d = x2d.shape
    tm = FFN_ROW_TILE
    d_ff = w_gate.shape[1]
    const = lambda i: (0, 0)
    row = lambda i: (i, 0)
    resident = functools.partial(pl.BlockSpec, index_map=const, pipeline_mode=pl.Buffered(1))
    return pl.pallas_call(
        functools.partial(_ffn_kernel, layer=layer),
        grid=(n // tm,),
        in_specs=[
            pl.BlockSpec((tm, d), row),
            pl.BlockSpec((tm, attn.shape[1]), row),
            pl.BlockSpec((tm, conv.shape[1]), row),
            resident(w_out.shape),
            pl.BlockSpec(gains.shape, const),
            resident(w_gate.shape),
            resident(w_up.shape),
            resident(w_down.shape),
        ],
        out_specs=pl.BlockSpec((tm, d), row),
        out_shape=jax.ShapeDtypeStruct((n, d), F32),
        scratch_shapes=[pltpu.VMEM((tm, d_ff), BF16)],
        compiler_params=pltpu.CompilerParams(
            dimension_semantics=("arbitrary",), vmem_limit_bytes=VMEM_LIMIT),
    )(x2d, attn, conv, w_out, gains, w_gate, w_up, w_down)


def kernel(x, norm_mix, w_in, q_norm, k_norm, conv_w, w_out, norm_ffn, w_gate, w_up, w_down):
    batch, seq, d = x.shape
    depth = w_in.shape[0]
    conv_dim = conv_w.shape[2]
    attn_dim = w_out.shape[1] - conv_dim
    assert seq % ROW_TILE == 0 and seq % (ATTN_TILES * SUB) == 0 and attn_dim % MXU_DIM == 0
    assert (batch * seq) % FFN_ROW_TILE == 0
    assert w_gate.shape[2] % FF_CHUNK == 0 and conv_w.shape[1] == CONV_SHIFTS + 1

    xf = x.reshape(batch * seq, d)
    for l in range(depth):
        q, k, v, conv, wo, wg, wu, wd = _inproj(
            xf, norm_mix, w_in, l, q_norm, k_norm, conv_w,
            (w_out, w_gate, w_up, w_down), seq=seq, attn_dim=attn_dim, conv_dim=conv_dim)
        attn = _attention(q, k, v, batch=batch, seq=seq)
        xf = _ffn(xf, attn, conv, wo, norm_ffn, l, wg, wu, wd)
    return xf.reshape(batch, seq, d)
```
